```python
import math
import jax, jax.numpy as jnp
from jax import lax
import numpy as np

D_MODEL = 1024
BATCH = 8
SEQ = 4096
DEPTH = 4

GRID_W = 64
Q_BLOCK = 128
ROPE_THETA = 500000.0
AXIAL_THETA = 10000.0
LN_EPS = 1e-5
RMS_EPS = 1e-6

A_HEADS = 4
A_QK_DIM = 64
A_V_DIM = 2 * A_QK_DIM
A_WIDTH = A_HEADS * A_V_DIM
A_ROT = A_QK_DIM // 4

B_Q_HEADS = 8
B_KV_HEADS = 2
B_GROUP = B_Q_HEADS // B_KV_HEADS
B_DIM = 64
B_WIDTH = B_Q_HEADS * B_DIM

EVEN_WIDTH = A_WIDTH + B_WIDTH
EV_SPLITS = [
    A_HEADS * 2 * A_QK_DIM,
    A_HEADS * 2 * A_QK_DIM,
    A_HEADS * A_V_DIM,
    B_Q_HEADS * B_DIM,
    B_KV_HEADS * B_DIM,
    B_KV_HEADS * B_DIM,
    EVEN_WIDTH,
]
EV_IN = sum(EV_SPLITS)

C_HEADS = 16
C_NOPE = 64
C_ROPE = 32
C_V = 64
C_Q_LORA = 256
C_KV_LORA = 128
C_WIDTH = C_HEADS * C_V
OD_SPLITS = [C_Q_LORA, C_KV_LORA, C_ROPE, C_WIDTH]
OD_IN = sum(OD_SPLITS)

N_EVEN = (DEPTH + 1) // 2
N_ODD = DEPTH // 2
ALPHA = (2 * DEPTH) ** 0.25
BETA = (8 * DEPTH) ** -0.25

kernel_name = "hybrid_diffattn_gqa_mla_deepnorm_encoder"


def _split_points(sizes):
    return [int(v) for v in np.cumsum(sizes)[:-1]]


def _rms(x, g):
    xf = x.astype(jnp.float32)
    y = xf * lax.rsqrt(jnp.mean(xf * xf, axis=-1, keepdims=True) + RMS_EPS)
    return (y * g.astype(jnp.float32)).astype(x.dtype)


def _layernorm(x, g, b):
    xf = x.astype(jnp.float32)
    mu = jnp.mean(xf, axis=-1, keepdims=True)
    var = jnp.mean(jnp.square(xf - mu), axis=-1, keepdims=True)
    y = (xf - mu) * lax.rsqrt(var + LN_EPS)
    return (y * g.astype(jnp.float32) + b.astype(jnp.float32)).astype(x.dtype)


def _rope_angles(pos, dims, theta):
    inv = theta ** (-jnp.arange(0, dims, 2, dtype=jnp.float32) / dims)
    ang = pos.astype(jnp.float32)[:, None] * inv[None, :]
    return jnp.cos(ang), jnp.sin(ang)


def _rotate(x, cos, sin):
    xf = x.astype(jnp.float32)
    x1, x2 = jnp.split(xf, 2, axis=-1)
    out = jnp.concatenate([x1 * cos - x2 * sin, x2 * cos + x1 * sin], axis=-1)
    return out.astype(x.dtype)


def _partial_rope(x, cos, sin, rot):
    return jnp.concatenate([_rotate(x[..., :rot], cos, sin), x[..., rot:]], axis=-1)


def _axial_rope(x, row_cs, col_cs):
    half = x.shape[-1] // 2
    return jnp.concatenate([_rotate(x[..., :half], *row_cs),
                            _rotate(x[..., half:], *col_cs)], axis=-1)


def _to_blocks(q):
    *lead, s, d = q.shape
    qb = q.reshape(*lead, s // Q_BLOCK, Q_BLOCK, d)
    return jnp.moveaxis(qb, -3, 0)


def _from_blocks(o):
    o = jnp.moveaxis(o, 0, -3)
    *lead, nb, blk, d = o.shape
    return o.reshape(*lead, nb * blk, d)


def _grouped_attention(q, k, v, scale):
    def one(qb):
        s = jnp.einsum('bgrqd,bgkd->bgrqk', qb, k,
                       preferred_element_type=jnp.float32) * scale
        p = jax.nn.softmax(s, axis=-1)
        return jnp.einsum('bgrqk,bgkd->bgrqd', p.astype(v.dtype), v)
    return _from_blocks(lax.map(one, _to_blocks(q)))


def _diff_attention(q, k, v, lam, scale):
    def one(qb):
        s = jnp.einsum('bhcqd,bhckd->bhcqk', qb, k,
                       preferred_element_type=jnp.float32) * scale
        p = jax.nn.softmax(s, axis=-1)
        pd = p[:, :, 0] - lam * p[:, :, 1]
        return jnp.einsum('bhqk,bhkd->bhqd', pd.astype(v.dtype), v)
    return _from_blocks(lax.map(one, _to_blocks(q)))


def _even_layer(x, w_in, w_out, lam_p, subln_g, qn_g, kn_g, ln_g, ln_b,
                layer_idx, rope_a, row_cs, col_cs):
    bsz, s, _ = x.shape
    h = jnp.einsum('bsd,de->bse', x, w_in)
    qa, ka, va, qb, kb, vb, gate = jnp.split(h, _split_points(EV_SPLITS), axis=-1)

    qa = qa.reshape(bsz, s, A_HEADS, 2, A_QK_DIM).transpose(0, 2, 3, 1, 4)
    ka = ka.reshape(bsz, s, A_HEADS, 2, A_QK_DIM).transpose(0, 2, 3, 1, 4)
    va = va.reshape(bsz, s, A_HEADS, A_V_DIM).transpose(0, 2, 1, 3)
    qa = _partial_rope(qa, *rope_a, A_ROT)
    ka = _partial_rope(ka, *rope_a, A_ROT)
    lam_init = 0.8 - 0.6 * math.exp(-0.3 * layer_idx)
    lp = lam_p.astype(jnp.float32)
    lam = (jnp.exp(jnp.sum(lp[0] * lp[1])) - jnp.exp(jnp.sum(lp[2] * lp[3]))
           + lam_init)
    oa = _diff_attention(qa, ka, va, lam, A_QK_DIM ** -0.5)
    oa = _rms(oa, subln_g) * (1.0 - lam_init)
    oa = oa.transpose(0, 2, 1, 3).reshape(bsz, s, A_WIDTH)

    qb = _rms(qb.reshape(bsz, s, B_Q_HEADS, B_DIM), qn_g).transpose(0, 2, 1, 3)
    kb = _rms(kb.reshape(bsz, s, B_KV_HEADS, B_DIM), kn_g).transpose(0, 2, 1, 3)
    vb = vb.reshape(bsz, s, B_KV_HEADS, B_DIM).transpose(0, 2, 1, 3)
    qb = _axial_rope(qb, row_cs, col_cs).reshape(bsz, B_KV_HEADS, B_GROUP, s, B_DIM)
    kb = _axial_rope(kb, row_cs, col_cs)
    ob = _grouped_attention(qb, kb, vb, B_DIM ** -0.5)
    ob = ob.transpose(0, 3, 1, 2, 4).reshape(bsz, s, B_WIDTH)

    o = jnp.concatenate([oa, ob], axis=-1) * jax.nn.silu(gate)
    y = jnp.einsum('bse,ed->bsd', o, w_out)
    return _layernorm(ALPHA * x + y, ln_g, ln_b)


def _odd_layer(x, w_in, q_norm, kv_norm, w_qb, w_kvb, w_out, ln_g, ln_b, rope_c):
    bsz, s, _ = x.shape
    h = jnp.einsum('bsd,de->bse', x, w_in)
    cq, ckv, kr, gate = jnp.split(h, _split_points(OD_SPLITS), axis=-1)

    cq = _rms(cq, q_norm)
    q = jnp.einsum('bsc,ce->bse', cq, w_qb).reshape(bsz, s, C_HEADS, C_NOPE + C_ROPE)
    q = q.transpose(0, 2, 1, 3)
    q = jnp.concatenate([q[..., :C_NOPE], _rotate(q[..., C_NOPE:], *rope_c)], axis=-1)

    ckv = _rms(ckv, kv_norm)
    kv = jnp.einsum('bsc,ce->bse', ckv, w_kvb).reshape(bsz, s, C_HEADS, C_NOPE + C_V)
    kv = kv.transpose(0, 2, 1, 3)
    k_nope, v = kv[..., :C_NOPE], kv[..., C_NOPE:]
    kr = _rotate(kr[:, None], *rope_c)
    kr = jnp.broadcast_to(kr, (bsz, C_HEADS, s, C_ROPE))
    k = jnp.concatenate([k_nope, kr], axis=-1)

    o = _grouped_attention(q[:, :, None], k, v, (C_NOPE + C_ROPE) ** -0.5)[:, :, 0]
    o = o.transpose(0, 2, 1, 3).reshape(bsz, s, C_WIDTH) * jax.nn.silu(gate)
    y = jnp.einsum('bse,ed->bsd', o, w_out)
    return _layernorm(ALPHA * x + y, ln_g, ln_b)


def setup_inputs(seed: int = 0) -> dict:
    key = jax.random.key(seed)
    ks = jax.random.split(key, 20)

    def nrm(k, shape, scale):
        return jax.random.normal(k, shape, jnp.float32) * scale

    return {
        "x": nrm(ks[0], (BATCH, SEQ, D_MODEL), 1.0),
        "ev_w_in": nrm(ks[1], (N_EVEN, D_MODEL, EV_IN), D_MODEL ** -0.5),
        "ev_w_out": nrm(ks[2], (N_EVEN, EVEN_WIDTH, D_MODEL), BETA * EVEN_WIDTH ** -0.5),
        "ev_lam": nrm(ks[3], (N_EVEN, 4, A_QK_DIM), 0.1),
        "ev_subln": 1.0 + nrm(ks[4], (N_EVEN, A_V_DIM), 0.02),
        "ev_qnorm": 1.0 + nrm(ks[5], (N_EVEN, B_DIM), 0.02),
        "ev_knorm": 1.0 + nrm(ks[6], (N_EVEN, B_DIM), 0.02),
        "ev_ln_g": 1.0 + nrm(ks[7], (N_EVEN, D_MODEL), 0.02),
        "ev_ln_b": nrm(ks[8], (N_EVEN, D_MODEL), 0.02),
        "od_w_in": nrm(ks[9], (N_ODD, D_MODEL, OD_IN), D_MODEL ** -0.5),
        "od_qnorm": 1.0 + nrm(ks[10], (N_ODD, C_Q_LORA), 0.02),
        "od_kvnorm": 1.0 + nrm(ks[11], (N_ODD, C_KV_LORA), 0.02),
        "od_w_qb": nrm(ks[12], (N_ODD, C_Q_LORA, C_HEADS * (C_NOPE + C_ROPE)), C_Q_LORA ** -0.5),
        "od_w_kvb": nrm(ks[13], (N_ODD, C_KV_LORA, C_HEADS * (C_NOPE + C_V)), C_KV_LORA ** -0.5),
        "od_w_out": nrm(ks[14], (N_ODD, C_WIDTH, D_MODEL), BETA * C_WIDTH ** -0.5),
        "od_ln_g": 1.0 + nrm(ks[15], (N_ODD, D_MODEL), 0.02),
        "od_ln_b": nrm(ks[16], (N_ODD, D_MODEL), 0.02),
    }


def reference(x, ev_w_in, ev_w_out, ev_lam, ev_subln, ev_qnorm, ev_knorm,
              ev_ln_g, ev_ln_b, od_w_in, od_qnorm, od_kvnorm, od_w_qb, od_w_kvb,
              od_w_out, od_ln_g, od_ln_b):
    s = x.shape[1]
    rows = s // GRID_W
    pos = jnp.arange(s, dtype=jnp.int32)
    row = jnp.repeat(jnp.arange(rows, dtype=jnp.int32), GRID_W)
    col = jnp.tile(jnp.arange(GRID_W, dtype=jnp.int32), rows)

    rope_a = _rope_angles(pos, A_ROT, ROPE_THETA)
    rope_c = _rope_angles(pos, C_ROPE, ROPE_THETA)
    row_cs = _rope_angles(row, B_DIM // 2, AXIAL_THETA)
    col_cs = _rope_angles(col, B_DIM // 2, AXIAL_THETA)

    for layer in range(DEPTH):
        i = layer // 2
        if layer % 2 == 0:
            x = _even_layer(x, ev_w_in[i], ev_w_out[i], ev_lam[i], ev_subln[i],
                            ev_qnorm[i], ev_knorm[i], ev_ln_g[i], ev_ln_b[i],
                            layer, rope_a, row_cs, col_cs)
        else:
            x = _odd_layer(x, od_w_in[i], od_qnorm[i], od_kvnorm[i], od_w_qb[i],
                           od_w_kvb[i], od_w_out[i], od_ln_g[i], od_ln_b[i], rope_c)
    return x
```

```python
import functools
import math

import jax
import jax.numpy as jnp
from jax import lax
from jax.experimental import pallas as pl
from jax.experimental.pallas import tpu as pltpu

F32 = jnp.float32
BF16 = jnp.bfloat16

D_MODEL = 1024
SEQ = 4096
DEPTH = 4
GRID_W = 64
ROPE_THETA = 500000.0
AXIAL_THETA = 10000.0
LN_EPS = 1e-5
RMS_EPS = 1e-6

A_HEADS = 4
A_QK_DIM = 64
A_V_DIM = 128
A_WIDTH = 512
A_ROT = 16
B_Q_HEADS = 8
B_KV_HEADS = 2
B_GROUP = 4
B_DIM = 64
B_WIDTH = 512
EV_IN = 3328

C_HEADS = 16
C_NOPE = 64
C_ROPE = 32
C_V = 64
C_Q_LORA = 256
C_KV_LORA = 128
C_QK = C_NOPE + C_ROPE
OD_IN = 1440

ALPHA = (2 * DEPTH) ** 0.25
LOG2E = 1.4426950408889634

LANES = 128
K_PAD = 128
TM = 512
TQ = 256
VMEM_LIMIT = 48 * 1024 * 1024


def _rep(ref, n):
    a = ref[...]
    return jnp.concatenate([a] * (n // LANES), axis=1)


def _rot(x1, x2, cos, sin):
    return x1 * cos - x2 * sin, x2 * cos + x1 * sin


def _silu(x):
    return x * jax.nn.sigmoid(x)


def _transpose_kernel(x_ref, o_ref):
    o_ref[0] = x_ref[0].T


def _to_feature_major(x):
    b, s, d = x.shape
    return pl.pallas_call(
        _transpose_kernel,
        grid=(b, s // TM),
        in_specs=[pl.BlockSpec((1, TM, d), lambda i, j: (i, j, 0))],
        out_specs=pl.BlockSpec((1, d, TM), lambda i, j: (i, 0, j)),
        out_shape=jax.ShapeDtypeStruct((b, d, s), x.dtype),
        compiler_params=pltpu.CompilerParams(
            dimension_semantics=("parallel", "parallel"), vmem_limit_bytes=VMEM_LIMIT),
        name="to_feature_major",
    )(x)


def _to_token_major(xt):
    b, d, s = xt.shape
    return pl.pallas_call(
        _transpose_kernel,
        grid=(b, s // TM),
        in_specs=[pl.BlockSpec((1, d, TM), lambda i, j: (i, 0, j))],
        out_specs=pl.BlockSpec((1, TM, d), lambda i, j: (i, j, 0)),
        out_shape=jax.ShapeDtypeStruct((b, s, d), xt.dtype),
        compiler_params=pltpu.CompilerParams(
            dimension_semantics=("parallel", "parallel"), vmem_limit_bytes=VMEM_LIMIT),
        name="to_token_major",
    )(xt)


def _even_in_kernel(x_ref, w_ref, tab_ref, qn_ref, kn_ref,
                    qa_ref, ka_ref, va_ref, qb_ref, kb_ref, vb_ref, g_ref):
    xb = x_ref[0].astype(BF16)

    def proj(r0, r1):
        return jnp.dot(w_ref[r0:r1, :], xb, preferred_element_type=F32)

    cos_a, sin_a = tab_ref[0:8, :], tab_ref[8:16, :]
    cos_r, sin_r = tab_ref[16:32, :], tab_ref[32:48, :]
    cos_c, sin_c = tab_ref[48:64, :], tab_ref[64:80, :]

    def rope_a(h):
        outs = []
        for hc in range(2 * A_HEADS):
            b = hc * A_QK_DIM
            r1, r2 = _rot(h[b:b + 8], h[b + 8:b + 16], cos_a, sin_a)
            outs.append(jnp.concatenate([r1, r2, h[b + 16:b + 64]], axis=0))
        return outs

    qa_ref[0] = jnp.concatenate(rope_a(proj(0, 512)), axis=0).astype(BF16)
    ks = rope_a(proj(512, 1024))
    for h in range(A_HEADS):
        kt = jnp.concatenate([ks[2 * h], ks[2 * h + 1]], axis=0)
        ka_ref[0, h] = kt.T.astype(BF16)
    va_ref[0, 0] = proj(1024, 1536).astype(BF16)

    def norm_axial(h, g):
        ms = jnp.mean(h * h, axis=0, keepdims=True)
        y = h * lax.rsqrt(ms + RMS_EPS) * g
        a1, a2 = _rot(y[0:16], y[16:32], cos_r, sin_r)
        b1, b2 = _rot(y[32:48], y[48:64], cos_c, sin_c)
        return jnp.concatenate([a1, a2, b1, b2], axis=0)

    qn = _rep(qn_ref, TM)
    kn = _rep(kn_ref, TM)
    hq = proj(1536, 2048)
    qb_ref[0] = jnp.concatenate(
        [norm_axial(hq[h * 64:(h + 1) * 64], qn) for h in range(B_Q_HEADS)], axis=0).astype(BF16)
    hkv = proj(2048, 2304)
    kt = jnp.concatenate(
        [norm_axial(hkv[h * 64:(h + 1) * 64], kn) for h in range(B_KV_HEADS)], axis=0)
    kb_ref[0] = kt.T.astype(BF16)
    vb_ref[0, 0] = hkv[128:256].astype(BF16)
    g_ref[0] = _silu(proj(2304, 3328))


def _even_in(xt, w_t, tab, qn, kn):
    b, d, s = xt.shape
    nt = s // TM
    tok = lambda i, j: (i, 0, j)
    chunk = lambda i, j: (i, j, 0, 0)
    full2 = lambda i, j: (0, 0)
    return pl.pallas_call(
        _even_in_kernel,
        grid=(b, nt),
        in_specs=[
            pl.BlockSpec((1, d, TM), tok),
            pl.BlockSpec((EV_IN, d), full2),
            pl.BlockSpec((80, TM), lambda i, j: (0, j)),
            pl.BlockSpec((B_DIM, LANES), full2),
            pl.BlockSpec((B_DIM, LANES), full2),
        ],
        out_specs=[
            pl.BlockSpec((1, 512, TM), tok),
            pl.BlockSpec((1, A_HEADS, TM, K_PAD), lambda i, j: (i, 0, j, 0)),
            pl.BlockSpec((1, 1, 512, TM), chunk),
            pl.BlockSpec((1, 512, TM), tok),
            pl.BlockSpec((1, TM, K_PAD), lambda i, j: (i, j, 0)),
            pl.BlockSpec((1, 1, 128, TM), chunk),
            pl.BlockSpec((1, 1024, TM), tok),
        ],
        out_shape=[
            jax.ShapeDtypeStruct((b, 512, s), BF16),
            jax.ShapeDtypeStruct((b, A_HEADS, s, K_PAD), BF16),
            jax.ShapeDtypeStruct((b, nt, 512, TM), BF16),
            jax.ShapeDtypeStruct((b, 512, s), BF16),
            jax.ShapeDtypeStruct((b, s, K_PAD), BF16),
            jax.ShapeDtypeStruct((b, nt, 128, TM), BF16),
            jax.ShapeDtypeStruct((b, 1024, s), F32),
        ],
        compiler_params=pltpu.CompilerParams(
            dimension_semantics=("parallel", "parallel"), vmem_limit_bytes=VMEM_LIMIT),
        name="even_in_proj",
    )(xt, w_t, tab, qn, kn)


def _odd_in_kernel(x_ref, w_ref, wq_ref, wkv_ref, tab_ref, qn_ref, kvn_ref,
                   q_ref, k_ref, v_ref, g_ref):
    xb = x_ref[0].astype(BF16)
    cos, sin = tab_ref[0:16, :], tab_ref[16:32, :]

    def rms(h, g):
        ms = jnp.mean(h * h, axis=0, keepdims=True)
        return h * lax.rsqrt(ms + RMS_EPS) * g

    lat = jnp.dot(w_ref[0:416, :], xb, preferred_element_type=F32)
    cqn = rms(lat[0:256], _rep(qn_ref, TM)).astype(BF16)
    q = jnp.dot(wq_ref[...], cqn, preferred_element_type=F32)
    outs = []
    for h in range(C_HEADS):
        b = h * C_QK
        r1, r2 = _rot(q[b + 64:b + 80], q[b + 80:b + 96], cos, sin)
        outs += [q[b:b + 64], r1, r2]
    q_ref[0] = jnp.concatenate(outs, axis=0).astype(BF16)

    ckvn = rms(lat[256:384], _rep(kvn_ref, TM)).astype(BF16)
    kv = jnp.dot(wkv_ref[...], ckvn, preferred_element_type=F32)
    r1, r2 = _rot(lat[384:400], lat[400:416], cos, sin)
    zpad = jnp.zeros((K_PAD - C_QK, TM), F32)
    for h in range(C_HEADS):
        kt = jnp.concatenate([kv[h * 128:h * 128 + 64], r1, r2, zpad], axis=0)
        k_ref[0, h] = kt.T.astype(BF16)
    v_ref[0, 0] = jnp.concatenate(
        [kv[h * 128 + 64:h * 128 + 128] for h in range(C_HEADS)], axis=0).astype(BF16)
    g_ref[0] = _silu(jnp.dot(w_ref[416:1440, :], xb, preferred_element_type=F32))


def _odd_in(xt, w_t, wq_t, wkv_t, tab, qn, kvn):
    b, d, s = xt.shape
    nt = s // TM
    tok = lambda i, j: (i, 0, j)
    full2 = lambda i, j: (0, 0)
    return pl.pallas_call(
        _odd_in_kernel,
        grid=(b, nt),
        in_specs=[
            pl.BlockSpec((1, d, TM), tok),
            pl.BlockSpec((OD_IN, d), full2),
            pl.BlockSpec((C_HEADS * C_QK, C_Q_LORA), full2),
            pl.BlockSpec((C_HEADS * 128, C_KV_LORA), full2),
            pl.BlockSpec((32, TM), lambda i, j: (0, j)),
            pl.BlockSpec((C_Q_LORA, LANES), full2),
            pl.BlockSpec((C_KV_LORA, LANES), full2),
        ],
        out_specs=[
            pl.BlockSpec((1, C_HEADS * C_QK, TM), tok),
            pl.BlockSpec((1, C_HEADS, TM, K_PAD), lambda i, j: (i, 0, j, 0)),
            pl.BlockSpec((1, 1, 1024, TM), lambda i, j: (i, j, 0, 0)),
            pl.BlockSpec((1, 1024, TM), tok),
        ],
        out_shape=[
            jax.ShapeDtypeStruct((b, C_HEADS * C_QK, s), BF16),
            jax.ShapeDtypeStruct((b, C_HEADS, s, K_PAD), BF16),
            jax.ShapeDtypeStruct((b, nt, 1024, TM), BF16),
            jax.ShapeDtypeStruct((b, 1024, s), F32),
        ],
        compiler_params=pltpu.CompilerParams(
            dimension_semantics=("parallel", "parallel"), vmem_limit_bytes=VMEM_LIMIT),
        name="odd_in_proj",
    )(xt, w_t, wq_t, wkv_t, tab, qn, kvn)


def _softmax_pv(qps, k_at, v_at, s_ref, c, dv):
    n = len(qps)
    nchunks = SEQ // TM

    def scores(ci, mx):
        r0 = pl.multiple_of(ci * TM, TM)
        kc = k_at(r0)
        out = []
        for j in range(n):
            s = jnp.dot(kc, qps[j], preferred_element_type=F32)
            s_ref[j, pl.ds(r0, TM), :] = s
            out.append(jnp.maximum(mx[j], jnp.max(s.reshape(TM // 8, 8, TQ), axis=0)))
        return tuple(out)

    mx = lax.fori_loop(0, nchunks, scores,
                       tuple(jnp.full((8, TQ), -jnp.inf, F32) for _ in range(n)))
    mc = [jnp.max(m, axis=0, keepdims=True) * c for m in mx]

    def weighted(ci, carry):
        ls, accs = carry
        r0 = pl.multiple_of(ci * TM, TM)
        v = v_at(ci)
        nls, naccs = [], []
        for j in range(n):
            p = jnp.exp2(s_ref[j, pl.ds(r0, TM), :] * c - mc[j])
            nls.append(ls[j] + jnp.sum(p.reshape(TM // 8, 8, TQ), axis=0))
            naccs.append(accs[j] + jnp.dot(v, p.astype(BF16), preferred_element_type=F32))
        return tuple(nls), tuple(naccs)

    ls, accs = lax.fori_loop(
        0, nchunks, weighted,
        (tuple(jnp.zeros((8, TQ), F32) for _ in range(n)),
         tuple(jnp.zeros((dv, TQ), F32) for _ in range(n))))
    return [(accs[j], jnp.sum(ls[j], axis=0, keepdims=True)) for j in range(n)]


def _attn_a_kernel(lam_ref, q_ref, k_ref, v_ref, gate_ref, sg_ref, o_ref, s_ref, *, lam_init):
    q = q_ref[0]
    z = jnp.zeros((A_QK_DIM, TQ), BF16)
    qps = [jnp.concatenate([q[0:64], z], axis=0), jnp.concatenate([z, q[64:128]], axis=0)]
    (a0, l0), (a1, l1) = _softmax_pv(
        qps, lambda r0: k_ref[0, 0, pl.ds(r0, TM), :], lambda ci: v_ref[0, ci],
        s_ref, A_QK_DIM ** -0.5 * LOG2E, A_V_DIM)
    lp = lam_ref[...]
    lam = (jnp.exp(jnp.sum(lp[0:1] * lp[1:2], axis=1, keepdims=True))
           - jnp.exp(jnp.sum(lp[2:3] * lp[3:4], axis=1, keepdims=True)) + lam_init)
    o = a0 / l0 - lam * (a1 / l1)
    ms = jnp.mean(o * o, axis=0, keepdims=True)
    o = o * lax.rsqrt(ms + RMS_EPS) * _rep(sg_ref, TQ) * (1.0 - lam_init)
    o_ref[0] = (o * gate_ref[0]).astype(BF16)


def _attn_a(lam_p, qa, ka, va, gate, sg, lam_init):
    b = qa.shape[0]
    nt = SEQ // TM
    return pl.pallas_call(
        functools.partial(_attn_a_kernel, lam_init=lam_init),
        grid=(b, A_HEADS, SEQ // TQ),
        in_specs=[
            pl.BlockSpec((4, A_QK_DIM), lambda i, h, t: (0, 0)),
            pl.BlockSpec((1, 128, TQ), lambda i, h, t: (i, h, t)),
            pl.BlockSpec((1, 1, SEQ, K_PAD), lambda i, h, t: (i, h, 0, 0)),
            pl.BlockSpec((1, nt, A_V_DIM, TM), lambda i, h, t: (i, 0, h, 0)),
            pl.BlockSpec((1, A_V_DIM, TQ), lambda i, h, t: (i, h, t)),
            pl.BlockSpec((A_V_DIM, LANES), lambda i, h, t: (0, 0)),
        ],
        out_specs=pl.BlockSpec((1, A_V_DIM, TQ), lambda i, h, t: (i, h, t)),
        out_shape=jax.ShapeDtypeStruct((b, A_WIDTH, SEQ), BF16),
        scratch_shapes=[pltpu.VMEM((2, SEQ, TQ), F32)],
        compiler_params=pltpu.CompilerParams(
            dimension_semantics=("parallel", "parallel", "parallel"),
            vmem_limit_bytes=VMEM_LIMIT),
        name="attn_diff",
    )(lam_p, qa, ka, va, gate, sg)


def _attn_b_kernel(q_ref, k_ref, v_ref, gate_ref, o_ref, s_ref):
    first = pl.program_id(1) < B_GROUP
    q = q_ref[0]
    z = jnp.zeros_like(q)
    qp = jnp.concatenate([jnp.where(first, q, z), jnp.where(first, z, q)], axis=0)
    ((a, l),) = _softmax_pv(
        [qp], lambda r0: k_ref[0, pl.ds(r0, TM), :], lambda ci: v_ref[0, ci],
        s_ref, B_DIM ** -0.5 * LOG2E, B_DIM)
    o_ref[0] = (a / l * gate_ref[0]).astype(BF16)


def _attn_b(qb, kb, vb, gate):
    b = qb.shape[0]
    nt = SEQ // TM
    return pl.pallas_call(
        _attn_b_kernel,
        grid=(b, B_Q_HEADS, SEQ // TQ),
        in_specs=[
            pl.BlockSpec((1, B_DIM, TQ), lambda i, h, t: (i, h, t)),
            pl.BlockSpec((1, SEQ, K_PAD), lambda i, h, t: (i, 0, 0)),
            pl.BlockSpec((1, nt, B_DIM, TM), lambda i, h, t: (i, 0, h // B_GROUP, 0)),
            pl.BlockSpec((1, B_DIM, TQ), lambda i, h, t: (i, A_WIDTH // B_DIM + h, t)),
        ],
        out_specs=pl.BlockSpec((1, B_DIM, TQ), lambda i, h, t: (i, h, t)),
        out_shape=jax.ShapeDtypeStruct((b, B_WIDTH, SEQ), BF16),
        scratch_shapes=[pltpu.VMEM((1, SEQ, TQ), F32)],
        compiler_params=pltpu.CompilerParams(
            dimension_semantics=("parallel", "parallel", "parallel"),
            vmem_limit_bytes=VMEM_LIMIT),
        name="attn_gqa",
    )(qb, kb, vb, gate)


def _attn_c_kernel(q_ref, k_ref, v_ref, gate_ref, o_ref, s_ref):
    qp = jnp.concatenate([q_ref[0], jnp.zeros((K_PAD - C_QK, TQ), BF16)], axis=0)
    ((a, l),) = _softmax_pv(
        [qp], lambda r0: k_ref[0, 0, pl.ds(r0, TM), :], lambda ci: v_ref[0, ci],
        s_ref, C_QK ** -0.5 * LOG2E, C_V)
    o_ref[0] = (a / l * gate_ref[0]).astype(BF16)


def _attn_c(q, k, v, gate):
    b = q.shape[0]
    nt = SEQ // TM
    return pl.pallas_call(
        _attn_c_kernel,
        grid=(b, C_HEADS, SEQ // TQ),
        in_specs=[
            pl.BlockSpec((1, C_QK, TQ), lambda i, h, t: (i, h, t)),
            pl.BlockSpec((1, 1, SEQ, K_PAD), lambda i, h, t: (i, h, 0, 0)),
            pl.BlockSpec((1, nt, C_V, TM), lambda i, h, t: (i, 0, h, 0)),
            pl.BlockSpec((1, C_V, TQ), lambda i, h, t: (i, h, t)),
        ],
        out_specs=pl.BlockSpec((1, C_V, TQ), lambda i, h, t: (i, h, t)),
        out_shape=jax.ShapeDtypeStruct((b, C_HEADS * C_V, SEQ), BF16),
        scratch_shapes=[pltpu.VMEM((1, SEQ, TQ), F32)],
        compiler_params=pltpu.CompilerParams(
            dimension_semantics=("parallel", "parallel", "parallel"),
            vmem_limit_bytes=VMEM_LIMIT),
        name="attn_mla",
    )(q, k, v, gate)


def _out_kernel(*refs, n_og):
    og_refs = refs[:n_og]
    x_ref, w_ref, lg_ref, lb_ref, o_ref = refs[n_og:]
    og = jnp.concatenate([r[0] for r in og_refs], axis=0) if n_og > 1 else og_refs[0][0]
    y = jnp.dot(w_ref[...], og, preferred_element_type=F32)
    z = ALPHA * x_ref[0] + y
    mu = jnp.mean(z, axis=0, keepdims=True)
    d = z - mu
    var = jnp.mean(d * d, axis=0, keepdims=True)
    o_ref[0] = d * lax.rsqrt(var + LN_EPS) * _rep(lg_ref, TM) + _rep(lb_ref, TM)


def _out_proj(ogs, xt, w_t, lg, lb):
    b, d, s = xt.shape
    tok = lambda i, j: (i, 0, j)
    full2 = lambda i, j: (0, 0)
    return pl.pallas_call(
        functools.partial(_out_kernel, n_og=len(ogs)),
        grid=(b, s // TM),
        in_specs=[pl.BlockSpec((1, og.shape[1], TM), tok) for og in ogs] + [
            pl.BlockSpec((1, d, TM), tok),
            pl.BlockSpec((d, d), full2),
            pl.BlockSpec((d, LANES), full2),
            pl.BlockSpec((d, LANES), full2),
        ],
        out_specs=pl.BlockSpec((1, d, TM), tok),
        out_shape=jax.ShapeDtypeStruct((b, d, s), F32),
        compiler_params=pltpu.CompilerParams(
            dimension_semantics=("parallel", "parallel"), vmem_limit_bytes=VMEM_LIMIT),
        name="out_proj_ln",
    )(*ogs, xt, w_t, lg, lb)


def _col(v):
    v = v.astype(F32)
    return jnp.broadcast_to(v[:, None], (v.shape[0], LANES))


def _angles_t(pos, dims, theta):
    inv = theta ** (-jnp.arange(0, dims, 2, dtype=F32) / dims)
    ang = pos.astype(F32)[:, None] * inv[None, :]
    return jnp.cos(ang).T, jnp.sin(ang).T


@jax.jit
def _forward(x, ev_w_in, ev_w_out, ev_lam, ev_subln, ev_qnorm, ev_knorm, ev_ln_g, ev_ln_b,
             od_w_in, od_qnorm, od_kvnorm, od_w_qb, od_w_kvb, od_w_out, od_ln_g, od_ln_b):
    s = x.shape[1]
    pos = jnp.arange(s, dtype=jnp.int32)
    row = jnp.repeat(jnp.arange(s // GRID_W, dtype=jnp.int32), GRID_W)
    col = jnp.tile(jnp.arange(GRID_W, dtype=jnp.int32), s // GRID_W)
    tab_ev = jnp.concatenate(
        _angles_t(pos, A_ROT, ROPE_THETA) + _angles_t(row, B_DIM // 2, AXIAL_THETA)
        + _angles_t(col, B_DIM // 2, AXIAL_THETA), axis=0)
    tab_od = jnp.concatenate(_angles_t(pos, C_ROPE, ROPE_THETA), axis=0)

    xt = _to_feature_major(x)
    for layer in range(DEPTH):
        i = layer // 2
        if layer % 2 == 0:
            qa, ka, va, qb, kb, vb, gate = _even_in(
                xt, ev_w_in[i].T.astype(BF16), tab_ev, _col(ev_qnorm[i]), _col(ev_knorm[i]))
            lam_init = 0.8 - 0.6 * math.exp(-0.3 * layer)
            oga = _attn_a(ev_lam[i].astype(F32), qa, ka, va, gate, _col(ev_subln[i]), lam_init)
            ogb = _attn_b(qb, kb, vb, gate)
            xt = _out_proj([oga, ogb], xt, ev_w_out[i].T.astype(BF16),
                           _col(ev_ln_g[i]), _col(ev_ln_b[i]))
        else:
            q, k, v, gate = _odd_in(
                xt, od_w_in[i].T.astype(BF16), od_w_qb[i].T.astype(BF16),
                od_w_kvb[i].T.astype(BF16), tab_od, _col(od_qnorm[i]), _col(od_kvnorm[i]))
            ogc = _attn_c(q, k, v, gate)
            xt = _out_proj([ogc], xt, od_w_out[i].T.astype(BF16),
                           _col(od_ln_g[i]), _col(od_ln_b[i]))
    return _to_token_major(xt)


def kernel(x, ev_w_in, ev_w_out, ev_lam, ev_subln, ev_qnorm, ev_knorm, ev_ln_g, ev_ln_b,
           od_w_in, od_qnorm, od_kvnorm, od_w_qb, od_w_kvb, od_w_out, od_ln_g, od_ln_b):
    return _forward(x, ev_w_in, ev_w_out, ev_lam, ev_subln, ev_qnorm, ev_knorm, ev_ln_g,
                    ev_ln_b, od_w_in, od_qnorm, od_kvnorm, od_w_qb, od_w_kvb, od_w_out,
                    od_ln_g, od_ln_b)
```

```python
import functools
import math

import jax
import jax.numpy as jnp
from jax import lax
from jax.experimental import pallas as pl
from jax.experimental.pallas import tpu as pltpu

F32 = jnp.float32
BF16 = jnp.bfloat16

D_MODEL = 1024
SEQ = 4096
DEPTH = 4
GRID_W = 64
ROPE_THETA = 500000.0
AXIAL_THETA = 10000.0
LN_EPS = 1e-5
RMS_EPS = 1e-6

A_HEADS = 4
A_QK_DIM = 64
A_V_DIM = 128
A_WIDTH = 512
A_ROT = 16
B_Q_HEADS = 8
B_KV_HEADS = 2
B_GROUP = 4
B_DIM = 64
B_WIDTH = 512
EV_IN = 3328

C_HEADS = 16
C_NOPE = 64
C_ROPE = 32
C_V = 64
C_Q_LORA = 256
C_KV_LORA = 128
C_QK = C_NOPE + C_ROPE
OD_IN = 1440

ALPHA = (2 * DEPTH) ** 0.25
LOG2E = 1.4426950408889634

QSCALE_AB = A_QK_DIM ** -0.5 * LOG2E
QSCALE_C = C_QK ** -0.5 * LOG2E

LANES = 128
MXU_TILE = 256
LOOKAHEAD = 2
ONES_ROWS = 16
K_PAD = 128
TM = 512
TQ = 256
VMEM_LIMIT = 48 * 1024 * 1024


def _rep(ref, n):
    a = ref[...]
    return jnp.concatenate([a] * (n // LANES), axis=1)


def _rot(x1, x2, cos, sin):
    return x1 * cos - x2 * sin, x2 * cos + x1 * sin


def _silu(x):
    return x * jax.nn.sigmoid(x)


def _transpose_kernel(x_ref, o_ref):
    o_ref[0] = x_ref[0].T


def _to_feature_major(x):
    b, s, d = x.shape
    return pl.pallas_call(
        _transpose_kernel,
        grid=(b, s // TM),
        in_specs=[pl.BlockSpec((1, TM, d), lambda i, j: (i, j, 0))],
        out_specs=pl.BlockSpec((1, d, TM), lambda i, j: (i, 0, j)),
        out_shape=jax.ShapeDtypeStruct((b, d, s), x.dtype),
        compiler_params=pltpu.CompilerParams(
            dimension_semantics=("parallel", "parallel"), vmem_limit_bytes=VMEM_LIMIT),
        name="to_feature_major",
    )(x)


def _to_token_major(xt):
    b, d, s = xt.shape
    return pl.pallas_call(
        _transpose_kernel,
        grid=(b, s // TM),
        in_specs=[pl.BlockSpec((1, d, TM), lambda i, j: (i, 0, j))],
        out_specs=pl.BlockSpec((1, TM, d), lambda i, j: (i, j, 0)),
        out_shape=jax.ShapeDtypeStruct((b, s, d), xt.dtype),
        compiler_params=pltpu.CompilerParams(
            dimension_semantics=("parallel", "parallel"), vmem_limit_bytes=VMEM_LIMIT),
        name="to_token_major",
    )(xt)


def _even_in_kernel(x_ref, w_ref, tab_ref, qn_ref, kn_ref,
                    qa_ref, ka_ref, va_ref, qb_ref, kb_ref, vb_ref, g_ref):
    xb = x_ref[0].astype(BF16)

    def proj(r0, r1):
        return jnp.dot(w_ref[r0:r1, :], xb, preferred_element_type=F32)

    cos_a, sin_a = tab_ref[0:8, :], tab_ref[8:16, :]
    cos_r, sin_r = tab_ref[16:32, :], tab_ref[32:48, :]
    cos_c, sin_c = tab_ref[48:64, :], tab_ref[64:80, :]

    def rope_a(h):
        outs = []
        for hc in range(2 * A_HEADS):
            b = hc * A_QK_DIM
            r1, r2 = _rot(h[b:b + 8], h[b + 8:b + 16], cos_a, sin_a)
            outs.append(jnp.concatenate([r1, r2, h[b + 16:b + 64]], axis=0))
        return outs

    qa_ref[0] = (jnp.concatenate(rope_a(proj(0, 512)), axis=0) * QSCALE_AB).astype(BF16)
    ks = rope_a(proj(512, 1024))
    for h in range(A_HEADS):
        kt = jnp.concatenate([ks[2 * h], ks[2 * h + 1]], axis=0)
        ka_ref[0, h] = kt.T.astype(BF16)
    va_ref[0, 0] = proj(1024, 1536).astype(BF16)

    def norm_axial(h, g):
        ms = jnp.mean(h * h, axis=0, keepdims=True)
        y = h * lax.rsqrt(ms + RMS_EPS) * g
        a1, a2 = _rot(y[0:16], y[16:32], cos_r, sin_r)
        b1, b2 = _rot(y[32:48], y[48:64], cos_c, sin_c)
        return jnp.concatenate([a1, a2, b1, b2], axis=0)

    qn = _rep(qn_ref, TM)
    kn = _rep(kn_ref, TM)
    hq = proj(1536, 2048)
    qb_ref[0] = (jnp.concatenate(
        [norm_axial(hq[h * 64:(h + 1) * 64], qn) for h in range(B_Q_HEADS)], axis=0)
                 * QSCALE_AB).astype(BF16)
    hkv = proj(2048, 2304)
    kt = jnp.concatenate(
        [norm_axial(hkv[h * 64:(h + 1) * 64], kn) for h in range(B_KV_HEADS)], axis=0)
    kb_ref[0] = kt.T.astype(BF16)
    vb_ref[0, 0] = hkv[128:256].astype(BF16)
    g_ref[0] = _silu(proj(2304, 3328))


def _even_in(xt, w_t, tab, qn, kn):
    b, d, s = xt.shape
    nt = s // TM
    tok = lambda i, j: (i, 0, j)
    chunk = lambda i, j: (i, j, 0, 0)
    full2 = lambda i, j: (0, 0)
    return pl.pallas_call(
        _even_in_kernel,
        grid=(b, nt),
        in_specs=[
            pl.BlockSpec((1, d, TM), tok),
            pl.BlockSpec((EV_IN, d), full2),
            pl.BlockSpec((80, TM), lambda i, j: (0, j)),
            pl.BlockSpec((B_DIM, LANES), full2),
            pl.BlockSpec((B_DIM, LANES), full2),
        ],
        out_specs=[
            pl.BlockSpec((1, 512, TM), tok),
            pl.BlockSpec((1, A_HEADS, TM, K_PAD), lambda i, j: (i, 0, j, 0)),
            pl.BlockSpec((1, 1, 512, TM), chunk),
            pl.BlockSpec((1, 512, TM), tok),
            pl.BlockSpec((1, TM, K_PAD), lambda i, j: (i, j, 0)),
            pl.BlockSpec((1, 1, 128, TM), chunk),
            pl.BlockSpec((1, 1024, TM), tok),
        ],
        out_shape=[
            jax.ShapeDtypeStruct((b, 512, s), BF16),
            jax.ShapeDtypeStruct((b, A_HEADS, s, K_PAD), BF16),
            jax.ShapeDtypeStruct((b, nt, 512, TM), BF16),
            jax.ShapeDtypeStruct((b, 512, s), BF16),
            jax.ShapeDtypeStruct((b, s, K_PAD), BF16),
            jax.ShapeDtypeStruct((b, nt, 128, TM), BF16),
            jax.ShapeDtypeStruct((b, 1024, s), F32),
        ],
        compiler_params=pltpu.CompilerParams(
            dimension_semantics=("parallel", "parallel"), vmem_limit_bytes=VMEM_LIMIT),
        name="even_in_proj",
    )(xt, w_t, tab, qn, kn)


def _odd_in_kernel(x_ref, w_ref, wq_ref, wkv_ref, tab_ref, qn_ref, kvn_ref,
                   q_ref, k_ref, v_ref, g_ref):
    xb = x_ref[0].astype(BF16)
    cos, sin = tab_ref[0:16, :], tab_ref[16:32, :]

    def rms(h, g):
        ms = jnp.mean(h * h, axis=0, keepdims=True)
        return h * lax.rsqrt(ms + RMS_EPS) * g

    lat = jnp.dot(w_ref[0:416, :], xb, preferred_element_type=F32)
    cqn = rms(lat[0:256], _rep(qn_ref, TM)).astype(BF16)
    q = jnp.dot(wq_ref[...], cqn, preferred_element_type=F32)
    outs = []
    for h in range(C_HEADS):
        b = h * C_QK
        r1, r2 = _rot(q[b + 64:b + 80], q[b + 80:b + 96], cos, sin)
        outs += [q[b:b + 64], r1, r2]
    q_ref[0] = (jnp.concatenate(outs, axis=0) * QSCALE_C).astype(BF16)

    ckvn = rms(lat[256:384], _rep(kvn_ref, TM)).astype(BF16)
    kv = jnp.dot(wkv_ref[...], ckvn, preferred_element_type=F32)
    r1, r2 = _rot(lat[384:400], lat[400:416], cos, sin)
    zpad = jnp.zeros((K_PAD - C_QK, TM), F32)
    for h in range(C_HEADS):
        kt = jnp.concatenate([kv[h * 128:h * 128 + 64], r1, r2, zpad], axis=0)
        k_ref[0, h] = kt.T.astype(BF16)
    v_ref[0, 0] = jnp.concatenate(
        [kv[h * 128 + 64:h * 128 + 128] for h in range(C_HEADS)], axis=0).astype(BF16)
    g_ref[0] = _silu(jnp.dot(w_ref[416:1440, :], xb, preferred_element_type=F32))


def _odd_in(xt, w_t, wq_t, wkv_t, tab, qn, kvn):
    b, d, s = xt.shape
    nt = s // TM
    tok = lambda i, j: (i, 0, j)
    full2 = lambda i, j: (0, 0)
    return pl.pallas_call(
        _odd_in_kernel,
        grid=(b, nt),
        in_specs=[
            pl.BlockSpec((1, d, TM), tok),
            pl.BlockSpec((OD_IN, d), full2),
            pl.BlockSpec((C_HEADS * C_QK, C_Q_LORA), full2),
            pl.BlockSpec((C_HEADS * 128, C_KV_LORA), full2),
            pl.BlockSpec((32, TM), lambda i, j: (0, j)),
            pl.BlockSpec((C_Q_LORA, LANES), full2),
            pl.BlockSpec((C_KV_LORA, LANES), full2),
        ],
        out_specs=[
            pl.BlockSpec((1, C_HEADS * C_QK, TM), tok),
            pl.BlockSpec((1, C_HEADS, TM, K_PAD), lambda i, j: (i, 0, j, 0)),
            pl.BlockSpec((1, 1, 1024, TM), lambda i, j: (i, j, 0, 0)),
            pl.BlockSpec((1, 1024, TM), tok),
        ],
        out_shape=[
            jax.ShapeDtypeStruct((b, C_HEADS * C_QK, s), BF16),
            jax.ShapeDtypeStruct((b, C_HEADS, s, K_PAD), BF16),
            jax.ShapeDtypeStruct((b, nt, 1024, TM), BF16),
            jax.ShapeDtypeStruct((b, 1024, s), F32),
        ],
        compiler_params=pltpu.CompilerParams(
            dimension_semantics=("parallel", "parallel"), vmem_limit_bytes=VMEM_LIMIT),
        name="odd_in_proj",
    )(xt, w_t, wq_t, wkv_t, tab, qn, kvn)


def _softmax_pv(qps, k_at, v_at, dv):
    n = len(qps)
    nchunks = SEQ // TM
    halves = TM // MXU_TILE
    ones = jnp.ones((ONES_ROWS, TM), BF16)

    def scores(ci):
        return [[jnp.dot(k_at(ci * TM + h * MXU_TILE), qps[j], preferred_element_type=F32)
                 for h in range(halves)] for j in range(n)]

    m = [None] * n
    acc = [None] * n
    pending = [scores(ci) for ci in range(LOOKAHEAD)]
    for ci in range(nchunks):
        if ci + LOOKAHEAD < nchunks:
            pending.append(scores(ci + LOOKAHEAD))
        sc = pending.pop(0)
        v = jnp.concatenate([v_at(ci), ones], axis=0)
        for j in range(n):
            cm = functools.reduce(jnp.maximum, [jnp.max(s, axis=0, keepdims=True) for s in sc[j]])
            m_new = cm if ci == 0 else jnp.maximum(m[j], cm)
            pv = functools.reduce(jnp.add, [
                jnp.dot(v[:, h * MXU_TILE:(h + 1) * MXU_TILE],
                        jnp.exp2(sc[j][h] - m_new).astype(BF16), preferred_element_type=F32)
                for h in range(halves)])
            acc[j] = pv if ci == 0 else acc[j] * jnp.exp2(m[j] - m_new) + pv
            m[j] = m_new
    return [(acc[j][0:dv], acc[j][dv:dv + 1]) for j in range(n)]


def _attn_a_kernel(lam_ref, q_ref, k_ref, v_ref, gate_ref, sg_ref, o_ref, *, lam_init):
    q = q_ref[0]
    z = jnp.zeros((A_QK_DIM, TQ), BF16)
    qps = [jnp.concatenate([q[0:64], z], axis=0), jnp.concatenate([z, q[64:128]], axis=0)]
    (a0, l0), (a1, l1) = _softmax_pv(
        qps, lambda r0: k_ref[0, 0, pl.ds(r0, MXU_TILE), :], lambda ci: v_ref[0, ci], A_V_DIM)
    lp = lam_ref[...]
    lam = (jnp.exp(jnp.sum(lp[0:1] * lp[1:2], axis=1, keepdims=True))
           - jnp.exp(jnp.sum(lp[2:3] * lp[3:4], axis=1, keepdims=True)) + lam_init)
    o = a0 / l0 - lam * (a1 / l1)
    ms = jnp.mean(o * o, axis=0, keepdims=True)
    o = o * lax.rsqrt(ms + RMS_EPS) * _rep(sg_ref, TQ) * (1.0 - lam_init)
    o_ref[0] = (o * gate_ref[0]).astype(BF16)


def _attn_a(lam_p, qa, ka, va, gate, sg, lam_init):
    b = qa.shape[0]
    nt = SEQ // TM
    return pl.pallas_call(
        functools.partial(_attn_a_kernel, lam_init=lam_init),
        grid=(b, A_HEADS, SEQ // TQ),
        in_specs=[
            pl.BlockSpec((4, A_QK_DIM), lambda i, h, t: (0, 0)),
            pl.BlockSpec((1, 128, TQ), lambda i, h, t: (i, h, t)),
            pl.BlockSpec((1, 1, SEQ, K_PAD), lambda i, h, t: (i, h, 0, 0)),
            pl.BlockSpec((1, nt, A_V_DIM, TM), lambda i, h, t: (i, 0, h, 0)),
            pl.BlockSpec((1, A_V_DIM, TQ), lambda i, h, t: (i, h, t)),
            pl.BlockSpec((A_V_DIM, LANES), lambda i, h, t: (0, 0)),
        ],
        out_specs=pl.BlockSpec((1, A_V_DIM, TQ), lambda i, h, t: (i, h, t)),
        out_shape=jax.ShapeDtypeStruct((b, A_WIDTH, SEQ), BF16),
        compiler_params=pltpu.CompilerParams(
            dimension_semantics=("parallel", "parallel", "parallel"),
            vmem_limit_bytes=VMEM_LIMIT),
        name="attn_diff",
    )(lam_p, qa, ka, va, gate, sg)


def _attn_b_kernel(q_ref, k_ref, v_ref, gate_ref, o_ref):
    first = pl.program_id(1) < B_GROUP
    q = q_ref[0]
    z = jnp.zeros_like(q)
    qp = jnp.concatenate([jnp.where(first, q, z), jnp.where(first, z, q)], axis=0)
    ((a, l),) = _softmax_pv(
        [qp], lambda r0: k_ref[0, pl.ds(r0, MXU_TILE), :], lambda ci: v_ref[0, ci], B_DIM)
    o_ref[0] = (a / l * gate_ref[0]).astype(BF16)


def _attn_b(qb, kb, vb, gate):
    b = qb.shape[0]
    nt = SEQ // TM
    return pl.pallas_call(
        _attn_b_kernel,
        grid=(b, B_Q_HEADS, SEQ // TQ),
        in_specs=[
            pl.BlockSpec((1, B_DIM, TQ), lambda i, h, t: (i, h, t)),
            pl.BlockSpec((1, SEQ, K_PAD), lambda i, h, t: (i, 0, 0)),
            pl.BlockSpec((1, nt, B_DIM, TM), lambda i, h, t: (i, 0, h // B_GROUP, 0)),
            pl.BlockSpec((1, B_DIM, TQ), lambda i, h, t: (i, A_WIDTH // B_DIM + h, t)),
        ],
        out_specs=pl.BlockSpec((1, B_DIM, TQ), lambda i, h, t: (i, h, t)),
        out_shape=jax.ShapeDtypeStruct((b, B_WIDTH, SEQ), BF16),
        compiler_params=pltpu.CompilerParams(
            dimension_semantics=("parallel", "parallel", "parallel"),
            vmem_limit_bytes=VMEM_LIMIT),
        name="attn_gqa",
    )(qb, kb, vb, gate)


def _attn_c_kernel(q_ref, k_ref, v_ref, gate_ref, o_ref):
    qp = jnp.concatenate([q_ref[0], jnp.zeros((K_PAD - C_QK, TQ), BF16)], axis=0)
    ((a, l),) = _softmax_pv(
        [qp], lambda r0: k_ref[0, 0, pl.ds(r0, MXU_TILE), :], lambda ci: v_ref[0, ci], C_V)
    o_ref[0] = (a / l * gate_ref[0]).astype(BF16)


def _attn_c(q, k, v, gate):
    b = q.shape[0]
    nt = SEQ // TM
    return pl.pallas_call(
        _attn_c_kernel,
        grid=(b, C_HEADS, SEQ // TQ),
        in_specs=[
            pl.BlockSpec((1, C_QK, TQ), lambda i, h, t: (i, h, t)),
            pl.BlockSpec((1, 1, SEQ, K_PAD), lambda i, h, t: (i, h, 0, 0)),
            pl.BlockSpec((1, nt, C_V, TM), lambda i, h, t: (i, 0, h, 0)),
            pl.BlockSpec((1, C_V, TQ), lambda i, h, t: (i, h, t)),
        ],
        out_specs=pl.BlockSpec((1, C_V, TQ), lambda i, h, t: (i, h, t)),
        out_shape=jax.ShapeDtypeStruct((b, C_HEADS * C_V, SEQ), BF16),
        compiler_params=pltpu.CompilerParams(
            dimension_semantics=("parallel", "parallel", "parallel"),
            vmem_limit_bytes=VMEM_LIMIT),
        name="attn_mla",
    )(q, k, v, gate)


def _out_kernel(*refs, n_og):
    og_refs = refs[:n_og]
    x_ref, w_ref, lg_ref, lb_ref, o_ref = refs[n_og:]
    og = jnp.concatenate([r[0] for r in og_refs], axis=0) if n_og > 1 else og_refs[0][0]
    y = jnp.dot(w_ref[...], og, preferred_element_type=F32)
    z = ALPHA * x_ref[0] + y
    mu = jnp.mean(z, axis=0, keepdims=True)
    d = z - mu
    var = jnp.mean(d * d, axis=0, keepdims=True)
    o_ref[0] = d * lax.rsqrt(var + LN_EPS) * _rep(lg_ref, TM) + _rep(lb_ref, TM)


def _out_proj(ogs, xt, w_t, lg, lb):
    b, d, s = xt.shape
    tok = lambda i, j: (i, 0, j)
    full2 = lambda i, j: (0, 0)
    return pl.pallas_call(
        functools.partial(_out_kernel, n_og=len(ogs)),
        grid=(b, s // TM),
        in_specs=[pl.BlockSpec((1, og.shape[1], TM), tok) for og in ogs] + [
            pl.BlockSpec((1, d, TM), tok),
            pl.BlockSpec((d, d), full2),
            pl.BlockSpec((d, LANES), full2),
            pl.BlockSpec((d, LANES), full2),
        ],
        out_specs=pl.BlockSpec((1, d, TM), tok),
        out_shape=jax.ShapeDtypeStruct((b, d, s), F32),
        compiler_params=pltpu.CompilerParams(
            dimension_semantics=("parallel", "parallel"), vmem_limit_bytes=VMEM_LIMIT),
        name="out_proj_ln",
    )(*ogs, xt, w_t, lg, lb)


def _col(v):
    v = v.astype(F32)
    return jnp.broadcast_to(v[:, None], (v.shape[0], LANES))


def _angles_t(pos, dims, theta):
    inv = theta ** (-jnp.arange(0, dims, 2, dtype=F32) / dims)
    ang = pos.astype(F32)[:, None] * inv[None, :]
    return jnp.cos(ang).T, jnp.sin(ang).T


@jax.jit
def _forward(x, ev_w_in, ev_w_out, ev_lam, ev_subln, ev_qnorm, ev_knorm, ev_ln_g, ev_ln_b,
             od_w_in, od_qnorm, od_kvnorm, od_w_qb, od_w_kvb, od_w_out, od_ln_g, od_ln_b):
    s = x.shape[1]
    pos = jnp.arange(s, dtype=jnp.int32)
    row = jnp.repeat(jnp.arange(s // GRID_W, dtype=jnp.int32), GRID_W)
    col = jnp.tile(jnp.arange(GRID_W, dtype=jnp.int32), s // GRID_W)
    tab_ev = jnp.concatenate(
        _angles_t(pos, A_ROT, ROPE_THETA) + _angles_t(row, B_DIM // 2, AXIAL_THETA)
        + _angles_t(col, B_DIM // 2, AXIAL_THETA), axis=0)
    tab_od = jnp.concatenate(_angles_t(pos, C_ROPE, ROPE_THETA), axis=0)

    xt = _to_feature_major(x)
    for layer in range(DEPTH):
        i = layer // 2
        if layer % 2 == 0:
            qa, ka, va, qb, kb, vb, gate = _even_in(
                xt, ev_w_in[i].T.astype(BF16), tab_ev, _col(ev_qnorm[i]), _col(ev_knorm[i]))
            lam_init = 0.8 - 0.6 * math.exp(-0.3 * layer)
            oga = _attn_a(ev_lam[i].astype(F32), qa, ka, va, gate, _col(ev_subln[i]), lam_init)
            ogb = _attn_b(qb, kb, vb, gate)
            xt = _out_proj([oga, ogb], xt, ev_w_out[i].T.astype(BF16),
                           _col(ev_ln_g[i]), _col(ev_ln_b[i]))
        else:
            q, k, v, gate = _odd_in(
                xt, od_w_in[i].T.astype(BF16), od_w_qb[i].T.astype(BF16),
                od_w_kvb[i].T.astype(BF16), tab_od, _col(od_qnorm[i]), _col(od_kvnorm[i]))
            ogc = _attn_c(q, k, v, gate)
            xt = _out_proj([ogc], xt, od_w_out[i].T.astype(BF16),
                           _col(od_ln_g[i]), _col(od_ln_b[i]))
    return _to_token_major(xt)


def kernel(x, ev_w_in, ev_w_out, ev_lam, ev_subln, ev_qnorm, ev_knorm, ev_ln_g, ev_ln_b,
           od_w_in, od_qnorm, od_kvnorm, od_w_qb, od_w_kvb, od_w_out, od_ln_g, od_ln_b):
    return _forward(x, ev_w_in, ev_w_out, ev_lam, ev_subln, ev_qnorm, ev_knorm, ev_ln_g,
                    ev_ln_b, od_w_in, od_qnorm, od_kvnorm, od_w_qb, od_w_kvb, od_w_out,
                    od_ln_g, od_ln_b)
```

```python
import functools
import math

import jax
import jax.numpy as jnp
from jax import lax
from jax.experimental import pallas as pl
from jax.experimental.pallas import tpu as pltpu

F32 = jnp.float32
BF16 = jnp.bfloat16

D_MODEL = 1024
SEQ = 4096
DEPTH = 4
GRID_W = 64
ROPE_THETA = 500000.0
AXIAL_THETA = 10000.0
LN_EPS = 1e-5
RMS_EPS = 1e-6

A_HEADS = 4
A_QK_DIM = 64
A_V_DIM = 128
A_WIDTH = 512
A_ROT = 16
B_Q_HEADS = 8
B_KV_HEADS = 2
B_GROUP = 4
B_DIM = 64
B_WIDTH = 512
EV_IN = 3328

C_HEADS = 16
C_NOPE = 64
C_ROPE = 32
C_V = 64
C_Q_LORA = 256
C_KV_LORA = 128
C_QK = C_NOPE + C_ROPE
OD_IN = 1440

ALPHA = (2 * DEPTH) ** 0.25
LOG2E = 1.4426950408889634

QSCALE_AB = A_QK_DIM ** -0.5 * LOG2E
QSCALE_C = C_QK ** -0.5 * LOG2E

LANES = 128
MXU_TILE = 256
LOOKAHEAD_ITEMS = {1: 2, 2: 3, 4: 4, 8: 6}
ONES_ROWS = 16
K_PAD = 128
TM = 512
TQ_A = 512
TQ_B = 1024
TQ_C = 1024
VMEM_LIMIT = 48 * 1024 * 1024


def _rep(ref, n):
    a = ref[...]
    return jnp.concatenate([a] * (n // LANES), axis=1)


def _rot(x1, x2, cos, sin):
    return x1 * cos - x2 * sin, x2 * cos + x1 * sin


def _silu(x):
    return x * jax.nn.sigmoid(x)


def _transpose_kernel(x_ref, o_ref):
    o_ref[0] = x_ref[0].T


def _to_feature_major(x):
    b, s, d = x.shape
    return pl.pallas_call(
        _transpose_kernel,
        grid=(b, s // TM),
        in_specs=[pl.BlockSpec((1, TM, d), lambda i, j: (i, j, 0))],
        out_specs=pl.BlockSpec((1, d, TM), lambda i, j: (i, 0, j)),
        out_shape=jax.ShapeDtypeStruct((b, d, s), x.dtype),
        compiler_params=pltpu.CompilerParams(
            dimension_semantics=("parallel", "parallel"), vmem_limit_bytes=VMEM_LIMIT),
        name="to_feature_major",
    )(x)


def _to_token_major(xt):
    b, d, s = xt.shape
    return pl.pallas_call(
        _transpose_kernel,
        grid=(b, s // TM),
        in_specs=[pl.BlockSpec((1, d, TM), lambda i, j: (i, 0, j))],
        out_specs=pl.BlockSpec((1, TM, d), lambda i, j: (i, j, 0)),
        out_shape=jax.ShapeDtypeStruct((b, s, d), xt.dtype),
        compiler_params=pltpu.CompilerParams(
            dimension_semantics=("parallel", "parallel"), vmem_limit_bytes=VMEM_LIMIT),
        name="to_token_major",
    )(xt)


def _even_in_kernel(x_ref, w_ref, tab_ref, qn_ref, kn_ref,
                    qa_ref, ka_ref, va_ref, qb_ref, kb_ref, vb_ref, g_ref):
    xb = x_ref[0].astype(BF16)

    def proj(r0, r1):
        return jnp.dot(w_ref[r0:r1, :], xb, preferred_element_type=F32)

    cos_a, sin_a = tab_ref[0:8, :], tab_ref[8:16, :]
    cos_r, sin_r = tab_ref[16:32, :], tab_ref[32:48, :]
    cos_c, sin_c = tab_ref[48:64, :], tab_ref[64:80, :]

    def rope_a(h):
        outs = []
        for hc in range(2 * A_HEADS):
            b = hc * A_QK_DIM
            r1, r2 = _rot(h[b:b + 8], h[b + 8:b + 16], cos_a, sin_a)
            outs.append(jnp.concatenate([r1, r2, h[b + 16:b + 64]], axis=0))
        return outs

    qa_ref[0] = (jnp.concatenate(rope_a(proj(0, 512)), axis=0) * QSCALE_AB).astype(BF16)
    ks = rope_a(proj(512, 1024))
    for h in range(A_HEADS):
        kt = jnp.concatenate([ks[2 * h], ks[2 * h + 1]], axis=0)
        ka_ref[0, h] = kt.T.astype(BF16)
    va_ref[0, 0] = proj(1024, 1536).astype(BF16)

    def norm_axial(h, g):
        ms = jnp.mean(h * h, axis=0, keepdims=True)
        y = h * lax.rsqrt(ms + RMS_EPS) * g
        a1, a2 = _rot(y[0:16], y[16:32], cos_r, sin_r)
        b1, b2 = _rot(y[32:48], y[48:64], cos_c, sin_c)
        return jnp.concatenate([a1, a2, b1, b2], axis=0)

    qn = _rep(qn_ref, TM)
    kn = _rep(kn_ref, TM)
    hq = proj(1536, 2048)
    qb_ref[0] = (jnp.concatenate(
        [norm_axial(hq[h * 64:(h + 1) * 64], qn) for h in range(B_Q_HEADS)], axis=0)
                 * QSCALE_AB).astype(BF16)
    hkv = proj(2048, 2304)
    kt = jnp.concatenate(
        [norm_axial(hkv[h * 64:(h + 1) * 64], kn) for h in range(B_KV_HEADS)], axis=0)
    kb_ref[0] = kt.T.astype(BF16)
    vb_ref[0, 0] = hkv[128:256].astype(BF16)
    g_ref[0] = _silu(proj(2304, 3328))


def _even_in(xt, w_t, tab, qn, kn):
    b, d, s = xt.shape
    nt = s // TM
    tok = lambda i, j: (i, 0, j)
    chunk = lambda i, j: (i, j, 0, 0)
    full2 = lambda i, j: (0, 0)
    return pl.pallas_call(
        _even_in_kernel,
        grid=(b, nt),
        in_specs=[
            pl.BlockSpec((1, d, TM), tok),
            pl.BlockSpec((EV_IN, d), full2),
            pl.BlockSpec((80, TM), lambda i, j: (0, j)),
            pl.BlockSpec((B_DIM, LANES), full2),
            pl.BlockSpec((B_DIM, LANES), full2),
        ],
        out_specs=[
            pl.BlockSpec((1, 512, TM), tok),
            pl.BlockSpec((1, A_HEADS, TM, K_PAD), lambda i, j: (i, 0, j, 0)),
            pl.BlockSpec((1, 1, 512, TM), chunk),
            pl.BlockSpec((1, 512, TM), tok),
            pl.BlockSpec((1, TM, K_PAD), lambda i, j: (i, j, 0)),
            pl.BlockSpec((1, 1, 128, TM), chunk),
            pl.BlockSpec((1, 1024, TM), tok),
        ],
        out_shape=[
            jax.ShapeDtypeStruct((b, 512, s), BF16),
            jax.ShapeDtypeStruct((b, A_HEADS, s, K_PAD), BF16),
            jax.ShapeDtypeStruct((b, nt, 512, TM), BF16),
            jax.ShapeDtypeStruct((b, 512, s), BF16),
            jax.ShapeDtypeStruct((b, s, K_PAD), BF16),
            jax.ShapeDtypeStruct((b, nt, 128, TM), BF16),
            jax.ShapeDtypeStruct((b, 1024, s), F32),
        ],
        compiler_params=pltpu.CompilerParams(
            dimension_semantics=("parallel", "parallel"), vmem_limit_bytes=VMEM_LIMIT),
        name="even_in_proj",
    )(xt, w_t, tab, qn, kn)


def _odd_in_kernel(x_ref, w_ref, wq_ref, wkv_ref, tab_ref, qn_ref, kvn_ref,
                   q_ref, k_ref, v_ref, g_ref):
    xb = x_ref[0].astype(BF16)
    cos, sin = tab_ref[0:16, :], tab_ref[16:32, :]

    def rms(h, g):
        ms = jnp.mean(h * h, axis=0, keepdims=True)
        return h * lax.rsqrt(ms + RMS_EPS) * g

    lat = jnp.dot(w_ref[0:416, :], xb, preferred_element_type=F32)
    cqn = rms(lat[0:256], _rep(qn_ref, TM)).astype(BF16)
    q = jnp.dot(wq_ref[...], cqn, preferred_element_type=F32)
    outs = []
    for h in range(C_HEADS):
        b = h * C_QK
        r1, r2 = _rot(q[b + 64:b + 80], q[b + 80:b + 96], cos, sin)
        outs += [q[b:b + 64], r1, r2]
    q_ref[0] = (jnp.concatenate(outs, axis=0) * QSCALE_C).astype(BF16)

    ckvn = rms(lat[256:384], _rep(kvn_ref, TM)).astype(BF16)
    kv = jnp.dot(wkv_ref[...], ckvn, preferred_element_type=F32)
    r1, r2 = _rot(lat[384:400], lat[400:416], cos, sin)
    zpad = jnp.zeros((K_PAD - C_QK, TM), F32)
    for h in range(C_HEADS):
        kt = jnp.concatenate([kv[h * 128:h * 128 + 64], r1, r2, zpad], axis=0)
        k_ref[0, h] = kt.T.astype(BF16)
    v_ref[0, 0] = jnp.concatenate(
        [kv[h * 128 + 64:h * 128 + 128] for h in range(C_HEADS)], axis=0).astype(BF16)
    g_ref[0] = _silu(jnp.dot(w_ref[416:1440, :], xb, preferred_element_type=F32))


def _odd_in(xt, w_t, wq_t, wkv_t, tab, qn, kvn):
    b, d, s = xt.shape
    nt = s // TM
    tok = lambda i, j: (i, 0, j)
    full2 = lambda i, j: (0, 0)
    return pl.pallas_call(
        _odd_in_kernel,
        grid=(b, nt),
        in_specs=[
            pl.BlockSpec((1, d, TM), tok),
            pl.BlockSpec((OD_IN, d), full2),
            pl.BlockSpec((C_HEADS * C_QK, C_Q_LORA), full2),
            pl.BlockSpec((C_HEADS * 128, C_KV_LORA), full2),
            pl.BlockSpec((32, TM), lambda i, j: (0, j)),
            pl.BlockSpec((C_Q_LORA, LANES), full2),
            pl.BlockSpec((C_KV_LORA, LANES), full2),
        ],
        out_specs=[
            pl.BlockSpec((1, C_HEADS * C_QK, TM), tok),
            pl.BlockSpec((1, C_HEADS, TM, K_PAD), lambda i, j: (i, 0, j, 0)),
            pl.BlockSpec((1, 1, 1024, TM), lambda i, j: (i, j, 0, 0)),
            pl.BlockSpec((1, 1024, TM), tok),
        ],
        out_shape=[
            jax.ShapeDtypeStruct((b, C_HEADS * C_QK, s), BF16),
            jax.ShapeDtypeStruct((b, C_HEADS, s, K_PAD), BF16),
            jax.ShapeDtypeStruct((b, nt, 1024, TM), BF16),
            jax.ShapeDtypeStruct((b, 1024, s), F32),
        ],
        compiler_params=pltpu.CompilerParams(
            dimension_semantics=("parallel", "parallel"), vmem_limit_bytes=VMEM_LIMIT),
        name="odd_in_proj",
    )(xt, w_t, wq_t, wkv_t, tab, qn, kvn)


def _softmax_pv(qps, k_at, v_at, dv):
    nsets = len(qps)
    ntiles = qps[0].shape[1] // MXU_TILE
    qps = [qp[:, t * MXU_TILE:(t + 1) * MXU_TILE] for qp in qps for t in range(ntiles)]
    n = len(qps)
    nchunks = SEQ // TM
    halves = TM // MXU_TILE
    ones = jnp.ones((ONES_ROWS, TM), BF16)

    def scores(ci, j):
        return [jnp.dot(k_at(ci * TM + h * MXU_TILE), qps[j], preferred_element_type=F32)
                for h in range(halves)]

    m = [None] * n
    acc = [None] * n
    items = [(ci, j) for ci in range(nchunks) for j in range(n)]
    lookahead = LOOKAHEAD_ITEMS[n]
    pending = [scores(*it) for it in items[:lookahead]]
    for idx, (ci, j) in enumerate(items):
        if idx + lookahead < len(items):
            pending.append(scores(*items[idx + lookahead]))
        sc = pending.pop(0)
        v = jnp.concatenate([v_at(ci), ones], axis=0)
        cm = functools.reduce(jnp.maximum, [jnp.max(s, axis=0, keepdims=True) for s in sc])
        m_new = cm if ci == 0 else jnp.maximum(m[j], cm)
        pv = functools.reduce(jnp.add, [
            jnp.dot(v[:, h * MXU_TILE:(h + 1) * MXU_TILE],
                    jnp.exp2(sc[h] - m_new).astype(BF16), preferred_element_type=F32)
            for h in range(halves)])
        acc[j] = pv if ci == 0 else acc[j] * jnp.exp2(m[j] - m_new) + pv
        m[j] = m_new
    out = []
    for j in range(nsets):
        a = jnp.concatenate(acc[j * ntiles:(j + 1) * ntiles], axis=1)
        out.append((a[0:dv], a[dv:dv + 1]))
    return out


def _attn_a_kernel(lam_ref, q_ref, k_ref, v_ref, gate_ref, sg_ref, o_ref, *, lam_init):
    q = q_ref[0]
    tq = q.shape[1]
    z = jnp.zeros((A_QK_DIM, tq), BF16)
    qps = [jnp.concatenate([q[0:64], z], axis=0), jnp.concatenate([z, q[64:128]], axis=0)]
    (a0, l0), (a1, l1) = _softmax_pv(
        qps, lambda r0: k_ref[0, 0, pl.ds(r0, MXU_TILE), :], lambda ci: v_ref[0, ci], A_V_DIM)
    lp = lam_ref[...]
    lam = (jnp.exp(jnp.sum(lp[0:1] * lp[1:2], axis=1, keepdims=True))
           - jnp.exp(jnp.sum(lp[2:3] * lp[3:4], axis=1, keepdims=True)) + lam_init)
    o = a0 / l0 - lam * (a1 / l1)
    ms = jnp.mean(o * o, axis=0, keepdims=True)
    o = o * lax.rsqrt(ms + RMS_EPS) * _rep(sg_ref, tq) * (1.0 - lam_init)
    o_ref[0] = (o * gate_ref[0]).astype(BF16)


def _attn_a(lam_p, qa, ka, va, gate, sg, lam_init):
    b = qa.shape[0]
    nt = SEQ // TM
    return pl.pallas_call(
        functools.partial(_attn_a_kernel, lam_init=lam_init),
        grid=(b, A_HEADS, SEQ // TQ_A),
        in_specs=[
            pl.BlockSpec((4, A_QK_DIM), lambda i, h, t: (0, 0)),
            pl.BlockSpec((1, 128, TQ_A), lambda i, h, t: (i, h, t)),
            pl.BlockSpec((1, 1, SEQ, K_PAD), lambda i, h, t: (i, h, 0, 0)),
            pl.BlockSpec((1, nt, A_V_DIM, TM), lambda i, h, t: (i, 0, h, 0)),
            pl.BlockSpec((1, A_V_DIM, TQ_A), lambda i, h, t: (i, h, t)),
            pl.BlockSpec((A_V_DIM, LANES), lambda i, h, t: (0, 0)),
        ],
        out_specs=pl.BlockSpec((1, A_V_DIM, TQ_A), lambda i, h, t: (i, h, t)),
        out_shape=jax.ShapeDtypeStruct((b, A_WIDTH, SEQ), BF16),
        compiler_params=pltpu.CompilerParams(
            dimension_semantics=("parallel", "parallel", "parallel"),
            vmem_limit_bytes=VMEM_LIMIT),
        name="attn_diff",
    )(lam_p, qa, ka, va, gate, sg)


def _attn_b_kernel(q_ref, k_ref, v_ref, gate_ref, o_ref):
    first = pl.program_id(1) < B_GROUP
    q = q_ref[0]
    z = jnp.zeros_like(q)
    qp = jnp.concatenate([jnp.where(first, q, z), jnp.where(first, z, q)], axis=0)
    ((a, l),) = _softmax_pv(
        [qp], lambda r0: k_ref[0, pl.ds(r0, MXU_TILE), :], lambda ci: v_ref[0, ci], B_DIM)
    o_ref[0] = (a / l * gate_ref[0]).astype(BF16)


def _attn_b(qb, kb, vb, gate):
    b = qb.shape[0]
    nt = SEQ // TM
    return pl.pallas_call(
        _attn_b_kernel,
        grid=(b, B_Q_HEADS, SEQ // TQ_B),
        in_specs=[
            pl.BlockSpec((1, B_DIM, TQ_B), lambda i, h, t: (i, h, t)),
            pl.BlockSpec((1, SEQ, K_PAD), lambda i, h, t: (i, 0, 0)),
            pl.BlockSpec((1, nt, B_DIM, TM), lambda i, h, t: (i, 0, h // B_GROUP, 0)),
            pl.BlockSpec((1, B_DIM, TQ_B), lambda i, h, t: (i, A_WIDTH // B_DIM + h, t)),
        ],
        out_specs=pl.BlockSpec((1, B_DIM, TQ_B), lambda i, h, t: (i, h, t)),
        out_shape=jax.ShapeDtypeStruct((b, B_WIDTH, SEQ), BF16),
        compiler_params=pltpu.CompilerParams(
            dimension_semantics=("parallel", "parallel", "parallel"),
            vmem_limit_bytes=VMEM_LIMIT),
        name="attn_gqa",
    )(qb, kb, vb, gate)


def _attn_c_kernel(q_ref, k_ref, v_ref, gate_ref, o_ref):
    q = q_ref[0]
    qp = jnp.concatenate([q, jnp.zeros((K_PAD - C_QK, q.shape[1]), BF16)], axis=0)
    ((a, l),) = _softmax_pv(
        [qp], lambda r0: k_ref[0, 0, pl.ds(r0, MXU_TILE), :], lambda ci: v_ref[0, ci], C_V)
    o_ref[0] = (a / l * gate_ref[0]).astype(BF16)


def _attn_c(q, k, v, gate):
    b = q.shape[0]
    nt = SEQ // TM
    return pl.pallas_call(
        _attn_c_kernel,
        grid=(b, C_HEADS, SEQ // TQ_C),
        in_specs=[
            pl.BlockSpec((1, C_QK, TQ_C), lambda i, h, t: (i, h, t)),
            pl.BlockSpec((1, 1, SEQ, K_PAD), lambda i, h, t: (i, h, 0, 0)),
            pl.BlockSpec((1, nt, C_V, TM), lambda i, h, t: (i, 0, h, 0)),
            pl.BlockSpec((1, C_V, TQ_C), lambda i, h, t: (i, h, t)),
        ],
        out_specs=pl.BlockSpec((1, C_V, TQ_C), lambda i, h, t: (i, h, t)),
        out_shape=jax.ShapeDtypeStruct((b, C_HEADS * C_V, SEQ), BF16),
        compiler_params=pltpu.CompilerParams(
            dimension_semantics=("parallel", "parallel", "parallel"),
            vmem_limit_bytes=VMEM_LIMIT),
        name="attn_mla",
    )(q, k, v, gate)


def _out_kernel(*refs, n_og):
    og_refs = refs[:n_og]
    x_ref, w_ref, lg_ref, lb_ref, o_ref = refs[n_og:]
    og = jnp.concatenate([r[0] for r in og_refs], axis=0) if n_og > 1 else og_refs[0][0]
    y = jnp.dot(w_ref[...], og, preferred_element_type=F32)
    z = ALPHA * x_ref[0] + y
    mu = jnp.mean(z, axis=0, keepdims=True)
    d = z - mu
    var = jnp.mean(d * d, axis=0, keepdims=True)
    o_ref[0] = d * lax.rsqrt(var + LN_EPS) * _rep(lg_ref, TM) + _rep(lb_ref, TM)


def _out_proj(ogs, xt, w_t, lg, lb):
    b, d, s = xt.shape
    tok = lambda i, j: (i, 0, j)
    full2 = lambda i, j: (0, 0)
    return pl.pallas_call(
        functools.partial(_out_kernel, n_og=len(ogs)),
        grid=(b, s // TM),
        in_specs=[pl.BlockSpec((1, og.shape[1], TM), tok) for og in ogs] + [
            pl.BlockSpec((1, d, TM), tok),
            pl.BlockSpec((d, d), full2),
            pl.BlockSpec((d, LANES), full2),
            pl.BlockSpec((d, LANES), full2),
        ],
        out_specs=pl.BlockSpec((1, d, TM), tok),
        out_shape=jax.ShapeDtypeStruct((b, d, s), F32),
        compiler_params=pltpu.CompilerParams(
            dimension_semantics=("parallel", "parallel"), vmem_limit_bytes=VMEM_LIMIT),
        name="out_proj_ln",
    )(*ogs, xt, w_t, lg, lb)


def _col(v):
    v = v.astype(F32)
    return jnp.broadcast_to(v[:, None], (v.shape[0], LANES))


def _angles_t(pos, dims, theta):
    inv = theta ** (-jnp.arange(0, dims, 2, dtype=F32) / dims)
    ang = pos.astype(F32)[:, None] * inv[None, :]
    return jnp.cos(ang).T, jnp.sin(ang).T


@jax.jit
def _forward(x, ev_w_in, ev_w_out, ev_lam, ev_subln, ev_qnorm, ev_knorm, ev_ln_g, ev_ln_b,
             od_w_in, od_qnorm, od_kvnorm, od_w_qb, od_w_kvb, od_w_out, od_ln_g, od_ln_b):
    s = x.shape[1]
    pos = jnp.arange(s, dtype=jnp.int32)
    row = jnp.repeat(jnp.arange(s // GRID_W, dtype=jnp.int32), GRID_W)
    col = jnp.tile(jnp.arange(GRID_W, dtype=jnp.int32), s // GRID_W)
    tab_ev = jnp.concatenate(
        _angles_t(pos, A_ROT, ROPE_THETA) + _angles_t(row, B_DIM // 2, AXIAL_THETA)
        + _angles_t(col, B_DIM // 2, AXIAL_THETA), axis=0)
    tab_od = jnp.concatenate(_angles_t(pos, C_ROPE, ROPE_THETA), axis=0)

    xt = _to_feature_major(x)
    for layer in range(DEPTH):
        i = layer // 2
        if layer % 2 == 0:
            qa, ka, va, qb, kb, vb, gate = _even_in(
                xt, ev_w_in[i].T.astype(BF16), tab_ev, _col(ev_qnorm[i]), _col(ev_knorm[i]))
            lam_init = 0.8 - 0.6 * math.exp(-0.3 * layer)
            oga = _attn_a(ev_lam[i].astype(F32), qa, ka, va, gate, _col(ev_subln[i]), lam_init)
            ogb = _attn_b(qb, kb, vb, gate)
            xt = _out_proj([oga, ogb], xt, ev_w_out[i].T.astype(BF16),
                           _col(ev_ln_g[i]), _col(ev_ln_b[i]))
        else:
            q, k, v, gate = _odd_in(
                xt, od_w_in[i].T.astype(BF16), od_w_qb[i].T.astype(BF16),
                od_w_kvb[i].T.astype(BF16), tab_od, _col(od_qnorm[i]), _col(od_kvnorm[i]))
            ogc = _attn_c(q, k, v, gate)
            xt = _out_proj([ogc], xt, od_w_out[i].T.astype(BF16),
                           _col(od_ln_g[i]), _col(od_ln_b[i]))
    return _to_token_major(xt)


def kernel(x, ev_w_in, ev_w_out, ev_lam, ev_subln, ev_qnorm, ev_knorm, ev_ln_g, ev_ln_b,
           od_w_in, od_qnorm, od_kvnorm, od_w_qb, od_w_kvb, od_w_out, od_ln_g, od_ln_b):
    return _forward(x, ev_w_in, ev_w_out, ev_lam, ev_subln, ev_qnorm, ev_knorm, ev_ln_g,
                    ev_ln_b, od_w_in, od_qnorm, od_kvnorm, od_w_qb, od_w_kvb, od_w_out,
                    od_ln_g, od_ln_b)
```

```python
import functools
import math

import jax
import jax.numpy as jnp
from jax import lax
from jax.experimental import pallas as pl
from jax.experimental.pallas import tpu as pltpu

F32 = jnp.float32
BF16 = jnp.bfloat16

D_MODEL = 1024
SEQ = 4096
DEPTH = 4
GRID_W = 64
ROPE_THETA = 500000.0
AXIAL_THETA = 10000.0
LN_EPS = 1e-5
RMS_EPS = 1e-6

A_HEADS = 4
A_QK_DIM = 64
A_V_DIM = 128
A_WIDTH = 512
A_ROT = 16
B_Q_HEADS = 8
B_KV_HEADS = 2
B_GROUP = 4
B_DIM = 64
B_WIDTH = 512
EV_IN = 3328

C_HEADS = 16
C_NOPE = 64
C_ROPE = 32
C_V = 64
C_Q_LORA = 256
C_KV_LORA = 128
C_QK = C_NOPE + C_ROPE
OD_IN = 1440

ALPHA = (2 * DEPTH) ** 0.25
LOG2E = 1.4426950408889634

QSCALE_AB = A_QK_DIM ** -0.5 * LOG2E
QSCALE_C = C_QK ** -0.5 * LOG2E

LANES = 128
MXU_TILE = 256
LOOKAHEAD_ITEMS = {1: 2, 2: 3, 4: 8, 8: 6}
ONES_ROWS = 16
K_PAD = 128
NORM_ROWS = 16
SAFE_BOUND = 60.0
BOUND_SLACK = 1.02
TM = 512
KV_CHUNK = 512
TQ_A = 512
TQ_B = 1024
TQ_C = 1024
VMEM_LIMIT = 48 * 1024 * 1024


def _rep(ref, n):
    a = ref[...]
    return jnp.concatenate([a] * (n // LANES), axis=1)


def _rot(x1, x2, cos, sin):
    return x1 * cos - x2 * sin, x2 * cos + x1 * sin


def _silu(x):
    return x * jax.nn.sigmoid(x)


def _sumsq(x):
    return jnp.sum(x * x, axis=0, keepdims=True)


def _transpose_kernel(x_ref, o_ref):
    o_ref[0] = x_ref[0].T


def _to_feature_major(x):
    b, s, d = x.shape
    return pl.pallas_call(
        _transpose_kernel,
        grid=(b, s // TM),
        in_specs=[pl.BlockSpec((1, TM, d), lambda i, j: (i, j, 0))],
        out_specs=pl.BlockSpec((1, d, TM), lambda i, j: (i, 0, j)),
        out_shape=jax.ShapeDtypeStruct((b, d, s), x.dtype),
        compiler_params=pltpu.CompilerParams(
            dimension_semantics=("parallel", "parallel"), vmem_limit_bytes=VMEM_LIMIT),
        name="to_feature_major",
    )(x)


def _to_token_major(xt):
    b, d, s = xt.shape
    return pl.pallas_call(
        _transpose_kernel,
        grid=(b, s // TM),
        in_specs=[pl.BlockSpec((1, d, TM), lambda i, j: (i, 0, j))],
        out_specs=pl.BlockSpec((1, TM, d), lambda i, j: (i, j, 0)),
        out_shape=jax.ShapeDtypeStruct((b, s, d), xt.dtype),
        compiler_params=pltpu.CompilerParams(
            dimension_semantics=("parallel", "parallel"), vmem_limit_bytes=VMEM_LIMIT),
        name="to_token_major",
    )(xt)


def _even_in_kernel(x_ref, w_ref, tab_ref, qn_ref, kn_ref,
                    qa_ref, ka_ref, va_ref, qb_ref, kb_ref, vb_ref, g_ref, nq_ref, nk_ref):
    xb = x_ref[0].astype(BF16)

    def proj(r0, r1):
        return jnp.dot(w_ref[r0:r1, :], xb, preferred_element_type=F32)

    cos_a, sin_a = tab_ref[0:8, :], tab_ref[8:16, :]
    cos_r, sin_r = tab_ref[16:32, :], tab_ref[32:48, :]
    cos_c, sin_c = tab_ref[48:64, :], tab_ref[64:80, :]

    def rope_a(h):
        outs = []
        for hc in range(2 * A_HEADS):
            b = hc * A_QK_DIM
            r1, r2 = _rot(h[b:b + 8], h[b + 8:b + 16], cos_a, sin_a)
            outs.append(jnp.concatenate([r1, r2, h[b + 16:b + 64]], axis=0))
        return outs

    qs = [x * QSCALE_AB for x in rope_a(proj(0, 512))]
    qa_ref[0] = jnp.concatenate(qs, axis=0).astype(BF16)
    ks = rope_a(proj(512, 1024))
    nq = [_sumsq(x) for x in qs]
    nk = [_sumsq(x) for x in ks]
    for h in range(A_HEADS):
        kt = jnp.concatenate([ks[2 * h], ks[2 * h + 1]], axis=0)
        ka_ref[0, h] = kt.T.astype(BF16)
    va_ref[0, 0] = proj(1024, 1536).astype(BF16)

    def norm_axial(h, g):
        ms = jnp.mean(h * h, axis=0, keepdims=True)
        y = h * lax.rsqrt(ms + RMS_EPS) * g
        a1, a2 = _rot(y[0:16], y[16:32], cos_r, sin_r)
        b1, b2 = _rot(y[32:48], y[48:64], cos_c, sin_c)
        return jnp.concatenate([a1, a2, b1, b2], axis=0)

    qn = _rep(qn_ref, TM)
    kn = _rep(kn_ref, TM)
    hq = proj(1536, 2048)
    qs = [norm_axial(hq[h * 64:(h + 1) * 64], qn) * QSCALE_AB for h in range(B_Q_HEADS)]
    qb_ref[0] = jnp.concatenate(qs, axis=0).astype(BF16)
    hkv = proj(2048, 2304)
    ks = [norm_axial(hkv[h * 64:(h + 1) * 64], kn) for h in range(B_KV_HEADS)]
    kb_ref[0] = jnp.concatenate(ks, axis=0).T.astype(BF16)
    nq_ref[0] = jnp.concatenate(nq + [_sumsq(x) for x in qs], axis=0)
    nk = nk + [_sumsq(x) for x in ks]
    nk_ref[0] = jnp.concatenate(nk + [jnp.zeros_like(nk[0])] * (NORM_ROWS - len(nk)), axis=0)
    vb_ref[0, 0] = hkv[128:256].astype(BF16)
    g_ref[0] = _silu(proj(2304, 3328))


def _even_in(xt, w_t, tab, qn, kn):
    b, d, s = xt.shape
    nt = s // TM
    tok = lambda i, j: (i, 0, j)
    chunk = lambda i, j: (i, j, 0, 0)
    full2 = lambda i, j: (0, 0)
    return pl.pallas_call(
        _even_in_kernel,
        grid=(b, nt),
        in_specs=[
            pl.BlockSpec((1, d, TM), tok),
            pl.BlockSpec((EV_IN, d), full2),
            pl.BlockSpec((80, TM), lambda i, j: (0, j)),
            pl.BlockSpec((B_DIM, LANES), full2),
            pl.BlockSpec((B_DIM, LANES), full2),
        ],
        out_specs=[
            pl.BlockSpec((1, 512, TM), tok),
            pl.BlockSpec((1, A_HEADS, TM, K_PAD), lambda i, j: (i, 0, j, 0)),
            pl.BlockSpec((1, 1, 512, TM), chunk),
            pl.BlockSpec((1, 512, TM), tok),
            pl.BlockSpec((1, TM, K_PAD), lambda i, j: (i, j, 0)),
            pl.BlockSpec((1, 1, 128, TM), chunk),
            pl.BlockSpec((1, 1024, TM), tok),
            pl.BlockSpec((1, NORM_ROWS, TM), tok),
            pl.BlockSpec((1, NORM_ROWS, TM), tok),
        ],
        out_shape=[
            jax.ShapeDtypeStruct((b, 512, s), BF16),
            jax.ShapeDtypeStruct((b, A_HEADS, s, K_PAD), BF16),
            jax.ShapeDtypeStruct((b, nt, 512, TM), BF16),
            jax.ShapeDtypeStruct((b, 512, s), BF16),
            jax.ShapeDtypeStruct((b, s, K_PAD), BF16),
            jax.ShapeDtypeStruct((b, nt, 128, TM), BF16),
            jax.ShapeDtypeStruct((b, 1024, s), F32),
            jax.ShapeDtypeStruct((b, NORM_ROWS, s), F32),
            jax.ShapeDtypeStruct((b, NORM_ROWS, s), F32),
        ],
        compiler_params=pltpu.CompilerParams(
            dimension_semantics=("parallel", "parallel"), vmem_limit_bytes=VMEM_LIMIT),
        name="even_in_proj",
    )(xt, w_t, tab, qn, kn)


def _odd_in_kernel(x_ref, w_ref, wq_ref, wkv_ref, tab_ref, qn_ref, kvn_ref,
                   q_ref, k_ref, v_ref, g_ref, nq_ref, nk_ref):
    xb = x_ref[0].astype(BF16)
    cos, sin = tab_ref[0:16, :], tab_ref[16:32, :]

    def rms(h, g):
        ms = jnp.mean(h * h, axis=0, keepdims=True)
        return h * lax.rsqrt(ms + RMS_EPS) * g

    lat = jnp.dot(w_ref[0:416, :], xb, preferred_element_type=F32)
    cqn = rms(lat[0:256], _rep(qn_ref, TM)).astype(BF16)
    q = jnp.dot(wq_ref[...], cqn, preferred_element_type=F32)
    qs = []
    for h in range(C_HEADS):
        b = h * C_QK
        r1, r2 = _rot(q[b + 64:b + 80], q[b + 80:b + 96], cos, sin)
        qs.append(jnp.concatenate([q[b:b + 64], r1, r2], axis=0) * QSCALE_C)
    q_ref[0] = jnp.concatenate(qs, axis=0).astype(BF16)
    nq_ref[0] = jnp.concatenate([_sumsq(x) for x in qs], axis=0)

    ckvn = rms(lat[256:384], _rep(kvn_ref, TM)).astype(BF16)
    kv = jnp.dot(wkv_ref[...], ckvn, preferred_element_type=F32)
    r1, r2 = _rot(lat[384:400], lat[400:416], cos, sin)
    zpad = jnp.zeros((K_PAD - C_QK, TM), F32)
    nkr = _sumsq(r1) + _sumsq(r2)
    nk = []
    for h in range(C_HEADS):
        kn = kv[h * 128:h * 128 + 64]
        nk.append(_sumsq(kn) + nkr)
        kt = jnp.concatenate([kn, r1, r2, zpad], axis=0)
        k_ref[0, h] = kt.T.astype(BF16)
    nk_ref[0] = jnp.concatenate(nk, axis=0)
    v_ref[0, 0] = jnp.concatenate(
        [kv[h * 128 + 64:h * 128 + 128] for h in range(C_HEADS)], axis=0).astype(BF16)
    g_ref[0] = _silu(jnp.dot(w_ref[416:1440, :], xb, preferred_element_type=F32))


def _odd_in(xt, w_t, wq_t, wkv_t, tab, qn, kvn):
    b, d, s = xt.shape
    nt = s // TM
    tok = lambda i, j: (i, 0, j)
    full2 = lambda i, j: (0, 0)
    return pl.pallas_call(
        _odd_in_kernel,
        grid=(b, nt),
        in_specs=[
            pl.BlockSpec((1, d, TM), tok),
            pl.BlockSpec((OD_IN, d), full2),
            pl.BlockSpec((C_HEADS * C_QK, C_Q_LORA), full2),
            pl.BlockSpec((C_HEADS * 128, C_KV_LORA), full2),
            pl.BlockSpec((32, TM), lambda i, j: (0, j)),
            pl.BlockSpec((C_Q_LORA, LANES), full2),
            pl.BlockSpec((C_KV_LORA, LANES), full2),
        ],
        out_specs=[
            pl.BlockSpec((1, C_HEADS * C_QK, TM), tok),
            pl.BlockSpec((1, C_HEADS, TM, K_PAD), lambda i, j: (i, 0, j, 0)),
            pl.BlockSpec((1, 1, 1024, TM), lambda i, j: (i, j, 0, 0)),
            pl.BlockSpec((1, 1024, TM), tok),
            pl.BlockSpec((1, NORM_ROWS, TM), tok),
            pl.BlockSpec((1, NORM_ROWS, TM), tok),
        ],
        out_shape=[
            jax.ShapeDtypeStruct((b, C_HEADS * C_QK, s), BF16),
            jax.ShapeDtypeStruct((b, C_HEADS, s, K_PAD), BF16),
            jax.ShapeDtypeStruct((b, nt, 1024, TM), BF16),
            jax.ShapeDtypeStruct((b, 1024, s), F32),
            jax.ShapeDtypeStruct((b, NORM_ROWS, s), F32),
            jax.ShapeDtypeStruct((b, NORM_ROWS, s), F32),
        ],
        compiler_params=pltpu.CompilerParams(
            dimension_semantics=("parallel", "parallel"), vmem_limit_bytes=VMEM_LIMIT),
        name="odd_in_proj",
    )(xt, w_t, wq_t, wkv_t, tab, qn, kvn)


def _v_chunk(v_ref, r0):
    return v_ref[0, r0 // TM, :, pl.ds(r0 % TM, KV_CHUNK)]


def _softmax_pv(qps, k_at, v_at, dv, bounds=None):
    nsets = len(qps)
    ntiles = qps[0].shape[1] // MXU_TILE
    qps = [qp[:, t * MXU_TILE:(t + 1) * MXU_TILE] for qp in qps for t in range(ntiles)]
    n = len(qps)
    nchunks = SEQ // KV_CHUNK
    halves = KV_CHUNK // MXU_TILE
    ones = jnp.ones((ONES_ROWS, KV_CHUNK), BF16)

    def scores(ci, j):
        return [jnp.dot(k_at(ci * KV_CHUNK + h * MXU_TILE), qps[j], preferred_element_type=F32)
                for h in range(halves)]

    m = [None] * n
    acc = [None] * n
    items = [(ci, j) for ci in range(nchunks) for j in range(n)]
    lookahead = LOOKAHEAD_ITEMS[n]
    pending = [scores(*it) for it in items[:lookahead]]
    for idx, (ci, j) in enumerate(items):
        if idx + lookahead < len(items):
            pending.append(scores(*items[idx + lookahead]))
        sc = pending.pop(0)
        v = jnp.concatenate([v_at(ci * KV_CHUNK), ones], axis=0)
        if bounds is None:
            cm = functools.reduce(jnp.maximum, [jnp.max(s, axis=0, keepdims=True) for s in sc])
            ref = cm if ci == 0 else jnp.maximum(m[j], cm)
        else:
            ref = bounds[j // ntiles]
        pv = functools.reduce(jnp.add, [
            jnp.dot(v[:, h * MXU_TILE:(h + 1) * MXU_TILE],
                    jnp.exp2(sc[h] - ref).astype(BF16), preferred_element_type=F32)
            for h in range(halves)])
        if ci == 0:
            acc[j] = pv
        elif bounds is None:
            acc[j] = acc[j] * jnp.exp2(m[j] - ref) + pv
        else:
            acc[j] = acc[j] + pv
        m[j] = ref
    out = []
    for j in range(nsets):
        a = jnp.concatenate(acc[j * ntiles:(j + 1) * ntiles], axis=1)
        out.append((a[0:dv], a[dv:dv + 1]))
    return out


def _attend(qps, bounds, k_at, v_at, dv, finish):
    safe = functools.reduce(jnp.logical_and, [b <= SAFE_BOUND for b in bounds])

    @pl.when(safe)
    def _():
        finish(_softmax_pv(qps, k_at, v_at, dv, bounds))

    @pl.when(jnp.logical_not(safe))
    def _():
        finish(_softmax_pv(qps, k_at, v_at, dv))


def _attn_params():
    return pltpu.CompilerParams(
        dimension_semantics=("parallel", "parallel", "parallel"), vmem_limit_bytes=VMEM_LIMIT)


def _attn_a_kernel(bnd_ref, lam_ref, q_ref, k_ref, v_ref, gate_ref, sg_ref, o_ref, *, lam_init):
    i, h = pl.program_id(0), pl.program_id(1)
    q = q_ref[0]
    tq = q.shape[1]
    z = jnp.zeros((A_QK_DIM, tq), BF16)
    qps = [jnp.concatenate([q[0:64], z], axis=0), jnp.concatenate([z, q[64:128]], axis=0)]

    def finish(res):
        (a0, l0), (a1, l1) = res
        lp = lam_ref[...]
        lam = (jnp.exp(jnp.sum(lp[0:1] * lp[1:2], axis=1, keepdims=True))
               - jnp.exp(jnp.sum(lp[2:3] * lp[3:4], axis=1, keepdims=True)) + lam_init)
        o = a0 / l0 - lam * (a1 / l1)
        ms = jnp.mean(o * o, axis=0, keepdims=True)
        o = o * lax.rsqrt(ms + RMS_EPS) * _rep(sg_ref, tq) * (1.0 - lam_init)
        o_ref[0] = (o * gate_ref[0]).astype(BF16)

    _attend(qps, [bnd_ref[i, 2 * h], bnd_ref[i, 2 * h + 1]],
            lambda r0: k_ref[0, 0, pl.ds(r0, MXU_TILE), :], functools.partial(_v_chunk, v_ref),
            A_V_DIM, finish)


def _attn_a(bnd, lam_p, qa, ka, va, gate, sg, lam_init):
    b = qa.shape[0]
    nt = SEQ // TM
    return pl.pallas_call(
        functools.partial(_attn_a_kernel, lam_init=lam_init),
        grid=(b, A_HEADS, SEQ // TQ_A),
        in_specs=[
            pl.BlockSpec(memory_space=pltpu.SMEM),
            pl.BlockSpec((4, A_QK_DIM), lambda i, h, t: (0, 0)),
            pl.BlockSpec((1, 128, TQ_A), lambda i, h, t: (i, h, t)),
            pl.BlockSpec((1, 1, SEQ, K_PAD), lambda i, h, t: (i, h, 0, 0)),
            pl.BlockSpec((1, nt, A_V_DIM, TM), lambda i, h, t: (i, 0, h, 0)),
            pl.BlockSpec((1, A_V_DIM, TQ_A), lambda i, h, t: (i, h, t)),
            pl.BlockSpec((A_V_DIM, LANES), lambda i, h, t: (0, 0)),
        ],
        out_specs=pl.BlockSpec((1, A_V_DIM, TQ_A), lambda i, h, t: (i, h, t)),
        out_shape=jax.ShapeDtypeStruct((b, A_WIDTH, SEQ), BF16),
        compiler_params=_attn_params(),
        name="attn_diff",
    )(bnd, lam_p, qa, ka, va, gate, sg)


def _attn_b_kernel(bnd_ref, q_ref, k_ref, v_ref, gate_ref, o_ref):
    i, h = pl.program_id(0), pl.program_id(1)
    first = h < B_GROUP
    q = q_ref[0]
    z = jnp.zeros_like(q)
    qp = jnp.concatenate([jnp.where(first, q, z), jnp.where(first, z, q)], axis=0)

    def finish(res):
        ((a, l),) = res
        o_ref[0] = (a / l * gate_ref[0]).astype(BF16)

    _attend([qp], [bnd_ref[i, 2 * A_HEADS + h]],
            lambda r0: k_ref[0, pl.ds(r0, MXU_TILE), :], functools.partial(_v_chunk, v_ref),
            B_DIM, finish)


def _attn_b(bnd, qb, kb, vb, gate):
    b = qb.shape[0]
    nt = SEQ // TM
    return pl.pallas_call(
        _attn_b_kernel,
        grid=(b, B_Q_HEADS, SEQ // TQ_B),
        in_specs=[
            pl.BlockSpec(memory_space=pltpu.SMEM),
            pl.BlockSpec((1, B_DIM, TQ_B), lambda i, h, t: (i, h, t)),
            pl.BlockSpec((1, SEQ, K_PAD), lambda i, h, t: (i, 0, 0)),
            pl.BlockSpec((1, nt, B_DIM, TM), lambda i, h, t: (i, 0, h // B_GROUP, 0)),
            pl.BlockSpec((1, B_DIM, TQ_B), lambda i, h, t: (i, A_WIDTH // B_DIM + h, t)),
        ],
        out_specs=pl.BlockSpec((1, B_DIM, TQ_B), lambda i, h, t: (i, h, t)),
        out_shape=jax.ShapeDtypeStruct((b, B_WIDTH, SEQ), BF16),
        compiler_params=_attn_params(),
        name="attn_gqa",
    )(bnd, qb, kb, vb, gate)


def _attn_c_kernel(bnd_ref, q_ref, k_ref, v_ref, gate_ref, o_ref):
    i, h = pl.program_id(0), pl.program_id(1)
    q = q_ref[0]
    qp = jnp.concatenate([q, jnp.zeros((K_PAD - C_QK, q.shape[1]), BF16)], axis=0)

    def finish(res):
        ((a, l),) = res
        o_ref[0] = (a / l * gate_ref[0]).astype(BF16)

    _attend([qp], [bnd_ref[i, h]],
            lambda r0: k_ref[0, 0, pl.ds(r0, MXU_TILE), :], functools.partial(_v_chunk, v_ref),
            C_V, finish)


def _attn_c(bnd, q, k, v, gate):
    b = q.shape[0]
    nt = SEQ // TM
    return pl.pallas_call(
        _attn_c_kernel,
        grid=(b, C_HEADS, SEQ // TQ_C),
        in_specs=[
            pl.BlockSpec(memory_space=pltpu.SMEM),
            pl.BlockSpec((1, C_QK, TQ_C), lambda i, h, t: (i, h, t)),
            pl.BlockSpec((1, 1, SEQ, K_PAD), lambda i, h, t: (i, h, 0, 0)),
            pl.BlockSpec((1, nt, C_V, TM), lambda i, h, t: (i, 0, h, 0)),
            pl.BlockSpec((1, C_V, TQ_C), lambda i, h, t: (i, h, t)),
        ],
        out_specs=pl.BlockSpec((1, C_V, TQ_C), lambda i, h, t: (i, h, t)),
        out_shape=jax.ShapeDtypeStruct((b, C_HEADS * C_V, SEQ), BF16),
        compiler_params=_attn_params(),
        name="attn_mla",
    )(bnd, q, k, v, gate)


def _out_kernel(*refs, n_og):
    og_refs = refs[:n_og]
    x_ref, w_ref, lg_ref, lb_ref, o_ref = refs[n_og:]
    og = jnp.concatenate([r[0] for r in og_refs], axis=0) if n_og > 1 else og_refs[0][0]
    y = jnp.dot(w_ref[...], og, preferred_element_type=F32)
    z = ALPHA * x_ref[0] + y
    mu = jnp.mean(z, axis=0, keepdims=True)
    d = z - mu
    var = jnp.mean(d * d, axis=0, keepdims=True)
    o_ref[0] = d * lax.rsqrt(var + LN_EPS) * _rep(lg_ref, TM) + _rep(lb_ref, TM)


def _out_proj(ogs, xt, w_t, lg, lb):
    b, d, s = xt.shape
    tok = lambda i, j: (i, 0, j)
    full2 = lambda i, j: (0, 0)
    return pl.pallas_call(
        functools.partial(_out_kernel, n_og=len(ogs)),
        grid=(b, s // TM),
        in_specs=[pl.BlockSpec((1, og.shape[1], TM), tok) for og in ogs] + [
            pl.BlockSpec((1, d, TM), tok),
            pl.BlockSpec((d, d), full2),
            pl.BlockSpec((d, LANES), full2),
            pl.BlockSpec((d, LANES), full2),
        ],
        out_specs=pl.BlockSpec((1, d, TM), tok),
        out_shape=jax.ShapeDtypeStruct((b, d, s), F32),
        compiler_params=pltpu.CompilerParams(
            dimension_semantics=("parallel", "parallel"), vmem_limit_bytes=VMEM_LIMIT),
        name="out_proj_ln",
    )(*ogs, xt, w_t, lg, lb)


def _col(v):
    v = v.astype(F32)
    return jnp.broadcast_to(v[:, None], (v.shape[0], LANES))


def _score_bounds(nq, kmax2):
    return jnp.sqrt(jnp.max(nq, axis=-1) * kmax2) * BOUND_SLACK


def _angles_t(pos, dims, theta):
    inv = theta ** (-jnp.arange(0, dims, 2, dtype=F32) / dims)
    ang = pos.astype(F32)[:, None] * inv[None, :]
    return jnp.cos(ang).T, jnp.sin(ang).T


@jax.jit
def _forward(x, ev_w_in, ev_w_out, ev_lam, ev_subln, ev_qnorm, ev_knorm, ev_ln_g, ev_ln_b,
             od_w_in, od_qnorm, od_kvnorm, od_w_qb, od_w_kvb, od_w_out, od_ln_g, od_ln_b):
    s = x.shape[1]
    pos = jnp.arange(s, dtype=jnp.int32)
    row = jnp.repeat(jnp.arange(s // GRID_W, dtype=jnp.int32), GRID_W)
    col = jnp.tile(jnp.arange(GRID_W, dtype=jnp.int32), s // GRID_W)
    tab_ev = jnp.concatenate(
        _angles_t(pos, A_ROT, ROPE_THETA) + _angles_t(row, B_DIM // 2, AXIAL_THETA)
        + _angles_t(col, B_DIM // 2, AXIAL_THETA), axis=0)
    tab_od = jnp.concatenate(_angles_t(pos, C_ROPE, ROPE_THETA), axis=0)

    xt = _to_feature_major(x)
    for layer in range(DEPTH):
        i = layer // 2
        if layer % 2 == 0:
            qa, ka, va, qb, kb, vb, gate, nq, nk = _even_in(
                xt, ev_w_in[i].T.astype(BF16), tab_ev, _col(ev_qnorm[i]), _col(ev_knorm[i]))
            kmax = jnp.max(nk, axis=-1)
            kmax = jnp.concatenate(
                [kmax[:, :2 * A_HEADS],
                 jnp.repeat(kmax[:, 2 * A_HEADS:2 * A_HEADS + B_KV_HEADS], B_GROUP, axis=1)],
                axis=1)
            bnd = _score_bounds(nq, kmax)
            lam_init = 0.8 - 0.6 * math.exp(-0.3 * layer)
            oga = _attn_a(bnd, ev_lam[i].astype(F32), qa, ka, va, gate, _col(ev_subln[i]),
                          lam_init)
            ogb = _attn_b(bnd, qb, kb, vb, gate)
            xt = _out_proj([oga, ogb], xt, ev_w_out[i].T.astype(BF16),
                           _col(ev_ln_g[i]), _col(ev_ln_b[i]))
        else:
            q, k, v, gate, nq, nk = _odd_in(
                xt, od_w_in[i].T.astype(BF16), od_w_qb[i].T.astype(BF16),
                od_w_kvb[i].T.astype(BF16), tab_od, _col(od_qnorm[i]), _col(od_kvnorm[i]))
            ogc = _attn_c(_score_bounds(nq, jnp.max(nk, axis=-1)), q, k, v, gate)
            xt = _out_proj([ogc], xt, od_w_out[i].T.astype(BF16),
                           _col(od_ln_g[i]), _col(od_ln_b[i]))
    return _to_token_major(xt)


def kernel(x, ev_w_in, ev_w_out, ev_lam, ev_subln, ev_qnorm, ev_knorm, ev_ln_g, ev_ln_b,
           od_w_in, od_qnorm, od_kvnorm, od_w_qb, od_w_kvb, od_w_out, od_ln_g, od_ln_b):
    return _forward(x, ev_w_in, ev_w_out, ev_lam, ev_subln, ev_qnorm, ev_knorm, ev_ln_g,
                    ev_ln_b, od_w_in, od_qnorm, od_kvnorm, od_w_qb, od_w_kvb, od_w_out,
                    od_ln_g, od_ln_b)
```

```python
import functools
import math

import jax
import jax.numpy as jnp
from jax import lax
from jax.experimental import pallas as pl
from jax.experimental.pallas import tpu as pltpu

F32 = jnp.float32
BF16 = jnp.bfloat16

D_MODEL = 1024
SEQ = 4096
DEPTH = 4
GRID_W = 64
ROPE_THETA = 500000.0
AXIAL_THETA = 10000.0
LN_EPS = 1e-5
RMS_EPS = 1e-6

A_HEADS = 4
A_QK_DIM = 64
A_V_DIM = 128
A_WIDTH = 512
A_ROT = 16
B_Q_HEADS = 8
B_KV_HEADS = 2
B_GROUP = 4
B_DIM = 64
B_WIDTH = 512
EV_IN = 3328

C_HEADS = 16
C_NOPE = 64
C_ROPE = 32
C_V = 64
C_Q_LORA = 256
C_KV_LORA = 128
C_QK = C_NOPE + C_ROPE
OD_IN = 1440

ALPHA = (2 * DEPTH) ** 0.25
LOG2E = 1.4426950408889634

QSCALE_AB = A_QK_DIM ** -0.5 * LOG2E
QSCALE_C = C_QK ** -0.5 * LOG2E

LANES = 128
MXU_TILE = 256
LOOKAHEAD_ITEMS = {1: 2, 2: 3, 4: 8}
MAX_LOCKSTEP = 4
ONES_ROWS = 16
K_PAD = 128
NORM_ROWS = 16
SAFE_BOUND = 60.0
BOUND_SLACK = 1.02
TM = 512
KV_CHUNK = 512
TQ_A = 1024
TQ_B = 2048
TQ_C = 2048
VMEM_LIMIT = 48 * 1024 * 1024


def _rep(ref, n):
    a = ref[...]
    return jnp.concatenate([a] * (n // LANES), axis=1)


def _rot(x1, x2, cos, sin):
    return x1 * cos - x2 * sin, x2 * cos + x1 * sin


def _silu(x):
    return x * jax.nn.sigmoid(x)


def _sumsq(x):
    return jnp.sum(x * x, axis=0, keepdims=True)


def _even_in_kernel(x_ref, w_ref, tab_ref, qn_ref, kn_ref,
                    qa_ref, ka_ref, va_ref, qb_ref, kb_ref, vb_ref, g_ref, nq_ref, nk_ref,
                    *, x_token_major):
    xb = x_ref[0].astype(BF16)
    contract = (((1,), (1 if x_token_major else 0,)), ((), ()))

    def proj(r0, r1):
        return lax.dot_general(w_ref[r0:r1, :], xb, contract, preferred_element_type=F32)

    cos_a, sin_a = tab_ref[0:8, :], tab_ref[8:16, :]
    cos_r, sin_r = tab_ref[16:32, :], tab_ref[32:48, :]
    cos_c, sin_c = tab_ref[48:64, :], tab_ref[64:80, :]

    def rope_a(h):
        outs = []
        for hc in range(2 * A_HEADS):
            b = hc * A_QK_DIM
            r1, r2 = _rot(h[b:b + 8], h[b + 8:b + 16], cos_a, sin_a)
            outs.append(jnp.concatenate([r1, r2, h[b + 16:b + 64]], axis=0))
        return outs

    qs = [x * QSCALE_AB for x in rope_a(proj(0, 512))]
    qa_ref[0] = jnp.concatenate(qs, axis=0).astype(BF16)
    ks = rope_a(proj(512, 1024))
    nq = [_sumsq(x) for x in qs]
    nk = [_sumsq(x) for x in ks]
    for h in range(A_HEADS):
        kt = jnp.concatenate([ks[2 * h], ks[2 * h + 1]], axis=0)
        ka_ref[0, h] = kt.T.astype(BF16)
    va_ref[0, 0] = proj(1024, 1536).astype(BF16)

    def norm_axial(h, g):
        ms = jnp.mean(h * h, axis=0, keepdims=True)
        y = h * lax.rsqrt(ms + RMS_EPS) * g
        a1, a2 = _rot(y[0:16], y[16:32], cos_r, sin_r)
        b1, b2 = _rot(y[32:48], y[48:64], cos_c, sin_c)
        return jnp.concatenate([a1, a2, b1, b2], axis=0)

    qn = _rep(qn_ref, TM)
    kn = _rep(kn_ref, TM)
    hq = proj(1536, 2048)
    qs = [norm_axial(hq[h * 64:(h + 1) * 64], qn) * QSCALE_AB for h in range(B_Q_HEADS)]
    qb_ref[0] = jnp.concatenate(qs, axis=0).astype(BF16)
    hkv = proj(2048, 2304)
    ks = [norm_axial(hkv[h * 64:(h + 1) * 64], kn) for h in range(B_KV_HEADS)]
    kb_ref[0] = jnp.concatenate(ks, axis=0).T.astype(BF16)
    nq_ref[0] = jnp.concatenate(nq + [_sumsq(x) for x in qs], axis=0)
    nk = nk + [_sumsq(x) for x in ks]
    nk_ref[0] = jnp.concatenate(nk + [jnp.zeros_like(nk[0])] * (NORM_ROWS - len(nk)), axis=0)
    vb_ref[0, 0] = hkv[128:256].astype(BF16)
    g_ref[0] = _silu(proj(2304, 3328)).astype(BF16)


def _even_in(xt, w_t, tab, qn, kn, x_token_major=False):
    b = xt.shape[0]
    d, s = D_MODEL, SEQ
    nt = s // TM
    tok = lambda i, j: (i, 0, j)
    chunk = lambda i, j: (i, j, 0, 0)
    full2 = lambda i, j: (0, 0)
    x_spec = (pl.BlockSpec((1, TM, d), lambda i, j: (i, j, 0)) if x_token_major
              else pl.BlockSpec((1, d, TM), tok))
    return pl.pallas_call(
        functools.partial(_even_in_kernel, x_token_major=x_token_major),
        grid=(b, nt),
        in_specs=[
            x_spec,
            pl.BlockSpec((EV_IN, d), full2),
            pl.BlockSpec((80, TM), lambda i, j: (0, j)),
            pl.BlockSpec((B_DIM, LANES), full2),
            pl.BlockSpec((B_DIM, LANES), full2),
        ],
        out_specs=[
            pl.BlockSpec((1, 512, TM), tok),
            pl.BlockSpec((1, A_HEADS, TM, K_PAD), lambda i, j: (i, 0, j, 0)),
            pl.BlockSpec((1, 1, 512, TM), chunk),
            pl.BlockSpec((1, 512, TM), tok),
            pl.BlockSpec((1, TM, K_PAD), lambda i, j: (i, j, 0)),
            pl.BlockSpec((1, 1, 128, TM), chunk),
            pl.BlockSpec((1, 1024, TM), tok),
            pl.BlockSpec((1, NORM_ROWS, TM), tok),
            pl.BlockSpec((1, NORM_ROWS, TM), tok),
        ],
        out_shape=[
            jax.ShapeDtypeStruct((b, 512, s), BF16),
            jax.ShapeDtypeStruct((b, A_HEADS, s, K_PAD), BF16),
            jax.ShapeDtypeStruct((b, nt, 512, TM), BF16),
            jax.ShapeDtypeStruct((b, 512, s), BF16),
            jax.ShapeDtypeStruct((b, s, K_PAD), BF16),
            jax.ShapeDtypeStruct((b, nt, 128, TM), BF16),
            jax.ShapeDtypeStruct((b, 1024, s), BF16),
            jax.ShapeDtypeStruct((b, NORM_ROWS, s), F32),
            jax.ShapeDtypeStruct((b, NORM_ROWS, s), F32),
        ],
        compiler_params=pltpu.CompilerParams(
            dimension_semantics=("parallel", "parallel"), vmem_limit_bytes=VMEM_LIMIT),
        name="even_in_proj",
    )(xt, w_t, tab, qn, kn)


def _odd_in_kernel(x_ref, w_ref, wq_ref, wkv_ref, tab_ref, qn_ref, kvn_ref,
                   q_ref, k_ref, v_ref, g_ref, nq_ref, nk_ref):
    xb = x_ref[0].astype(BF16)
    cos, sin = tab_ref[0:16, :], tab_ref[16:32, :]

    def rms(h, g):
        ms = jnp.mean(h * h, axis=0, keepdims=True)
        return h * lax.rsqrt(ms + RMS_EPS) * g

    lat = jnp.dot(w_ref[0:416, :], xb, preferred_element_type=F32)
    cqn = rms(lat[0:256], _rep(qn_ref, TM)).astype(BF16)
    q = jnp.dot(wq_ref[...], cqn, preferred_element_type=F32)
    qs = []
    for h in range(C_HEADS):
        b = h * C_QK
        r1, r2 = _rot(q[b + 64:b + 80], q[b + 80:b + 96], cos, sin)
        qs.append(jnp.concatenate([q[b:b + 64], r1, r2], axis=0) * QSCALE_C)
    q_ref[0] = jnp.concatenate(qs, axis=0).astype(BF16)
    nq_ref[0] = jnp.concatenate([_sumsq(x) for x in qs], axis=0)

    ckvn = rms(lat[256:384], _rep(kvn_ref, TM)).astype(BF16)
    kv = jnp.dot(wkv_ref[...], ckvn, preferred_element_type=F32)
    r1, r2 = _rot(lat[384:400], lat[400:416], cos, sin)
    zpad = jnp.zeros((K_PAD - C_QK, TM), F32)
    nkr = _sumsq(r1) + _sumsq(r2)
    nk = []
    for h in range(C_HEADS):
        kn = kv[h * 128:h * 128 + 64]
        nk.append(_sumsq(kn) + nkr)
        kt = jnp.concatenate([kn, r1, r2, zpad], axis=0)
        k_ref[0, h] = kt.T.astype(BF16)
    nk_ref[0] = jnp.concatenate(nk, axis=0)
    v_ref[0, 0] = jnp.concatenate(
        [kv[h * 128 + 64:h * 128 + 128] for h in range(C_HEADS)], axis=0).astype(BF16)
    g_ref[0] = _silu(jnp.dot(w_ref[416:1440, :], xb, preferred_element_type=F32)).astype(BF16)


def _odd_in(xt, w_t, wq_t, wkv_t, tab, qn, kvn):
    b, d, s = xt.shape
    nt = s // TM
    tok = lambda i, j: (i, 0, j)
    full2 = lambda i, j: (0, 0)
    return pl.pallas_call(
        _odd_in_kernel,
        grid=(b, nt),
        in_specs=[
            pl.BlockSpec((1, d, TM), tok),
            pl.BlockSpec((OD_IN, d), full2),
            pl.BlockSpec((C_HEADS * C_QK, C_Q_LORA), full2),
            pl.BlockSpec((C_HEADS * 128, C_KV_LORA), full2),
            pl.BlockSpec((32, TM), lambda i, j: (0, j)),
            pl.BlockSpec((C_Q_LORA, LANES), full2),
            pl.BlockSpec((C_KV_LORA, LANES), full2),
        ],
        out_specs=[
            pl.BlockSpec((1, C_HEADS * C_QK, TM), tok),
            pl.BlockSpec((1, C_HEADS, TM, K_PAD), lambda i, j: (i, 0, j, 0)),
            pl.BlockSpec((1, 1, 1024, TM), lambda i, j: (i, j, 0, 0)),
            pl.BlockSpec((1, 1024, TM), tok),
            pl.BlockSpec((1, NORM_ROWS, TM), tok),
            pl.BlockSpec((1, NORM_ROWS, TM), tok),
        ],
        out_shape=[
            jax.ShapeDtypeStruct((b, C_HEADS * C_QK, s), BF16),
            jax.ShapeDtypeStruct((b, C_HEADS, s, K_PAD), BF16),
            jax.ShapeDtypeStruct((b, nt, 1024, TM), BF16),
            jax.ShapeDtypeStruct((b, 1024, s), BF16),
            jax.ShapeDtypeStruct((b, NORM_ROWS, s), F32),
            jax.ShapeDtypeStruct((b, NORM_ROWS, s), F32),
        ],
        compiler_params=pltpu.CompilerParams(
            dimension_semantics=("parallel", "parallel"), vmem_limit_bytes=VMEM_LIMIT),
        name="odd_in_proj",
    )(xt, w_t, wq_t, wkv_t, tab, qn, kvn)


def _v_chunk(v_ref, r0):
    return v_ref[0, r0 // TM, :, pl.ds(r0 % TM, KV_CHUNK)]


def _softmax_pv(qps, k_at, v_at, dv, bounds=None):
    nsets = len(qps)
    ntiles = qps[0].shape[1] // MXU_TILE
    qps = [qp[:, t * MXU_TILE:(t + 1) * MXU_TILE] for qp in qps for t in range(ntiles)]
    n = len(qps)
    nchunks = SEQ // KV_CHUNK
    halves = KV_CHUNK // MXU_TILE
    ones = jnp.ones((ONES_ROWS, KV_CHUNK), BF16)

    def scores(ci, j):
        return [jnp.dot(k_at(ci * KV_CHUNK + h * MXU_TILE), qps[j], preferred_element_type=F32)
                for h in range(halves)]

    m = [None] * n
    acc = [None] * n
    items = [(ci, j) for g in range(0, n, MAX_LOCKSTEP) for ci in range(nchunks)
             for j in range(g, min(g + MAX_LOCKSTEP, n))]
    lookahead = LOOKAHEAD_ITEMS[min(n, MAX_LOCKSTEP)]
    pending = [scores(*it) for it in items[:lookahead]]
    for idx, (ci, j) in enumerate(items):
        if idx + lookahead < len(items):
            pending.append(scores(*items[idx + lookahead]))
        sc = pending.pop(0)
        v = jnp.concatenate([v_at(ci * KV_CHUNK), ones], axis=0)
        if bounds is None:
            cm = functools.reduce(jnp.maximum, [jnp.max(s, axis=0, keepdims=True) for s in sc])
            ref = cm if ci == 0 else jnp.maximum(m[j], cm)
        else:
            ref = bounds[j // ntiles]
        pv = functools.reduce(jnp.add, [
            jnp.dot(v[:, h * MXU_TILE:(h + 1) * MXU_TILE],
                    jnp.exp2(sc[h] - ref).astype(BF16), preferred_element_type=F32)
            for h in range(halves)])
        if ci == 0:
            acc[j] = pv
        elif bounds is None:
            acc[j] = acc[j] * jnp.exp2(m[j] - ref) + pv
        else:
            acc[j] = acc[j] + pv
        m[j] = ref
    out = []
    for j in range(nsets):
        a = jnp.concatenate(acc[j * ntiles:(j + 1) * ntiles], axis=1)
        out.append((a[0:dv], a[dv:dv + 1]))
    return out


def _attend(qps, bounds, k_at, v_at, dv, finish):
    safe = functools.reduce(jnp.logical_and, [b <= SAFE_BOUND for b in bounds])

    @pl.when(safe)
    def _():
        finish(_softmax_pv(qps, k_at, v_at, dv, bounds))

    @pl.when(jnp.logical_not(safe))
    def _():
        finish(_softmax_pv(qps, k_at, v_at, dv))


def _attn_params():
    return pltpu.CompilerParams(
        dimension_semantics=("parallel", "parallel", "parallel"), vmem_limit_bytes=VMEM_LIMIT)


def _attn_a_kernel(bnd_ref, lam_ref, q_ref, k_ref, v_ref, gate_ref, sg_ref, o_ref, *, lam_init):
    i, h = pl.program_id(0), pl.program_id(1)
    q = q_ref[0]
    tq = q.shape[1]
    z = jnp.zeros((A_QK_DIM, tq), BF16)
    qps = [jnp.concatenate([q[0:64], z], axis=0), jnp.concatenate([z, q[64:128]], axis=0)]

    def finish(res):
        (a0, l0), (a1, l1) = res
        lp = lam_ref[...]
        lam = (jnp.exp(jnp.sum(lp[0:1] * lp[1:2], axis=1, keepdims=True))
               - jnp.exp(jnp.sum(lp[2:3] * lp[3:4], axis=1, keepdims=True)) + lam_init)
        o = a0 / l0 - lam * (a1 / l1)
        ms = jnp.mean(o * o, axis=0, keepdims=True)
        o = o * lax.rsqrt(ms + RMS_EPS) * _rep(sg_ref, tq) * (1.0 - lam_init)
        o_ref[0] = (o * gate_ref[0]).astype(BF16)

    _attend(qps, [bnd_ref[i, 2 * h], bnd_ref[i, 2 * h + 1]],
            lambda r0: k_ref[0, 0, pl.ds(r0, MXU_TILE), :], functools.partial(_v_chunk, v_ref),
            A_V_DIM, finish)


def _attn_a(bnd, lam_p, qa, ka, va, gate, sg, lam_init):
    b = qa.shape[0]
    nt = SEQ // TM
    return pl.pallas_call(
        functools.partial(_attn_a_kernel, lam_init=lam_init),
        grid=(b, A_HEADS, SEQ // TQ_A),
        in_specs=[
            pl.BlockSpec(memory_space=pltpu.SMEM),
            pl.BlockSpec((4, A_QK_DIM), lambda i, h, t: (0, 0)),
            pl.BlockSpec((1, 128, TQ_A), lambda i, h, t: (i, h, t)),
            pl.BlockSpec((1, 1, SEQ, K_PAD), lambda i, h, t: (i, h, 0, 0)),
            pl.BlockSpec((1, nt, A_V_DIM, TM), lambda i, h, t: (i, 0, h, 0)),
            pl.BlockSpec((1, A_V_DIM, TQ_A), lambda i, h, t: (i, h, t)),
            pl.BlockSpec((A_V_DIM, LANES), lambda i, h, t: (0, 0)),
        ],
        out_specs=pl.BlockSpec((1, A_V_DIM, TQ_A), lambda i, h, t: (i, h, t)),
        out_shape=jax.ShapeDtypeStruct((b, A_WIDTH, SEQ), BF16),
        compiler_params=_attn_params(),
        name="attn_diff",
    )(bnd, lam_p, qa, ka, va, gate, sg)


def _attn_b_kernel(bnd_ref, q_ref, k_ref, v_ref, gate_ref, o_ref):
    i, h = pl.program_id(0), pl.program_id(1)
    first = h < B_GROUP
    q = q_ref[0]
    z = jnp.zeros_like(q)
    qp = jnp.concatenate([jnp.where(first, q, z), jnp.where(first, z, q)], axis=0)

    def finish(res):
        ((a, l),) = res
        o_ref[0] = (a / l * gate_ref[0]).astype(BF16)

    _attend([qp], [bnd_ref[i, 2 * A_HEADS + h]],
            lambda r0: k_ref[0, pl.ds(r0, MXU_TILE), :], functools.partial(_v_chunk, v_ref),
            B_DIM, finish)


def _attn_b(bnd, qb, kb, vb, gate):
    b = qb.shape[0]
    nt = SEQ // TM
    return pl.pallas_call(
        _attn_b_kernel,
        grid=(b, B_Q_HEADS, SEQ // TQ_B),
        in_specs=[
            pl.BlockSpec(memory_space=pltpu.SMEM),
            pl.BlockSpec((1, B_DIM, TQ_B), lambda i, h, t: (i, h, t)),
            pl.BlockSpec((1, SEQ, K_PAD), lambda i, h, t: (i, 0, 0)),
            pl.BlockSpec((1, nt, B_DIM, TM), lambda i, h, t: (i, 0, h // B_GROUP, 0)),
            pl.BlockSpec((1, B_DIM, TQ_B), lambda i, h, t: (i, A_WIDTH // B_DIM + h, t)),
        ],
        out_specs=pl.BlockSpec((1, B_DIM, TQ_B), lambda i, h, t: (i, h, t)),
        out_shape=jax.ShapeDtypeStruct((b, B_WIDTH, SEQ), BF16),
        compiler_params=_attn_params(),
        name="attn_gqa",
    )(bnd, qb, kb, vb, gate)


def _attn_c_kernel(bnd_ref, q_ref, k_ref, v_ref, gate_ref, o_ref):
    i, h = pl.program_id(0), pl.program_id(1)
    q = q_ref[0]
    qp = jnp.concatenate([q, jnp.zeros((K_PAD - C_QK, q.shape[1]), BF16)], axis=0)

    def finish(res):
        ((a, l),) = res
        o_ref[0] = (a / l * gate_ref[0]).astype(BF16)

    _attend([qp], [bnd_ref[i, h]],
            lambda r0: k_ref[0, 0, pl.ds(r0, MXU_TILE), :], functools.partial(_v_chunk, v_ref),
            C_V, finish)


def _attn_c(bnd, q, k, v, gate):
    b = q.shape[0]
    nt = SEQ // TM
    return pl.pallas_call(
        _attn_c_kernel,
        grid=(b, C_HEADS, SEQ // TQ_C),
        in_specs=[
            pl.BlockSpec(memory_space=pltpu.SMEM),
            pl.BlockSpec((1, C_QK, TQ_C), lambda i, h, t: (i, h, t)),
            pl.BlockSpec((1, 1, SEQ, K_PAD), lambda i, h, t: (i, h, 0, 0)),
            pl.BlockSpec((1, nt, C_V, TM), lambda i, h, t: (i, 0, h, 0)),
            pl.BlockSpec((1, C_V, TQ_C), lambda i, h, t: (i, h, t)),
        ],
        out_specs=pl.BlockSpec((1, C_V, TQ_C), lambda i, h, t: (i, h, t)),
        out_shape=jax.ShapeDtypeStruct((b, C_HEADS * C_V, SEQ), BF16),
        compiler_params=_attn_params(),
        name="attn_mla",
    )(bnd, q, k, v, gate)


def _out_kernel(*refs, n_og, x_token_major, out_token_major):
    og_refs = refs[:n_og]
    x_ref, w_ref, lg_ref, lb_ref, o_ref = refs[n_og:]
    og = jnp.concatenate([r[0] for r in og_refs], axis=0) if n_og > 1 else og_refs[0][0]
    y = jnp.dot(w_ref[...], og, preferred_element_type=F32)
    x = x_ref[0].T if x_token_major else x_ref[0]
    z = ALPHA * x + y
    mu = jnp.mean(z, axis=0, keepdims=True)
    d = z - mu
    var = jnp.mean(d * d, axis=0, keepdims=True)
    out = d * lax.rsqrt(var + LN_EPS) * _rep(lg_ref, TM) + _rep(lb_ref, TM)
    o_ref[0] = out.T if out_token_major else out


def _out_proj(ogs, xt, w_t, lg, lb, x_token_major=False, out_token_major=False):
    b = xt.shape[0]
    d, s = D_MODEL, SEQ
    tok = lambda i, j: (i, 0, j)
    full2 = lambda i, j: (0, 0)
    fm_spec = pl.BlockSpec((1, d, TM), tok)
    tm_spec = pl.BlockSpec((1, TM, d), lambda i, j: (i, j, 0))
    return pl.pallas_call(
        functools.partial(_out_kernel, n_og=len(ogs), x_token_major=x_token_major,
                          out_token_major=out_token_major),
        grid=(b, s // TM),
        in_specs=[pl.BlockSpec((1, og.shape[1], TM), tok) for og in ogs] + [
            tm_spec if x_token_major else fm_spec,
            pl.BlockSpec((d, d), full2),
            pl.BlockSpec((d, LANES), full2),
            pl.BlockSpec((d, LANES), full2),
        ],
        out_specs=tm_spec if out_token_major else fm_spec,
        out_shape=jax.ShapeDtypeStruct((b, s, d) if out_token_major else (b, d, s), F32),
        compiler_params=pltpu.CompilerParams(
            dimension_semantics=("parallel", "parallel"), vmem_limit_bytes=VMEM_LIMIT),
        name="out_proj_ln",
    )(*ogs, xt, w_t, lg, lb)


def _col(v):
    v = v.astype(F32)
    return jnp.broadcast_to(v[:, None], (v.shape[0], LANES))


def _score_bounds(nq, kmax2):
    return jnp.sqrt(jnp.max(nq, axis=-1) * kmax2) * BOUND_SLACK


def _angles_t(pos, dims, theta):
    inv = theta ** (-jnp.arange(0, dims, 2, dtype=F32) / dims)
    ang = pos.astype(F32)[:, None] * inv[None, :]
    return jnp.cos(ang).T, jnp.sin(ang).T


@jax.jit
def _forward(x, ev_w_in, ev_w_out, ev_lam, ev_subln, ev_qnorm, ev_knorm, ev_ln_g, ev_ln_b,
             od_w_in, od_qnorm, od_kvnorm, od_w_qb, od_w_kvb, od_w_out, od_ln_g, od_ln_b):
    s = x.shape[1]
    pos = jnp.arange(s, dtype=jnp.int32)
    row = jnp.repeat(jnp.arange(s // GRID_W, dtype=jnp.int32), GRID_W)
    col = jnp.tile(jnp.arange(GRID_W, dtype=jnp.int32), s // GRID_W)
    tab_ev = jnp.concatenate(
        _angles_t(pos, A_ROT, ROPE_THETA) + _angles_t(row, B_DIM // 2, AXIAL_THETA)
        + _angles_t(col, B_DIM // 2, AXIAL_THETA), axis=0)
    tab_od = jnp.concatenate(_angles_t(pos, C_ROPE, ROPE_THETA), axis=0)

    xt = x
    for layer in range(DEPTH):
        i = layer // 2
        first, last = layer == 0, layer == DEPTH - 1
        if layer % 2 == 0:
            qa, ka, va, qb, kb, vb, gate, nq, nk = _even_in(
                xt, ev_w_in[i].T.astype(BF16), tab_ev, _col(ev_qnorm[i]), _col(ev_knorm[i]),
                x_token_major=first)
            kmax = jnp.max(nk, axis=-1)
            kmax = jnp.concatenate(
                [kmax[:, :2 * A_HEADS],
                 jnp.repeat(kmax[:, 2 * A_HEADS:2 * A_HEADS + B_KV_HEADS], B_GROUP, axis=1)],
                axis=1)
            bnd = _score_bounds(nq, kmax)
            lam_init = 0.8 - 0.6 * math.exp(-0.3 * layer)
            oga = _attn_a(bnd, ev_lam[i].astype(F32), qa, ka, va, gate, _col(ev_subln[i]),
                          lam_init)
            ogb = _attn_b(bnd, qb, kb, vb, gate)
            xt = _out_proj([oga, ogb], xt, ev_w_out[i].T.astype(BF16),
                           _col(ev_ln_g[i]), _col(ev_ln_b[i]),
                           x_token_major=first, out_token_major=last)
        else:
            q, k, v, gate, nq, nk = _odd_in(
                xt, od_w_in[i].T.astype(BF16), od_w_qb[i].T.astype(BF16),
                od_w_kvb[i].T.astype(BF16), tab_od, _col(od_qnorm[i]), _col(od_kvnorm[i]))
            ogc = _attn_c(_score_bounds(nq, jnp.max(nk, axis=-1)), q, k, v, gate)
            xt = _out_proj([ogc], xt, od_w_out[i].T.astype(BF16),
                           _col(od_ln_g[i]), _col(od_ln_b[i]), out_token_major=last)
    return xt


def kernel(x, ev_w_in, ev_w_out, ev_lam, ev_subln, ev_qnorm, ev_knorm, ev_ln_g, ev_ln_b,
           od_w_in, od_qnorm, od_kvnorm, od_w_qb, od_w_kvb, od_w_out, od_ln_g, od_ln_b):
    return _forward(x, ev_w_in, ev_w_out, ev_lam, ev_subln, ev_qnorm, ev_knorm, ev_ln_g,
                    ev_ln_b, od_w_in, od_qnorm, od_kvnorm, od_w_qb, od_w_kvb, od_w_out,
                    od_ln_g, od_ln_b)
```

```python
import functools
import math

import jax
import jax.numpy as jnp
from jax import lax
from jax.experimental import pallas as pl
from jax.experimental.pallas import tpu as pltpu

F32 = jnp.float32
BF16 = jnp.bfloat16

D_MODEL = 1024
SEQ = 4096
DEPTH = 4
GRID_W = 64
ROPE_THETA = 500000.0
AXIAL_THETA = 10000.0
LN_EPS = 1e-5
RMS_EPS = 1e-6

A_HEADS = 4
A_QK_DIM = 64
A_V_DIM = 128
A_WIDTH = 512
A_ROT = 16
B_Q_HEADS = 8
B_KV_HEADS = 2
B_GROUP = 4
B_DIM = 64
B_WIDTH = 512
EV_IN = 3328

C_HEADS = 16
C_NOPE = 64
C_ROPE = 32
C_V = 64
C_Q_LORA = 256
C_KV_LORA = 128
C_QK = C_NOPE + C_ROPE
OD_IN = 1440

ALPHA = (2 * DEPTH) ** 0.25
LOG2E = 1.4426950408889634

QSCALE_AB = A_QK_DIM ** -0.5 * LOG2E
QSCALE_C = C_QK ** -0.5 * LOG2E

LANES = 128
MXU_TILE = 256
LOOKAHEAD_ITEMS = {1: 2, 2: 3, 4: 8}
MAX_LOCKSTEP = 4
ONES_ROWS = 16
K_PAD = 128
NORM_ROWS = 16
SAFE_BOUND = 60.0
BOUND_SLACK = 1.02
TM = 512
KV_CHUNK_A = 512
KV_CHUNK_BC = 256
TQ_A = 1024
TQ_B = 2048
TQ_C = 2048
VMEM_LIMIT = 48 * 1024 * 1024


def _rep(ref, n):
    a = ref[...]
    return jnp.concatenate([a] * (n // LANES), axis=1)


def _rot(x1, x2, cos, sin):
    return x1 * cos - x2 * sin, x2 * cos + x1 * sin


def _silu(x):
    return x * jax.nn.sigmoid(x)


def _sumsq(x):
    return jnp.sum(x * x, axis=0, keepdims=True)


def _even_in_kernel(x_ref, w_ref, tab_ref, qn_ref, kn_ref,
                    qa_ref, ka_ref, va_ref, qb_ref, kb_ref, vb_ref, g_ref, nq_ref, nk_ref,
                    *, x_token_major):
    xb = x_ref[0].astype(BF16)
    contract = (((1,), (1 if x_token_major else 0,)), ((), ()))

    def proj(r0, r1):
        return lax.dot_general(w_ref[r0:r1, :], xb, contract, preferred_element_type=F32)

    cos_a, sin_a = tab_ref[0:8, :], tab_ref[8:16, :]
    cos_r, sin_r = tab_ref[16:32, :], tab_ref[32:48, :]
    cos_c, sin_c = tab_ref[48:64, :], tab_ref[64:80, :]

    def rope_a(h):
        outs = []
        for hc in range(2 * A_HEADS):
            b = hc * A_QK_DIM
            r1, r2 = _rot(h[b:b + 8], h[b + 8:b + 16], cos_a, sin_a)
            outs.append(jnp.concatenate([r1, r2, h[b + 16:b + 64]], axis=0))
        return outs

    qs = [x * QSCALE_AB for x in rope_a(proj(0, 512))]
    qa_ref[0] = jnp.concatenate(qs, axis=0).astype(BF16)
    ks = rope_a(proj(512, 1024))
    nq = [_sumsq(x) for x in qs]
    nk = [_sumsq(x) for x in ks]
    for h in range(A_HEADS):
        kt = jnp.concatenate([ks[2 * h], ks[2 * h + 1]], axis=0)
        ka_ref[0, h] = kt.T.astype(BF16)
    va_ref[0, 0] = proj(1024, 1536).astype(BF16)

    def norm_axial(h, g):
        ms = jnp.mean(h * h, axis=0, keepdims=True)
        y = h * lax.rsqrt(ms + RMS_EPS) * g
        a1, a2 = _rot(y[0:16], y[16:32], cos_r, sin_r)
        b1, b2 = _rot(y[32:48], y[48:64], cos_c, sin_c)
        return jnp.concatenate([a1, a2, b1, b2], axis=0)

    qn = _rep(qn_ref, TM)
    kn = _rep(kn_ref, TM)
    hq = proj(1536, 2048)
    qs = [norm_axial(hq[h * 64:(h + 1) * 64], qn) * QSCALE_AB for h in range(B_Q_HEADS)]
    qb_ref[0] = jnp.concatenate(qs, axis=0).astype(BF16)
    hkv = proj(2048, 2304)
    ks = [norm_axial(hkv[h * 64:(h + 1) * 64], kn) for h in range(B_KV_HEADS)]
    kb_ref[0] = jnp.concatenate(ks, axis=0).T.astype(BF16)
    nq_ref[0] = jnp.concatenate(nq + [_sumsq(x) for x in qs], axis=0)
    nk = nk + [_sumsq(x) for x in ks]
    nk_ref[0] = jnp.concatenate(nk + [jnp.zeros_like(nk[0])] * (NORM_ROWS - len(nk)), axis=0)
    vb_ref[0, 0] = hkv[128:256].astype(BF16)
    g_ref[0] = _silu(proj(2304, 3328)).astype(BF16)


def _even_in(xt, w_t, tab, qn, kn, x_token_major=False):
    b = xt.shape[0]
    d, s = D_MODEL, SEQ
    nt = s // TM
    tok = lambda i, j: (i, 0, j)
    chunk = lambda i, j: (i, j, 0, 0)
    full2 = lambda i, j: (0, 0)
    x_spec = (pl.BlockSpec((1, TM, d), lambda i, j: (i, j, 0)) if x_token_major
              else pl.BlockSpec((1, d, TM), tok))
    return pl.pallas_call(
        functools.partial(_even_in_kernel, x_token_major=x_token_major),
        grid=(b, nt),
        in_specs=[
            x_spec,
            pl.BlockSpec((EV_IN, d), full2),
            pl.BlockSpec((80, TM), lambda i, j: (0, j)),
            pl.BlockSpec((B_DIM, LANES), full2),
            pl.BlockSpec((B_DIM, LANES), full2),
        ],
        out_specs=[
            pl.BlockSpec((1, 512, TM), tok),
            pl.BlockSpec((1, A_HEADS, TM, K_PAD), lambda i, j: (i, 0, j, 0)),
            pl.BlockSpec((1, 1, 512, TM), chunk),
            pl.BlockSpec((1, 512, TM), tok),
            pl.BlockSpec((1, TM, K_PAD), lambda i, j: (i, j, 0)),
            pl.BlockSpec((1, 1, 128, TM), chunk),
            pl.BlockSpec((1, 1024, TM), tok),
            pl.BlockSpec((1, NORM_ROWS, TM), tok),
            pl.BlockSpec((1, NORM_ROWS, TM), tok),
        ],
        out_shape=[
            jax.ShapeDtypeStruct((b, 512, s), BF16),
            jax.ShapeDtypeStruct((b, A_HEADS, s, K_PAD), BF16),
            jax.ShapeDtypeStruct((b, nt, 512, TM), BF16),
            jax.ShapeDtypeStruct((b, 512, s), BF16),
            jax.ShapeDtypeStruct((b, s, K_PAD), BF16),
            jax.ShapeDtypeStruct((b, nt, 128, TM), BF16),
            jax.ShapeDtypeStruct((b, 1024, s), BF16),
            jax.ShapeDtypeStruct((b, NORM_ROWS, s), F32),
            jax.ShapeDtypeStruct((b, NORM_ROWS, s), F32),
        ],
        compiler_params=pltpu.CompilerParams(
            dimension_semantics=("parallel", "parallel"), vmem_limit_bytes=VMEM_LIMIT),
        name="even_in_proj",
    )(xt, w_t, tab, qn, kn)


def _odd_in_kernel(x_ref, w_ref, wq_ref, wkv_ref, tab_ref, qn_ref, kvn_ref,
                   q_ref, k_ref, v_ref, g_ref, nq_ref, nk_ref):
    xb = x_ref[0].astype(BF16)
    cos, sin = tab_ref[0:16, :], tab_ref[16:32, :]

    def rms(h, g):
        ms = jnp.mean(h * h, axis=0, keepdims=True)
        return h * lax.rsqrt(ms + RMS_EPS) * g

    lat = jnp.dot(w_ref[0:416, :], xb, preferred_element_type=F32)
    cqn = rms(lat[0:256], _rep(qn_ref, TM)).astype(BF16)
    q = jnp.dot(wq_ref[...], cqn, preferred_element_type=F32)
    qs = []
    for h in range(C_HEADS):
        b = h * C_QK
        r1, r2 = _rot(q[b + 64:b + 80], q[b + 80:b + 96], cos, sin)
        qs.append(jnp.concatenate([q[b:b + 64], r1, r2], axis=0) * QSCALE_C)
    q_ref[0] = jnp.concatenate(qs, axis=0).astype(BF16)
    nq_ref[0] = jnp.concatenate([_sumsq(x) for x in qs], axis=0)

    ckvn = rms(lat[256:384], _rep(kvn_ref, TM)).astype(BF16)
    kv = jnp.dot(wkv_ref[...], ckvn, preferred_element_type=F32)
    r1, r2 = _rot(lat[384:400], lat[400:416], cos, sin)
    zpad = jnp.zeros((K_PAD - C_QK, TM), F32)
    nkr = _sumsq(r1) + _sumsq(r2)
    nk = []
    for h in range(C_HEADS):
        kn = kv[h * 128:h * 128 + 64]
        nk.append(_sumsq(kn) + nkr)
        kt = jnp.concatenate([kn, r1, r2, zpad], axis=0)
        k_ref[0, h] = kt.T.astype(BF16)
    nk_ref[0] = jnp.concatenate(nk, axis=0)
    v_ref[0, 0] = jnp.concatenate(
        [kv[h * 128 + 64:h * 128 + 128] for h in range(C_HEADS)], axis=0).astype(BF16)
    g_ref[0] = _silu(jnp.dot(w_ref[416:1440, :], xb, preferred_element_type=F32)).astype(BF16)


def _odd_in(xt, w_t, wq_t, wkv_t, tab, qn, kvn):
    b, d, s = xt.shape
    nt = s // TM
    tok = lambda i, j: (i, 0, j)
    full2 = lambda i, j: (0, 0)
    return pl.pallas_call(
        _odd_in_kernel,
        grid=(b, nt),
        in_specs=[
            pl.BlockSpec((1, d, TM), tok),
            pl.BlockSpec((OD_IN, d), full2),
            pl.BlockSpec((C_HEADS * C_QK, C_Q_LORA), full2),
            pl.BlockSpec((C_HEADS * 128, C_KV_LORA), full2),
            pl.BlockSpec((32, TM), lambda i, j: (0, j)),
            pl.BlockSpec((C_Q_LORA, LANES), full2),
            pl.BlockSpec((C_KV_LORA, LANES), full2),
        ],
        out_specs=[
            pl.BlockSpec((1, C_HEADS * C_QK, TM), tok),
            pl.BlockSpec((1, C_HEADS, TM, K_PAD), lambda i, j: (i, 0, j, 0)),
            pl.BlockSpec((1, 1, 1024, TM), lambda i, j: (i, j, 0, 0)),
            pl.BlockSpec((1, 1024, TM), tok),
            pl.BlockSpec((1, NORM_ROWS, TM), tok),
            pl.BlockSpec((1, NORM_ROWS, TM), tok),
        ],
        out_shape=[
            jax.ShapeDtypeStruct((b, C_HEADS * C_QK, s), BF16),
            jax.ShapeDtypeStruct((b, C_HEADS, s, K_PAD), BF16),
            jax.ShapeDtypeStruct((b, nt, 1024, TM), BF16),
            jax.ShapeDtypeStruct((b, 1024, s), BF16),
            jax.ShapeDtypeStruct((b, NORM_ROWS, s), F32),
            jax.ShapeDtypeStruct((b, NORM_ROWS, s), F32),
        ],
        compiler_params=pltpu.CompilerParams(
            dimension_semantics=("parallel", "parallel"), vmem_limit_bytes=VMEM_LIMIT),
        name="odd_in_proj",
    )(xt, w_t, wq_t, wkv_t, tab, qn, kvn)


def _softmax_pv(qps, k_at, v_ref, dv, kv_chunk, bounds=None):
    nsets = len(qps)
    ntiles = qps[0].shape[1] // MXU_TILE
    qps = [qp[:, t * MXU_TILE:(t + 1) * MXU_TILE] for qp in qps for t in range(ntiles)]
    n = len(qps)
    nchunks = SEQ // kv_chunk
    halves = kv_chunk // MXU_TILE
    ones = jnp.ones((ONES_ROWS, kv_chunk), BF16)

    def scores(ci, j):
        return [jnp.dot(k_at(ci * kv_chunk + h * MXU_TILE), qps[j], preferred_element_type=F32)
                for h in range(halves)]

    m = [None] * n
    acc = [None] * n
    items = [(ci, j) for g in range(0, n, MAX_LOCKSTEP) for ci in range(nchunks)
             for j in range(g, min(g + MAX_LOCKSTEP, n))]
    lookahead = LOOKAHEAD_ITEMS[min(n, MAX_LOCKSTEP)]
    pending = [scores(*it) for it in items[:lookahead]]
    for idx, (ci, j) in enumerate(items):
        if idx + lookahead < len(items):
            pending.append(scores(*items[idx + lookahead]))
        sc = pending.pop(0)
        r0 = ci * kv_chunk
        v = jnp.concatenate([v_ref[0, r0 // TM, :, pl.ds(r0 % TM, kv_chunk)], ones],
                            axis=0)
        if bounds is None:
            cm = functools.reduce(jnp.maximum, [jnp.max(s, axis=0, keepdims=True) for s in sc])
            ref = cm if ci == 0 else jnp.maximum(m[j], cm)
        else:
            ref = bounds[j // ntiles]
        pv = functools.reduce(jnp.add, [
            jnp.dot(v[:, h * MXU_TILE:(h + 1) * MXU_TILE],
                    jnp.exp2(sc[h] - ref).astype(BF16), preferred_element_type=F32)
            for h in range(halves)])
        if ci == 0:
            acc[j] = pv
        elif bounds is None:
            acc[j] = acc[j] * jnp.exp2(m[j] - ref) + pv
        else:
            acc[j] = acc[j] + pv
        m[j] = ref
    out = []
    for j in range(nsets):
        a = jnp.concatenate(acc[j * ntiles:(j + 1) * ntiles], axis=1)
        out.append((a[0:dv], a[dv:dv + 1]))
    return out


def _attend(qps, bounds, k_at, v_ref, dv, kv_chunk, finish):
    safe = functools.reduce(jnp.logical_and, [b <= SAFE_BOUND for b in bounds])

    @pl.when(safe)
    def _():
        finish(_softmax_pv(qps, k_at, v_ref, dv, kv_chunk, bounds))

    @pl.when(jnp.logical_not(safe))
    def _():
        finish(_softmax_pv(qps, k_at, v_ref, dv, kv_chunk))


def _attn_params():
    return pltpu.CompilerParams(
        dimension_semantics=("parallel", "parallel", "parallel"), vmem_limit_bytes=VMEM_LIMIT)


def _attn_a_kernel(bnd_ref, lam_ref, q_ref, k_ref, v_ref, gate_ref, sg_ref, o_ref, *, lam_init):
    i, h = pl.program_id(0), pl.program_id(1)
    q = q_ref[0]
    tq = q.shape[1]
    z = jnp.zeros((A_QK_DIM, tq), BF16)
    qps = [jnp.concatenate([q[0:64], z], axis=0), jnp.concatenate([z, q[64:128]], axis=0)]

    def finish(res):
        (a0, l0), (a1, l1) = res
        lp = lam_ref[...]
        lam = (jnp.exp(jnp.sum(lp[0:1] * lp[1:2], axis=1, keepdims=True))
               - jnp.exp(jnp.sum(lp[2:3] * lp[3:4], axis=1, keepdims=True)) + lam_init)
        o = a0 / l0 - lam * (a1 / l1)
        ms = jnp.mean(o * o, axis=0, keepdims=True)
        o = o * lax.rsqrt(ms + RMS_EPS) * _rep(sg_ref, tq) * (1.0 - lam_init)
        o_ref[0] = (o * gate_ref[0]).astype(BF16)

    _attend(qps, [bnd_ref[i, 2 * h], bnd_ref[i, 2 * h + 1]],
            lambda r0: k_ref[0, 0, pl.ds(r0, MXU_TILE), :], v_ref, A_V_DIM, KV_CHUNK_A, finish)


def _attn_a(bnd, lam_p, qa, ka, va, gate, sg, lam_init):
    b = qa.shape[0]
    nt = SEQ // TM
    return pl.pallas_call(
        functools.partial(_attn_a_kernel, lam_init=lam_init),
        grid=(b, A_HEADS, SEQ // TQ_A),
        in_specs=[
            pl.BlockSpec(memory_space=pltpu.SMEM),
            pl.BlockSpec((4, A_QK_DIM), lambda i, h, t: (0, 0)),
            pl.BlockSpec((1, 128, TQ_A), lambda i, h, t: (i, h, t)),
            pl.BlockSpec((1, 1, SEQ, K_PAD), lambda i, h, t: (i, h, 0, 0)),
            pl.BlockSpec((1, nt, A_V_DIM, TM), lambda i, h, t: (i, 0, h, 0)),
            pl.BlockSpec((1, A_V_DIM, TQ_A), lambda i, h, t: (i, h, t)),
            pl.BlockSpec((A_V_DIM, LANES), lambda i, h, t: (0, 0)),
        ],
        out_specs=pl.BlockSpec((1, A_V_DIM, TQ_A), lambda i, h, t: (i, h, t)),
        out_shape=jax.ShapeDtypeStruct((b, A_WIDTH, SEQ), BF16),
        compiler_params=_attn_params(),
        name="attn_diff",
    )(bnd, lam_p, qa, ka, va, gate, sg)


def _attn_b_kernel(bnd_ref, q_ref, k_ref, v_ref, gate_ref, o_ref):
    i, h = pl.program_id(0), pl.program_id(1)
    first = h < B_GROUP
    q = q_ref[0]
    z = jnp.zeros_like(q)
    qp = jnp.concatenate([jnp.where(first, q, z), jnp.where(first, z, q)], axis=0)

    def finish(res):
        ((a, l),) = res
        o_ref[0] = (a / l * gate_ref[0]).astype(BF16)

    _attend([qp], [bnd_ref[i, 2 * A_HEADS + h]],
            lambda r0: k_ref[0, pl.ds(r0, MXU_TILE), :], v_ref, B_DIM, KV_CHUNK_BC, finish)


def _attn_b(bnd, qb, kb, vb, gate):
    b = qb.shape[0]
    nt = SEQ // TM
    return pl.pallas_call(
        _attn_b_kernel,
        grid=(b, B_Q_HEADS, SEQ // TQ_B),
        in_specs=[
            pl.BlockSpec(memory_space=pltpu.SMEM),
            pl.BlockSpec((1, B_DIM, TQ_B), lambda i, h, t: (i, h, t)),
            pl.BlockSpec((1, SEQ, K_PAD), lambda i, h, t: (i, 0, 0)),
            pl.BlockSpec((1, nt, B_DIM, TM), lambda i, h, t: (i, 0, h // B_GROUP, 0)),
            pl.BlockSpec((1, B_DIM, TQ_B), lambda i, h, t: (i, A_WIDTH // B_DIM + h, t)),
        ],
        out_specs=pl.BlockSpec((1, B_DIM, TQ_B), lambda i, h, t: (i, h, t)),
        out_shape=jax.ShapeDtypeStruct((b, B_WIDTH, SEQ), BF16),
        compiler_params=_attn_params(),
        name="attn_gqa",
    )(bnd, qb, kb, vb, gate)


def _attn_c_kernel(bnd_ref, q_ref, k_ref, v_ref, gate_ref, o_ref):
    i, h = pl.program_id(0), pl.program_id(1)
    q = q_ref[0]
    qp = jnp.concatenate([q, jnp.zeros((K_PAD - C_QK, q.shape[1]), BF16)], axis=0)

    def finish(res):
        ((a, l),) = res
        o_ref[0] = (a / l * gate_ref[0]).astype(BF16)

    _attend([qp], [bnd_ref[i, h]],
            lambda r0: k_ref[0, 0, pl.ds(r0, MXU_TILE), :], v_ref, C_V, KV_CHUNK_BC, finish)


def _attn_c(bnd, q, k, v, gate):
    b = q.shape[0]
    nt = SEQ // TM
    return pl.pallas_call(
        _attn_c_kernel,
        grid=(b, C_HEADS, SEQ // TQ_C),
        in_specs=[
            pl.BlockSpec(memory_space=pltpu.SMEM),
            pl.BlockSpec((1, C_QK, TQ_C), lambda i, h, t: (i, h, t)),
            pl.BlockSpec((1, 1, SEQ, K_PAD), lambda i, h, t: (i, h, 0, 0)),
            pl.BlockSpec((1, nt, C_V, TM), lambda i, h, t: (i, 0, h, 0)),
            pl.BlockSpec((1, C_V, TQ_C), lambda i, h, t: (i, h, t)),
        ],
        out_specs=pl.BlockSpec((1, C_V, TQ_C), lambda i, h, t: (i, h, t)),
        out_shape=jax.ShapeDtypeStruct((b, C_HEADS * C_V, SEQ), BF16),
        compiler_params=_attn_params(),
        name="attn_mla",
    )(bnd, q, k, v, gate)


def _out_kernel(*refs, n_og, x_token_major, out_token_major):
    og_refs = refs[:n_og]
    x_ref, w_ref, lg_ref, lb_ref, o_ref = refs[n_og:]
    og = jnp.concatenate([r[0] for r in og_refs], axis=0) if n_og > 1 else og_refs[0][0]
    y = jnp.dot(w_ref[...], og, preferred_element_type=F32)
    x = x_ref[0].T if x_token_major else x_ref[0]
    z = ALPHA * x + y
    mu = jnp.mean(z, axis=0, keepdims=True)
    d = z - mu
    var = jnp.mean(d * d, axis=0, keepdims=True)
    out = d * lax.rsqrt(var + LN_EPS) * _rep(lg_ref, TM) + _rep(lb_ref, TM)
    o_ref[0] = out.T if out_token_major else out


def _out_proj(ogs, xt, w_t, lg, lb, x_token_major=False, out_token_major=False):
    b = xt.shape[0]
    d, s = D_MODEL, SEQ
    tok = lambda i, j: (i, 0, j)
    full2 = lambda i, j: (0, 0)
    fm_spec = pl.BlockSpec((1, d, TM), tok)
    tm_spec = pl.BlockSpec((1, TM, d), lambda i, j: (i, j, 0))
    return pl.pallas_call(
        functools.partial(_out_kernel, n_og=len(ogs), x_token_major=x_token_major,
                          out_token_major=out_token_major),
        grid=(b, s // TM),
        in_specs=[pl.BlockSpec((1, og.shape[1], TM), tok) for og in ogs] + [
            tm_spec if x_token_major else fm_spec,
            pl.BlockSpec((d, d), full2),
            pl.BlockSpec((d, LANES), full2),
            pl.BlockSpec((d, LANES), full2),
        ],
        out_specs=tm_spec if out_token_major else fm_spec,
        out_shape=jax.ShapeDtypeStruct((b, s, d) if out_token_major else (b, d, s), F32),
        compiler_params=pltpu.CompilerParams(
            dimension_semantics=("parallel", "parallel"), vmem_limit_bytes=VMEM_LIMIT),
        name="out_proj_ln",
    )(*ogs, xt, w_t, lg, lb)


def _col(v):
    v = v.astype(F32)
    return jnp.broadcast_to(v[:, None], (v.shape[0], LANES))


def _score_bounds(nq, kmax2):
    return jnp.sqrt(jnp.max(nq, axis=-1) * kmax2) * BOUND_SLACK


def _angles_t(pos, dims, theta):
    inv = theta ** (-jnp.arange(0, dims, 2, dtype=F32) / dims)
    ang = pos.astype(F32)[:, None] * inv[None, :]
    return jnp.cos(ang).T, jnp.sin(ang).T


@jax.jit
def _forward(x, ev_w_in, ev_w_out, ev_lam, ev_subln, ev_qnorm, ev_knorm, ev_ln_g, ev_ln_b,
             od_w_in, od_qnorm, od_kvnorm, od_w_qb, od_w_kvb, od_w_out, od_ln_g, od_ln_b):
    s = x.shape[1]
    pos = jnp.arange(s, dtype=jnp.int32)
    row = jnp.repeat(jnp.arange(s // GRID_W, dtype=jnp.int32), GRID_W)
    col = jnp.tile(jnp.arange(GRID_W, dtype=jnp.int32), s // GRID_W)
    tab_ev = jnp.concatenate(
        _angles_t(pos, A_ROT, ROPE_THETA) + _angles_t(row, B_DIM // 2, AXIAL_THETA)
        + _angles_t(col, B_DIM // 2, AXIAL_THETA), axis=0)
    tab_od = jnp.concatenate(_angles_t(pos, C_ROPE, ROPE_THETA), axis=0)

    xt = x
    for layer in range(DEPTH):
        i = layer // 2
        first, last = layer == 0, layer == DEPTH - 1
        if layer % 2 == 0:
            qa, ka, va, qb, kb, vb, gate, nq, nk = _even_in(
                xt, ev_w_in[i].T.astype(BF16), tab_ev, _col(ev_qnorm[i]), _col(ev_knorm[i]),
                x_token_major=first)
            kmax = jnp.max(nk, axis=-1)
            kmax = jnp.concatenate(
                [kmax[:, :2 * A_HEADS],
                 jnp.repeat(kmax[:, 2 * A_HEADS:2 * A_HEADS + B_KV_HEADS], B_GROUP, axis=1)],
                axis=1)
            bnd = _score_bounds(nq, kmax)
            lam_init = 0.8 - 0.6 * math.exp(-0.3 * layer)
            oga = _attn_a(bnd, ev_lam[i].astype(F32), qa, ka, va, gate, _col(ev_subln[i]),
                          lam_init)
            ogb = _attn_b(bnd, qb, kb, vb, gate)
            xt = _out_proj([oga, ogb], xt, ev_w_out[i].T.astype(BF16),
                           _col(ev_ln_g[i]), _col(ev_ln_b[i]),
                           x_token_major=first, out_token_major=last)
        else:
            q, k, v, gate, nq, nk = _odd_in(
                xt, od_w_in[i].T.astype(BF16), od_w_qb[i].T.astype(BF16),
                od_w_kvb[i].T.astype(BF16), tab_od, _col(od_qnorm[i]), _col(od_kvnorm[i]))
            ogc = _attn_c(_score_bounds(nq, jnp.max(nk, axis=-1)), q, k, v, gate)
            xt = _out_proj([ogc], xt, od_w_out[i].T.astype(BF16),
                           _col(od_ln_g[i]), _col(od_ln_b[i]), out_token_major=last)
    return xt


def kernel(x, ev_w_in, ev_w_out, ev_lam, ev_subln, ev_qnorm, ev_knorm, ev_ln_g, ev_ln_b,
           od_w_in, od_qnorm, od_kvnorm, od_w_qb, od_w_kvb, od_w_out, od_ln_g, od_ln_b):
    return _forward(x, ev_w_in, ev_w_out, ev_lam, ev_subln, ev_qnorm, ev_knorm, ev_ln_g,
                    ev_ln_b, od_w_in, od_qnorm, od_kvnorm, od_w_qb, od_w_kvb, od_w_out,
                    od_ln_g, od_ln_b)
```

```python
import functools
import math

import jax
import jax.numpy as jnp
from jax import lax
from jax.experimental import pallas as pl
from jax.experimental.pallas import tpu as pltpu

F32 = jnp.float32
BF16 = jnp.bfloat16

D_MODEL = 1024
SEQ = 4096
DEPTH = 4
GRID_W = 64
ROPE_THETA = 500000.0
AXIAL_THETA = 10000.0
LN_EPS = 1e-5
RMS_EPS = 1e-6

A_HEADS = 4
A_QK_DIM = 64
A_V_DIM = 128
A_WIDTH = 512
A_ROT = 16
B_Q_HEADS = 8
B_KV_HEADS = 2
B_GROUP = 4
B_DIM = 64
B_WIDTH = 512
EV_IN = 3328

C_HEADS = 16
C_NOPE = 64
C_ROPE = 32
C_V = 64
C_Q_LORA = 256
C_KV_LORA = 128
C_QK = C_NOPE + C_ROPE
OD_IN = 1440

ALPHA = (2 * DEPTH) ** 0.25
LOG2E = 1.4426950408889634

QSCALE_AB = A_QK_DIM ** -0.5 * LOG2E
QSCALE_C = C_QK ** -0.5 * LOG2E

LANES = 128
MXU_TILE = 256
LOOKAHEAD_ITEMS = {1: 2, 2: 3, 4: 8}
MAX_LOCKSTEP = 4
ONES_ROWS = 16
K_PAD = 128
NORM_ROWS = 16
SAFE_BOUND = 60.0
BOUND_SLACK = 1.02
TM = 512
KV_CHUNK_A = 512
KV_CHUNK_BC = 256
TQ_A = 1024
TQ_B = 2048
TQ_C = 2048
VMEM_LIMIT = 56 * 1024 * 1024


def _rep(ref, n):
    a = ref[...]
    return jnp.concatenate([a] * (n // LANES), axis=1)


def _rot(x1, x2, cos, sin):
    return x1 * cos - x2 * sin, x2 * cos + x1 * sin


def _silu(x):
    return x * jax.nn.sigmoid(x)


def _sumsq(x):
    return jnp.sum(x * x, axis=0, keepdims=True)


def _tok(i, j):
    return (i, 0, j)


def _const2(i, j):
    return (0, 0)


def _resident(shape):
    return pl.BlockSpec(shape, _const2, pipeline_mode=pl.Buffered(1))


def _proj_params():
    return pltpu.CompilerParams(
        dimension_semantics=("parallel", "parallel"), vmem_limit_bytes=VMEM_LIMIT)


def _even_in_body(xb, w_ref, tab_ref, qn_ref, kn_ref,
                  qa_ref, ka_ref, va_ref, qb_ref, kb_ref, vb_ref, g_ref, nq_ref, nk_ref,
                  *, x_token_major=False):
    contract = (((1,), (1 if x_token_major else 0,)), ((), ()))

    def proj(r0, r1):
        return lax.dot_general(w_ref[r0:r1, :], xb, contract, preferred_element_type=F32)

    cos_a, sin_a = tab_ref[0:8, :], tab_ref[8:16, :]
    cos_r, sin_r = tab_ref[16:32, :], tab_ref[32:48, :]
    cos_c, sin_c = tab_ref[48:64, :], tab_ref[64:80, :]

    def rope_a(h):
        outs = []
        for hc in range(2 * A_HEADS):
            b = hc * A_QK_DIM
            r1, r2 = _rot(h[b:b + 8], h[b + 8:b + 16], cos_a, sin_a)
            outs.append(jnp.concatenate([r1, r2, h[b + 16:b + 64]], axis=0))
        return outs

    qs = [x * QSCALE_AB for x in rope_a(proj(0, 512))]
    qa_ref[0] = jnp.concatenate(qs, axis=0).astype(BF16)
    ks = rope_a(proj(512, 1024))
    nq = [_sumsq(x) for x in qs]
    nk = [_sumsq(x) for x in ks]
    for h in range(A_HEADS):
        kt = jnp.concatenate([ks[2 * h], ks[2 * h + 1]], axis=0)
        ka_ref[0, h] = kt.T.astype(BF16)
    va_ref[0, 0] = proj(1024, 1536).astype(BF16)

    def norm_axial(h, g):
        ms = jnp.mean(h * h, axis=0, keepdims=True)
        y = h * lax.rsqrt(ms + RMS_EPS) * g
        a1, a2 = _rot(y[0:16], y[16:32], cos_r, sin_r)
        b1, b2 = _rot(y[32:48], y[48:64], cos_c, sin_c)
        return jnp.concatenate([a1, a2, b1, b2], axis=0)

    qn = _rep(qn_ref, TM)
    kn = _rep(kn_ref, TM)
    hq = proj(1536, 2048)
    qs = [norm_axial(hq[h * 64:(h + 1) * 64], qn) * QSCALE_AB for h in range(B_Q_HEADS)]
    qb_ref[0] = jnp.concatenate(qs, axis=0).astype(BF16)
    hkv = proj(2048, 2304)
    ks = [norm_axial(hkv[h * 64:(h + 1) * 64], kn) for h in range(B_KV_HEADS)]
    kb_ref[0] = jnp.concatenate(ks, axis=0).T.astype(BF16)
    nq_ref[0] = jnp.concatenate(nq + [_sumsq(x) for x in qs], axis=0)
    nk = nk + [_sumsq(x) for x in ks]
    nk_ref[0] = jnp.concatenate(nk + [jnp.zeros_like(nk[0])] * (NORM_ROWS - len(nk)), axis=0)
    vb_ref[0, 0] = hkv[128:256].astype(BF16)
    g_ref[0] = _silu(proj(2304, 3328)).astype(BF16)


def _even_in_kernel(x_ref, *refs):
    _even_in_body(x_ref[0].astype(BF16), *refs, x_token_major=True)


def _even_in_specs(b):
    d, s = D_MODEL, SEQ
    nt = s // TM
    tok = _tok
    chunk = lambda i, j: (i, j, 0, 0)
    return (
        [
            _resident((EV_IN, d)),
            pl.BlockSpec((80, TM), lambda i, j: (0, j)),
            _resident((B_DIM, LANES)),
            _resident((B_DIM, LANES)),
        ],
        [
            pl.BlockSpec((1, 512, TM), tok),
            pl.BlockSpec((1, A_HEADS, TM, K_PAD), lambda i, j: (i, 0, j, 0)),
            pl.BlockSpec((1, 1, 512, TM), chunk),
            pl.BlockSpec((1, 512, TM), tok),
            pl.BlockSpec((1, TM, K_PAD), lambda i, j: (i, j, 0)),
            pl.BlockSpec((1, 1, 128, TM), chunk),
            pl.BlockSpec((1, 1024, TM), tok),
            pl.BlockSpec((1, NORM_ROWS, TM), tok),
            pl.BlockSpec((1, NORM_ROWS, TM), tok),
        ],
        [
            jax.ShapeDtypeStruct((b, 512, s), BF16),
            jax.ShapeDtypeStruct((b, A_HEADS, s, K_PAD), BF16),
            jax.ShapeDtypeStruct((b, nt, 512, TM), BF16),
            jax.ShapeDtypeStruct((b, 512, s), BF16),
            jax.ShapeDtypeStruct((b, s, K_PAD), BF16),
            jax.ShapeDtypeStruct((b, nt, 128, TM), BF16),
            jax.ShapeDtypeStruct((b, 1024, s), BF16),
            jax.ShapeDtypeStruct((b, NORM_ROWS, s), F32),
            jax.ShapeDtypeStruct((b, NORM_ROWS, s), F32),
        ],
    )


def _even_in(x, w_t, tab, qn, kn):
    b = x.shape[0]
    in_specs, out_specs, out_shape = _even_in_specs(b)
    return pl.pallas_call(
        _even_in_kernel,
        grid=(b, SEQ // TM),
        in_specs=[pl.BlockSpec((1, TM, D_MODEL), lambda i, j: (i, j, 0))] + in_specs,
        out_specs=out_specs,
        out_shape=out_shape,
        compiler_params=_proj_params(),
        name="even_in_proj",
    )(x, w_t, tab, qn, kn)


def _odd_in_body(xb, w_ref, wq_ref, wkv_ref, tab_ref, qn_ref, kvn_ref,
                 q_ref, k_ref, v_ref, g_ref, nq_ref, nk_ref):
    cos, sin = tab_ref[0:16, :], tab_ref[16:32, :]

    def rms(h, g):
        ms = jnp.mean(h * h, axis=0, keepdims=True)
        return h * lax.rsqrt(ms + RMS_EPS) * g

    lat = jnp.dot(w_ref[0:416, :], xb, preferred_element_type=F32)
    cqn = rms(lat[0:256], _rep(qn_ref, TM)).astype(BF16)
    q = jnp.dot(wq_ref[...], cqn, preferred_element_type=F32)
    qs = []
    for h in range(C_HEADS):
        b = h * C_QK
        r1, r2 = _rot(q[b + 64:b + 80], q[b + 80:b + 96], cos, sin)
        qs.append(jnp.concatenate([q[b:b + 64], r1, r2], axis=0) * QSCALE_C)
    q_ref[0] = jnp.concatenate(qs, axis=0).astype(BF16)
    nq_ref[0] = jnp.concatenate([_sumsq(x) for x in qs], axis=0)

    ckvn = rms(lat[256:384], _rep(kvn_ref, TM)).astype(BF16)
    kv = jnp.dot(wkv_ref[...], ckvn, preferred_element_type=F32)
    r1, r2 = _rot(lat[384:400], lat[400:416], cos, sin)
    zpad = jnp.zeros((K_PAD - C_QK, TM), F32)
    nkr = _sumsq(r1) + _sumsq(r2)
    nk = []
    for h in range(C_HEADS):
        kn = kv[h * 128:h * 128 + 64]
        nk.append(_sumsq(kn) + nkr)
        kt = jnp.concatenate([kn, r1, r2, zpad], axis=0)
        k_ref[0, h] = kt.T.astype(BF16)
    nk_ref[0] = jnp.concatenate(nk, axis=0)
    v_ref[0, 0] = jnp.concatenate(
        [kv[h * 128 + 64:h * 128 + 128] for h in range(C_HEADS)], axis=0).astype(BF16)
    g_ref[0] = _silu(jnp.dot(w_ref[416:1440, :], xb, preferred_element_type=F32)).astype(BF16)


def _odd_in_specs(b):
    d, s = D_MODEL, SEQ
    nt = s // TM
    tok = _tok
    return (
        [
            _resident((OD_IN, d)),
            _resident((C_HEADS * C_QK, C_Q_LORA)),
            _resident((C_HEADS * 128, C_KV_LORA)),
            pl.BlockSpec((32, TM), lambda i, j: (0, j)),
            _resident((C_Q_LORA, LANES)),
            _resident((C_KV_LORA, LANES)),
        ],
        [
            pl.BlockSpec((1, C_HEADS * C_QK, TM), tok),
            pl.BlockSpec((1, C_HEADS, TM, K_PAD), lambda i, j: (i, 0, j, 0)),
            pl.BlockSpec((1, 1, 1024, TM), lambda i, j: (i, j, 0, 0)),
            pl.BlockSpec((1, 1024, TM), tok),
            pl.BlockSpec((1, NORM_ROWS, TM), tok),
            pl.BlockSpec((1, NORM_ROWS, TM), tok),
        ],
        [
            jax.ShapeDtypeStruct((b, C_HEADS * C_QK, s), BF16),
            jax.ShapeDtypeStruct((b, C_HEADS, s, K_PAD), BF16),
            jax.ShapeDtypeStruct((b, nt, 1024, TM), BF16),
            jax.ShapeDtypeStruct((b, 1024, s), BF16),
            jax.ShapeDtypeStruct((b, NORM_ROWS, s), F32),
            jax.ShapeDtypeStruct((b, NORM_ROWS, s), F32),
        ],
    )


def _softmax_pv(qps, k_at, v_ref, dv, kv_chunk, bounds=None):
    nsets = len(qps)
    ntiles = qps[0].shape[1] // MXU_TILE
    qps = [qp[:, t * MXU_TILE:(t + 1) * MXU_TILE] for qp in qps for t in range(ntiles)]
    n = len(qps)
    nchunks = SEQ // kv_chunk
    halves = kv_chunk // MXU_TILE
    ones = jnp.ones((ONES_ROWS, kv_chunk), BF16)

    def scores(ci, j):
        return [jnp.dot(k_at(ci * kv_chunk + h * MXU_TILE), qps[j], preferred_element_type=F32)
                for h in range(halves)]

    m = [None] * n
    acc = [None] * n
    items = [(ci, j) for g in range(0, n, MAX_LOCKSTEP) for ci in range(nchunks)
             for j in range(g, min(g + MAX_LOCKSTEP, n))]
    lookahead = LOOKAHEAD_ITEMS[min(n, MAX_LOCKSTEP)]
    pending = [scores(*it) for it in items[:lookahead]]
    for idx, (ci, j) in enumerate(items):
        if idx + lookahead < len(items):
            pending.append(scores(*items[idx + lookahead]))
        sc = pending.pop(0)
        r0 = ci * kv_chunk
        v = jnp.concatenate([v_ref[0, r0 // TM, :, pl.ds(r0 % TM, kv_chunk)], ones],
                            axis=0)
        if bounds is None:
            cm = functools.reduce(jnp.maximum, [jnp.max(s, axis=0, keepdims=True) for s in sc])
            ref = cm if ci == 0 else jnp.maximum(m[j], cm)
        else:
            ref = bounds[j // ntiles]
        pv = functools.reduce(jnp.add, [
            jnp.dot(v[:, h * MXU_TILE:(h + 1) * MXU_TILE],
                    jnp.exp2(sc[h] - ref).astype(BF16), preferred_element_type=F32)
            for h in range(halves)])
        if ci == 0:
            acc[j] = pv
        elif bounds is None:
            acc[j] = acc[j] * jnp.exp2(m[j] - ref) + pv
        else:
            acc[j] = acc[j] + pv
        m[j] = ref
    out = []
    for j in range(nsets):
        a = jnp.concatenate(acc[j * ntiles:(j + 1) * ntiles], axis=1)
        out.append((a[0:dv], a[dv:dv + 1]))
    return out


def _attend(qps, bounds, k_at, v_ref, dv, kv_chunk, finish):
    safe = functools.reduce(jnp.logical_and, [b <= SAFE_BOUND for b in bounds])

    @pl.when(safe)
    def _():
        finish(_softmax_pv(qps, k_at, v_ref, dv, kv_chunk, bounds))

    @pl.when(jnp.logical_not(safe))
    def _():
        finish(_softmax_pv(qps, k_at, v_ref, dv, kv_chunk))


def _attn_params():
    return pltpu.CompilerParams(
        dimension_semantics=("parallel", "parallel", "parallel"), vmem_limit_bytes=VMEM_LIMIT)


def _attn_a_kernel(bnd_ref, lam_ref, q_ref, k_ref, v_ref, gate_ref, sg_ref, o_ref, *, lam_init):
    i, h = pl.program_id(0), pl.program_id(1)
    q = q_ref[0]
    tq = q.shape[1]
    z = jnp.zeros((A_QK_DIM, tq), BF16)
    qps = [jnp.concatenate([q[0:64], z], axis=0), jnp.concatenate([z, q[64:128]], axis=0)]

    def finish(res):
        (a0, l0), (a1, l1) = res
        lp = lam_ref[...]
        lam = (jnp.exp(jnp.sum(lp[0:1] * lp[1:2], axis=1, keepdims=True))
               - jnp.exp(jnp.sum(lp[2:3] * lp[3:4], axis=1, keepdims=True)) + lam_init)
        o = a0 / l0 - lam * (a1 / l1)
        ms = jnp.mean(o * o, axis=0, keepdims=True)
        o = o * lax.rsqrt(ms + RMS_EPS) * _rep(sg_ref, tq) * (1.0 - lam_init)
        o_ref[0] = (o * gate_ref[0]).astype(BF16)

    _attend(qps, [bnd_ref[i, 2 * h], bnd_ref[i, 2 * h + 1]],
            lambda r0: k_ref[0, 0, pl.ds(r0, MXU_TILE), :], v_ref, A_V_DIM, KV_CHUNK_A, finish)


def _attn_a(bnd, lam_p, qa, ka, va, gate, sg, lam_init):
    b = qa.shape[0]
    nt = SEQ // TM
    return pl.pallas_call(
        functools.partial(_attn_a_kernel, lam_init=lam_init),
        grid=(b, A_HEADS, SEQ // TQ_A),
        in_specs=[
            pl.BlockSpec(memory_space=pltpu.SMEM),
            pl.BlockSpec((4, A_QK_DIM), lambda i, h, t: (0, 0)),
            pl.BlockSpec((1, 128, TQ_A), lambda i, h, t: (i, h, t)),
            pl.BlockSpec((1, 1, SEQ, K_PAD), lambda i, h, t: (i, h, 0, 0)),
            pl.BlockSpec((1, nt, A_V_DIM, TM), lambda i, h, t: (i, 0, h, 0)),
            pl.BlockSpec((1, A_V_DIM, TQ_A), lambda i, h, t: (i, h, t)),
            pl.BlockSpec((A_V_DIM, LANES), lambda i, h, t: (0, 0)),
        ],
        out_specs=pl.BlockSpec((1, A_V_DIM, TQ_A), lambda i, h, t: (i, h, t)),
        out_shape=jax.ShapeDtypeStruct((b, A_WIDTH, SEQ), BF16),
        compiler_params=_attn_params(),
        name="attn_diff",
    )(bnd, lam_p, qa, ka, va, gate, sg)


def _attn_b_kernel(bnd_ref, q_ref, k_ref, v_ref, gate_ref, o_ref):
    i, h = pl.program_id(0), pl.program_id(1)
    first = h < B_GROUP
    q = q_ref[0]
    z = jnp.zeros_like(q)
    qp = jnp.concatenate([jnp.where(first, q, z), jnp.where(first, z, q)], axis=0)

    def finish(res):
        ((a, l),) = res
        o_ref[0] = (a / l * gate_ref[0]).astype(BF16)

    _attend([qp], [bnd_ref[i, 2 * A_HEADS + h]],
            lambda r0: k_ref[0, pl.ds(r0, MXU_TILE), :], v_ref, B_DIM, KV_CHUNK_BC, finish)


def _attn_b(bnd, qb, kb, vb, gate):
    b = qb.shape[0]
    nt = SEQ // TM
    return pl.pallas_call(
        _attn_b_kernel,
        grid=(b, B_Q_HEADS, SEQ // TQ_B),
        in_specs=[
            pl.BlockSpec(memory_space=pltpu.SMEM),
            pl.BlockSpec((1, B_DIM, TQ_B), lambda i, h, t: (i, h, t)),
            pl.BlockSpec((1, SEQ, K_PAD), lambda i, h, t: (i, 0, 0)),
            pl.BlockSpec((1, nt, B_DIM, TM), lambda i, h, t: (i, 0, h // B_GROUP, 0)),
            pl.BlockSpec((1, B_DIM, TQ_B), lambda i, h, t: (i, A_WIDTH // B_DIM + h, t)),
        ],
        out_specs=pl.BlockSpec((1, B_DIM, TQ_B), lambda i, h, t: (i, h, t)),
        out_shape=jax.ShapeDtypeStruct((b, B_WIDTH, SEQ), BF16),
        compiler_params=_attn_params(),
        name="attn_gqa",
    )(bnd, qb, kb, vb, gate)


def _attn_c_kernel(bnd_ref, q_ref, k_ref, v_ref, gate_ref, o_ref):
    i, h = pl.program_id(0), pl.program_id(1)
    q = q_ref[0]
    qp = jnp.concatenate([q, jnp.zeros((K_PAD - C_QK, q.shape[1]), BF16)], axis=0)

    def finish(res):
        ((a, l),) = res
        o_ref[0] = (a / l * gate_ref[0]).astype(BF16)

    _attend([qp], [bnd_ref[i, h]],
            lambda r0: k_ref[0, 0, pl.ds(r0, MXU_TILE), :], v_ref, C_V, KV_CHUNK_BC, finish)


def _attn_c(bnd, q, k, v, gate):
    b = q.shape[0]
    nt = SEQ // TM
    return pl.pallas_call(
        _attn_c_kernel,
        grid=(b, C_HEADS, SEQ // TQ_C),
        in_specs=[
            pl.BlockSpec(memory_space=pltpu.SMEM),
            pl.BlockSpec((1, C_QK, TQ_C), lambda i, h, t: (i, h, t)),
            pl.BlockSpec((1, 1, SEQ, K_PAD), lambda i, h, t: (i, h, 0, 0)),
            pl.BlockSpec((1, nt, C_V, TM), lambda i, h, t: (i, 0, h, 0)),
            pl.BlockSpec((1, C_V, TQ_C), lambda i, h, t: (i, h, t)),
        ],
        out_specs=pl.BlockSpec((1, C_V, TQ_C), lambda i, h, t: (i, h, t)),
        out_shape=jax.ShapeDtypeStruct((b, C_HEADS * C_V, SEQ), BF16),
        compiler_params=_attn_params(),
        name="attn_mla",
    )(bnd, q, k, v, gate)


def _out_body(og_refs, x_ref, w_ref, lg_ref, lb_ref, x_token_major):
    og = jnp.concatenate([r[0] for r in og_refs], axis=0) if len(og_refs) > 1 else og_refs[0][0]
    y = jnp.dot(w_ref[...], og, preferred_element_type=F32)
    x = x_ref[0].T if x_token_major else x_ref[0]
    z = ALPHA * x + y
    mu = jnp.mean(z, axis=0, keepdims=True)
    d = z - mu
    var = jnp.mean(d * d, axis=0, keepdims=True)
    return d * lax.rsqrt(var + LN_EPS) * _rep(lg_ref, TM) + _rep(lb_ref, TM)


def _out_kernel(*refs, n_og):
    x_ref, w_ref, lg_ref, lb_ref, o_ref = refs[n_og:]
    o_ref[0] = _out_body(refs[:n_og], x_ref, w_ref, lg_ref, lb_ref, False).T


def _out_in_kernel(*refs, n_og, n_in, x_token_major, in_body):
    x_ref, w_ref, lg_ref, lb_ref = refs[n_og:n_og + 4]
    in_params = refs[n_og + 4:n_og + 4 + n_in]
    xo_ref = refs[n_og + 4 + n_in]
    in_outs = refs[n_og + 5 + n_in:]
    out = _out_body(refs[:n_og], x_ref, w_ref, lg_ref, lb_ref, x_token_major)
    xo_ref[0] = out
    in_body(out.astype(BF16), *in_params, *in_outs)


def _out_specs(ogs, x_token_major):
    d = D_MODEL
    x_spec = (pl.BlockSpec((1, TM, d), lambda i, j: (i, j, 0)) if x_token_major
              else pl.BlockSpec((1, d, TM), _tok))
    return [pl.BlockSpec((1, og.shape[1], TM), _tok) for og in ogs] + [
        x_spec, _resident((d, d)), _resident((d, LANES)), _resident((d, LANES))]


def _out_proj_last(ogs, xt, w_t, lg, lb):
    b = xt.shape[0]
    return pl.pallas_call(
        functools.partial(_out_kernel, n_og=len(ogs)),
        grid=(b, SEQ // TM),
        in_specs=_out_specs(ogs, False),
        out_specs=pl.BlockSpec((1, TM, D_MODEL), lambda i, j: (i, j, 0)),
        out_shape=jax.ShapeDtypeStruct((b, SEQ, D_MODEL), F32),
        compiler_params=_proj_params(),
        name="out_proj_ln",
    )(*ogs, xt, w_t, lg, lb)


def _out_in(ogs, xt, w_t, lg, lb, in_body, in_specs_fn, in_args, x_token_major, name):
    b = xt.shape[0]
    in_specs, out_specs, out_shape = in_specs_fn(b)
    return pl.pallas_call(
        functools.partial(_out_in_kernel, n_og=len(ogs), n_in=len(in_args),
                          x_token_major=x_token_major, in_body=in_body),
        grid=(b, SEQ // TM),
        in_specs=_out_specs(ogs, x_token_major) + in_specs,
        out_specs=[pl.BlockSpec((1, D_MODEL, TM), _tok)] + out_specs,
        out_shape=[jax.ShapeDtypeStruct((b, D_MODEL, SEQ), F32)] + out_shape,
        compiler_params=_proj_params(),
        name=name,
    )(*ogs, xt, w_t, lg, lb, *in_args)


def _col(v):
    v = v.astype(F32)
    return jnp.broadcast_to(v[:, None], (v.shape[0], LANES))


def _score_bounds(nq, kmax2):
    return jnp.sqrt(jnp.max(nq, axis=-1) * kmax2) * BOUND_SLACK


def _angles_t(pos, dims, theta):
    inv = theta ** (-jnp.arange(0, dims, 2, dtype=F32) / dims)
    ang = pos.astype(F32)[:, None] * inv[None, :]
    return jnp.cos(ang).T, jnp.sin(ang).T


@jax.jit
def _forward(x, ev_w_in, ev_w_out, ev_lam, ev_subln, ev_qnorm, ev_knorm, ev_ln_g, ev_ln_b,
             od_w_in, od_qnorm, od_kvnorm, od_w_qb, od_w_kvb, od_w_out, od_ln_g, od_ln_b):
    s = x.shape[1]
    pos = jnp.arange(s, dtype=jnp.int32)
    row = jnp.repeat(jnp.arange(s // GRID_W, dtype=jnp.int32), GRID_W)
    col = jnp.tile(jnp.arange(GRID_W, dtype=jnp.int32), s // GRID_W)
    tab_ev = jnp.concatenate(
        _angles_t(pos, A_ROT, ROPE_THETA) + _angles_t(row, B_DIM // 2, AXIAL_THETA)
        + _angles_t(col, B_DIM // 2, AXIAL_THETA), axis=0)
    tab_od = jnp.concatenate(_angles_t(pos, C_ROPE, ROPE_THETA), axis=0)

    def even_args(i):
        return (ev_w_in[i].T.astype(BF16), tab_ev, _col(ev_qnorm[i]), _col(ev_knorm[i]))

    def odd_args(i):
        return (od_w_in[i].T.astype(BF16), od_w_qb[i].T.astype(BF16),
                od_w_kvb[i].T.astype(BF16), tab_od, _col(od_qnorm[i]), _col(od_kvnorm[i]))

    xt = x
    proj = _even_in(x, *even_args(0))
    for layer in range(DEPTH):
        i = layer // 2
        if layer % 2 == 0:
            qa, ka, va, qb, kb, vb, gate, nq, nk = proj
            kmax = jnp.max(nk, axis=-1)
            kmax = jnp.concatenate(
                [kmax[:, :2 * A_HEADS],
                 jnp.repeat(kmax[:, 2 * A_HEADS:2 * A_HEADS + B_KV_HEADS], B_GROUP, axis=1)],
                axis=1)
            bnd = _score_bounds(nq, kmax)
            lam_init = 0.8 - 0.6 * math.exp(-0.3 * layer)
            ogs = [_attn_a(bnd, ev_lam[i].astype(F32), qa, ka, va, gate, _col(ev_subln[i]),
                           lam_init),
                   _attn_b(bnd, qb, kb, vb, gate)]
            out_args = (ev_w_out[i].T.astype(BF16), _col(ev_ln_g[i]), _col(ev_ln_b[i]))
        else:
            q, k, v, gate, nq, nk = proj
            ogs = [_attn_c(_score_bounds(nq, jnp.max(nk, axis=-1)), q, k, v, gate)]
            out_args = (od_w_out[i].T.astype(BF16), _col(od_ln_g[i]), _col(od_ln_b[i]))
        if layer == DEPTH - 1:
            return _out_proj_last(ogs, xt, *out_args)
        nxt = (layer + 1) // 2
        if layer % 2 == 0:
            xt, *proj = _out_in(ogs, xt, *out_args, _odd_in_body, _odd_in_specs, odd_args(nxt),
                                layer == 0, "out_odd_in_proj")
        else:
            xt, *proj = _out_in(ogs, xt, *out_args, _even_in_body, _even_in_specs,
                                even_args(nxt), False, "out_even_in_proj")


def kernel(x, ev_w_in, ev_w_out, ev_lam, ev_subln, ev_qnorm, ev_knorm, ev_ln_g, ev_ln_b,
           od_w_in, od_qnorm, od_kvnorm, od_w_qb, od_w_kvb, od_w_out, od_ln_g, od_ln_b):
    return _forward(x, ev_w_in, ev_w_out, ev_lam, ev_subln, ev_qnorm, ev_knorm, ev_ln_g,
                    ev_ln_b, od_w_in, od_qnorm, od_kvnorm, od_w_qb, od_w_kvb, od_w_out,
                    od_ln_g, od_ln_b)
```

```python
import functools
import math

import jax
import jax.numpy as jnp
from jax import lax
from jax.experimental import pallas as pl
from jax.experimental.pallas import tpu as pltpu

F32 = jnp.float32
BF16 = jnp.bfloat16

D_MODEL = 1024
SEQ = 4096
DEPTH = 4
GRID_W = 64
ROPE_THETA = 500000.0
AXIAL_THETA = 10000.0
LN_EPS = 1e-5
RMS_EPS = 1e-6

A_HEADS = 4
A_QK_DIM = 64
A_V_DIM = 128
A_WIDTH = 512
A_ROT = 16
B_Q_HEADS = 8
B_KV_HEADS = 2
B_GROUP = 4
B_DIM = 64
B_WIDTH = 512
EV_IN = 3328

C_HEADS = 16
C_NOPE = 64
C_ROPE = 32
C_V = 64
C_Q_LORA = 256
C_KV_LORA = 128
C_QK = C_NOPE + C_ROPE
OD_IN = 1440

ALPHA = (2 * DEPTH) ** 0.25
LOG2E = 1.4426950408889634

QSCALE_AB = A_QK_DIM ** -0.5 * LOG2E
QSCALE_C = C_QK ** -0.5 * LOG2E

LANES = 128
MXU_TILE = 256
LOOKAHEAD_ITEMS = {1: 2, 2: 3, 4: 8}
MAX_LOCKSTEP = 4
ONES_ROWS = 16
K_PAD = 128
NORM_ROWS = 16
SAFE_BOUND = 60.0
BOUND_SLACK = 1.02
TM = 512
SUB_TILES = 2
EVEN_TAB_POS, EVEN_OUT_TOKEN_AXES = 1, (2, 2, 3, 2, 1, 3, 2, 2, 2)
ODD_TAB_POS, ODD_OUT_TOKEN_AXES = 3, (2, 2, 3, 2, 2, 2)
KV_CHUNK_A = 512
KV_CHUNK_BC = 256
TQ_A = 1024
TQ_B = 2048
TQ_C = 2048
VMEM_LIMIT = 56 * 1024 * 1024


def _rep(ref, n):
    a = ref[...]
    return jnp.concatenate([a] * (n // LANES), axis=1)


def _rot(x1, x2, cos, sin):
    return x1 * cos - x2 * sin, x2 * cos + x1 * sin


def _silu(x):
    return x * jax.nn.sigmoid(x)


def _sumsq(x):
    return jnp.sum(x * x, axis=0, keepdims=True)


def _tok(i, j):
    return (i, 0, j)


def _const2(i, j):
    return (0, 0)


def _resident(shape):
    return pl.BlockSpec(shape, _const2, pipeline_mode=pl.Buffered(1))


def _proj_params():
    return pltpu.CompilerParams(
        dimension_semantics=("parallel", "parallel"), vmem_limit_bytes=VMEM_LIMIT)


def _even_in_body(xb, w_ref, tab_ref, qn_ref, kn_ref,
                  qa_ref, ka_ref, va_ref, qb_ref, kb_ref, vb_ref, g_ref, nq_ref, nk_ref,
                  *, x_token_major=False):
    contract = (((1,), (1 if x_token_major else 0,)), ((), ()))

    def proj(r0, r1):
        return lax.dot_general(w_ref[r0:r1, :], xb, contract, preferred_element_type=F32)

    cos_a, sin_a = tab_ref[0:8, :], tab_ref[8:16, :]
    cos_r, sin_r = tab_ref[16:32, :], tab_ref[32:48, :]
    cos_c, sin_c = tab_ref[48:64, :], tab_ref[64:80, :]

    def rope_a(h):
        outs = []
        for hc in range(2 * A_HEADS):
            b = hc * A_QK_DIM
            r1, r2 = _rot(h[b:b + 8], h[b + 8:b + 16], cos_a, sin_a)
            outs.append(jnp.concatenate([r1, r2, h[b + 16:b + 64]], axis=0))
        return outs

    g_ref[0] = _silu(proj(2304, 3328)).astype(BF16)
    qs = [x * QSCALE_AB for x in rope_a(proj(0, 512))]
    qa_ref[0] = jnp.concatenate(qs, axis=0).astype(BF16)
    ks = rope_a(proj(512, 1024))
    nq = [_sumsq(x) for x in qs]
    nk = [_sumsq(x) for x in ks]
    for h in range(A_HEADS):
        kt = jnp.concatenate([ks[2 * h], ks[2 * h + 1]], axis=0)
        ka_ref[0, h] = kt.T.astype(BF16)

    def norm_axial(h, g):
        ms = jnp.mean(h * h, axis=0, keepdims=True)
        y = h * lax.rsqrt(ms + RMS_EPS) * g
        a1, a2 = _rot(y[0:16], y[16:32], cos_r, sin_r)
        b1, b2 = _rot(y[32:48], y[48:64], cos_c, sin_c)
        return jnp.concatenate([a1, a2, b1, b2], axis=0)

    width = xb.shape[0 if x_token_major else 1]
    qn = _rep(qn_ref, width)
    kn = _rep(kn_ref, width)
    hq = proj(1536, 2048)
    qs = [norm_axial(hq[h * 64:(h + 1) * 64], qn) * QSCALE_AB for h in range(B_Q_HEADS)]
    qb_ref[0] = jnp.concatenate(qs, axis=0).astype(BF16)
    hkv = proj(2048, 2304)
    ks = [norm_axial(hkv[h * 64:(h + 1) * 64], kn) for h in range(B_KV_HEADS)]
    kb_ref[0] = jnp.concatenate(ks, axis=0).T.astype(BF16)
    nq_ref[0] = jnp.concatenate(nq + [_sumsq(x) for x in qs], axis=0)
    nk = nk + [_sumsq(x) for x in ks]
    nk_ref[0] = jnp.concatenate(nk + [jnp.zeros_like(nk[0])] * (NORM_ROWS - len(nk)), axis=0)
    vb_ref[0, 0] = hkv[128:256].astype(BF16)
    va_ref[0, 0] = proj(1024, 1536).astype(BF16)


def _even_in_kernel(x_ref, *refs):
    n_in = len(refs) - len(EVEN_OUT_TOKEN_AXES)
    width = TM // SUB_TILES
    for t in range(SUB_TILES):
        off = t * width
        params = [_tokens(r, 1, off, width) if p == EVEN_TAB_POS else r
                  for p, r in enumerate(refs[:n_in])]
        views = [_tokens(r, ax, off, width) for r, ax in zip(refs[n_in:], EVEN_OUT_TOKEN_AXES)]
        _even_in_body(x_ref[0, pl.ds(off, width), :].astype(BF16), *params, *views,
                      x_token_major=True)


def _even_in_specs(b):
    d, s = D_MODEL, SEQ
    nt = s // TM
    tok = _tok
    chunk = lambda i, j: (i, j, 0, 0)
    return (
        [
            _resident((EV_IN, d)),
            pl.BlockSpec((80, TM), lambda i, j: (0, j)),
            _resident((B_DIM, LANES)),
            _resident((B_DIM, LANES)),
        ],
        [
            pl.BlockSpec((1, 512, TM), tok),
            pl.BlockSpec((1, A_HEADS, TM, K_PAD), lambda i, j: (i, 0, j, 0)),
            pl.BlockSpec((1, 1, 512, TM), chunk),
            pl.BlockSpec((1, 512, TM), tok),
            pl.BlockSpec((1, TM, K_PAD), lambda i, j: (i, j, 0)),
            pl.BlockSpec((1, 1, 128, TM), chunk),
            pl.BlockSpec((1, 1024, TM), tok),
            pl.BlockSpec((1, NORM_ROWS, TM), tok),
            pl.BlockSpec((1, NORM_ROWS, TM), tok),
        ],
        [
            jax.ShapeDtypeStruct((b, 512, s), BF16),
            jax.ShapeDtypeStruct((b, A_HEADS, s, K_PAD), BF16),
            jax.ShapeDtypeStruct((b, nt, 512, TM), BF16),
            jax.ShapeDtypeStruct((b, 512, s), BF16),
            jax.ShapeDtypeStruct((b, s, K_PAD), BF16),
            jax.ShapeDtypeStruct((b, nt, 128, TM), BF16),
            jax.ShapeDtypeStruct((b, 1024, s), BF16),
            jax.ShapeDtypeStruct((b, NORM_ROWS, s), F32),
            jax.ShapeDtypeStruct((b, NORM_ROWS, s), F32),
        ],
    )


def _even_in(x, w_t, tab, qn, kn):
    b = x.shape[0]
    in_specs, out_specs, out_shape = _even_in_specs(b)
    return pl.pallas_call(
        _even_in_kernel,
        grid=(b, SEQ // TM),
        in_specs=[pl.BlockSpec((1, TM, D_MODEL), lambda i, j: (i, j, 0))] + in_specs,
        out_specs=out_specs,
        out_shape=out_shape,
        compiler_params=_proj_params(),
        name="even_in_proj",
    )(x, w_t, tab, qn, kn)


def _odd_in_body(xb, w_ref, wq_ref, wkv_ref, tab_ref, qn_ref, kvn_ref,
                 q_ref, k_ref, v_ref, g_ref, nq_ref, nk_ref):
    cos, sin = tab_ref[0:16, :], tab_ref[16:32, :]

    def rms(h, g):
        ms = jnp.mean(h * h, axis=0, keepdims=True)
        return h * lax.rsqrt(ms + RMS_EPS) * g

    g_ref[0] = _silu(jnp.dot(w_ref[416:1440, :], xb, preferred_element_type=F32)).astype(BF16)
    lat = jnp.dot(w_ref[0:416, :], xb, preferred_element_type=F32)
    width = xb.shape[1]
    cqn = rms(lat[0:256], _rep(qn_ref, width)).astype(BF16)
    q = jnp.dot(wq_ref[...], cqn, preferred_element_type=F32)
    qs = []
    for h in range(C_HEADS):
        b = h * C_QK
        r1, r2 = _rot(q[b + 64:b + 80], q[b + 80:b + 96], cos, sin)
        qs.append(jnp.concatenate([q[b:b + 64], r1, r2], axis=0) * QSCALE_C)
    q_ref[0] = jnp.concatenate(qs, axis=0).astype(BF16)
    nq_ref[0] = jnp.concatenate([_sumsq(x) for x in qs], axis=0)

    ckvn = rms(lat[256:384], _rep(kvn_ref, width)).astype(BF16)
    kv = jnp.dot(wkv_ref[...], ckvn, preferred_element_type=F32)
    r1, r2 = _rot(lat[384:400], lat[400:416], cos, sin)
    zpad = jnp.zeros((K_PAD - C_QK, width), F32)
    nkr = _sumsq(r1) + _sumsq(r2)
    nk = []
    for h in range(C_HEADS):
        kn = kv[h * 128:h * 128 + 64]
        nk.append(_sumsq(kn) + nkr)
        kt = jnp.concatenate([kn, r1, r2, zpad], axis=0)
        k_ref[0, h] = kt.T.astype(BF16)
    nk_ref[0] = jnp.concatenate(nk, axis=0)
    v_ref[0, 0] = jnp.concatenate(
        [kv[h * 128 + 64:h * 128 + 128] for h in range(C_HEADS)], axis=0).astype(BF16)


def _odd_in_specs(b):
    d, s = D_MODEL, SEQ
    nt = s // TM
    tok = _tok
    return (
        [
            _resident((OD_IN, d)),
            _resident((C_HEADS * C_QK, C_Q_LORA)),
            _resident((C_HEADS * 128, C_KV_LORA)),
            pl.BlockSpec((32, TM), lambda i, j: (0, j)),
            _resident((C_Q_LORA, LANES)),
            _resident((C_KV_LORA, LANES)),
        ],
        [
            pl.BlockSpec((1, C_HEADS * C_QK, TM), tok),
            pl.BlockSpec((1, C_HEADS, TM, K_PAD), lambda i, j: (i, 0, j, 0)),
            pl.BlockSpec((1, 1, 1024, TM), lambda i, j: (i, j, 0, 0)),
            pl.BlockSpec((1, 1024, TM), tok),
            pl.BlockSpec((1, NORM_ROWS, TM), tok),
            pl.BlockSpec((1, NORM_ROWS, TM), tok),
        ],
        [
            jax.ShapeDtypeStruct((b, C_HEADS * C_QK, s), BF16),
            jax.ShapeDtypeStruct((b, C_HEADS, s, K_PAD), BF16),
            jax.ShapeDtypeStruct((b, nt, 1024, TM), BF16),
            jax.ShapeDtypeStruct((b, 1024, s), BF16),
            jax.ShapeDtypeStruct((b, NORM_ROWS, s), F32),
            jax.ShapeDtypeStruct((b, NORM_ROWS, s), F32),
        ],
    )


def _softmax_pv(qps, k_at, v_ref, dv, kv_chunk, bounds=None):
    nsets = len(qps)
    ntiles = qps[0].shape[1] // MXU_TILE
    qps = [qp[:, t * MXU_TILE:(t + 1) * MXU_TILE] for qp in qps for t in range(ntiles)]
    n = len(qps)
    nchunks = SEQ // kv_chunk
    halves = kv_chunk // MXU_TILE
    ones = jnp.ones((ONES_ROWS, kv_chunk), BF16)

    def scores(ci, j):
        return [jnp.dot(k_at(ci * kv_chunk + h * MXU_TILE), qps[j], preferred_element_type=F32)
                for h in range(halves)]

    m = [None] * n
    acc = [None] * n
    items = [(ci, j) for g in range(0, n, MAX_LOCKSTEP) for ci in range(nchunks)
             for j in range(g, min(g + MAX_LOCKSTEP, n))]
    lookahead = LOOKAHEAD_ITEMS[min(n, MAX_LOCKSTEP)]
    pending = [scores(*it) for it in items[:lookahead]]
    for idx, (ci, j) in enumerate(items):
        if idx + lookahead < len(items):
            pending.append(scores(*items[idx + lookahead]))
        sc = pending.pop(0)
        r0 = ci * kv_chunk
        v = jnp.concatenate([v_ref[0, r0 // TM, :, pl.ds(r0 % TM, kv_chunk)], ones],
                            axis=0)
        if bounds is None:
            cm = functools.reduce(jnp.maximum, [jnp.max(s, axis=0, keepdims=True) for s in sc])
            ref = cm if ci == 0 else jnp.maximum(m[j], cm)
        else:
            ref = bounds[j // ntiles]
        pv = functools.reduce(jnp.add, [
            jnp.dot(v[:, h * MXU_TILE:(h + 1) * MXU_TILE],
                    jnp.exp2(sc[h] - ref).astype(BF16), preferred_element_type=F32)
            for h in range(halves)])
        if ci == 0:
            acc[j] = pv
        elif bounds is None:
            acc[j] = acc[j] * jnp.exp2(m[j] - ref) + pv
        else:
            acc[j] = acc[j] + pv
        m[j] = ref
    out = []
    for j in range(nsets):
        a = jnp.concatenate(acc[j * ntiles:(j + 1) * ntiles], axis=1)
        out.append((a[0:dv], a[dv:dv + 1]))
    return out


def _attend(qps, bounds, k_at, v_ref, dv, kv_chunk, finish):
    safe = functools.reduce(jnp.logical_and, [b <= SAFE_BOUND for b in bounds])

    @pl.when(safe)
    def _():
        finish(_softmax_pv(qps, k_at, v_ref, dv, kv_chunk, bounds))

    @pl.when(jnp.logical_not(safe))
    def _():
        finish(_softmax_pv(qps, k_at, v_ref, dv, kv_chunk))


def _attn_params():
    return pltpu.CompilerParams(
        dimension_semantics=("parallel", "parallel", "parallel"), vmem_limit_bytes=VMEM_LIMIT)


def _attn_a_kernel(bnd_ref, lam_ref, q_ref, k_ref, v_ref, gate_ref, sg_ref, o_ref, *, lam_init):
    i, h = pl.program_id(0), pl.program_id(1)
    q = q_ref[0]
    tq = q.shape[1]
    z = jnp.zeros((A_QK_DIM, tq), BF16)
    qps = [jnp.concatenate([q[0:64], z], axis=0), jnp.concatenate([z, q[64:128]], axis=0)]

    def finish(res):
        (a0, l0), (a1, l1) = res
        lp = lam_ref[...]
        lam = (jnp.exp(jnp.sum(lp[0:1] * lp[1:2], axis=1, keepdims=True))
               - jnp.exp(jnp.sum(lp[2:3] * lp[3:4], axis=1, keepdims=True)) + lam_init)
        o = a0 / l0 - lam * (a1 / l1)
        ms = jnp.mean(o * o, axis=0, keepdims=True)
        o = o * lax.rsqrt(ms + RMS_EPS) * _rep(sg_ref, tq) * (1.0 - lam_init)
        o_ref[0] = (o * gate_ref[0]).astype(BF16)

    _attend(qps, [bnd_ref[i, 2 * h], bnd_ref[i, 2 * h + 1]],
            lambda r0: k_ref[0, 0, pl.ds(r0, MXU_TILE), :], v_ref, A_V_DIM, KV_CHUNK_A, finish)


def _attn_a(bnd, lam_p, qa, ka, va, gate, sg, lam_init):
    b = qa.shape[0]
    nt = SEQ // TM
    return pl.pallas_call(
        functools.partial(_attn_a_kernel, lam_init=lam_init),
        grid=(b, A_HEADS, SEQ // TQ_A),
        in_specs=[
            pl.BlockSpec(memory_space=pltpu.SMEM),
            pl.BlockSpec((4, A_QK_DIM), lambda i, h, t: (0, 0)),
            pl.BlockSpec((1, 128, TQ_A), lambda i, h, t: (i, h, t)),
            pl.BlockSpec((1, 1, SEQ, K_PAD), lambda i, h, t: (i, h, 0, 0)),
            pl.BlockSpec((1, nt, A_V_DIM, TM), lambda i, h, t: (i, 0, h, 0)),
            pl.BlockSpec((1, A_V_DIM, TQ_A), lambda i, h, t: (i, h, t)),
            pl.BlockSpec((A_V_DIM, LANES), lambda i, h, t: (0, 0)),
        ],
        out_specs=pl.BlockSpec((1, A_V_DIM, TQ_A), lambda i, h, t: (i, h, t)),
        out_shape=jax.ShapeDtypeStruct((b, A_WIDTH, SEQ), BF16),
        compiler_params=_attn_params(),
        name="attn_diff",
    )(bnd, lam_p, qa, ka, va, gate, sg)


def _attn_b_kernel(bnd_ref, q_ref, k_ref, v_ref, gate_ref, o_ref):
    i, h = pl.program_id(0), pl.program_id(1)
    first = h < B_GROUP
    q = q_ref[0]
    z = jnp.zeros_like(q)
    qp = jnp.concatenate([jnp.where(first, q, z), jnp.where(first, z, q)], axis=0)

    def finish(res):
        ((a, l),) = res
        o_ref[0] = (a / l * gate_ref[0]).astype(BF16)

    _attend([qp], [bnd_ref[i, 2 * A_HEADS + h]],
            lambda r0: k_ref[0, pl.ds(r0, MXU_TILE), :], v_ref, B_DIM, KV_CHUNK_BC, finish)


def _attn_b(bnd, qb, kb, vb, gate):
    b = qb.shape[0]
    nt = SEQ // TM
    return pl.pallas_call(
        _attn_b_kernel,
        grid=(b, B_Q_HEADS, SEQ // TQ_B),
        in_specs=[
            pl.BlockSpec(memory_space=pltpu.SMEM),
            pl.BlockSpec((1, B_DIM, TQ_B), lambda i, h, t: (i, h, t)),
            pl.BlockSpec((1, SEQ, K_PAD), lambda i, h, t: (i, 0, 0)),
            pl.BlockSpec((1, nt, B_DIM, TM), lambda i, h, t: (i, 0, h // B_GROUP, 0)),
            pl.BlockSpec((1, B_DIM, TQ_B), lambda i, h, t: (i, A_WIDTH // B_DIM + h, t)),
        ],
        out_specs=pl.BlockSpec((1, B_DIM, TQ_B), lambda i, h, t: (i, h, t)),
        out_shape=jax.ShapeDtypeStruct((b, B_WIDTH, SEQ), BF16),
        compiler_params=_attn_params(),
        name="attn_gqa",
    )(bnd, qb, kb, vb, gate)


def _attn_c_kernel(bnd_ref, q_ref, k_ref, v_ref, gate_ref, o_ref):
    i, h = pl.program_id(0), pl.program_id(1)
    q = q_ref[0]
    qp = jnp.concatenate([q, jnp.zeros((K_PAD - C_QK, q.shape[1]), BF16)], axis=0)

    def finish(res):
        ((a, l),) = res
        o_ref[0] = (a / l * gate_ref[0]).astype(BF16)

    _attend([qp], [bnd_ref[i, h]],
            lambda r0: k_ref[0, 0, pl.ds(r0, MXU_TILE), :], v_ref, C_V, KV_CHUNK_BC, finish)


def _attn_c(bnd, q, k, v, gate):
    b = q.shape[0]
    nt = SEQ // TM
    return pl.pallas_call(
        _attn_c_kernel,
        grid=(b, C_HEADS, SEQ // TQ_C),
        in_specs=[
            pl.BlockSpec(memory_space=pltpu.SMEM),
            pl.BlockSpec((1, C_QK, TQ_C), lambda i, h, t: (i, h, t)),
            pl.BlockSpec((1, 1, SEQ, K_PAD), lambda i, h, t: (i, h, 0, 0)),
            pl.BlockSpec((1, nt, C_V, TM), lambda i, h, t: (i, 0, h, 0)),
            pl.BlockSpec((1, C_V, TQ_C), lambda i, h, t: (i, h, t)),
        ],
        out_specs=pl.BlockSpec((1, C_V, TQ_C), lambda i, h, t: (i, h, t)),
        out_shape=jax.ShapeDtypeStruct((b, C_HEADS * C_V, SEQ), BF16),
        compiler_params=_attn_params(),
        name="attn_mla",
    )(bnd, q, k, v, gate)


def _out_body(og_refs, x_ref, w_ref, lg_ref, lb_ref, x_token_major):
    og = jnp.concatenate([r[0] for r in og_refs], axis=0) if len(og_refs) > 1 else og_refs[0][0]
    y = jnp.dot(w_ref[...], og, preferred_element_type=F32)
    x = x_ref[0].T if x_token_major else x_ref[0]
    z = ALPHA * x + y
    mu = jnp.mean(z, axis=0, keepdims=True)
    d = z - mu
    var = jnp.mean(d * d, axis=0, keepdims=True)
    width = y.shape[1]
    return d * lax.rsqrt(var + LN_EPS) * _rep(lg_ref, width) + _rep(lb_ref, width)


def _out_kernel(*refs, n_og):
    x_ref, w_ref, lg_ref, lb_ref, o_ref = refs[n_og:]
    width = TM // SUB_TILES
    for t in range(SUB_TILES):
        off = t * width
        og_views = [_tokens(r, 2, off, width) for r in refs[:n_og]]
        out = _out_body(og_views, _tokens(x_ref, 2, off, width), w_ref, lg_ref, lb_ref, False)
        o_ref[0, pl.ds(off, width), :] = out.T


def _tokens(ref, axis, off, width):
    idx = [slice(None)] * len(ref.shape)
    idx[axis] = pl.ds(off, width)
    return ref.at[tuple(idx)]


def _out_in_kernel(*refs, n_og, n_in, x_token_major, in_body, tab_pos, out_axes):
    x_ref, w_ref, lg_ref, lb_ref = refs[n_og:n_og + 4]
    in_params = refs[n_og + 4:n_og + 4 + n_in]
    xo_ref = refs[n_og + 4 + n_in]
    in_outs = refs[n_og + 5 + n_in:]
    width = TM // SUB_TILES
    outs = []
    for t in range(SUB_TILES):
        off = t * width
        og_views = [_tokens(r, 2, off, width) for r in refs[:n_og]]
        x_view = _tokens(x_ref, 1 if x_token_major else 2, off, width)
        out = _out_body(og_views, x_view, w_ref, lg_ref, lb_ref, x_token_major)
        xo_ref[0, :, pl.ds(off, width)] = out
        outs.append(out.astype(BF16))
    for t in range(SUB_TILES):
        off = t * width
        params = [_tokens(r, 1, off, width) if p == tab_pos else r
                  for p, r in enumerate(in_params)]
        views = [_tokens(r, ax, off, width) for r, ax in zip(in_outs, out_axes)]
        in_body(outs[t], *params, *views)


def _out_specs(ogs, x_token_major):
    d = D_MODEL
    x_spec = (pl.BlockSpec((1, TM, d), lambda i, j: (i, j, 0)) if x_token_major
              else pl.BlockSpec((1, d, TM), _tok))
    return [pl.BlockSpec((1, og.shape[1], TM), _tok) for og in ogs] + [
        x_spec, _resident((d, d)), _resident((d, LANES)), _resident((d, LANES))]


def _out_proj_last(ogs, xt, w_t, lg, lb):
    b = xt.shape[0]
    return pl.pallas_call(
        functools.partial(_out_kernel, n_og=len(ogs)),
        grid=(b, SEQ // TM),
        in_specs=_out_specs(ogs, False),
        out_specs=pl.BlockSpec((1, TM, D_MODEL), lambda i, j: (i, j, 0)),
        out_shape=jax.ShapeDtypeStruct((b, SEQ, D_MODEL), F32),
        compiler_params=_proj_params(),
        name="out_proj_ln",
    )(*ogs, xt, w_t, lg, lb)


def _out_in(ogs, xt, w_t, lg, lb, next_even, in_args, x_token_major):
    b = xt.shape[0]
    if next_even:
        in_body, (in_specs, out_specs, out_shape) = _even_in_body, _even_in_specs(b)
        tab_pos, out_axes, name = EVEN_TAB_POS, EVEN_OUT_TOKEN_AXES, "out_even_in_proj"
    else:
        in_body, (in_specs, out_specs, out_shape) = _odd_in_body, _odd_in_specs(b)
        tab_pos, out_axes, name = ODD_TAB_POS, ODD_OUT_TOKEN_AXES, "out_odd_in_proj"
    return pl.pallas_call(
        functools.partial(_out_in_kernel, n_og=len(ogs), n_in=len(in_args),
                          x_token_major=x_token_major, in_body=in_body, tab_pos=tab_pos,
                          out_axes=out_axes),
        grid=(b, SEQ // TM),
        in_specs=_out_specs(ogs, x_token_major) + in_specs,
        out_specs=[pl.BlockSpec((1, D_MODEL, TM), _tok)] + out_specs,
        out_shape=[jax.ShapeDtypeStruct((b, D_MODEL, SEQ), F32)] + out_shape,
        compiler_params=_proj_params(),
        name=name,
    )(*ogs, xt, w_t, lg, lb, *in_args)


def _col(v):
    v = v.astype(F32)
    return jnp.broadcast_to(v[:, None], (v.shape[0], LANES))


def _score_bounds(nq, kmax2):
    return jnp.sqrt(jnp.max(nq, axis=-1) * kmax2) * BOUND_SLACK


def _angles_t(pos, dims, theta):
    inv = theta ** (-jnp.arange(0, dims, 2, dtype=F32) / dims)
    ang = pos.astype(F32)[:, None] * inv[None, :]
    return jnp.cos(ang).T, jnp.sin(ang).T


@jax.jit
def _forward(x, ev_w_in, ev_w_out, ev_lam, ev_subln, ev_qnorm, ev_knorm, ev_ln_g, ev_ln_b,
             od_w_in, od_qnorm, od_kvnorm, od_w_qb, od_w_kvb, od_w_out, od_ln_g, od_ln_b):
    s = x.shape[1]
    pos = jnp.arange(s, dtype=jnp.int32)
    row = jnp.repeat(jnp.arange(s // GRID_W, dtype=jnp.int32), GRID_W)
    col = jnp.tile(jnp.arange(GRID_W, dtype=jnp.int32), s // GRID_W)
    tab_ev = jnp.concatenate(
        _angles_t(pos, A_ROT, ROPE_THETA) + _angles_t(row, B_DIM // 2, AXIAL_THETA)
        + _angles_t(col, B_DIM // 2, AXIAL_THETA), axis=0)
    tab_od = jnp.concatenate(_angles_t(pos, C_ROPE, ROPE_THETA), axis=0)

    def even_args(i):
        return (ev_w_in[i].T.astype(BF16), tab_ev, _col(ev_qnorm[i]), _col(ev_knorm[i]))

    def odd_args(i):
        return (od_w_in[i].T.astype(BF16), od_w_qb[i].T.astype(BF16),
                od_w_kvb[i].T.astype(BF16), tab_od, _col(od_qnorm[i]), _col(od_kvnorm[i]))

    xt = x
    proj = _even_in(x, *even_args(0))
    for layer in range(DEPTH):
        i = layer // 2
        if layer % 2 == 0:
            qa, ka, va, qb, kb, vb, gate, nq, nk = proj
            kmax = jnp.max(nk, axis=-1)
            kmax = jnp.concatenate(
                [kmax[:, :2 * A_HEADS],
                 jnp.repeat(kmax[:, 2 * A_HEADS:2 * A_HEADS + B_KV_HEADS], B_GROUP, axis=1)],
                axis=1)
            bnd = _score_bounds(nq, kmax)
            lam_init = 0.8 - 0.6 * math.exp(-0.3 * layer)
            ogs = [_attn_a(bnd, ev_lam[i].astype(F32), qa, ka, va, gate, _col(ev_subln[i]),
                           lam_init),
                   _attn_b(bnd, qb, kb, vb, gate)]
            out_args = (ev_w_out[i].T.astype(BF16), _col(ev_ln_g[i]), _col(ev_ln_b[i]))
        else:
            q, k, v, gate, nq, nk = proj
            ogs = [_attn_c(_score_bounds(nq, jnp.max(nk, axis=-1)), q, k, v, gate)]
            out_args = (od_w_out[i].T.astype(BF16), _col(od_ln_g[i]), _col(od_ln_b[i]))
        if layer == DEPTH - 1:
            return _out_proj_last(ogs, xt, *out_args)
        nxt = (layer + 1) // 2
        if layer % 2 == 0:
            xt, *proj = _out_in(ogs, xt, *out_args, False, odd_args(nxt), layer == 0)
        else:
            xt, *proj = _out_in(ogs, xt, *out_args, True, even_args(nxt), False)


def kernel(x, ev_w_in, ev_w_out, ev_lam, ev_subln, ev_qnorm, ev_knorm, ev_ln_g, ev_ln_b,
           od_w_in, od_qnorm, od_kvnorm, od_w_qb, od_w_kvb, od_w_out, od_ln_g, od_ln_b):
    return _forward(x, ev_w_in, ev_w_out, ev_lam, ev_subln, ev_qnorm, ev_knorm, ev_ln_g,
                    ev_ln_b, od_w_in, od_qnorm, od_kvnorm, od_w_qb, od_w_kvb, od_w_out,
                    od_ln_g, od_ln_b)
```

```python
import functools
import math

import jax
import jax.numpy as jnp
from jax import lax
from jax.experimental import pallas as pl
from jax.experimental.pallas import tpu as pltpu

F32 = jnp.float32
BF16 = jnp.bfloat16

D_MODEL = 1024
SEQ = 4096
DEPTH = 4
GRID_W = 64
ROPE_THETA = 500000.0
AXIAL_THETA = 10000.0
LN_EPS = 1e-5
RMS_EPS = 1e-6

A_HEADS = 4
A_QK_DIM = 64
A_V_DIM = 128
A_WIDTH = 512
A_ROT = 16
B_Q_HEADS = 8
B_KV_HEADS = 2
B_GROUP = 4
B_DIM = 64
B_WIDTH = 512
EV_IN = 3328

C_HEADS = 16
C_NOPE = 64
C_ROPE = 32
C_V = 64
C_Q_LORA = 256
C_KV_LORA = 128
C_QK = C_NOPE + C_ROPE
OD_IN = 1440

ALPHA = (2 * DEPTH) ** 0.25
LOG2E = 1.4426950408889634

QSCALE_AB = A_QK_DIM ** -0.5 * LOG2E
QSCALE_C = C_QK ** -0.5 * LOG2E

LANES = 128
MXU_TILE = 256
LOOKAHEAD_ITEMS = {1: 2, 2: 3, 4: 8}
MAX_LOCKSTEP = 4
ONES_ROWS = 16
K_PAD = 128
NORM_ROWS = 16
SAFE_BOUND = 60.0
BOUND_SLACK = 1.02
TM = 512
SUB_TILES = 2
EVEN_TAB_POS, EVEN_OUT_TOKEN_AXES = 1, (2, 2, 3, 2, 1, 3, 2, 2, 2)
ODD_TAB_POS, ODD_OUT_TOKEN_AXES = 3, (2, 2, 3, 2, 2, 2)
KV_CHUNK_A = 512
KV_CHUNK_BC = 256
TQ_A = 2048
TQ_B = 4096
TQ_C = 4096
VMEM_LIMIT = 56 * 1024 * 1024


def _rep(ref, n):
    a = ref[...]
    return jnp.concatenate([a] * (n // LANES), axis=1)


def _rot(x1, x2, cos, sin):
    return x1 * cos - x2 * sin, x2 * cos + x1 * sin


def _silu(x):
    return x * jax.nn.sigmoid(x)


def _sumsq(x):
    return jnp.sum(x * x, axis=0, keepdims=True)


def _tok(i, j):
    return (i, 0, j)


def _const2(i, j):
    return (0, 0)


def _resident(shape):
    return pl.BlockSpec(shape, _const2, pipeline_mode=pl.Buffered(1))


def _proj_params():
    return pltpu.CompilerParams(
        dimension_semantics=("parallel", "parallel"), vmem_limit_bytes=VMEM_LIMIT)


def _even_in_body(xb, w_ref, tab_ref, qn_ref, kn_ref,
                  qa_ref, ka_ref, va_ref, qb_ref, kb_ref, vb_ref, g_ref, nq_ref, nk_ref,
                  *, x_token_major=False):
    contract = (((1,), (1 if x_token_major else 0,)), ((), ()))

    def proj(r0, r1):
        return lax.dot_general(w_ref[r0:r1, :], xb, contract, preferred_element_type=F32)

    cos_a, sin_a = tab_ref[0:8, :], tab_ref[8:16, :]
    cos_r, sin_r = tab_ref[16:32, :], tab_ref[32:48, :]
    cos_c, sin_c = tab_ref[48:64, :], tab_ref[64:80, :]

    def rope_a(h):
        outs = []
        for hc in range(2 * A_HEADS):
            b = hc * A_QK_DIM
            r1, r2 = _rot(h[b:b + 8], h[b + 8:b + 16], cos_a, sin_a)
            outs.append(jnp.concatenate([r1, r2, h[b + 16:b + 64]], axis=0))
        return outs

    g_ref[0] = _silu(proj(2304, 3328)).astype(BF16)
    qs = [x * QSCALE_AB for x in rope_a(proj(0, 512))]
    qa_ref[0] = jnp.concatenate(qs, axis=0).astype(BF16)
    ks = rope_a(proj(512, 1024))
    nq = [_sumsq(x) for x in qs]
    nk = [_sumsq(x) for x in ks]
    for h in range(A_HEADS):
        kt = jnp.concatenate([ks[2 * h], ks[2 * h + 1]], axis=0)
        ka_ref[0, h] = kt.T.astype(BF16)

    def norm_axial(h, g):
        ms = jnp.mean(h * h, axis=0, keepdims=True)
        y = h * lax.rsqrt(ms + RMS_EPS) * g
        a1, a2 = _rot(y[0:16], y[16:32], cos_r, sin_r)
        b1, b2 = _rot(y[32:48], y[48:64], cos_c, sin_c)
        return jnp.concatenate([a1, a2, b1, b2], axis=0)

    width = xb.shape[0 if x_token_major else 1]
    qn = _rep(qn_ref, width)
    kn = _rep(kn_ref, width)
    hq = proj(1536, 2048)
    qs = [norm_axial(hq[h * 64:(h + 1) * 64], qn) * QSCALE_AB for h in range(B_Q_HEADS)]
    qb_ref[0] = jnp.concatenate(qs, axis=0).astype(BF16)
    hkv = proj(2048, 2304)
    ks = [norm_axial(hkv[h * 64:(h + 1) * 64], kn) for h in range(B_KV_HEADS)]
    kb_ref[0] = jnp.concatenate(ks, axis=0).T.astype(BF16)
    nq_ref[0] = jnp.concatenate(nq + [_sumsq(x) for x in qs], axis=0)
    nk = nk + [_sumsq(x) for x in ks]
    nk_ref[0] = jnp.concatenate(nk + [jnp.zeros_like(nk[0])] * (NORM_ROWS - len(nk)), axis=0)
    vb_ref[0, 0] = hkv[128:256].astype(BF16)
    va_ref[0, 0] = proj(1024, 1536).astype(BF16)


def _even_in_kernel(x_ref, *refs):
    n_in = len(refs) - len(EVEN_OUT_TOKEN_AXES)
    width = TM // SUB_TILES
    for t in range(SUB_TILES):
        off = t * width
        params = [_tokens(r, 1, off, width) if p == EVEN_TAB_POS else r
                  for p, r in enumerate(refs[:n_in])]
        views = [_tokens(r, ax, off, width) for r, ax in zip(refs[n_in:], EVEN_OUT_TOKEN_AXES)]
        _even_in_body(x_ref[0, pl.ds(off, width), :].astype(BF16), *params, *views,
                      x_token_major=True)


def _even_in_specs(b):
    d, s = D_MODEL, SEQ
    nt = s // TM
    tok = _tok
    chunk = lambda i, j: (i, j, 0, 0)
    return (
        [
            _resident((EV_IN, d)),
            pl.BlockSpec((80, TM), lambda i, j: (0, j)),
            _resident((B_DIM, LANES)),
            _resident((B_DIM, LANES)),
        ],
        [
            pl.BlockSpec((1, 512, TM), tok),
            pl.BlockSpec((1, A_HEADS, TM, K_PAD), lambda i, j: (i, 0, j, 0)),
            pl.BlockSpec((1, 1, 512, TM), chunk),
            pl.BlockSpec((1, 512, TM), tok),
            pl.BlockSpec((1, TM, K_PAD), lambda i, j: (i, j, 0)),
            pl.BlockSpec((1, 1, 128, TM), chunk),
            pl.BlockSpec((1, 1024, TM), tok),
            pl.BlockSpec((1, NORM_ROWS, TM), tok),
            pl.BlockSpec((1, NORM_ROWS, TM), tok),
        ],
        [
            jax.ShapeDtypeStruct((b, 512, s), BF16),
            jax.ShapeDtypeStruct((b, A_HEADS, s, K_PAD), BF16),
            jax.ShapeDtypeStruct((b, nt, 512, TM), BF16),
            jax.ShapeDtypeStruct((b, 512, s), BF16),
            jax.ShapeDtypeStruct((b, s, K_PAD), BF16),
            jax.ShapeDtypeStruct((b, nt, 128, TM), BF16),
            jax.ShapeDtypeStruct((b, 1024, s), BF16),
            jax.ShapeDtypeStruct((b, NORM_ROWS, s), F32),
            jax.ShapeDtypeStruct((b, NORM_ROWS, s), F32),
        ],
    )


def _even_in(x, w_t, tab, qn, kn):
    b = x.shape[0]
    in_specs, out_specs, out_shape = _even_in_specs(b)
    return pl.pallas_call(
        _even_in_kernel,
        grid=(b, SEQ // TM),
        in_specs=[pl.BlockSpec((1, TM, D_MODEL), lambda i, j: (i, j, 0))] + in_specs,
        out_specs=out_specs,
        out_shape=out_shape,
        compiler_params=_proj_params(),
        name="even_in_proj",
    )(x, w_t, tab, qn, kn)


def _odd_in_body(xb, w_ref, wq_ref, wkv_ref, tab_ref, qn_ref, kvn_ref,
                 q_ref, k_ref, v_ref, g_ref, nq_ref, nk_ref):
    cos, sin = tab_ref[0:16, :], tab_ref[16:32, :]

    def rms(h, g):
        ms = jnp.mean(h * h, axis=0, keepdims=True)
        return h * lax.rsqrt(ms + RMS_EPS) * g

    g_ref[0] = _silu(jnp.dot(w_ref[416:1440, :], xb, preferred_element_type=F32)).astype(BF16)
    lat = jnp.dot(w_ref[0:416, :], xb, preferred_element_type=F32)
    width = xb.shape[1]
    cqn = rms(lat[0:256], _rep(qn_ref, width)).astype(BF16)
    q = jnp.dot(wq_ref[...], cqn, preferred_element_type=F32)
    qs = []
    for h in range(C_HEADS):
        b = h * C_QK
        r1, r2 = _rot(q[b + 64:b + 80], q[b + 80:b + 96], cos, sin)
        qs.append(jnp.concatenate([q[b:b + 64], r1, r2], axis=0) * QSCALE_C)
    q_ref[0] = jnp.concatenate(qs, axis=0).astype(BF16)
    nq_ref[0] = jnp.concatenate([_sumsq(x) for x in qs], axis=0)

    ckvn = rms(lat[256:384], _rep(kvn_ref, width)).astype(BF16)
    kv = jnp.dot(wkv_ref[...], ckvn, preferred_element_type=F32)
    r1, r2 = _rot(lat[384:400], lat[400:416], cos, sin)
    zpad = jnp.zeros((K_PAD - C_QK, width), F32)
    nkr = _sumsq(r1) + _sumsq(r2)
    nk = []
    for h in range(C_HEADS):
        kn = kv[h * 128:h * 128 + 64]
        nk.append(_sumsq(kn) + nkr)
        kt = jnp.concatenate([kn, r1, r2, zpad], axis=0)
        k_ref[0, h] = kt.T.astype(BF16)
    nk_ref[0] = jnp.concatenate(nk, axis=0)
    v_ref[0, 0] = jnp.concatenate(
        [kv[h * 128 + 64:h * 128 + 128] for h in range(C_HEADS)], axis=0).astype(BF16)


def _odd_in_specs(b):
    d, s = D_MODEL, SEQ
    nt = s // TM
    tok = _tok
    return (
        [
            _resident((OD_IN, d)),
            _resident((C_HEADS * C_QK, C_Q_LORA)),
            _resident((C_HEADS * 128, C_KV_LORA)),
            pl.BlockSpec((32, TM), lambda i, j: (0, j)),
            _resident((C_Q_LORA, LANES)),
            _resident((C_KV_LORA, LANES)),
        ],
        [
            pl.BlockSpec((1, C_HEADS * C_QK, TM), tok),
            pl.BlockSpec((1, C_HEADS, TM, K_PAD), lambda i, j: (i, 0, j, 0)),
            pl.BlockSpec((1, 1, 1024, TM), lambda i, j: (i, j, 0, 0)),
            pl.BlockSpec((1, 1024, TM), tok),
            pl.BlockSpec((1, NORM_ROWS, TM), tok),
            pl.BlockSpec((1, NORM_ROWS, TM), tok),
        ],
        [
            jax.ShapeDtypeStruct((b, C_HEADS * C_QK, s), BF16),
            jax.ShapeDtypeStruct((b, C_HEADS, s, K_PAD), BF16),
            jax.ShapeDtypeStruct((b, nt, 1024, TM), BF16),
            jax.ShapeDtypeStruct((b, 1024, s), BF16),
            jax.ShapeDtypeStruct((b, NORM_ROWS, s), F32),
            jax.ShapeDtypeStruct((b, NORM_ROWS, s), F32),
        ],
    )


def _softmax_pv(qps, k_at, v_ref, dv, kv_chunk, bounds=None):
    nsets = len(qps)
    ntiles = qps[0].shape[1] // MXU_TILE
    qps = [qp[:, t * MXU_TILE:(t + 1) * MXU_TILE] for qp in qps for t in range(ntiles)]
    n = len(qps)
    nchunks = SEQ // kv_chunk
    halves = kv_chunk // MXU_TILE
    ones = jnp.ones((ONES_ROWS, kv_chunk), BF16)

    def scores(ci, j):
        return [jnp.dot(k_at(ci * kv_chunk + h * MXU_TILE), qps[j], preferred_element_type=F32)
                for h in range(halves)]

    m = [None] * n
    acc = [None] * n
    items = [(ci, j) for g in range(0, n, MAX_LOCKSTEP) for ci in range(nchunks)
             for j in range(g, min(g + MAX_LOCKSTEP, n))]
    lookahead = LOOKAHEAD_ITEMS[min(n, MAX_LOCKSTEP)]
    pending = [scores(*it) for it in items[:lookahead]]
    for idx, (ci, j) in enumerate(items):
        if idx + lookahead < len(items):
            pending.append(scores(*items[idx + lookahead]))
        sc = pending.pop(0)
        r0 = ci * kv_chunk
        v = jnp.concatenate([v_ref[0, r0 // TM, :, pl.ds(r0 % TM, kv_chunk)], ones],
                            axis=0)
        if bounds is None:
            cm = functools.reduce(jnp.maximum, [jnp.max(s, axis=0, keepdims=True) for s in sc])
            ref = cm if ci == 0 else jnp.maximum(m[j], cm)
        else:
            ref = bounds[j // ntiles]
        pv = functools.reduce(jnp.add, [
            jnp.dot(v[:, h * MXU_TILE:(h + 1) * MXU_TILE],
                    jnp.exp2(sc[h] - ref).astype(BF16), preferred_element_type=F32)
            for h in range(halves)])
        if ci == 0:
            acc[j] = pv
        elif bounds is None:
            acc[j] = acc[j] * jnp.exp2(m[j] - ref) + pv
        else:
            acc[j] = acc[j] + pv
        m[j] = ref
    out = []
    for j in range(nsets):
        a = jnp.concatenate(acc[j * ntiles:(j + 1) * ntiles], axis=1)
        out.append((a[0:dv], a[dv:dv + 1]))
    return out


def _attend(qps, bounds, k_at, v_ref, dv, kv_chunk, finish):
    safe = functools.reduce(jnp.logical_and, [b <= SAFE_BOUND for b in bounds])

    @pl.when(safe)
    def _():
        finish(_softmax_pv(qps, k_at, v_ref, dv, kv_chunk, bounds))

    @pl.when(jnp.logical_not(safe))
    def _():
        finish(_softmax_pv(qps, k_at, v_ref, dv, kv_chunk))


def _attn_params():
    return pltpu.CompilerParams(
        dimension_semantics=("parallel", "parallel", "parallel"), vmem_limit_bytes=VMEM_LIMIT)


def _attn_a_kernel(bnd_ref, lam_ref, q_ref, k_ref, v_ref, gate_ref, sg_ref, o_ref, *, lam_init):
    i, h = pl.program_id(0), pl.program_id(1)
    q = q_ref[0]
    tq = q.shape[1]
    z = jnp.zeros((A_QK_DIM, tq), BF16)
    qps = [jnp.concatenate([q[0:64], z], axis=0), jnp.concatenate([z, q[64:128]], axis=0)]

    def finish(res):
        (a0, l0), (a1, l1) = res
        lp = lam_ref[...]
        lam = (jnp.exp(jnp.sum(lp[0:1] * lp[1:2], axis=1, keepdims=True))
               - jnp.exp(jnp.sum(lp[2:3] * lp[3:4], axis=1, keepdims=True)) + lam_init)
        o = a0 / l0 - lam * (a1 / l1)
        ms = jnp.mean(o * o, axis=0, keepdims=True)
        o = o * lax.rsqrt(ms + RMS_EPS) * _rep(sg_ref, tq) * (1.0 - lam_init)
        o_ref[0] = (o * gate_ref[0]).astype(BF16)

    _attend(qps, [bnd_ref[i, 2 * h], bnd_ref[i, 2 * h + 1]],
            lambda r0: k_ref[0, 0, pl.ds(r0, MXU_TILE), :], v_ref, A_V_DIM, KV_CHUNK_A, finish)


def _attn_a(bnd, lam_p, qa, ka, va, gate, sg, lam_init):
    b = qa.shape[0]
    nt = SEQ // TM
    return pl.pallas_call(
        functools.partial(_attn_a_kernel, lam_init=lam_init),
        grid=(b, A_HEADS, SEQ // TQ_A),
        in_specs=[
            pl.BlockSpec(memory_space=pltpu.SMEM),
            pl.BlockSpec((4, A_QK_DIM), lambda i, h, t: (0, 0)),
            pl.BlockSpec((1, 128, TQ_A), lambda i, h, t: (i, h, t)),
            pl.BlockSpec((1, 1, SEQ, K_PAD), lambda i, h, t: (i, h, 0, 0)),
            pl.BlockSpec((1, nt, A_V_DIM, TM), lambda i, h, t: (i, 0, h, 0)),
            pl.BlockSpec((1, A_V_DIM, TQ_A), lambda i, h, t: (i, h, t)),
            pl.BlockSpec((A_V_DIM, LANES), lambda i, h, t: (0, 0)),
        ],
        out_specs=pl.BlockSpec((1, A_V_DIM, TQ_A), lambda i, h, t: (i, h, t)),
        out_shape=jax.ShapeDtypeStruct((b, A_WIDTH, SEQ), BF16),
        compiler_params=_attn_params(),
        name="attn_diff",
    )(bnd, lam_p, qa, ka, va, gate, sg)


def _attn_b_kernel(bnd_ref, q_ref, k_ref, v_ref, gate_ref, o_ref):
    i, h = pl.program_id(0), pl.program_id(1)
    first = h < B_GROUP
    q = q_ref[0]
    z = jnp.zeros_like(q)
    qp = jnp.concatenate([jnp.where(first, q, z), jnp.where(first, z, q)], axis=0)

    def finish(res):
        ((a, l),) = res
        o_ref[0] = (a / l * gate_ref[0]).astype(BF16)

    _attend([qp], [bnd_ref[i, 2 * A_HEADS + h]],
            lambda r0: k_ref[0, pl.ds(r0, MXU_TILE), :], v_ref, B_DIM, KV_CHUNK_BC, finish)


def _attn_b(bnd, qb, kb, vb, gate):
    b = qb.shape[0]
    nt = SEQ // TM
    return pl.pallas_call(
        _attn_b_kernel,
        grid=(b, B_Q_HEADS, SEQ // TQ_B),
        in_specs=[
            pl.BlockSpec(memory_space=pltpu.SMEM),
            pl.BlockSpec((1, B_DIM, TQ_B), lambda i, h, t: (i, h, t)),
            pl.BlockSpec((1, SEQ, K_PAD), lambda i, h, t: (i, 0, 0)),
            pl.BlockSpec((1, nt, B_DIM, TM), lambda i, h, t: (i, 0, h // B_GROUP, 0)),
            pl.BlockSpec((1, B_DIM, TQ_B), lambda i, h, t: (i, A_WIDTH // B_DIM + h, t)),
        ],
        out_specs=pl.BlockSpec((1, B_DIM, TQ_B), lambda i, h, t: (i, h, t)),
        out_shape=jax.ShapeDtypeStruct((b, B_WIDTH, SEQ), BF16),
        compiler_params=_attn_params(),
        name="attn_gqa",
    )(bnd, qb, kb, vb, gate)


def _attn_c_kernel(bnd_ref, q_ref, k_ref, v_ref, gate_ref, o_ref):
    i, h = pl.program_id(0), pl.program_id(1)
    q = q_ref[0]
    qp = jnp.concatenate([q, jnp.zeros((K_PAD - C_QK, q.shape[1]), BF16)], axis=0)

    def finish(res):
        ((a, l),) = res
        o_ref[0] = (a / l * gate_ref[0]).astype(BF16)

    _attend([qp], [bnd_ref[i, h]],
            lambda r0: k_ref[0, 0, pl.ds(r0, MXU_TILE), :], v_ref, C_V, KV_CHUNK_BC, finish)


def _attn_c(bnd, q, k, v, gate):
    b = q.shape[0]
    nt = SEQ // TM
    return pl.pallas_call(
        _attn_c_kernel,
        grid=(b, C_HEADS, SEQ // TQ_C),
        in_specs=[
            pl.BlockSpec(memory_space=pltpu.SMEM),
            pl.BlockSpec((1, C_QK, TQ_C), lambda i, h, t: (i, h, t)),
            pl.BlockSpec((1, 1, SEQ, K_PAD), lambda i, h, t: (i, h, 0, 0)),
            pl.BlockSpec((1, nt, C_V, TM), lambda i, h, t: (i, 0, h, 0)),
            pl.BlockSpec((1, C_V, TQ_C), lambda i, h, t: (i, h, t)),
        ],
        out_specs=pl.BlockSpec((1, C_V, TQ_C), lambda i, h, t: (i, h, t)),
        out_shape=jax.ShapeDtypeStruct((b, C_HEADS * C_V, SEQ), BF16),
        compiler_params=_attn_params(),
        name="attn_mla",
    )(bnd, q, k, v, gate)


def _out_body(og_refs, x_ref, w_ref, lg_ref, lb_ref, x_token_major):
    og = jnp.concatenate([r[0] for r in og_refs], axis=0) if len(og_refs) > 1 else og_refs[0][0]
    y = jnp.dot(w_ref[...], og, preferred_element_type=F32)
    x = x_ref[0].T if x_token_major else x_ref[0]
    z = ALPHA * x + y
    mu = jnp.mean(z, axis=0, keepdims=True)
    d = z - mu
    var = jnp.mean(d * d, axis=0, keepdims=True)
    width = y.shape[1]
    return d * lax.rsqrt(var + LN_EPS) * _rep(lg_ref, width) + _rep(lb_ref, width)


def _out_kernel(*refs, n_og):
    x_ref, w_ref, lg_ref, lb_ref, o_ref = refs[n_og:]
    width = TM // SUB_TILES
    for t in range(SUB_TILES):
        off = t * width
        og_views = [_tokens(r, 2, off, width) for r in refs[:n_og]]
        out = _out_body(og_views, _tokens(x_ref, 2, off, width), w_ref, lg_ref, lb_ref, False)
        o_ref[0, pl.ds(off, width), :] = out.T


def _tokens(ref, axis, off, width):
    idx = [slice(None)] * len(ref.shape)
    idx[axis] = pl.ds(off, width)
    return ref.at[tuple(idx)]


def _out_in_kernel(*refs, n_og, n_in, x_token_major, in_body, tab_pos, out_axes):
    x_ref, w_ref, lg_ref, lb_ref = refs[n_og:n_og + 4]
    in_params = refs[n_og + 4:n_og + 4 + n_in]
    xo_ref = refs[n_og + 4 + n_in]
    in_outs = refs[n_og + 5 + n_in:]
    width = TM // SUB_TILES
    outs = []
    for t in range(SUB_TILES):
        off = t * width
        og_views = [_tokens(r, 2, off, width) for r in refs[:n_og]]
        x_view = _tokens(x_ref, 1 if x_token_major else 2, off, width)
        out = _out_body(og_views, x_view, w_ref, lg_ref, lb_ref, x_token_major)
        xo_ref[0, :, pl.ds(off, width)] = out
        outs.append(out.astype(BF16))
    for t in range(SUB_TILES):
        off = t * width
        params = [_tokens(r, 1, off, width) if p == tab_pos else r
                  for p, r in enumerate(in_params)]
        views = [_tokens(r, ax, off, width) for r, ax in zip(in_outs, out_axes)]
        in_body(outs[t], *params, *views)


def _out_specs(ogs, x_token_major):
    d = D_MODEL
    x_spec = (pl.BlockSpec((1, TM, d), lambda i, j: (i, j, 0)) if x_token_major
              else pl.BlockSpec((1, d, TM), _tok))
    return [pl.BlockSpec((1, og.shape[1], TM), _tok) for og in ogs] + [
        x_spec, _resident((d, d)), _resident((d, LANES)), _resident((d, LANES))]


def _out_proj_last(ogs, xt, w_t, lg, lb):
    b = xt.shape[0]
    return pl.pallas_call(
        functools.partial(_out_kernel, n_og=len(ogs)),
        grid=(b, SEQ // TM),
        in_specs=_out_specs(ogs, False),
        out_specs=pl.BlockSpec((1, TM, D_MODEL), lambda i, j: (i, j, 0)),
        out_shape=jax.ShapeDtypeStruct((b, SEQ, D_MODEL), F32),
        compiler_params=_proj_params(),
        name="out_proj_ln",
    )(*ogs, xt, w_t, lg, lb)


def _out_in(ogs, xt, w_t, lg, lb, next_even, in_args, x_token_major):
    b = xt.shape[0]
    if next_even:
        in_body, (in_specs, out_specs, out_shape) = _even_in_body, _even_in_specs(b)
        tab_pos, out_axes, name = EVEN_TAB_POS, EVEN_OUT_TOKEN_AXES, "out_even_in_proj"
    else:
        in_body, (in_specs, out_specs, out_shape) = _odd_in_body, _odd_in_specs(b)
        tab_pos, out_axes, name = ODD_TAB_POS, ODD_OUT_TOKEN_AXES, "out_odd_in_proj"
    return pl.pallas_call(
        functools.partial(_out_in_kernel, n_og=len(ogs), n_in=len(in_args),
                          x_token_major=x_token_major, in_body=in_body, tab_pos=tab_pos,
                          out_axes=out_axes),
        grid=(b, SEQ // TM),
        in_specs=_out_specs(ogs, x_token_major) + in_specs,
        out_specs=[pl.BlockSpec((1, D_MODEL, TM), _tok)] + out_specs,
        out_shape=[jax.ShapeDtypeStruct((b, D_MODEL, SEQ), F32)] + out_shape,
        compiler_params=_proj_params(),
        name=name,
    )(*ogs, xt, w_t, lg, lb, *in_args)


def _col(v):
    v = v.astype(F32)
    return jnp.broadcast_to(v[:, None], (v.shape[0], LANES))


def _score_bounds(nq, kmax2):
    return jnp.sqrt(jnp.max(nq, axis=-1) * kmax2) * BOUND_SLACK


def _angles_t(pos, dims, theta):
    inv = theta ** (-jnp.arange(0, dims, 2, dtype=F32) / dims)
    ang = pos.astype(F32)[:, None] * inv[None, :]
    return jnp.cos(ang).T, jnp.sin(ang).T


@jax.jit
def _forward(x, ev_w_in, ev_w_out, ev_lam, ev_subln, ev_qnorm, ev_knorm, ev_ln_g, ev_ln_b,
             od_w_in, od_qnorm, od_kvnorm, od_w_qb, od_w_kvb, od_w_out, od_ln_g, od_ln_b):
    s = x.shape[1]
    pos = jnp.arange(s, dtype=jnp.int32)
    row = jnp.repeat(jnp.arange(s // GRID_W, dtype=jnp.int32), GRID_W)
    col = jnp.tile(jnp.arange(GRID_W, dtype=jnp.int32), s // GRID_W)
    tab_ev = jnp.concatenate(
        _angles_t(pos, A_ROT, ROPE_THETA) + _angles_t(row, B_DIM // 2, AXIAL_THETA)
        + _angles_t(col, B_DIM // 2, AXIAL_THETA), axis=0)
    tab_od = jnp.concatenate(_angles_t(pos, C_ROPE, ROPE_THETA), axis=0)

    def even_args(i):
        return (ev_w_in[i].T.astype(BF16), tab_ev, _col(ev_qnorm[i]), _col(ev_knorm[i]))

    def odd_args(i):
        return (od_w_in[i].T.astype(BF16), od_w_qb[i].T.astype(BF16),
                od_w_kvb[i].T.astype(BF16), tab_od, _col(od_qnorm[i]), _col(od_kvnorm[i]))

    xt = x
    proj = _even_in(x, *even_args(0))
    for layer in range(DEPTH):
        i = layer // 2
        if layer % 2 == 0:
            qa, ka, va, qb, kb, vb, gate, nq, nk = proj
            kmax = jnp.max(nk, axis=-1)
            kmax = jnp.concatenate(
                [kmax[:, :2 * A_HEADS],
                 jnp.repeat(kmax[:, 2 * A_HEADS:2 * A_HEADS + B_KV_HEADS], B_GROUP, axis=1)],
                axis=1)
            bnd = _score_bounds(nq, kmax)
            lam_init = 0.8 - 0.6 * math.exp(-0.3 * layer)
            ogs = [_attn_a(bnd, ev_lam[i].astype(F32), qa, ka, va, gate, _col(ev_subln[i]),
                           lam_init),
                   _attn_b(bnd, qb, kb, vb, gate)]
            out_args = (ev_w_out[i].T.astype(BF16), _col(ev_ln_g[i]), _col(ev_ln_b[i]))
        else:
            q, k, v, gate, nq, nk = proj
            ogs = [_attn_c(_score_bounds(nq, jnp.max(nk, axis=-1)), q, k, v, gate)]
            out_args = (od_w_out[i].T.astype(BF16), _col(od_ln_g[i]), _col(od_ln_b[i]))
        if layer == DEPTH - 1:
            return _out_proj_last(ogs, xt, *out_args)
        nxt = (layer + 1) // 2
        if layer % 2 == 0:
            xt, *proj = _out_in(ogs, xt, *out_args, False, odd_args(nxt), layer == 0)
        else:
            xt, *proj = _out_in(ogs, xt, *out_args, True, even_args(nxt), False)


def kernel(x, ev_w_in, ev_w_out, ev_lam, ev_subln, ev_qnorm, ev_knorm, ev_ln_g, ev_ln_b,
           od_w_in, od_qnorm, od_kvnorm, od_w_qb, od_w_kvb, od_w_out, od_ln_g, od_ln_b):
    return _forward(x, ev_w_in, ev_w_out, ev_lam, ev_subln, ev_qnorm, ev_knorm, ev_ln_g,
                    ev_ln_b, od_w_in, od_qnorm, od_kvnorm, od_w_qb, od_w_kvb, od_w_out,
                    od_ln_g, od_ln_b)
```

```python
import functools
import math

import jax
import jax.numpy as jnp
from jax import lax
from jax.experimental import pallas as pl
from jax.experimental.pallas import tpu as pltpu

F32 = jnp.float32
BF16 = jnp.bfloat16

D_MODEL = 1024
SEQ = 4096
DEPTH = 4
GRID_W = 64
ROPE_THETA = 500000.0
AXIAL_THETA = 10000.0
LN_EPS = 1e-5
RMS_EPS = 1e-6

A_HEADS = 4
A_QK_DIM = 64
A_V_DIM = 128
A_WIDTH = 512
A_ROT = 16
B_Q_HEADS = 8
B_KV_HEADS = 2
B_GROUP = 4
B_DIM = 64
B_WIDTH = 512
EV_IN = 3328

C_HEADS = 16
C_NOPE = 64
C_ROPE = 32
C_V = 64
C_Q_LORA = 256
C_KV_LORA = 128
C_QK = C_NOPE + C_ROPE
OD_IN = 1440

ALPHA = (2 * DEPTH) ** 0.25
LOG2E = 1.4426950408889634

QSCALE_AB = A_QK_DIM ** -0.5 * LOG2E
QSCALE_C = C_QK ** -0.5 * LOG2E

LANES = 128
MXU_TILE = 256
LOOKAHEAD_ITEMS = {1: 2, 2: 3, 4: 8}
MAX_LOCKSTEP = 4
ONES_ROWS = 16
K_PAD = 128
NORM_ROWS = 16
SAFE_BOUND = 60.0
BOUND_SLACK = 1.02
TM = 512
SUB_TILES = 2
EVEN_TAB_POS, EVEN_OUT_TOKEN_AXES = 1, (2, 2, 3, 2, 1, 3, 2, 2, 2)
ODD_TAB_POS, ODD_OUT_TOKEN_AXES = 3, (2, 2, 3, 2, 2, 2)
KV_CHUNK_A = 512
KV_CHUNK_BC = 256
TQ_A = 4096
TQ_B = 4096
TQ_C = 4096
VMEM_LIMIT = 56 * 1024 * 1024


def _rep(ref, n):
    a = ref[...]
    return jnp.concatenate([a] * (n // LANES), axis=1)


def _rot(x1, x2, cos, sin):
    return x1 * cos - x2 * sin, x2 * cos + x1 * sin


def _silu(x):
    return x * jax.nn.sigmoid(x)


def _sumsq(x):
    return jnp.sum(x * x, axis=0, keepdims=True)


def _tok(i, j):
    return (i, 0, j)


def _const2(i, j):
    return (0, 0)


def _resident(shape):
    return pl.BlockSpec(shape, _const2, pipeline_mode=pl.Buffered(1))


def _proj_params():
    return pltpu.CompilerParams(
        dimension_semantics=("parallel", "parallel"), vmem_limit_bytes=VMEM_LIMIT)


def _even_in_body(xb, w_ref, tab_ref, qn_ref, kn_ref,
                  qa_ref, ka_ref, va_ref, qb_ref, kb_ref, vb_ref, g_ref, nq_ref, nk_ref,
                  *, x_token_major=False):
    contract = (((1,), (1 if x_token_major else 0,)), ((), ()))

    def proj(r0, r1):
        return lax.dot_general(w_ref[r0:r1, :], xb, contract, preferred_element_type=F32)

    cos_a, sin_a = tab_ref[0:8, :], tab_ref[8:16, :]
    cos_r, sin_r = tab_ref[16:32, :], tab_ref[32:48, :]
    cos_c, sin_c = tab_ref[48:64, :], tab_ref[64:80, :]

    def rope_a(h):
        outs = []
        for hc in range(2 * A_HEADS):
            b = hc * A_QK_DIM
            r1, r2 = _rot(h[b:b + 8], h[b + 8:b + 16], cos_a, sin_a)
            outs.append(jnp.concatenate([r1, r2, h[b + 16:b + 64]], axis=0))
        return outs

    g_ref[0] = _silu(proj(2304, 3328)).astype(BF16)
    qs = [x * QSCALE_AB for x in rope_a(proj(0, 512))]
    qa_ref[0] = jnp.concatenate(qs, axis=0).astype(BF16)
    ks = rope_a(proj(512, 1024))
    nq = [_sumsq(x) for x in qs]
    nk = [_sumsq(x) for x in ks]
    for h in range(A_HEADS):
        kt = jnp.concatenate([ks[2 * h], ks[2 * h + 1]], axis=0)
        ka_ref[0, h] = kt.T.astype(BF16)

    def norm_axial(h, g):
        ms = jnp.mean(h * h, axis=0, keepdims=True)
        y = h * lax.rsqrt(ms + RMS_EPS) * g
        a1, a2 = _rot(y[0:16], y[16:32], cos_r, sin_r)
        b1, b2 = _rot(y[32:48], y[48:64], cos_c, sin_c)
        return jnp.concatenate([a1, a2, b1, b2], axis=0)

    width = xb.shape[0 if x_token_major else 1]
    qn = _rep(qn_ref, width)
    kn = _rep(kn_ref, width)
    hq = proj(1536, 2048)
    qs = [norm_axial(hq[h * 64:(h + 1) * 64], qn) * QSCALE_AB for h in range(B_Q_HEADS)]
    qb_ref[0] = jnp.concatenate(qs, axis=0).astype(BF16)
    hkv = proj(2048, 2304)
    ks = [norm_axial(hkv[h * 64:(h + 1) * 64], kn) for h in range(B_KV_HEADS)]
    kb_ref[0] = jnp.concatenate(ks, axis=0).T.astype(BF16)
    nq_ref[0] = jnp.concatenate(nq + [_sumsq(x) for x in qs], axis=0)
    nk = nk + [_sumsq(x) for x in ks]
    nk_ref[0] = jnp.concatenate(nk + [jnp.zeros_like(nk[0])] * (NORM_ROWS - len(nk)), axis=0)
    vb_ref[0, 0] = hkv[128:256].astype(BF16)
    va_ref[0, 0] = proj(1024, 1536).astype(BF16)


def _even_in_kernel(x_ref, *refs):
    n_in = len(refs) - len(EVEN_OUT_TOKEN_AXES)
    width = TM // SUB_TILES
    for t in range(SUB_TILES):
        off = t * width
        params = [_tokens(r, 1, off, width) if p == EVEN_TAB_POS else r
                  for p, r in enumerate(refs[:n_in])]
        views = [_tokens(r, ax, off, width) for r, ax in zip(refs[n_in:], EVEN_OUT_TOKEN_AXES)]
        _even_in_body(x_ref[0, pl.ds(off, width), :].astype(BF16), *params, *views,
                      x_token_major=True)


def _even_in_specs(b):
    d, s = D_MODEL, SEQ
    nt = s // TM
    tok = _tok
    chunk = lambda i, j: (i, j, 0, 0)
    return (
        [
            _resident((EV_IN, d)),
            pl.BlockSpec((80, TM), lambda i, j: (0, j)),
            _resident((B_DIM, LANES)),
            _resident((B_DIM, LANES)),
        ],
        [
            pl.BlockSpec((1, 512, TM), tok),
            pl.BlockSpec((1, A_HEADS, TM, K_PAD), lambda i, j: (i, 0, j, 0)),
            pl.BlockSpec((1, 1, 512, TM), chunk),
            pl.BlockSpec((1, 512, TM), tok),
            pl.BlockSpec((1, TM, K_PAD), lambda i, j: (i, j, 0)),
            pl.BlockSpec((1, 1, 128, TM), chunk),
            pl.BlockSpec((1, 1024, TM), tok),
            pl.BlockSpec((1, NORM_ROWS, TM), tok),
            pl.BlockSpec((1, NORM_ROWS, TM), tok),
        ],
        [
            jax.ShapeDtypeStruct((b, 512, s), BF16),
            jax.ShapeDtypeStruct((b, A_HEADS, s, K_PAD), BF16),
            jax.ShapeDtypeStruct((b, nt, 512, TM), BF16),
            jax.ShapeDtypeStruct((b, 512, s), BF16),
            jax.ShapeDtypeStruct((b, s, K_PAD), BF16),
            jax.ShapeDtypeStruct((b, nt, 128, TM), BF16),
            jax.ShapeDtypeStruct((b, 1024, s), BF16),
            jax.ShapeDtypeStruct((b, NORM_ROWS, s), F32),
            jax.ShapeDtypeStruct((b, NORM_ROWS, s), F32),
        ],
    )


def _even_in(x, w_t, tab, qn, kn):
    b = x.shape[0]
    in_specs, out_specs, out_shape = _even_in_specs(b)
    return pl.pallas_call(
        _even_in_kernel,
        grid=(b, SEQ // TM),
        in_specs=[pl.BlockSpec((1, TM, D_MODEL), lambda i, j: (i, j, 0))] + in_specs,
        out_specs=out_specs,
        out_shape=out_shape,
        compiler_params=_proj_params(),
        name="even_in_proj",
    )(x, w_t, tab, qn, kn)


def _odd_in_body(xb, w_ref, wq_ref, wkv_ref, tab_ref, qn_ref, kvn_ref,
                 q_ref, k_ref, v_ref, g_ref, nq_ref, nk_ref):
    cos, sin = tab_ref[0:16, :], tab_ref[16:32, :]

    def rms(h, g):
        ms = jnp.mean(h * h, axis=0, keepdims=True)
        return h * lax.rsqrt(ms + RMS_EPS) * g

    g_ref[0] = _silu(jnp.dot(w_ref[416:1440, :], xb, preferred_element_type=F32)).astype(BF16)
    lat = jnp.dot(w_ref[0:416, :], xb, preferred_element_type=F32)
    width = xb.shape[1]
    cqn = rms(lat[0:256], _rep(qn_ref, width)).astype(BF16)
    q = jnp.dot(wq_ref[...], cqn, preferred_element_type=F32)
    qs = []
    for h in range(C_HEADS):
        b = h * C_QK
        r1, r2 = _rot(q[b + 64:b + 80], q[b + 80:b + 96], cos, sin)
        qs.append(jnp.concatenate([q[b:b + 64], r1, r2], axis=0) * QSCALE_C)
    q_ref[0] = jnp.concatenate(qs, axis=0).astype(BF16)
    nq_ref[0] = jnp.concatenate([_sumsq(x) for x in qs], axis=0)

    ckvn = rms(lat[256:384], _rep(kvn_ref, width)).astype(BF16)
    kv = jnp.dot(wkv_ref[...], ckvn, preferred_element_type=F32)
    r1, r2 = _rot(lat[384:400], lat[400:416], cos, sin)
    zpad = jnp.zeros((K_PAD - C_QK, width), F32)
    nkr = _sumsq(r1) + _sumsq(r2)
    nk = []
    for h in range(C_HEADS):
        kn = kv[h * 128:h * 128 + 64]
        nk.append(_sumsq(kn) + nkr)
        kt = jnp.concatenate([kn, r1, r2, zpad], axis=0)
        k_ref[0, h] = kt.T.astype(BF16)
    nk_ref[0] = jnp.concatenate(nk, axis=0)
    v_ref[0, 0] = jnp.concatenate(
        [kv[h * 128 + 64:h * 128 + 128] for h in range(C_HEADS)], axis=0).astype(BF16)


def _odd_in_specs(b):
    d, s = D_MODEL, SEQ
    nt = s // TM
    tok = _tok
    return (
        [
            _resident((OD_IN, d)),
            _resident((C_HEADS * C_QK, C_Q_LORA)),
            _resident((C_HEADS * 128, C_KV_LORA)),
            pl.BlockSpec((32, TM), lambda i, j: (0, j)),
            _resident((C_Q_LORA, LANES)),
            _resident((C_KV_LORA, LANES)),
        ],
        [
            pl.BlockSpec((1, C_HEADS * C_QK, TM), tok),
            pl.BlockSpec((1, C_HEADS, TM, K_PAD), lambda i, j: (i, 0, j, 0)),
            pl.BlockSpec((1, 1, 1024, TM), lambda i, j: (i, j, 0, 0)),
            pl.BlockSpec((1, 1024, TM), tok),
            pl.BlockSpec((1, NORM_ROWS, TM), tok),
            pl.BlockSpec((1, NORM_ROWS, TM), tok),
        ],
        [
            jax.ShapeDtypeStruct((b, C_HEADS * C_QK, s), BF16),
            jax.ShapeDtypeStruct((b, C_HEADS, s, K_PAD), BF16),
            jax.ShapeDtypeStruct((b, nt, 1024, TM), BF16),
            jax.ShapeDtypeStruct((b, 1024, s), BF16),
            jax.ShapeDtypeStruct((b, NORM_ROWS, s), F32),
            jax.ShapeDtypeStruct((b, NORM_ROWS, s), F32),
        ],
    )


def _softmax_pv(qps, k_at, v_ref, dv, kv_chunk, bounds=None):
    nsets = len(qps)
    ntiles = qps[0].shape[1] // MXU_TILE
    qps = [qp[:, t * MXU_TILE:(t + 1) * MXU_TILE] for qp in qps for t in range(ntiles)]
    n = len(qps)
    nchunks = SEQ // kv_chunk
    halves = kv_chunk // MXU_TILE
    ones = jnp.ones((ONES_ROWS, kv_chunk), BF16)

    def scores(ci, j):
        return [jnp.dot(k_at(ci * kv_chunk + h * MXU_TILE), qps[j], preferred_element_type=F32)
                for h in range(halves)]

    m = [None] * n
    acc = [None] * n
    items = [(ci, j) for g in range(0, n, MAX_LOCKSTEP) for ci in range(nchunks)
             for j in range(g, min(g + MAX_LOCKSTEP, n))]
    lookahead = LOOKAHEAD_ITEMS[min(n, MAX_LOCKSTEP)]
    pending = [scores(*it) for it in items[:lookahead]]
    for idx, (ci, j) in enumerate(items):
        if idx + lookahead < len(items):
            pending.append(scores(*items[idx + lookahead]))
        sc = pending.pop(0)
        r0 = ci * kv_chunk
        v = jnp.concatenate([v_ref[0, r0 // TM, :, pl.ds(r0 % TM, kv_chunk)], ones],
                            axis=0)
        if bounds is None:
            cm = functools.reduce(jnp.maximum, [jnp.max(s, axis=0, keepdims=True) for s in sc])
            ref = cm if ci == 0 else jnp.maximum(m[j], cm)
        else:
            ref = bounds[j // ntiles]
        pv = functools.reduce(jnp.add, [
            jnp.dot(v[:, h * MXU_TILE:(h + 1) * MXU_TILE],
                    jnp.exp2(sc[h] - ref).astype(BF16), preferred_element_type=F32)
            for h in range(halves)])
        if ci == 0:
            acc[j] = pv
        elif bounds is None:
            acc[j] = acc[j] * jnp.exp2(m[j] - ref) + pv
        else:
            acc[j] = acc[j] + pv
        m[j] = ref
    out = []
    for j in range(nsets):
        a = jnp.concatenate(acc[j * ntiles:(j + 1) * ntiles], axis=1)
        out.append((a[0:dv], a[dv:dv + 1]))
    return out


def _attend(qps, bounds, k_at, v_ref, dv, kv_chunk, finish):
    safe = functools.reduce(jnp.logical_and, [b <= SAFE_BOUND for b in bounds])

    @pl.when(safe)
    def _():
        finish(_softmax_pv(qps, k_at, v_ref, dv, kv_chunk, bounds))

    @pl.when(jnp.logical_not(safe))
    def _():
        finish(_softmax_pv(qps, k_at, v_ref, dv, kv_chunk))


def _attn_params():
    return pltpu.CompilerParams(
        dimension_semantics=("parallel", "parallel", "parallel"), vmem_limit_bytes=VMEM_LIMIT)


def _attn_a_kernel(bnd_ref, lam_ref, q_ref, k_ref, v_ref, gate_ref, sg_ref, o_ref, *, lam_init):
    i, h = pl.program_id(0), pl.program_id(1)
    q = q_ref[0]
    tq = q.shape[1]
    z = jnp.zeros((A_QK_DIM, tq), BF16)
    qps = [jnp.concatenate([q[0:64], z], axis=0), jnp.concatenate([z, q[64:128]], axis=0)]

    def finish(res):
        (a0, l0), (a1, l1) = res
        lp = lam_ref[...]
        lam = (jnp.exp(jnp.sum(lp[0:1] * lp[1:2], axis=1, keepdims=True))
               - jnp.exp(jnp.sum(lp[2:3] * lp[3:4], axis=1, keepdims=True)) + lam_init)
        o = a0 / l0 - lam * (a1 / l1)
        ms = jnp.mean(o * o, axis=0, keepdims=True)
        o = o * lax.rsqrt(ms + RMS_EPS) * _rep(sg_ref, tq) * (1.0 - lam_init)
        o_ref[0] = (o * gate_ref[0]).astype(BF16)

    _attend(qps, [bnd_ref[i, 2 * h], bnd_ref[i, 2 * h + 1]],
            lambda r0: k_ref[0, 0, pl.ds(r0, MXU_TILE), :], v_ref, A_V_DIM, KV_CHUNK_A, finish)


def _attn_a(bnd, lam_p, qa, ka, va, gate, sg, lam_init):
    b = qa.shape[0]
    nt = SEQ // TM
    return pl.pallas_call(
        functools.partial(_attn_a_kernel, lam_init=lam_init),
        grid=(b, A_HEADS, SEQ // TQ_A),
        in_specs=[
            pl.BlockSpec(memory_space=pltpu.SMEM),
            pl.BlockSpec((4, A_QK_DIM), lambda i, h, t: (0, 0)),
            pl.BlockSpec((1, 128, TQ_A), lambda i, h, t: (i, h, t)),
            pl.BlockSpec((1, 1, SEQ, K_PAD), lambda i, h, t: (i, h, 0, 0)),
            pl.BlockSpec((1, nt, A_V_DIM, TM), lambda i, h, t: (i, 0, h, 0)),
            pl.BlockSpec((1, A_V_DIM, TQ_A), lambda i, h, t: (i, h, t)),
            pl.BlockSpec((A_V_DIM, LANES), lambda i, h, t: (0, 0)),
        ],
        out_specs=pl.BlockSpec((1, A_V_DIM, TQ_A), lambda i, h, t: (i, h, t)),
        out_shape=jax.ShapeDtypeStruct((b, A_WIDTH, SEQ), BF16),
        compiler_params=_attn_params(),
        name="attn_diff",
    )(bnd, lam_p, qa, ka, va, gate, sg)


def _attn_b_kernel(bnd_ref, q_ref, k_ref, v_ref, gate_ref, o_ref):
    i, h = pl.program_id(0), pl.program_id(1)
    first = h < B_GROUP
    q = q_ref[0]
    z = jnp.zeros_like(q)
    qp = jnp.concatenate([jnp.where(first, q, z), jnp.where(first, z, q)], axis=0)

    def finish(res):
        ((a, l),) = res
        o_ref[0] = (a / l * gate_ref[0]).astype(BF16)

    _attend([qp], [bnd_ref[i, 2 * A_HEADS + h]],
            lambda r0: k_ref[0, pl.ds(r0, MXU_TILE), :], v_ref, B_DIM, KV_CHUNK_BC, finish)


def _attn_b(bnd, qb, kb, vb, gate):
    b = qb.shape[0]
    nt = SEQ // TM
    return pl.pallas_call(
        _attn_b_kernel,
        grid=(b, B_Q_HEADS, SEQ // TQ_B),
        in_specs=[
            pl.BlockSpec(memory_space=pltpu.SMEM),
            pl.BlockSpec((1, B_DIM, TQ_B), lambda i, h, t: (i, h, t)),
            pl.BlockSpec((1, SEQ, K_PAD), lambda i, h, t: (i, 0, 0)),
            pl.BlockSpec((1, nt, B_DIM, TM), lambda i, h, t: (i, 0, h // B_GROUP, 0)),
            pl.BlockSpec((1, B_DIM, TQ_B), lambda i, h, t: (i, A_WIDTH // B_DIM + h, t)),
        ],
        out_specs=pl.BlockSpec((1, B_DIM, TQ_B), lambda i, h, t: (i, h, t)),
        out_shape=jax.ShapeDtypeStruct((b, B_WIDTH, SEQ), BF16),
        compiler_params=_attn_params(),
        name="attn_gqa",
    )(bnd, qb, kb, vb, gate)


def _attn_c_kernel(bnd_ref, q_ref, k_ref, v_ref, gate_ref, o_ref):
    i, h = pl.program_id(0), pl.program_id(1)
    q = q_ref[0]
    qp = jnp.concatenate([q, jnp.zeros((K_PAD - C_QK, q.shape[1]), BF16)], axis=0)

    def finish(res):
        ((a, l),) = res
        o_ref[0] = (a / l * gate_ref[0]).astype(BF16)

    _attend([qp], [bnd_ref[i, h]],
            lambda r0: k_ref[0, 0, pl.ds(r0, MXU_TILE), :], v_ref, C_V, KV_CHUNK_BC, finish)


def _attn_c(bnd, q, k, v, gate):
    b = q.shape[0]
    nt = SEQ // TM
    return pl.pallas_call(
        _attn_c_kernel,
        grid=(b, C_HEADS, SEQ // TQ_C),
        in_specs=[
            pl.BlockSpec(memory_space=pltpu.SMEM),
            pl.BlockSpec((1, C_QK, TQ_C), lambda i, h, t: (i, h, t)),
            pl.BlockSpec((1, 1, SEQ, K_PAD), lambda i, h, t: (i, h, 0, 0)),
            pl.BlockSpec((1, nt, C_V, TM), lambda i, h, t: (i, 0, h, 0)),
            pl.BlockSpec((1, C_V, TQ_C), lambda i, h, t: (i, h, t)),
        ],
        out_specs=pl.BlockSpec((1, C_V, TQ_C), lambda i, h, t: (i, h, t)),
        out_shape=jax.ShapeDtypeStruct((b, C_HEADS * C_V, SEQ), BF16),
        compiler_params=_attn_params(),
        name="attn_mla",
    )(bnd, q, k, v, gate)


def _out_body(og_refs, x_ref, w_ref, lg_ref, lb_ref, x_token_major):
    og = jnp.concatenate([r[0] for r in og_refs], axis=0) if len(og_refs) > 1 else og_refs[0][0]
    y = jnp.dot(w_ref[...], og, preferred_element_type=F32)
    x = x_ref[0].T if x_token_major else x_ref[0]
    z = ALPHA * x + y
    mu = jnp.mean(z, axis=0, keepdims=True)
    d = z - mu
    var = jnp.mean(d * d, axis=0, keepdims=True)
    width = y.shape[1]
    return d * lax.rsqrt(var + LN_EPS) * _rep(lg_ref, width) + _rep(lb_ref, width)


def _out_kernel(*refs, n_og):
    x_ref, w_ref, lg_ref, lb_ref, o_ref = refs[n_og:]
    width = TM // SUB_TILES
    for t in range(SUB_TILES):
        off = t * width
        og_views = [_tokens(r, 2, off, width) for r in refs[:n_og]]
        out = _out_body(og_views, _tokens(x_ref, 2, off, width), w_ref, lg_ref, lb_ref, False)
        o_ref[0, pl.ds(off, width), :] = out.T


def _tokens(ref, axis, off, width):
    idx = [slice(None)] * len(ref.shape)
    idx[axis] = pl.ds(off, width)
    return ref.at[tuple(idx)]


def _out_in_kernel(*refs, n_og, n_in, x_token_major, in_body, tab_pos, out_axes):
    x_ref, w_ref, lg_ref, lb_ref = refs[n_og:n_og + 4]
    in_params = refs[n_og + 4:n_og + 4 + n_in]
    xo_ref = refs[n_og + 4 + n_in]
    in_outs = refs[n_og + 5 + n_in:]
    width = TM // SUB_TILES
    outs = []
    for t in range(SUB_TILES):
        off = t * width
        og_views = [_tokens(r, 2, off, width) for r in refs[:n_og]]
        x_view = _tokens(x_ref, 1 if x_token_major else 2, off, width)
        out = _out_body(og_views, x_view, w_ref, lg_ref, lb_ref, x_token_major)
        xo_ref[0, :, pl.ds(off, width)] = out
        outs.append(out.astype(BF16))
    for t in range(SUB_TILES):
        off = t * width
        params = [_tokens(r, 1, off, width) if p == tab_pos else r
                  for p, r in enumerate(in_params)]
        views = [_tokens(r, ax, off, width) for r, ax in zip(in_outs, out_axes)]
        in_body(outs[t], *params, *views)


def _out_specs(ogs, x_token_major):
    d = D_MODEL
    x_spec = (pl.BlockSpec((1, TM, d), lambda i, j: (i, j, 0)) if x_token_major
              else pl.BlockSpec((1, d, TM), _tok))
    return [pl.BlockSpec((1, og.shape[1], TM), _tok) for og in ogs] + [
        x_spec, _resident((d, d)), _resident((d, LANES)), _resident((d, LANES))]


def _out_proj_last(ogs, xt, w_t, lg, lb):
    b = xt.shape[0]
    return pl.pallas_call(
        functools.partial(_out_kernel, n_og=len(ogs)),
        grid=(b, SEQ // TM),
        in_specs=_out_specs(ogs, False),
        out_specs=pl.BlockSpec((1, TM, D_MODEL), lambda i, j: (i, j, 0)),
        out_shape=jax.ShapeDtypeStruct((b, SEQ, D_MODEL), F32),
        compiler_params=_proj_params(),
        name="out_proj_ln",
    )(*ogs, xt, w_t, lg, lb)


def _out_in(ogs, xt, w_t, lg, lb, next_even, in_args, x_token_major):
    b = xt.shape[0]
    if next_even:
        in_body, (in_specs, out_specs, out_shape) = _even_in_body, _even_in_specs(b)
        tab_pos, out_axes, name = EVEN_TAB_POS, EVEN_OUT_TOKEN_AXES, "out_even_in_proj"
    else:
        in_body, (in_specs, out_specs, out_shape) = _odd_in_body, _odd_in_specs(b)
        tab_pos, out_axes, name = ODD_TAB_POS, ODD_OUT_TOKEN_AXES, "out_odd_in_proj"
    return pl.pallas_call(
        functools.partial(_out_in_kernel, n_og=len(ogs), n_in=len(in_args),
                          x_token_major=x_token_major, in_body=in_body, tab_pos=tab_pos,
                          out_axes=out_axes),
        grid=(b, SEQ // TM),
        in_specs=_out_specs(ogs, x_token_major) + in_specs,
        out_specs=[pl.BlockSpec((1, D_MODEL, TM), _tok)] + out_specs,
        out_shape=[jax.ShapeDtypeStruct((b, D_MODEL, SEQ), F32)] + out_shape,
        compiler_params=_proj_params(),
        name=name,
    )(*ogs, xt, w_t, lg, lb, *in_args)


def _col(v):
    v = v.astype(F32)
    return jnp.broadcast_to(v[:, None], (v.shape[0], LANES))


def _score_bounds(nq, kmax2):
    return jnp.sqrt(jnp.max(nq, axis=-1) * kmax2) * BOUND_SLACK


def _angles_t(pos, dims, theta):
    inv = theta ** (-jnp.arange(0, dims, 2, dtype=F32) / dims)
    ang = pos.astype(F32)[:, None] * inv[None, :]
    return jnp.cos(ang).T, jnp.sin(ang).T


@jax.jit
def _forward(x, ev_w_in, ev_w_out, ev_lam, ev_subln, ev_qnorm, ev_knorm, ev_ln_g, ev_ln_b,
             od_w_in, od_qnorm, od_kvnorm, od_w_qb, od_w_kvb, od_w_out, od_ln_g, od_ln_b):
    s = x.shape[1]
    pos = jnp.arange(s, dtype=jnp.int32)
    row = jnp.repeat(jnp.arange(s // GRID_W, dtype=jnp.int32), GRID_W)
    col = jnp.tile(jnp.arange(GRID_W, dtype=jnp.int32), s // GRID_W)
    tab_ev = jnp.concatenate(
        _angles_t(pos, A_ROT, ROPE_THETA) + _angles_t(row, B_DIM // 2, AXIAL_THETA)
        + _angles_t(col, B_DIM // 2, AXIAL_THETA), axis=0)
    tab_od = jnp.concatenate(_angles_t(pos, C_ROPE, ROPE_THETA), axis=0)

    def even_args(i):
        return (ev_w_in[i].T.astype(BF16), tab_ev, _col(ev_qnorm[i]), _col(ev_knorm[i]))

    def odd_args(i):
        return (od_w_in[i].T.astype(BF16), od_w_qb[i].T.astype(BF16),
                od_w_kvb[i].T.astype(BF16), tab_od, _col(od_qnorm[i]), _col(od_kvnorm[i]))

    xt = x
    proj = _even_in(x, *even_args(0))
    for layer in range(DEPTH):
        i = layer // 2
        if layer % 2 == 0:
            qa, ka, va, qb, kb, vb, gate, nq, nk = proj
            kmax = jnp.max(nk, axis=-1)
            kmax = jnp.concatenate(
                [kmax[:, :2 * A_HEADS],
                 jnp.repeat(kmax[:, 2 * A_HEADS:2 * A_HEADS + B_KV_HEADS], B_GROUP, axis=1)],
                axis=1)
            bnd = _score_bounds(nq, kmax)
            lam_init = 0.8 - 0.6 * math.exp(-0.3 * layer)
            ogs = [_attn_a(bnd, ev_lam[i].astype(F32), qa, ka, va, gate, _col(ev_subln[i]),
                           lam_init),
                   _attn_b(bnd, qb, kb, vb, gate)]
            out_args = (ev_w_out[i].T.astype(BF16), _col(ev_ln_g[i]), _col(ev_ln_b[i]))
        else:
            q, k, v, gate, nq, nk = proj
            ogs = [_attn_c(_score_bounds(nq, jnp.max(nk, axis=-1)), q, k, v, gate)]
            out_args = (od_w_out[i].T.astype(BF16), _col(od_ln_g[i]), _col(od_ln_b[i]))
        if layer == DEPTH - 1:
            return _out_proj_last(ogs, xt, *out_args)
        nxt = (layer + 1) // 2
        if layer % 2 == 0:
            xt, *proj = _out_in(ogs, xt, *out_args, False, odd_args(nxt), layer == 0)
        else:
            xt, *proj = _out_in(ogs, xt, *out_args, True, even_args(nxt), False)


def kernel(x, ev_w_in, ev_w_out, ev_lam, ev_subln, ev_qnorm, ev_knorm, ev_ln_g, ev_ln_b,
           od_w_in, od_qnorm, od_kvnorm, od_w_qb, od_w_kvb, od_w_out, od_ln_g, od_ln_b):
    return _forward(x, ev_w_in, ev_w_out, ev_lam, ev_subln, ev_qnorm, ev_knorm, ev_ln_g,
                    ev_ln_b, od_w_in, od_qnorm, od_kvnorm, od_w_qb, od_w_kvb, od_w_out,
                    od_ln_g, od_ln_b)
```

```python
import functools
import math

import jax
import jax.numpy as jnp
from jax import lax
from jax.experimental import pallas as pl
from jax.experimental.pallas import tpu as pltpu

F32 = jnp.float32
BF16 = jnp.bfloat16

D_MODEL = 1024
SEQ = 4096
DEPTH = 4
GRID_W = 64
ROPE_THETA = 500000.0
AXIAL_THETA = 10000.0
LN_EPS = 1e-5
RMS_EPS = 1e-6

A_HEADS = 4
A_QK_DIM = 64
A_V_DIM = 128
A_WIDTH = 512
A_ROT = 16
B_Q_HEADS = 8
B_KV_HEADS = 2
B_GROUP = 4
B_DIM = 64
B_WIDTH = 512
EV_IN = 3328

C_HEADS = 16
C_NOPE = 64
C_ROPE = 32
C_V = 64
C_Q_LORA = 256
C_KV_LORA = 128
C_QK = C_NOPE + C_ROPE
OD_IN = 1440

ALPHA = (2 * DEPTH) ** 0.25
LOG2E = 1.4426950408889634

QSCALE_AB = A_QK_DIM ** -0.5 * LOG2E
QSCALE_C = C_QK ** -0.5 * LOG2E

LANES = 128
MXU_TILE = 256
LOOKAHEAD_ITEMS = {1: 2, 2: 3, 4: 8}
MAX_LOCKSTEP = 4
ONES_ROWS = 16
K_PAD = 128
NORM_ROWS = 16
SAFE_BOUND = 60.0
BOUND_SLACK = 1.02
TM = 512
SUB_TILES = 2
EVEN_TAB_POS, EVEN_OUT_TOKEN_AXES = 1, (2, 2, 3, 2, 1, 3, 2, 2, 2)
ODD_TAB_POS, ODD_OUT_TOKEN_AXES = 3, (2, 2, 3, 2, 2, 2)
KV_CHUNK_A = 512
KV_CHUNK_BC = 256
TQ_A = 2048
TQ_B = 4096
TQ_C = 4096
VMEM_LIMIT = 56 * 1024 * 1024


def _rep(ref, n):
    a = ref[...]
    return jnp.concatenate([a] * (n // LANES), axis=1)


def _rot(x1, x2, cos, sin):
    return x1 * cos - x2 * sin, x2 * cos + x1 * sin


def _silu(x):
    return x * jax.nn.sigmoid(x)


def _sumsq(x):
    return jnp.sum(x * x, axis=0, keepdims=True)


def _tok(i, j):
    return (i, 0, j)


def _const2(i, j):
    return (0, 0)


def _resident(shape):
    return pl.BlockSpec(shape, _const2, pipeline_mode=pl.Buffered(1))


def _proj_params():
    return pltpu.CompilerParams(
        dimension_semantics=("parallel", "parallel"), vmem_limit_bytes=VMEM_LIMIT)


def _even_in_body(xb, w_ref, tab_ref, qn_ref, kn_ref,
                  qa_ref, ka_ref, va_ref, qb_ref, kb_ref, vb_ref, g_ref, nq_ref, nk_ref,
                  *, x_token_major=False):
    contract = (((1,), (1 if x_token_major else 0,)), ((), ()))

    def proj(r0, r1):
        return lax.dot_general(w_ref[r0:r1, :], xb, contract, preferred_element_type=F32)

    cos_a, sin_a = tab_ref[0:8, :], tab_ref[8:16, :]
    cos_r, sin_r = tab_ref[16:32, :], tab_ref[32:48, :]
    cos_c, sin_c = tab_ref[48:64, :], tab_ref[64:80, :]

    def rope_a(h):
        outs = []
        for hc in range(2 * A_HEADS):
            b = hc * A_QK_DIM
            r1, r2 = _rot(h[b:b + 8], h[b + 8:b + 16], cos_a, sin_a)
            outs.append(jnp.concatenate([r1, r2, h[b + 16:b + 64]], axis=0))
        return outs

    g_ref[0] = _silu(proj(2304, 3328)).astype(BF16)
    qs = [x * QSCALE_AB for x in rope_a(proj(0, 512))]
    qa_ref[0] = jnp.concatenate(qs, axis=0).astype(BF16)
    ks = rope_a(proj(512, 1024))
    nq = [_sumsq(x) for x in qs]
    nk = [_sumsq(x) for x in ks]
    for h in range(A_HEADS):
        kt = jnp.concatenate([ks[2 * h], ks[2 * h + 1]], axis=0)
        ka_ref[0, h] = kt.T.astype(BF16)

    def norm_axial(h, g):
        ms = jnp.mean(h * h, axis=0, keepdims=True)
        y = h * lax.rsqrt(ms + RMS_EPS) * g
        a1, a2 = _rot(y[0:16], y[16:32], cos_r, sin_r)
        b1, b2 = _rot(y[32:48], y[48:64], cos_c, sin_c)
        return jnp.concatenate([a1, a2, b1, b2], axis=0)

    width = xb.shape[0 if x_token_major else 1]
    qn = _rep(qn_ref, width)
    kn = _rep(kn_ref, width)
    hq = proj(1536, 2048)
    qs = [norm_axial(hq[h * 64:(h + 1) * 64], qn) * QSCALE_AB for h in range(B_Q_HEADS)]
    qb_ref[0] = jnp.concatenate(qs, axis=0).astype(BF16)
    hkv = proj(2048, 2304)
    ks = [norm_axial(hkv[h * 64:(h + 1) * 64], kn) for h in range(B_KV_HEADS)]
    kb_ref[0] = jnp.concatenate(ks, axis=0).T.astype(BF16)
    nq_ref[0] = jnp.concatenate(nq + [_sumsq(x) for x in qs], axis=0)
    nk = nk + [_sumsq(x) for x in ks]
    nk_ref[0] = jnp.concatenate(nk + [jnp.zeros_like(nk[0])] * (NORM_ROWS - len(nk)), axis=0)
    vb_ref[0, 0] = hkv[128:256].astype(BF16)
    va_ref[0, 0] = proj(1024, 1536).astype(BF16)


def _even_in_kernel(x_ref, *refs):
    n_in = len(refs) - len(EVEN_OUT_TOKEN_AXES)
    width = TM // SUB_TILES
    for t in range(SUB_TILES):
        off = t * width
        params = [_tokens(r, 1, off, width) if p == EVEN_TAB_POS else r
                  for p, r in enumerate(refs[:n_in])]
        views = [_tokens(r, ax, off, width) for r, ax in zip(refs[n_in:], EVEN_OUT_TOKEN_AXES)]
        _even_in_body(x_ref[0, pl.ds(off, width), :].astype(BF16), *params, *views,
                      x_token_major=True)


def _even_in_specs(b):
    d, s = D_MODEL, SEQ
    nt = s // TM
    tok = _tok
    chunk = lambda i, j: (i, j, 0, 0)
    return (
        [
            _resident((EV_IN, d)),
            pl.BlockSpec((80, TM), lambda i, j: (0, j)),
            _resident((B_DIM, LANES)),
            _resident((B_DIM, LANES)),
        ],
        [
            pl.BlockSpec((1, 512, TM), tok),
            pl.BlockSpec((1, A_HEADS, TM, K_PAD), lambda i, j: (i, 0, j, 0)),
            pl.BlockSpec((1, 1, 512, TM), chunk),
            pl.BlockSpec((1, 512, TM), tok),
            pl.BlockSpec((1, TM, K_PAD), lambda i, j: (i, j, 0)),
            pl.BlockSpec((1, 1, 128, TM), chunk),
            pl.BlockSpec((1, 1024, TM), tok),
            pl.BlockSpec((1, NORM_ROWS, TM), tok),
            pl.BlockSpec((1, NORM_ROWS, TM), tok),
        ],
        [
            jax.ShapeDtypeStruct((b, 512, s), BF16),
            jax.ShapeDtypeStruct((b, A_HEADS, s, K_PAD), BF16),
            jax.ShapeDtypeStruct((b, nt, 512, TM), BF16),
            jax.ShapeDtypeStruct((b, 512, s), BF16),
            jax.ShapeDtypeStruct((b, s, K_PAD), BF16),
            jax.ShapeDtypeStruct((b, nt, 128, TM), BF16),
            jax.ShapeDtypeStruct((b, 1024, s), BF16),
            jax.ShapeDtypeStruct((b, NORM_ROWS, s), F32),
            jax.ShapeDtypeStruct((b, NORM_ROWS, s), F32),
        ],
    )


def _even_in(x, w_t, tab, qn, kn):
    b = x.shape[0]
    in_specs, out_specs, out_shape = _even_in_specs(b)
    return pl.pallas_call(
        _even_in_kernel,
        grid=(b, SEQ // TM),
        in_specs=[pl.BlockSpec((1, TM, D_MODEL), lambda i, j: (i, j, 0))] + in_specs,
        out_specs=out_specs,
        out_shape=out_shape,
        compiler_params=_proj_params(),
        name="even_in_proj",
    )(x, w_t, tab, qn, kn)


def _odd_in_body(xb, w_ref, wq_ref, wkv_ref, tab_ref, qn_ref, kvn_ref,
                 q_ref, k_ref, v_ref, g_ref, nq_ref, nk_ref):
    cos, sin = tab_ref[0:16, :], tab_ref[16:32, :]

    def rms(h, g):
        ms = jnp.mean(h * h, axis=0, keepdims=True)
        return h * lax.rsqrt(ms + RMS_EPS) * g

    g_ref[0] = _silu(jnp.dot(w_ref[416:1440, :], xb, preferred_element_type=F32)).astype(BF16)
    lat = jnp.dot(w_ref[0:416, :], xb, preferred_element_type=F32)
    width = xb.shape[1]
    cqn = rms(lat[0:256], _rep(qn_ref, width)).astype(BF16)
    q = jnp.dot(wq_ref[...], cqn, preferred_element_type=F32)
    qs = []
    for h in range(C_HEADS):
        b = h * C_QK
        r1, r2 = _rot(q[b + 64:b + 80], q[b + 80:b + 96], cos, sin)
        qs.append(jnp.concatenate([q[b:b + 64], r1, r2], axis=0) * QSCALE_C)
    q_ref[0] = jnp.concatenate(qs, axis=0).astype(BF16)
    nq_ref[0] = jnp.concatenate([_sumsq(x) for x in qs], axis=0)

    ckvn = rms(lat[256:384], _rep(kvn_ref, width)).astype(BF16)
    kv = jnp.dot(wkv_ref[...], ckvn, preferred_element_type=F32)
    r1, r2 = _rot(lat[384:400], lat[400:416], cos, sin)
    zpad = jnp.zeros((K_PAD - C_QK, width), F32)
    nkr = _sumsq(r1) + _sumsq(r2)
    nk = []
    for h in range(C_HEADS):
        kn = kv[h * 128:h * 128 + 64]
        nk.append(_sumsq(kn) + nkr)
        kt = jnp.concatenate([kn, r1, r2, zpad], axis=0)
        k_ref[0, h] = kt.T.astype(BF16)
    nk_ref[0] = jnp.concatenate(nk, axis=0)
    v_ref[0, 0] = jnp.concatenate(
        [kv[h * 128 + 64:h * 128 + 128] for h in range(C_HEADS)], axis=0).astype(BF16)


def _odd_in_specs(b):
    d, s = D_MODEL, SEQ
    nt = s // TM
    tok = _tok
    return (
        [
            _resident((OD_IN, d)),
            _resident((C_HEADS * C_QK, C_Q_LORA)),
            _resident((C_HEADS * 128, C_KV_LORA)),
            pl.BlockSpec((32, TM), lambda i, j: (0, j)),
            _resident((C_Q_LORA, LANES)),
            _resident((C_KV_LORA, LANES)),
        ],
        [
            pl.BlockSpec((1, C_HEADS * C_QK, TM), tok),
            pl.BlockSpec((1, C_HEADS, TM, K_PAD), lambda i, j: (i, 0, j, 0)),
            pl.BlockSpec((1, 1, 1024, TM), lambda i, j: (i, j, 0, 0)),
            pl.BlockSpec((1, 1024, TM), tok),
            pl.BlockSpec((1, NORM_ROWS, TM), tok),
            pl.BlockSpec((1, NORM_ROWS, TM), tok),
        ],
        [
            jax.ShapeDtypeStruct((b, C_HEADS * C_QK, s), BF16),
            jax.ShapeDtypeStruct((b, C_HEADS, s, K_PAD), BF16),
            jax.ShapeDtypeStruct((b, nt, 1024, TM), BF16),
            jax.ShapeDtypeStruct((b, 1024, s), BF16),
            jax.ShapeDtypeStruct((b, NORM_ROWS, s), F32),
            jax.ShapeDtypeStruct((b, NORM_ROWS, s), F32),
        ],
    )


def _softmax_pv(qps, k_at, v_ref, dv, kv_chunk, bounds=None):
    nsets = len(qps)
    ntiles = qps[0].shape[1] // MXU_TILE
    qps = [qp[:, t * MXU_TILE:(t + 1) * MXU_TILE] for qp in qps for t in range(ntiles)]
    n = len(qps)
    nchunks = SEQ // kv_chunk
    halves = kv_chunk // MXU_TILE
    ones = jnp.ones((ONES_ROWS, kv_chunk), BF16)

    def scores(ci, j):
        return [jnp.dot(k_at(ci * kv_chunk + h * MXU_TILE), qps[j], preferred_element_type=F32)
                for h in range(halves)]

    m = [None] * n
    acc = [None] * n
    lsum = [None] * n
    items = [(ci, j) for g in range(0, n, MAX_LOCKSTEP) for ci in range(nchunks)
             for j in range(g, min(g + MAX_LOCKSTEP, n))]
    lookahead = LOOKAHEAD_ITEMS[min(n, MAX_LOCKSTEP)]
    pending = [scores(*it) for it in items[:lookahead]]
    for idx, (ci, j) in enumerate(items):
        if idx + lookahead < len(items):
            pending.append(scores(*items[idx + lookahead]))
        sc = pending.pop(0)
        r0 = ci * kv_chunk
        v = v_ref[0, r0 // TM, :, pl.ds(r0 % TM, kv_chunk)]
        if bounds is None:
            v = jnp.concatenate([v, ones], axis=0)
            cm = functools.reduce(jnp.maximum, [jnp.max(s, axis=0, keepdims=True) for s in sc])
            ref = cm if ci == 0 else jnp.maximum(m[j], cm)
        else:
            ref = bounds[j // ntiles]
        ps = [jnp.exp2(s - ref) for s in sc]
        pv = functools.reduce(jnp.add, [
            jnp.dot(v[:, h * MXU_TILE:(h + 1) * MXU_TILE], ps[h].astype(BF16),
                    preferred_element_type=F32) for h in range(halves)])
        if bounds is not None:
            part = functools.reduce(jnp.add, [
                jnp.sum(p.reshape(MXU_TILE // 8, 8, MXU_TILE), axis=0) for p in ps])
            lsum[j] = part if ci == 0 else lsum[j] + part
        if ci == 0:
            acc[j] = pv
        elif bounds is None:
            acc[j] = acc[j] * jnp.exp2(m[j] - ref) + pv
        else:
            acc[j] = acc[j] + pv
        m[j] = ref
    out = []
    for j in range(nsets):
        a = jnp.concatenate(acc[j * ntiles:(j + 1) * ntiles], axis=1)
        if bounds is None:
            out.append((a[0:dv], a[dv:dv + 1]))
        else:
            l8 = jnp.concatenate(lsum[j * ntiles:(j + 1) * ntiles], axis=1)
            out.append((a, jnp.sum(l8, axis=0, keepdims=True)))
    return out


def _attend(qps, bounds, k_at, v_ref, dv, kv_chunk, finish):
    safe = functools.reduce(jnp.logical_and, [b <= SAFE_BOUND for b in bounds])

    @pl.when(safe)
    def _():
        finish(_softmax_pv(qps, k_at, v_ref, dv, kv_chunk, bounds))

    @pl.when(jnp.logical_not(safe))
    def _():
        finish(_softmax_pv(qps, k_at, v_ref, dv, kv_chunk))


def _attn_params():
    return pltpu.CompilerParams(
        dimension_semantics=("parallel", "parallel", "parallel"), vmem_limit_bytes=VMEM_LIMIT)


def _attn_a_kernel(bnd_ref, lam_ref, q_ref, k_ref, v_ref, gate_ref, sg_ref, o_ref, *, lam_init):
    i, h = pl.program_id(0), pl.program_id(1)
    q = q_ref[0]
    tq = q.shape[1]
    z = jnp.zeros((A_QK_DIM, tq), BF16)
    qps = [jnp.concatenate([q[0:64], z], axis=0), jnp.concatenate([z, q[64:128]], axis=0)]

    def finish(res):
        (a0, l0), (a1, l1) = res
        lp = lam_ref[...]
        lam = (jnp.exp(jnp.sum(lp[0:1] * lp[1:2], axis=1, keepdims=True))
               - jnp.exp(jnp.sum(lp[2:3] * lp[3:4], axis=1, keepdims=True)) + lam_init)
        o = a0 / l0 - lam * (a1 / l1)
        ms = jnp.mean(o * o, axis=0, keepdims=True)
        o = o * lax.rsqrt(ms + RMS_EPS) * _rep(sg_ref, tq) * (1.0 - lam_init)
        o_ref[0] = (o * gate_ref[0]).astype(BF16)

    _attend(qps, [bnd_ref[i, 2 * h], bnd_ref[i, 2 * h + 1]],
            lambda r0: k_ref[0, 0, pl.ds(r0, MXU_TILE), :], v_ref, A_V_DIM, KV_CHUNK_A, finish)


def _attn_a(bnd, lam_p, qa, ka, va, gate, sg, lam_init):
    b = qa.shape[0]
    nt = SEQ // TM
    return pl.pallas_call(
        functools.partial(_attn_a_kernel, lam_init=lam_init),
        grid=(b, A_HEADS, SEQ // TQ_A),
        in_specs=[
            pl.BlockSpec(memory_space=pltpu.SMEM),
            pl.BlockSpec((4, A_QK_DIM), lambda i, h, t: (0, 0)),
            pl.BlockSpec((1, 128, TQ_A), lambda i, h, t: (i, h, t)),
            pl.BlockSpec((1, 1, SEQ, K_PAD), lambda i, h, t: (i, h, 0, 0)),
            pl.BlockSpec((1, nt, A_V_DIM, TM), lambda i, h, t: (i, 0, h, 0)),
            pl.BlockSpec((1, A_V_DIM, TQ_A), lambda i, h, t: (i, h, t)),
            pl.BlockSpec((A_V_DIM, LANES), lambda i, h, t: (0, 0)),
        ],
        out_specs=pl.BlockSpec((1, A_V_DIM, TQ_A), lambda i, h, t: (i, h, t)),
        out_shape=jax.ShapeDtypeStruct((b, A_WIDTH, SEQ), BF16),
        compiler_params=_attn_params(),
        name="attn_diff",
    )(bnd, lam_p, qa, ka, va, gate, sg)


def _attn_b_kernel(bnd_ref, q_ref, k_ref, v_ref, gate_ref, o_ref):
    i, h = pl.program_id(0), pl.program_id(1)
    first = h < B_GROUP
    q = q_ref[0]
    z = jnp.zeros_like(q)
    qp = jnp.concatenate([jnp.where(first, q, z), jnp.where(first, z, q)], axis=0)

    def finish(res):
        ((a, l),) = res
        o_ref[0] = (a / l * gate_ref[0]).astype(BF16)

    _attend([qp], [bnd_ref[i, 2 * A_HEADS + h]],
            lambda r0: k_ref[0, pl.ds(r0, MXU_TILE), :], v_ref, B_DIM, KV_CHUNK_BC, finish)


def _attn_b(bnd, qb, kb, vb, gate):
    b = qb.shape[0]
    nt = SEQ // TM
    return pl.pallas_call(
        _attn_b_kernel,
        grid=(b, B_Q_HEADS, SEQ // TQ_B),
        in_specs=[
            pl.BlockSpec(memory_space=pltpu.SMEM),
            pl.BlockSpec((1, B_DIM, TQ_B), lambda i, h, t: (i, h, t)),
            pl.BlockSpec((1, SEQ, K_PAD), lambda i, h, t: (i, 0, 0)),
            pl.BlockSpec((1, nt, B_DIM, TM), lambda i, h, t: (i, 0, h // B_GROUP, 0)),
            pl.BlockSpec((1, B_DIM, TQ_B), lambda i, h, t: (i, A_WIDTH // B_DIM + h, t)),
        ],
        out_specs=pl.BlockSpec((1, B_DIM, TQ_B), lambda i, h, t: (i, h, t)),
        out_shape=jax.ShapeDtypeStruct((b, B_WIDTH, SEQ), BF16),
        compiler_params=_attn_params(),
        name="attn_gqa",
    )(bnd, qb, kb, vb, gate)


def _attn_c_kernel(bnd_ref, q_ref, k_ref, v_ref, gate_ref, o_ref):
    i, h = pl.program_id(0), pl.program_id(1)
    q = q_ref[0]
    qp = jnp.concatenate([q, jnp.zeros((K_PAD - C_QK, q.shape[1]), BF16)], axis=0)

    def finish(res):
        ((a, l),) = res
        o_ref[0] = (a / l * gate_ref[0]).astype(BF16)

    _attend([qp], [bnd_ref[i, h]],
            lambda r0: k_ref[0, 0, pl.ds(r0, MXU_TILE), :], v_ref, C_V, KV_CHUNK_BC, finish)


def _attn_c(bnd, q, k, v, gate):
    b = q.shape[0]
    nt = SEQ // TM
    return pl.pallas_call(
        _attn_c_kernel,
        grid=(b, C_HEADS, SEQ // TQ_C),
        in_specs=[
            pl.BlockSpec(memory_space=pltpu.SMEM),
            pl.BlockSpec((1, C_QK, TQ_C), lambda i, h, t: (i, h, t)),
            pl.BlockSpec((1, 1, SEQ, K_PAD), lambda i, h, t: (i, h, 0, 0)),
            pl.BlockSpec((1, nt, C_V, TM), lambda i, h, t: (i, 0, h, 0)),
            pl.BlockSpec((1, C_V, TQ_C), lambda i, h, t: (i, h, t)),
        ],
        out_specs=pl.BlockSpec((1, C_V, TQ_C), lambda i, h, t: (i, h, t)),
        out_shape=jax.ShapeDtypeStruct((b, C_HEADS * C_V, SEQ), BF16),
        compiler_params=_attn_params(),
        name="attn_mla",
    )(bnd, q, k, v, gate)


def _out_body(og_refs, x_ref, w_ref, lg_ref, lb_ref, x_token_major):
    og = jnp.concatenate([r[0] for r in og_refs], axis=0) if len(og_refs) > 1 else og_refs[0][0]
    y = jnp.dot(w_ref[...], og, preferred_element_type=F32)
    x = x_ref[0].T if x_token_major else x_ref[0]
    z = ALPHA * x + y
    mu = jnp.mean(z, axis=0, keepdims=True)
    d = z - mu
    var = jnp.mean(d * d, axis=0, keepdims=True)
    width = y.shape[1]
    return d * lax.rsqrt(var + LN_EPS) * _rep(lg_ref, width) + _rep(lb_ref, width)


def _out_kernel(*refs, n_og):
    x_ref, w_ref, lg_ref, lb_ref, o_ref = refs[n_og:]
    width = TM // SUB_TILES
    for t in range(SUB_TILES):
        off = t * width
        og_views = [_tokens(r, 2, off, width) for r in refs[:n_og]]
        out = _out_body(og_views, _tokens(x_ref, 2, off, width), w_ref, lg_ref, lb_ref, False)
        o_ref[0, pl.ds(off, width), :] = out.T


def _tokens(ref, axis, off, width):
    idx = [slice(None)] * len(ref.shape)
    idx[axis] = pl.ds(off, width)
    return ref.at[tuple(idx)]


def _out_in_kernel(*refs, n_og, n_in, x_token_major, in_body, tab_pos, out_axes):
    x_ref, w_ref, lg_ref, lb_ref = refs[n_og:n_og + 4]
    in_params = refs[n_og + 4:n_og + 4 + n_in]
    xo_ref = refs[n_og + 4 + n_in]
    in_outs = refs[n_og + 5 + n_in:]
    width = TM // SUB_TILES
    outs = []
    for t in range(SUB_TILES):
        off = t * width
        og_views = [_tokens(r, 2, off, width) for r in refs[:n_og]]
        x_view = _tokens(x_ref, 1 if x_token_major else 2, off, width)
        out = _out_body(og_views, x_view, w_ref, lg_ref, lb_ref, x_token_major)
        xo_ref[0, :, pl.ds(off, width)] = out
        outs.append(out.astype(BF16))
    for t in range(SUB_TILES):
        off = t * width
        params = [_tokens(r, 1, off, width) if p == tab_pos else r
                  for p, r in enumerate(in_params)]
        views = [_tokens(r, ax, off, width) for r, ax in zip(in_outs, out_axes)]
        in_body(outs[t], *params, *views)


def _out_specs(ogs, x_token_major):
    d = D_MODEL
    x_spec = (pl.BlockSpec((1, TM, d), lambda i, j: (i, j, 0)) if x_token_major
              else pl.BlockSpec((1, d, TM), _tok))
    return [pl.BlockSpec((1, og.shape[1], TM), _tok) for og in ogs] + [
        x_spec, _resident((d, d)), _resident((d, LANES)), _resident((d, LANES))]


def _out_proj_last(ogs, xt, w_t, lg, lb):
    b = xt.shape[0]
    return pl.pallas_call(
        functools.partial(_out_kernel, n_og=len(ogs)),
        grid=(b, SEQ // TM),
        in_specs=_out_specs(ogs, False),
        out_specs=pl.BlockSpec((1, TM, D_MODEL), lambda i, j: (i, j, 0)),
        out_shape=jax.ShapeDtypeStruct((b, SEQ, D_MODEL), F32),
        compiler_params=_proj_params(),
        name="out_proj_ln",
    )(*ogs, xt, w_t, lg, lb)


def _out_in(ogs, xt, w_t, lg, lb, next_even, in_args, x_token_major):
    b = xt.shape[0]
    if next_even:
        in_body, (in_specs, out_specs, out_shape) = _even_in_body, _even_in_specs(b)
        tab_pos, out_axes, name = EVEN_TAB_POS, EVEN_OUT_TOKEN_AXES, "out_even_in_proj"
    else:
        in_body, (in_specs, out_specs, out_shape) = _odd_in_body, _odd_in_specs(b)
        tab_pos, out_axes, name = ODD_TAB_POS, ODD_OUT_TOKEN_AXES, "out_odd_in_proj"
    return pl.pallas_call(
        functools.partial(_out_in_kernel, n_og=len(ogs), n_in=len(in_args),
                          x_token_major=x_token_major, in_body=in_body, tab_pos=tab_pos,
                          out_axes=out_axes),
        grid=(b, SEQ // TM),
        in_specs=_out_specs(ogs, x_token_major) + in_specs,
        out_specs=[pl.BlockSpec((1, D_MODEL, TM), _tok)] + out_specs,
        out_shape=[jax.ShapeDtypeStruct((b, D_MODEL, SEQ), F32)] + out_shape,
        compiler_params=_proj_params(),
        name=name,
    )(*ogs, xt, w_t, lg, lb, *in_args)


def _col(v):
    v = v.astype(F32)
    return jnp.broadcast_to(v[:, None], (v.shape[0], LANES))


def _score_bounds(nq, kmax2):
    return jnp.sqrt(jnp.max(nq, axis=-1) * kmax2) * BOUND_SLACK


def _angles_t(pos, dims, theta):
    inv = theta ** (-jnp.arange(0, dims, 2, dtype=F32) / dims)
    ang = pos.astype(F32)[:, None] * inv[None, :]
    return jnp.cos(ang).T, jnp.sin(ang).T


@jax.jit
def _forward(x, ev_w_in, ev_w_out, ev_lam, ev_subln, ev_qnorm, ev_knorm, ev_ln_g, ev_ln_b,
             od_w_in, od_qnorm, od_kvnorm, od_w_qb, od_w_kvb, od_w_out, od_ln_g, od_ln_b):
    s = x.shape[1]
    pos = jnp.arange(s, dtype=jnp.int32)
    row = jnp.repeat(jnp.arange(s // GRID_W, dtype=jnp.int32), GRID_W)
    col = jnp.tile(jnp.arange(GRID_W, dtype=jnp.int32), s // GRID_W)
    tab_ev = jnp.concatenate(
        _angles_t(pos, A_ROT, ROPE_THETA) + _angles_t(row, B_DIM // 2, AXIAL_THETA)
        + _angles_t(col, B_DIM // 2, AXIAL_THETA), axis=0)
    tab_od = jnp.concatenate(_angles_t(pos, C_ROPE, ROPE_THETA), axis=0)

    def even_args(i):
        return (ev_w_in[i].T.astype(BF16), tab_ev, _col(ev_qnorm[i]), _col(ev_knorm[i]))

    def odd_args(i):
        return (od_w_in[i].T.astype(BF16), od_w_qb[i].T.astype(BF16),
                od_w_kvb[i].T.astype(BF16), tab_od, _col(od_qnorm[i]), _col(od_kvnorm[i]))

    xt = x
    proj = _even_in(x, *even_args(0))
    for layer in range(DEPTH):
        i = layer // 2
        if layer % 2 == 0:
            qa, ka, va, qb, kb, vb, gate, nq, nk = proj
            kmax = jnp.max(nk, axis=-1)
            kmax = jnp.concatenate(
                [kmax[:, :2 * A_HEADS],
                 jnp.repeat(kmax[:, 2 * A_HEADS:2 * A_HEADS + B_KV_HEADS], B_GROUP, axis=1)],
                axis=1)
            bnd = _score_bounds(nq, kmax)
            lam_init = 0.8 - 0.6 * math.exp(-0.3 * layer)
            ogs = [_attn_a(bnd, ev_lam[i].astype(F32), qa, ka, va, gate, _col(ev_subln[i]),
                           lam_init),
                   _attn_b(bnd, qb, kb, vb, gate)]
            out_args = (ev_w_out[i].T.astype(BF16), _col(ev_ln_g[i]), _col(ev_ln_b[i]))
        else:
            q, k, v, gate, nq, nk = proj
            ogs = [_attn_c(_score_bounds(nq, jnp.max(nk, axis=-1)), q, k, v, gate)]
            out_args = (od_w_out[i].T.astype(BF16), _col(od_ln_g[i]), _col(od_ln_b[i]))
        if layer == DEPTH - 1:
            return _out_proj_last(ogs, xt, *out_args)
        nxt = (layer + 1) // 2
        if layer % 2 == 0:
            xt, *proj = _out_in(ogs, xt, *out_args, False, odd_args(nxt), layer == 0)
        else:
            xt, *proj = _out_in(ogs, xt, *out_args, True, even_args(nxt), False)


def kernel(x, ev_w_in, ev_w_out, ev_lam, ev_subln, ev_qnorm, ev_knorm, ev_ln_g, ev_ln_b,
           od_w_in, od_qnorm, od_kvnorm, od_w_qb, od_w_kvb, od_w_out, od_ln_g, od_ln_b):
    return _forward(x, ev_w_in, ev_w_out, ev_lam, ev_subln, ev_qnorm, ev_knorm, ev_ln_g,
                    ev_ln_b, od_w_in, od_qnorm, od_kvnorm, od_w_qb, od_w_kvb, od_w_out,
                    od_ln_g, od_ln_b)
```

```python
import functools
import math

import jax
import jax.numpy as jnp
from jax import lax
from jax.experimental import pallas as pl
from jax.experimental.pallas import tpu as pltpu

F32 = jnp.float32
BF16 = jnp.bfloat16

D_MODEL = 1024
SEQ = 4096
DEPTH = 4
GRID_W = 64
ROPE_THETA = 500000.0
AXIAL_THETA = 10000.0
LN_EPS = 1e-5
RMS_EPS = 1e-6

A_HEADS = 4
A_QK_DIM = 64
A_V_DIM = 128
A_WIDTH = 512
A_ROT = 16
B_Q_HEADS = 8
B_KV_HEADS = 2
B_GROUP = 4
B_DIM = 64
B_WIDTH = 512
EV_IN = 3328

C_HEADS = 16
C_NOPE = 64
C_ROPE = 32
C_V = 64
C_Q_LORA = 256
C_KV_LORA = 128
C_QK = C_NOPE + C_ROPE
OD_IN = 1440

ALPHA = (2 * DEPTH) ** 0.25
LOG2E = 1.4426950408889634

QSCALE_AB = A_QK_DIM ** -0.5 * LOG2E
QSCALE_C = C_QK ** -0.5 * LOG2E

LANES = 128
MXU_TILE = 256
LOOKAHEAD_ITEMS = {1: 2, 2: 3, 4: 8}
MAX_LOCKSTEP = 4
ONES_ROWS = 16
K_PAD = 128
NORM_ROWS = 16
SAFE_BOUND = 60.0
BOUND_SLACK = 1.02
TM = 512
SUB_TILES = 2
EVEN_TAB_POS, EVEN_OUT_TOKEN_AXES = 1, (2, 2, 3, 2, 1, 3, 2, 2, 2)
ODD_TAB_POS, ODD_OUT_TOKEN_AXES = 3, (2, 2, 3, 2, 2, 2)
KV_CHUNK_A = 256
KV_CHUNK_BC = 256
TQ_A = 2048
TQ_B = 4096
TQ_C = 4096
VMEM_LIMIT = 56 * 1024 * 1024


def _rep(ref, n):
    a = ref[...]
    return jnp.concatenate([a] * (n // LANES), axis=1)


def _rot(x1, x2, cos, sin):
    return x1 * cos - x2 * sin, x2 * cos + x1 * sin


def _silu(x):
    return x * jax.nn.sigmoid(x)


def _sumsq(x):
    return jnp.sum(x * x, axis=0, keepdims=True)


def _tok(i, j):
    return (i, 0, j)


def _const2(i, j):
    return (0, 0)


def _resident(shape):
    return pl.BlockSpec(shape, _const2, pipeline_mode=pl.Buffered(1))


def _proj_params():
    return pltpu.CompilerParams(
        dimension_semantics=("parallel", "parallel"), vmem_limit_bytes=VMEM_LIMIT)


def _even_in_body(xb, w_ref, tab_ref, qn_ref, kn_ref,
                  qa_ref, ka_ref, va_ref, qb_ref, kb_ref, vb_ref, g_ref, nq_ref, nk_ref,
                  *, x_token_major=False):
    contract = (((1,), (1 if x_token_major else 0,)), ((), ()))

    def proj(r0, r1):
        return lax.dot_general(w_ref[r0:r1, :], xb, contract, preferred_element_type=F32)

    cos_a, sin_a = tab_ref[0:8, :], tab_ref[8:16, :]
    cos_r, sin_r = tab_ref[16:32, :], tab_ref[32:48, :]
    cos_c, sin_c = tab_ref[48:64, :], tab_ref[64:80, :]

    def rope_a(h):
        outs = []
        for hc in range(2 * A_HEADS):
            b = hc * A_QK_DIM
            r1, r2 = _rot(h[b:b + 8], h[b + 8:b + 16], cos_a, sin_a)
            outs.append(jnp.concatenate([r1, r2, h[b + 16:b + 64]], axis=0))
        return outs

    g_ref[0] = _silu(proj(2304, 3328)).astype(BF16)
    qs = [x * QSCALE_AB for x in rope_a(proj(0, 512))]
    qa_ref[0] = jnp.concatenate(qs, axis=0).astype(BF16)
    ks = rope_a(proj(512, 1024))
    nq = [_sumsq(x) for x in qs]
    nk = [_sumsq(x) for x in ks]
    for h in range(A_HEADS):
        kt = jnp.concatenate([ks[2 * h], ks[2 * h + 1]], axis=0)
        ka_ref[0, h] = kt.T.astype(BF16)

    def norm_axial(h, g):
        ms = jnp.mean(h * h, axis=0, keepdims=True)
        y = h * lax.rsqrt(ms + RMS_EPS) * g
        a1, a2 = _rot(y[0:16], y[16:32], cos_r, sin_r)
        b1, b2 = _rot(y[32:48], y[48:64], cos_c, sin_c)
        return jnp.concatenate([a1, a2, b1, b2], axis=0)

    width = xb.shape[0 if x_token_major else 1]
    qn = _rep(qn_ref, width)
    kn = _rep(kn_ref, width)
    hq = proj(1536, 2048)
    qs = [norm_axial(hq[h * 64:(h + 1) * 64], qn) * QSCALE_AB for h in range(B_Q_HEADS)]
    qb_ref[0] = jnp.concatenate(qs, axis=0).astype(BF16)
    hkv = proj(2048, 2304)
    ks = [norm_axial(hkv[h * 64:(h + 1) * 64], kn) for h in range(B_KV_HEADS)]
    kb_ref[0] = jnp.concatenate(ks, axis=0).T.astype(BF16)
    nq_ref[0] = jnp.concatenate(nq + [_sumsq(x) for x in qs], axis=0)
    nk = nk + [_sumsq(x) for x in ks]
    nk_ref[0] = jnp.concatenate(nk + [jnp.zeros_like(nk[0])] * (NORM_ROWS - len(nk)), axis=0)
    vb_ref[0, 0] = hkv[128:256].astype(BF16)
    va_ref[0, 0] = proj(1024, 1536).astype(BF16)


def _even_in_kernel(x_ref, *refs):
    n_in = len(refs) - len(EVEN_OUT_TOKEN_AXES)
    width = TM // SUB_TILES
    for t in range(SUB_TILES):
        off = t * width
        params = [_tokens(r, 1, off, width) if p == EVEN_TAB_POS else r
                  for p, r in enumerate(refs[:n_in])]
        views = [_tokens(r, ax, off, width) for r, ax in zip(refs[n_in:], EVEN_OUT_TOKEN_AXES)]
        _even_in_body(x_ref[0, pl.ds(off, width), :].astype(BF16), *params, *views,
                      x_token_major=True)


def _even_in_specs(b):
    d, s = D_MODEL, SEQ
    nt = s // TM
    tok = _tok
    chunk = lambda i, j: (i, j, 0, 0)
    return (
        [
            _resident((EV_IN, d)),
            pl.BlockSpec((80, TM), lambda i, j: (0, j)),
            _resident((B_DIM, LANES)),
            _resident((B_DIM, LANES)),
        ],
        [
            pl.BlockSpec((1, 512, TM), tok),
            pl.BlockSpec((1, A_HEADS, TM, K_PAD), lambda i, j: (i, 0, j, 0)),
            pl.BlockSpec((1, 1, 512, TM), chunk),
            pl.BlockSpec((1, 512, TM), tok),
            pl.BlockSpec((1, TM, K_PAD), lambda i, j: (i, j, 0)),
            pl.BlockSpec((1, 1, 128, TM), chunk),
            pl.BlockSpec((1, 1024, TM), tok),
            pl.BlockSpec((1, NORM_ROWS, TM), tok),
            pl.BlockSpec((1, NORM_ROWS, TM), tok),
        ],
        [
            jax.ShapeDtypeStruct((b, 512, s), BF16),
            jax.ShapeDtypeStruct((b, A_HEADS, s, K_PAD), BF16),
            jax.ShapeDtypeStruct((b, nt, 512, TM), BF16),
            jax.ShapeDtypeStruct((b, 512, s), BF16),
            jax.ShapeDtypeStruct((b, s, K_PAD), BF16),
            jax.ShapeDtypeStruct((b, nt, 128, TM), BF16),
            jax.ShapeDtypeStruct((b, 1024, s), BF16),
            jax.ShapeDtypeStruct((b, NORM_ROWS, s), F32),
            jax.ShapeDtypeStruct((b, NORM_ROWS, s), F32),
        ],
    )


def _even_in(x, w_t, tab, qn, kn):
    b = x.shape[0]
    in_specs, out_specs, out_shape = _even_in_specs(b)
    return pl.pallas_call(
        _even_in_kernel,
        grid=(b, SEQ // TM),
        in_specs=[pl.BlockSpec((1, TM, D_MODEL), lambda i, j: (i, j, 0))] + in_specs,
        out_specs=out_specs,
        out_shape=out_shape,
        compiler_params=_proj_params(),
        name="even_in_proj",
    )(x, w_t, tab, qn, kn)


def _odd_in_body(xb, w_ref, wq_ref, wkv_ref, tab_ref, qn_ref, kvn_ref,
                 q_ref, k_ref, v_ref, g_ref, nq_ref, nk_ref):
    cos, sin = tab_ref[0:16, :], tab_ref[16:32, :]

    def rms(h, g):
        ms = jnp.mean(h * h, axis=0, keepdims=True)
        return h * lax.rsqrt(ms + RMS_EPS) * g

    g_ref[0] = _silu(jnp.dot(w_ref[416:1440, :], xb, preferred_element_type=F32)).astype(BF16)
    lat = jnp.dot(w_ref[0:416, :], xb, preferred_element_type=F32)
    width = xb.shape[1]
    cqn = rms(lat[0:256], _rep(qn_ref, width)).astype(BF16)
    q = jnp.dot(wq_ref[...], cqn, preferred_element_type=F32)
    qs = []
    for h in range(C_HEADS):
        b = h * C_QK
        r1, r2 = _rot(q[b + 64:b + 80], q[b + 80:b + 96], cos, sin)
        qs.append(jnp.concatenate([q[b:b + 64], r1, r2], axis=0) * QSCALE_C)
    q_ref[0] = jnp.concatenate(qs, axis=0).astype(BF16)
    nq_ref[0] = jnp.concatenate([_sumsq(x) for x in qs], axis=0)

    ckvn = rms(lat[256:384], _rep(kvn_ref, width)).astype(BF16)
    kv = jnp.dot(wkv_ref[...], ckvn, preferred_element_type=F32)
    r1, r2 = _rot(lat[384:400], lat[400:416], cos, sin)
    zpad = jnp.zeros((K_PAD - C_QK, width), F32)
    nkr = _sumsq(r1) + _sumsq(r2)
    nk = []
    for h in range(C_HEADS):
        kn = kv[h * 128:h * 128 + 64]
        nk.append(_sumsq(kn) + nkr)
        kt = jnp.concatenate([kn, r1, r2, zpad], axis=0)
        k_ref[0, h] = kt.T.astype(BF16)
    nk_ref[0] = jnp.concatenate(nk, axis=0)
    v_ref[0, 0] = jnp.concatenate(
        [kv[h * 128 + 64:h * 128 + 128] for h in range(C_HEADS)], axis=0).astype(BF16)


def _odd_in_specs(b):
    d, s = D_MODEL, SEQ
    nt = s // TM
    tok = _tok
    return (
        [
            _resident((OD_IN, d)),
            _resident((C_HEADS * C_QK, C_Q_LORA)),
            _resident((C_HEADS * 128, C_KV_LORA)),
            pl.BlockSpec((32, TM), lambda i, j: (0, j)),
            _resident((C_Q_LORA, LANES)),
            _resident((C_KV_LORA, LANES)),
        ],
        [
            pl.BlockSpec((1, C_HEADS * C_QK, TM), tok),
            pl.BlockSpec((1, C_HEADS, TM, K_PAD), lambda i, j: (i, 0, j, 0)),
            pl.BlockSpec((1, 1, 1024, TM), lambda i, j: (i, j, 0, 0)),
            pl.BlockSpec((1, 1024, TM), tok),
            pl.BlockSpec((1, NORM_ROWS, TM), tok),
            pl.BlockSpec((1, NORM_ROWS, TM), tok),
        ],
        [
            jax.ShapeDtypeStruct((b, C_HEADS * C_QK, s), BF16),
            jax.ShapeDtypeStruct((b, C_HEADS, s, K_PAD), BF16),
            jax.ShapeDtypeStruct((b, nt, 1024, TM), BF16),
            jax.ShapeDtypeStruct((b, 1024, s), BF16),
            jax.ShapeDtypeStruct((b, NORM_ROWS, s), F32),
            jax.ShapeDtypeStruct((b, NORM_ROWS, s), F32),
        ],
    )


def _softmax_pv(qps, k_at, v_ref, dv, kv_chunk, bounds=None):
    nsets = len(qps)
    ntiles = qps[0].shape[1] // MXU_TILE
    qps = [qp[:, t * MXU_TILE:(t + 1) * MXU_TILE] for qp in qps for t in range(ntiles)]
    n = len(qps)
    nchunks = SEQ // kv_chunk
    halves = kv_chunk // MXU_TILE
    ones = jnp.ones((ONES_ROWS, kv_chunk), BF16)

    def scores(ci, j):
        return [jnp.dot(k_at(ci * kv_chunk + h * MXU_TILE), qps[j], preferred_element_type=F32)
                for h in range(halves)]

    m = [None] * n
    acc = [None] * n
    lsum = [None] * n
    items = [(ci, j) for g in range(0, n, MAX_LOCKSTEP) for ci in range(nchunks)
             for j in range(g, min(g + MAX_LOCKSTEP, n))]
    lookahead = LOOKAHEAD_ITEMS[min(n, MAX_LOCKSTEP)]
    pending = [scores(*it) for it in items[:lookahead]]
    for idx, (ci, j) in enumerate(items):
        if idx + lookahead < len(items):
            pending.append(scores(*items[idx + lookahead]))
        sc = pending.pop(0)
        r0 = ci * kv_chunk
        v = v_ref[0, r0 // TM, :, pl.ds(r0 % TM, kv_chunk)]
        if bounds is None:
            v = jnp.concatenate([v, ones], axis=0)
            cm = functools.reduce(jnp.maximum, [jnp.max(s, axis=0, keepdims=True) for s in sc])
            ref = cm if ci == 0 else jnp.maximum(m[j], cm)
        else:
            ref = bounds[j // ntiles]
        ps = [jnp.exp2(s - ref) for s in sc]
        pv = functools.reduce(jnp.add, [
            jnp.dot(v[:, h * MXU_TILE:(h + 1) * MXU_TILE], ps[h].astype(BF16),
                    preferred_element_type=F32) for h in range(halves)])
        if bounds is not None:
            part = functools.reduce(jnp.add, [
                jnp.sum(p.reshape(MXU_TILE // 8, 8, MXU_TILE), axis=0) for p in ps])
            lsum[j] = part if ci == 0 else lsum[j] + part
        if ci == 0:
            acc[j] = pv
        elif bounds is None:
            acc[j] = acc[j] * jnp.exp2(m[j] - ref) + pv
        else:
            acc[j] = acc[j] + pv
        m[j] = ref
    out = []
    for j in range(nsets):
        a = jnp.concatenate(acc[j * ntiles:(j + 1) * ntiles], axis=1)
        if bounds is None:
            out.append((a[0:dv], a[dv:dv + 1]))
        else:
            l8 = jnp.concatenate(lsum[j * ntiles:(j + 1) * ntiles], axis=1)
            out.append((a, jnp.sum(l8, axis=0, keepdims=True)))
    return out


def _attend(qps, bounds, k_at, v_ref, dv, kv_chunk, finish):
    safe = functools.reduce(jnp.logical_and, [b <= SAFE_BOUND for b in bounds])

    @pl.when(safe)
    def _():
        finish(_softmax_pv(qps, k_at, v_ref, dv, kv_chunk, bounds))

    @pl.when(jnp.logical_not(safe))
    def _():
        finish(_softmax_pv(qps, k_at, v_ref, dv, kv_chunk))


def _attn_params():
    return pltpu.CompilerParams(
        dimension_semantics=("parallel", "parallel", "parallel"), vmem_limit_bytes=VMEM_LIMIT)


def _attn_a_kernel(bnd_ref, lam_ref, q_ref, k_ref, v_ref, gate_ref, sg_ref, o_ref, *, lam_init):
    i, h = pl.program_id(0), pl.program_id(1)
    q = q_ref[0]
    tq = q.shape[1]
    z = jnp.zeros((A_QK_DIM, tq), BF16)
    qps = [jnp.concatenate([q[0:64], z], axis=0), jnp.concatenate([z, q[64:128]], axis=0)]

    def finish(res):
        (a0, l0), (a1, l1) = res
        lp = lam_ref[...]
        lam = (jnp.exp(jnp.sum(lp[0:1] * lp[1:2], axis=1, keepdims=True))
               - jnp.exp(jnp.sum(lp[2:3] * lp[3:4], axis=1, keepdims=True)) + lam_init)
        o = a0 / l0 - lam * (a1 / l1)
        ms = jnp.mean(o * o, axis=0, keepdims=True)
        o = o * lax.rsqrt(ms + RMS_EPS) * _rep(sg_ref, tq) * (1.0 - lam_init)
        o_ref[0] = (o * gate_ref[0]).astype(BF16)

    _attend(qps, [bnd_ref[i, 2 * h], bnd_ref[i, 2 * h + 1]],
            lambda r0: k_ref[0, 0, pl.ds(r0, MXU_TILE), :], v_ref, A_V_DIM, KV_CHUNK_A, finish)


def _attn_a(bnd, lam_p, qa, ka, va, gate, sg, lam_init):
    b = qa.shape[0]
    nt = SEQ // TM
    return pl.pallas_call(
        functools.partial(_attn_a_kernel, lam_init=lam_init),
        grid=(b, A_HEADS, SEQ // TQ_A),
        in_specs=[
            pl.BlockSpec(memory_space=pltpu.SMEM),
            pl.BlockSpec((4, A_QK_DIM), lambda i, h, t: (0, 0)),
            pl.BlockSpec((1, 128, TQ_A), lambda i, h, t: (i, h, t)),
            pl.BlockSpec((1, 1, SEQ, K_PAD), lambda i, h, t: (i, h, 0, 0)),
            pl.BlockSpec((1, nt, A_V_DIM, TM), lambda i, h, t: (i, 0, h, 0)),
            pl.BlockSpec((1, A_V_DIM, TQ_A), lambda i, h, t: (i, h, t)),
            pl.BlockSpec((A_V_DIM, LANES), lambda i, h, t: (0, 0)),
        ],
        out_specs=pl.BlockSpec((1, A_V_DIM, TQ_A), lambda i, h, t: (i, h, t)),
        out_shape=jax.ShapeDtypeStruct((b, A_WIDTH, SEQ), BF16),
        compiler_params=_attn_params(),
        name="attn_diff",
    )(bnd, lam_p, qa, ka, va, gate, sg)


def _attn_b_kernel(bnd_ref, q_ref, k_ref, v_ref, gate_ref, o_ref):
    i, h = pl.program_id(0), pl.program_id(1)
    first = h < B_GROUP
    q = q_ref[0]
    z = jnp.zeros_like(q)
    qp = jnp.concatenate([jnp.where(first, q, z), jnp.where(first, z, q)], axis=0)

    def finish(res):
        ((a, l),) = res
        o_ref[0] = (a / l * gate_ref[0]).astype(BF16)

    _attend([qp], [bnd_ref[i, 2 * A_HEADS + h]],
            lambda r0: k_ref[0, pl.ds(r0, MXU_TILE), :], v_ref, B_DIM, KV_CHUNK_BC, finish)


def _attn_b(bnd, qb, kb, vb, gate):
    b = qb.shape[0]
    nt = SEQ // TM
    return pl.pallas_call(
        _attn_b_kernel,
        grid=(b, B_Q_HEADS, SEQ // TQ_B),
        in_specs=[
            pl.BlockSpec(memory_space=pltpu.SMEM),
            pl.BlockSpec((1, B_DIM, TQ_B), lambda i, h, t: (i, h, t)),
            pl.BlockSpec((1, SEQ, K_PAD), lambda i, h, t: (i, 0, 0)),
            pl.BlockSpec((1, nt, B_DIM, TM), lambda i, h, t: (i, 0, h // B_GROUP, 0)),
            pl.BlockSpec((1, B_DIM, TQ_B), lambda i, h, t: (i, A_WIDTH // B_DIM + h, t)),
        ],
        out_specs=pl.BlockSpec((1, B_DIM, TQ_B), lambda i, h, t: (i, h, t)),
        out_shape=jax.ShapeDtypeStruct((b, B_WIDTH, SEQ), BF16),
        compiler_params=_attn_params(),
        name="attn_gqa",
    )(bnd, qb, kb, vb, gate)


def _attn_c_kernel(bnd_ref, q_ref, k_ref, v_ref, gate_ref, o_ref):
    i, h = pl.program_id(0), pl.program_id(1)
    q = q_ref[0]
    qp = jnp.concatenate([q, jnp.zeros((K_PAD - C_QK, q.shape[1]), BF16)], axis=0)

    def finish(res):
        ((a, l),) = res
        o_ref[0] = (a / l * gate_ref[0]).astype(BF16)

    _attend([qp], [bnd_ref[i, h]],
            lambda r0: k_ref[0, 0, pl.ds(r0, MXU_TILE), :], v_ref, C_V, KV_CHUNK_BC, finish)


def _attn_c(bnd, q, k, v, gate):
    b = q.shape[0]
    nt = SEQ // TM
    return pl.pallas_call(
        _attn_c_kernel,
        grid=(b, C_HEADS, SEQ // TQ_C),
        in_specs=[
            pl.BlockSpec(memory_space=pltpu.SMEM),
            pl.BlockSpec((1, C_QK, TQ_C), lambda i, h, t: (i, h, t)),
            pl.BlockSpec((1, 1, SEQ, K_PAD), lambda i, h, t: (i, h, 0, 0)),
            pl.BlockSpec((1, nt, C_V, TM), lambda i, h, t: (i, 0, h, 0)),
            pl.BlockSpec((1, C_V, TQ_C), lambda i, h, t: (i, h, t)),
        ],
        out_specs=pl.BlockSpec((1, C_V, TQ_C), lambda i, h, t: (i, h, t)),
        out_shape=jax.ShapeDtypeStruct((b, C_HEADS * C_V, SEQ), BF16),
        compiler_params=_attn_params(),
        name="attn_mla",
    )(bnd, q, k, v, gate)


def _out_body(og_refs, x_ref, w_ref, lg_ref, lb_ref, x_token_major):
    og = jnp.concatenate([r[0] for r in og_refs], axis=0) if len(og_refs) > 1 else og_refs[0][0]
    y = jnp.dot(w_ref[...], og, preferred_element_type=F32)
    x = x_ref[0].T if x_token_major else x_ref[0]
    z = ALPHA * x + y
    mu = jnp.mean(z, axis=0, keepdims=True)
    d = z - mu
    var = jnp.mean(d * d, axis=0, keepdims=True)
    width = y.shape[1]
    return d * lax.rsqrt(var + LN_EPS) * _rep(lg_ref, width) + _rep(lb_ref, width)


def _out_kernel(*refs, n_og):
    x_ref, w_ref, lg_ref, lb_ref, o_ref = refs[n_og:]
    width = TM // SUB_TILES
    for t in range(SUB_TILES):
        off = t * width
        og_views = [_tokens(r, 2, off, width) for r in refs[:n_og]]
        out = _out_body(og_views, _tokens(x_ref, 2, off, width), w_ref, lg_ref, lb_ref, False)
        o_ref[0, pl.ds(off, width), :] = out.T


def _tokens(ref, axis, off, width):
    idx = [slice(None)] * len(ref.shape)
    idx[axis] = pl.ds(off, width)
    return ref.at[tuple(idx)]


def _out_in_kernel(*refs, n_og, n_in, x_token_major, in_body, tab_pos, out_axes):
    x_ref, w_ref, lg_ref, lb_ref = refs[n_og:n_og + 4]
    in_params = refs[n_og + 4:n_og + 4 + n_in]
    xo_ref = refs[n_og + 4 + n_in]
    in_outs = refs[n_og + 5 + n_in:]
    width = TM // SUB_TILES
    outs = []
    for t in range(SUB_TILES):
        off = t * width
        og_views = [_tokens(r, 2, off, width) for r in refs[:n_og]]
        x_view = _tokens(x_ref, 1 if x_token_major else 2, off, width)
        out = _out_body(og_views, x_view, w_ref, lg_ref, lb_ref, x_token_major)
        xo_ref[0, :, pl.ds(off, width)] = out
        outs.append(out.astype(BF16))
    for t in range(SUB_TILES):
        off = t * width
        params = [_tokens(r, 1, off, width) if p == tab_pos else r
                  for p, r in enumerate(in_params)]
        views = [_tokens(r, ax, off, width) for r, ax in zip(in_outs, out_axes)]
        in_body(outs[t], *params, *views)


def _out_specs(ogs, x_token_major):
    d = D_MODEL
    x_spec = (pl.BlockSpec((1, TM, d), lambda i, j: (i, j, 0)) if x_token_major
              else pl.BlockSpec((1, d, TM), _tok))
    return [pl.BlockSpec((1, og.shape[1], TM), _tok) for og in ogs] + [
        x_spec, _resident((d, d)), _resident((d, LANES)), _resident((d, LANES))]


def _out_proj_last(ogs, xt, w_t, lg, lb):
    b = xt.shape[0]
    return pl.pallas_call(
        functools.partial(_out_kernel, n_og=len(ogs)),
        grid=(b, SEQ // TM),
        in_specs=_out_specs(ogs, False),
        out_specs=pl.BlockSpec((1, TM, D_MODEL), lambda i, j: (i, j, 0)),
        out_shape=jax.ShapeDtypeStruct((b, SEQ, D_MODEL), F32),
        compiler_params=_proj_params(),
        name="out_proj_ln",
    )(*ogs, xt, w_t, lg, lb)


def _out_in(ogs, xt, w_t, lg, lb, next_even, in_args, x_token_major):
    b = xt.shape[0]
    if next_even:
        in_body, (in_specs, out_specs, out_shape) = _even_in_body, _even_in_specs(b)
        tab_pos, out_axes, name = EVEN_TAB_POS, EVEN_OUT_TOKEN_AXES, "out_even_in_proj"
    else:
        in_body, (in_specs, out_specs, out_shape) = _odd_in_body, _odd_in_specs(b)
        tab_pos, out_axes, name = ODD_TAB_POS, ODD_OUT_TOKEN_AXES, "out_odd_in_proj"
    return pl.pallas_call(
        functools.partial(_out_in_kernel, n_og=len(ogs), n_in=len(in_args),
                          x_token_major=x_token_major, in_body=in_body, tab_pos=tab_pos,
                          out_axes=out_axes),
        grid=(b, SEQ // TM),
        in_specs=_out_specs(ogs, x_token_major) + in_specs,
        out_specs=[pl.BlockSpec((1, D_MODEL, TM), _tok)] + out_specs,
        out_shape=[jax.ShapeDtypeStruct((b, D_MODEL, SEQ), F32)] + out_shape,
        compiler_params=_proj_params(),
        name=name,
    )(*ogs, xt, w_t, lg, lb, *in_args)


def _col(v):
    v = v.astype(F32)
    return jnp.broadcast_to(v[:, None], (v.shape[0], LANES))


def _score_bounds(nq, kmax2):
    return jnp.sqrt(jnp.max(nq, axis=-1) * kmax2) * BOUND_SLACK


def _angles_t(pos, dims, theta):
    inv = theta ** (-jnp.arange(0, dims, 2, dtype=F32) / dims)
    ang = pos.astype(F32)[:, None] * inv[None, :]
    return jnp.cos(ang).T, jnp.sin(ang).T


@jax.jit
def _forward(x, ev_w_in, ev_w_out, ev_lam, ev_subln, ev_qnorm, ev_knorm, ev_ln_g, ev_ln_b,
             od_w_in, od_qnorm, od_kvnorm, od_w_qb, od_w_kvb, od_w_out, od_ln_g, od_ln_b):
    s = x.shape[1]
    pos = jnp.arange(s, dtype=jnp.int32)
    row = jnp.repeat(jnp.arange(s // GRID_W, dtype=jnp.int32), GRID_W)
    col = jnp.tile(jnp.arange(GRID_W, dtype=jnp.int32), s // GRID_W)
    tab_ev = jnp.concatenate(
        _angles_t(pos, A_ROT, ROPE_THETA) + _angles_t(row, B_DIM // 2, AXIAL_THETA)
        + _angles_t(col, B_DIM // 2, AXIAL_THETA), axis=0)
    tab_od = jnp.concatenate(_angles_t(pos, C_ROPE, ROPE_THETA), axis=0)

    def even_args(i):
        return (ev_w_in[i].T.astype(BF16), tab_ev, _col(ev_qnorm[i]), _col(ev_knorm[i]))

    def odd_args(i):
        return (od_w_in[i].T.astype(BF16), od_w_qb[i].T.astype(BF16),
                od_w_kvb[i].T.astype(BF16), tab_od, _col(od_qnorm[i]), _col(od_kvnorm[i]))

    xt = x
    proj = _even_in(x, *even_args(0))
    for layer in range(DEPTH):
        i = layer // 2
        if layer % 2 == 0:
            qa, ka, va, qb, kb, vb, gate, nq, nk = proj
            kmax = jnp.max(nk, axis=-1)
            kmax = jnp.concatenate(
                [kmax[:, :2 * A_HEADS],
                 jnp.repeat(kmax[:, 2 * A_HEADS:2 * A_HEADS + B_KV_HEADS], B_GROUP, axis=1)],
                axis=1)
            bnd = _score_bounds(nq, kmax)
            lam_init = 0.8 - 0.6 * math.exp(-0.3 * layer)
            ogs = [_attn_a(bnd, ev_lam[i].astype(F32), qa, ka, va, gate, _col(ev_subln[i]),
                           lam_init),
                   _attn_b(bnd, qb, kb, vb, gate)]
            out_args = (ev_w_out[i].T.astype(BF16), _col(ev_ln_g[i]), _col(ev_ln_b[i]))
        else:
            q, k, v, gate, nq, nk = proj
            ogs = [_attn_c(_score_bounds(nq, jnp.max(nk, axis=-1)), q, k, v, gate)]
            out_args = (od_w_out[i].T.astype(BF16), _col(od_ln_g[i]), _col(od_ln_b[i]))
        if layer == DEPTH - 1:
            return _out_proj_last(ogs, xt, *out_args)
        nxt = (layer + 1) // 2
        if layer % 2 == 0:
            xt, *proj = _out_in(ogs, xt, *out_args, False, odd_args(nxt), layer == 0)
        else:
            xt, *proj = _out_in(ogs, xt, *out_args, True, even_args(nxt), False)


def kernel(x, ev_w_in, ev_w_out, ev_lam, ev_subln, ev_qnorm, ev_knorm, ev_ln_g, ev_ln_b,
           od_w_in, od_qnorm, od_kvnorm, od_w_qb, od_w_kvb, od_w_out, od_ln_g, od_ln_b):
    return _forward(x, ev_w_in, ev_w_out, ev_lam, ev_subln, ev_qnorm, ev_knorm, ev_ln_g,
                    ev_ln_b, od_w_in, od_qnorm, od_kvnorm, od_w_qb, od_w_kvb, od_w_out,
                    od_ln_g, od_ln_b)
```

```python
import functools
import math

import jax
import jax.numpy as jnp
from jax import lax
from jax.experimental import pallas as pl
from jax.experimental.pallas import tpu as pltpu

F32 = jnp.float32
BF16 = jnp.bfloat16

D_MODEL = 1024
SEQ = 4096
DEPTH = 4
GRID_W = 64
ROPE_THETA = 500000.0
AXIAL_THETA = 10000.0
LN_EPS = 1e-5
RMS_EPS = 1e-6

A_HEADS = 4
A_QK_DIM = 64
A_V_DIM = 128
A_WIDTH = 512
A_ROT = 16
B_Q_HEADS = 8
B_KV_HEADS = 2
B_GROUP = 4
B_DIM = 64
B_WIDTH = 512
EV_IN = 3328

C_HEADS = 16
C_NOPE = 64
C_ROPE = 32
C_V = 64
C_Q_LORA = 256
C_KV_LORA = 128
C_QK = C_NOPE + C_ROPE
OD_IN = 1440

ALPHA = (2 * DEPTH) ** 0.25
LOG2E = 1.4426950408889634

QSCALE_AB = A_QK_DIM ** -0.5 * LOG2E
QSCALE_C = C_QK ** -0.5 * LOG2E

LANES = 128
MXU_TILE = 256
LOOKAHEAD_ITEMS = {1: 2, 2: 3, 4: 8}
MAX_LOCKSTEP = 4
ONES_ROWS = 16
K_PAD = 128
NORM_ROWS = 16
SAFE_BOUND = 60.0
BOUND_SLACK = 1.02
TM = 512
SUB_TILES = 2
EVEN_TAB_POS, EVEN_OUT_TOKEN_AXES = 1, (2, 2, 3, 2, 1, 3, 2, 2, 2)
ODD_TAB_POS, ODD_OUT_TOKEN_AXES = 3, (2, 2, 3, 2, 2, 2)
KV_CHUNK_A = 512
KV_CHUNK_BC = 256
TQ_A = 1024
TQ_B = 4096
TQ_C = 4096
VMEM_LIMIT = 56 * 1024 * 1024


def _rep(ref, n):
    a = ref[...]
    return jnp.concatenate([a] * (n // LANES), axis=1)


def _rot(x1, x2, cos, sin):
    return x1 * cos - x2 * sin, x2 * cos + x1 * sin


def _silu(x):
    return x * jax.nn.sigmoid(x)


def _sumsq(x):
    return jnp.sum(x * x, axis=0, keepdims=True)


def _tok(i, j):
    return (i, 0, j)


def _const2(i, j):
    return (0, 0)


def _resident(shape):
    return pl.BlockSpec(shape, _const2, pipeline_mode=pl.Buffered(1))


def _proj_params():
    return pltpu.CompilerParams(
        dimension_semantics=("parallel", "parallel"), vmem_limit_bytes=VMEM_LIMIT)


def _even_in_body(xb, w_ref, tab_ref, qn_ref, kn_ref,
                  qa_ref, ka_ref, va_ref, qb_ref, kb_ref, vb_ref, g_ref, nq_ref, nk_ref,
                  *, x_token_major=False):
    contract = (((1,), (1 if x_token_major else 0,)), ((), ()))

    def proj(r0, r1):
        return lax.dot_general(w_ref[r0:r1, :], xb, contract, preferred_element_type=F32)

    cos_a, sin_a = tab_ref[0:8, :], tab_ref[8:16, :]
    cos_r, sin_r = tab_ref[16:32, :], tab_ref[32:48, :]
    cos_c, sin_c = tab_ref[48:64, :], tab_ref[64:80, :]

    def rope_a(h):
        outs = []
        for hc in range(2 * A_HEADS):
            b = hc * A_QK_DIM
            r1, r2 = _rot(h[b:b + 8], h[b + 8:b + 16], cos_a, sin_a)
            outs.append(jnp.concatenate([r1, r2, h[b + 16:b + 64]], axis=0))
        return outs

    g_ref[0] = _silu(proj(2304, 3328)).astype(BF16)
    qs = [x * QSCALE_AB for x in rope_a(proj(0, 512))]
    qa_ref[0] = jnp.concatenate(qs, axis=0).astype(BF16)
    ks = rope_a(proj(512, 1024))
    nq = [_sumsq(x) for x in qs]
    nk = [_sumsq(x) for x in ks]
    for h in range(A_HEADS):
        kt = jnp.concatenate([ks[2 * h], ks[2 * h + 1]], axis=0)
        ka_ref[0, h] = kt.T.astype(BF16)

    def norm_axial(h, g):
        ms = jnp.mean(h * h, axis=0, keepdims=True)
        y = h * lax.rsqrt(ms + RMS_EPS) * g
        a1, a2 = _rot(y[0:16], y[16:32], cos_r, sin_r)
        b1, b2 = _rot(y[32:48], y[48:64], cos_c, sin_c)
        return jnp.concatenate([a1, a2, b1, b2], axis=0)

    width = xb.shape[0 if x_token_major else 1]
    qn = _rep(qn_ref, width)
    kn = _rep(kn_ref, width)
    hq = proj(1536, 2048)
    qs = [norm_axial(hq[h * 64:(h + 1) * 64], qn) * QSCALE_AB for h in range(B_Q_HEADS)]
    qb_ref[0] = jnp.concatenate(qs, axis=0).astype(BF16)
    hkv = proj(2048, 2304)
    ks = [norm_axial(hkv[h * 64:(h + 1) * 64], kn) for h in range(B_KV_HEADS)]
    kb_ref[0] = jnp.concatenate(ks, axis=0).T.astype(BF16)
    nq_ref[0] = jnp.concatenate(nq + [_sumsq(x) for x in qs], axis=0)
    nk = nk + [_sumsq(x) for x in ks]
    nk_ref[0] = jnp.concatenate(nk + [jnp.zeros_like(nk[0])] * (NORM_ROWS - len(nk)), axis=0)
    vb_ref[0, 0] = hkv[128:256].astype(BF16)
    va_ref[0, 0] = proj(1024, 1536).astype(BF16)


def _even_in_kernel(x_ref, *refs):
    n_in = len(refs) - len(EVEN_OUT_TOKEN_AXES)
    width = TM // SUB_TILES
    for t in range(SUB_TILES):
        off = t * width
        params = [_tokens(r, 1, off, width) if p == EVEN_TAB_POS else r
                  for p, r in enumerate(refs[:n_in])]
        views = [_tokens(r, ax, off, width) for r, ax in zip(refs[n_in:], EVEN_OUT_TOKEN_AXES)]
        _even_in_body(x_ref[0, pl.ds(off, width), :].astype(BF16), *params, *views,
                      x_token_major=True)


def _even_in_specs(b):
    d, s = D_MODEL, SEQ
    nt = s // TM
    tok = _tok
    chunk = lambda i, j: (i, j, 0, 0)
    return (
        [
            _resident((EV_IN, d)),
            pl.BlockSpec((80, TM), lambda i, j: (0, j)),
            _resident((B_DIM, LANES)),
            _resident((B_DIM, LANES)),
        ],
        [
            pl.BlockSpec((1, 512, TM), tok),
            pl.BlockSpec((1, A_HEADS, TM, K_PAD), lambda i, j: (i, 0, j, 0)),
            pl.BlockSpec((1, 1, 512, TM), chunk),
            pl.BlockSpec((1, 512, TM), tok),
            pl.BlockSpec((1, TM, K_PAD), lambda i, j: (i, j, 0)),
            pl.BlockSpec((1, 1, 128, TM), chunk),
            pl.BlockSpec((1, 1024, TM), tok),
            pl.BlockSpec((1, NORM_ROWS, TM), tok),
            pl.BlockSpec((1, NORM_ROWS, TM), tok),
        ],
        [
            jax.ShapeDtypeStruct((b, 512, s), BF16),
            jax.ShapeDtypeStruct((b, A_HEADS, s, K_PAD), BF16),
            jax.ShapeDtypeStruct((b, nt, 512, TM), BF16),
            jax.ShapeDtypeStruct((b, 512, s), BF16),
            jax.ShapeDtypeStruct((b, s, K_PAD), BF16),
            jax.ShapeDtypeStruct((b, nt, 128, TM), BF16),
            jax.ShapeDtypeStruct((b, 1024, s), BF16),
            jax.ShapeDtypeStruct((b, NORM_ROWS, s), F32),
            jax.ShapeDtypeStruct((b, NORM_ROWS, s), F32),
        ],
    )


def _even_in(x, w_t, tab, qn, kn):
    b = x.shape[0]
    in_specs, out_specs, out_shape = _even_in_specs(b)
    return pl.pallas_call(
        _even_in_kernel,
        grid=(b, SEQ // TM),
        in_specs=[pl.BlockSpec((1, TM, D_MODEL), lambda i, j: (i, j, 0))] + in_specs,
        out_specs=out_specs,
        out_shape=out_shape,
        compiler_params=_proj_params(),
        name="even_in_proj",
    )(x, w_t, tab, qn, kn)


def _odd_in_body(xb, w_ref, wq_ref, wkv_ref, tab_ref, qn_ref, kvn_ref,
                 q_ref, k_ref, v_ref, g_ref, nq_ref, nk_ref):
    cos, sin = tab_ref[0:16, :], tab_ref[16:32, :]

    def rms(h, g):
        ms = jnp.mean(h * h, axis=0, keepdims=True)
        return h * lax.rsqrt(ms + RMS_EPS) * g

    g_ref[0] = _silu(jnp.dot(w_ref[416:1440, :], xb, preferred_element_type=F32)).astype(BF16)
    lat = jnp.dot(w_ref[0:416, :], xb, preferred_element_type=F32)
    width = xb.shape[1]
    cqn = rms(lat[0:256], _rep(qn_ref, width)).astype(BF16)
    q = jnp.dot(wq_ref[...], cqn, preferred_element_type=F32)
    qs = []
    for h in range(C_HEADS):
        b = h * C_QK
        r1, r2 = _rot(q[b + 64:b + 80], q[b + 80:b + 96], cos, sin)
        qs.append(jnp.concatenate([q[b:b + 64], r1, r2], axis=0) * QSCALE_C)
    q_ref[0] = jnp.concatenate(qs, axis=0).astype(BF16)
    nq_ref[0] = jnp.concatenate([_sumsq(x) for x in qs], axis=0)

    ckvn = rms(lat[256:384], _rep(kvn_ref, width)).astype(BF16)
    kv = jnp.dot(wkv_ref[...], ckvn, preferred_element_type=F32)
    r1, r2 = _rot(lat[384:400], lat[400:416], cos, sin)
    zpad = jnp.zeros((K_PAD - C_QK, width), F32)
    nkr = _sumsq(r1) + _sumsq(r2)
    nk = []
    for h in range(C_HEADS):
        kn = kv[h * 128:h * 128 + 64]
        nk.append(_sumsq(kn) + nkr)
        kt = jnp.concatenate([kn, r1, r2, zpad], axis=0)
        k_ref[0, h] = kt.T.astype(BF16)
    nk_ref[0] = jnp.concatenate(nk, axis=0)
    v_ref[0, 0] = jnp.concatenate(
        [kv[h * 128 + 64:h * 128 + 128] for h in range(C_HEADS)], axis=0).astype(BF16)


def _odd_in_specs(b):
    d, s = D_MODEL, SEQ
    nt = s // TM
    tok = _tok
    return (
        [
            _resident((OD_IN, d)),
            _resident((C_HEADS * C_QK, C_Q_LORA)),
            _resident((C_HEADS * 128, C_KV_LORA)),
            pl.BlockSpec((32, TM), lambda i, j: (0, j)),
            _resident((C_Q_LORA, LANES)),
            _resident((C_KV_LORA, LANES)),
        ],
        [
            pl.BlockSpec((1, C_HEADS * C_QK, TM), tok),
            pl.BlockSpec((1, C_HEADS, TM, K_PAD), lambda i, j: (i, 0, j, 0)),
            pl.BlockSpec((1, 1, 1024, TM), lambda i, j: (i, j, 0, 0)),
            pl.BlockSpec((1, 1024, TM), tok),
            pl.BlockSpec((1, NORM_ROWS, TM), tok),
            pl.BlockSpec((1, NORM_ROWS, TM), tok),
        ],
        [
            jax.ShapeDtypeStruct((b, C_HEADS * C_QK, s), BF16),
            jax.ShapeDtypeStruct((b, C_HEADS, s, K_PAD), BF16),
            jax.ShapeDtypeStruct((b, nt, 1024, TM), BF16),
            jax.ShapeDtypeStruct((b, 1024, s), BF16),
            jax.ShapeDtypeStruct((b, NORM_ROWS, s), F32),
            jax.ShapeDtypeStruct((b, NORM_ROWS, s), F32),
        ],
    )


def _softmax_pv(qps, k_at, v_ref, dv, kv_chunk, bounds=None):
    nsets = len(qps)
    ntiles = qps[0].shape[1] // MXU_TILE
    qps = [qp[:, t * MXU_TILE:(t + 1) * MXU_TILE] for qp in qps for t in range(ntiles)]
    n = len(qps)
    nchunks = SEQ // kv_chunk
    halves = kv_chunk // MXU_TILE
    ones = jnp.ones((ONES_ROWS, kv_chunk), BF16)

    def scores(ci, j):
        return [jnp.dot(k_at(ci * kv_chunk + h * MXU_TILE), qps[j], preferred_element_type=F32)
                for h in range(halves)]

    m = [None] * n
    acc = [None] * n
    lsum = [None] * n
    items = [(ci, j) for g in range(0, n, MAX_LOCKSTEP) for ci in range(nchunks)
             for j in range(g, min(g + MAX_LOCKSTEP, n))]
    lookahead = LOOKAHEAD_ITEMS[min(n, MAX_LOCKSTEP)]
    pending = [scores(*it) for it in items[:lookahead]]
    for idx, (ci, j) in enumerate(items):
        if idx + lookahead < len(items):
            pending.append(scores(*items[idx + lookahead]))
        sc = pending.pop(0)
        r0 = ci * kv_chunk
        v = v_ref[0, r0 // TM, :, pl.ds(r0 % TM, kv_chunk)]
        if bounds is None:
            v = jnp.concatenate([v, ones], axis=0)
            cm = functools.reduce(jnp.maximum, [jnp.max(s, axis=0, keepdims=True) for s in sc])
            ref = cm if ci == 0 else jnp.maximum(m[j], cm)
        else:
            ref = bounds[j // ntiles]
        ps = [jnp.exp2(s - ref) for s in sc]
        pv = functools.reduce(jnp.add, [
            jnp.dot(v[:, h * MXU_TILE:(h + 1) * MXU_TILE], ps[h].astype(BF16),
                    preferred_element_type=F32) for h in range(halves)])
        if bounds is not None:
            part = functools.reduce(jnp.add, [
                jnp.sum(p.reshape(MXU_TILE // 8, 8, MXU_TILE), axis=0) for p in ps])
            lsum[j] = part if ci == 0 else lsum[j] + part
        if ci == 0:
            acc[j] = pv
        elif bounds is None:
            acc[j] = acc[j] * jnp.exp2(m[j] - ref) + pv
        else:
            acc[j] = acc[j] + pv
        m[j] = ref
    out = []
    for j in range(nsets):
        a = jnp.concatenate(acc[j * ntiles:(j + 1) * ntiles], axis=1)
        if bounds is None:
            out.append((a[0:dv], a[dv:dv + 1]))
        else:
            l8 = jnp.concatenate(lsum[j * ntiles:(j + 1) * ntiles], axis=1)
            out.append((a, jnp.sum(l8, axis=0, keepdims=True)))
    return out


def _attend(qps, bounds, k_at, v_ref, dv, kv_chunk, finish):
    safe = functools.reduce(jnp.logical_and, [b <= SAFE_BOUND for b in bounds])

    @pl.when(safe)
    def _():
        finish(_softmax_pv(qps, k_at, v_ref, dv, kv_chunk, bounds))

    @pl.when(jnp.logical_not(safe))
    def _():
        finish(_softmax_pv(qps, k_at, v_ref, dv, kv_chunk))


def _attn_params():
    return pltpu.CompilerParams(
        dimension_semantics=("parallel", "parallel", "parallel"), vmem_limit_bytes=VMEM_LIMIT)


def _attn_a_kernel(bnd_ref, lam_ref, q_ref, k_ref, v_ref, gate_ref, sg_ref, o_ref, *, lam_init):
    i, h = pl.program_id(0), pl.program_id(1)
    q = q_ref[0]
    tq = q.shape[1]
    z = jnp.zeros((A_QK_DIM, tq), BF16)
    qps = [jnp.concatenate([q[0:64], z], axis=0), jnp.concatenate([z, q[64:128]], axis=0)]

    def finish(res):
        (a0, l0), (a1, l1) = res
        lp = lam_ref[...]
        lam = (jnp.exp(jnp.sum(lp[0:1] * lp[1:2], axis=1, keepdims=True))
               - jnp.exp(jnp.sum(lp[2:3] * lp[3:4], axis=1, keepdims=True)) + lam_init)
        o = a0 / l0 - lam * (a1 / l1)
        ms = jnp.mean(o * o, axis=0, keepdims=True)
        o = o * lax.rsqrt(ms + RMS_EPS) * _rep(sg_ref, tq) * (1.0 - lam_init)
        o_ref[0] = (o * gate_ref[0]).astype(BF16)

    _attend(qps, [bnd_ref[i, 2 * h], bnd_ref[i, 2 * h + 1]],
            lambda r0: k_ref[0, 0, pl.ds(r0, MXU_TILE), :], v_ref, A_V_DIM, KV_CHUNK_A, finish)


def _attn_a(bnd, lam_p, qa, ka, va, gate, sg, lam_init):
    b = qa.shape[0]
    nt = SEQ // TM
    return pl.pallas_call(
        functools.partial(_attn_a_kernel, lam_init=lam_init),
        grid=(b, A_HEADS, SEQ // TQ_A),
        in_specs=[
            pl.BlockSpec(memory_space=pltpu.SMEM),
            pl.BlockSpec((4, A_QK_DIM), lambda i, h, t: (0, 0)),
            pl.BlockSpec((1, 128, TQ_A), lambda i, h, t: (i, h, t)),
            pl.BlockSpec((1, 1, SEQ, K_PAD), lambda i, h, t: (i, h, 0, 0)),
            pl.BlockSpec((1, nt, A_V_DIM, TM), lambda i, h, t: (i, 0, h, 0)),
            pl.BlockSpec((1, A_V_DIM, TQ_A), lambda i, h, t: (i, h, t)),
            pl.BlockSpec((A_V_DIM, LANES), lambda i, h, t: (0, 0)),
        ],
        out_specs=pl.BlockSpec((1, A_V_DIM, TQ_A), lambda i, h, t: (i, h, t)),
        out_shape=jax.ShapeDtypeStruct((b, A_WIDTH, SEQ), BF16),
        compiler_params=_attn_params(),
        name="attn_diff",
    )(bnd, lam_p, qa, ka, va, gate, sg)


def _attn_b_kernel(bnd_ref, q_ref, k_ref, v_ref, gate_ref, o_ref):
    i, h = pl.program_id(0), pl.program_id(1)
    first = h < B_GROUP
    q = q_ref[0]
    z = jnp.zeros_like(q)
    qp = jnp.concatenate([jnp.where(first, q, z), jnp.where(first, z, q)], axis=0)

    def finish(res):
        ((a, l),) = res
        o_ref[0] = (a / l * gate_ref[0]).astype(BF16)

    _attend([qp], [bnd_ref[i, 2 * A_HEADS + h]],
            lambda r0: k_ref[0, pl.ds(r0, MXU_TILE), :], v_ref, B_DIM, KV_CHUNK_BC, finish)


def _attn_b(bnd, qb, kb, vb, gate):
    b = qb.shape[0]
    nt = SEQ // TM
    return pl.pallas_call(
        _attn_b_kernel,
        grid=(b, B_Q_HEADS, SEQ // TQ_B),
        in_specs=[
            pl.BlockSpec(memory_space=pltpu.SMEM),
            pl.BlockSpec((1, B_DIM, TQ_B), lambda i, h, t: (i, h, t)),
            pl.BlockSpec((1, SEQ, K_PAD), lambda i, h, t: (i, 0, 0)),
            pl.BlockSpec((1, nt, B_DIM, TM), lambda i, h, t: (i, 0, h // B_GROUP, 0)),
            pl.BlockSpec((1, B_DIM, TQ_B), lambda i, h, t: (i, A_WIDTH // B_DIM + h, t)),
        ],
        out_specs=pl.BlockSpec((1, B_DIM, TQ_B), lambda i, h, t: (i, h, t)),
        out_shape=jax.ShapeDtypeStruct((b, B_WIDTH, SEQ), BF16),
        compiler_params=_attn_params(),
        name="attn_gqa",
    )(bnd, qb, kb, vb, gate)


def _attn_c_kernel(bnd_ref, q_ref, k_ref, v_ref, gate_ref, o_ref):
    i, h = pl.program_id(0), pl.program_id(1)
    q = q_ref[0]
    qp = jnp.concatenate([q, jnp.zeros((K_PAD - C_QK, q.shape[1]), BF16)], axis=0)

    def finish(res):
        ((a, l),) = res
        o_ref[0] = (a / l * gate_ref[0]).astype(BF16)

    _attend([qp], [bnd_ref[i, h]],
            lambda r0: k_ref[0, 0, pl.ds(r0, MXU_TILE), :], v_ref, C_V, KV_CHUNK_BC, finish)


def _attn_c(bnd, q, k, v, gate):
    b = q.shape[0]
    nt = SEQ // TM
    return pl.pallas_call(
        _attn_c_kernel,
        grid=(b, C_HEADS, SEQ // TQ_C),
        in_specs=[
            pl.BlockSpec(memory_space=pltpu.SMEM),
            pl.BlockSpec((1, C_QK, TQ_C), lambda i, h, t: (i, h, t)),
            pl.BlockSpec((1, 1, SEQ, K_PAD), lambda i, h, t: (i, h, 0, 0)),
            pl.BlockSpec((1, nt, C_V, TM), lambda i, h, t: (i, 0, h, 0)),
            pl.BlockSpec((1, C_V, TQ_C), lambda i, h, t: (i, h, t)),
        ],
        out_specs=pl.BlockSpec((1, C_V, TQ_C), lambda i, h, t: (i, h, t)),
        out_shape=jax.ShapeDtypeStruct((b, C_HEADS * C_V, SEQ), BF16),
        compiler_params=_attn_params(),
        name="attn_mla",
    )(bnd, q, k, v, gate)


def _out_body(og_refs, x_ref, w_ref, lg_ref, lb_ref, x_token_major):
    og = jnp.concatenate([r[0] for r in og_refs], axis=0) if len(og_refs) > 1 else og_refs[0][0]
    y = jnp.dot(w_ref[...], og, preferred_element_type=F32)
    x = x_ref[0].T if x_token_major else x_ref[0]
    z = ALPHA * x + y
    mu = jnp.mean(z, axis=0, keepdims=True)
    d = z - mu
    var = jnp.mean(d * d, axis=0, keepdims=True)
    width = y.shape[1]
    return d * lax.rsqrt(var + LN_EPS) * _rep(lg_ref, width) + _rep(lb_ref, width)


def _out_kernel(*refs, n_og):
    x_ref, w_ref, lg_ref, lb_ref, o_ref = refs[n_og:]
    width = TM // SUB_TILES
    for t in range(SUB_TILES):
        off = t * width
        og_views = [_tokens(r, 2, off, width) for r in refs[:n_og]]
        out = _out_body(og_views, _tokens(x_ref, 2, off, width), w_ref, lg_ref, lb_ref, False)
        o_ref[0, pl.ds(off, width), :] = out.T


def _tokens(ref, axis, off, width):
    idx = [slice(None)] * len(ref.shape)
    idx[axis] = pl.ds(off, width)
    return ref.at[tuple(idx)]


def _out_in_kernel(*refs, n_og, n_in, x_token_major, in_body, tab_pos, out_axes):
    x_ref, w_ref, lg_ref, lb_ref = refs[n_og:n_og + 4]
    in_params = refs[n_og + 4:n_og + 4 + n_in]
    xo_ref = refs[n_og + 4 + n_in]
    in_outs = refs[n_og + 5 + n_in:]
    width = TM // SUB_TILES
    outs = []
    for t in range(SUB_TILES):
        off = t * width
        og_views = [_tokens(r, 2, off, width) for r in refs[:n_og]]
        x_view = _tokens(x_ref, 1 if x_token_major else 2, off, width)
        out = _out_body(og_views, x_view, w_ref, lg_ref, lb_ref, x_token_major)
        xo_ref[0, :, pl.ds(off, width)] = out
        outs.append(out.astype(BF16))
    for t in range(SUB_TILES):
        off = t * width
        params = [_tokens(r, 1, off, width) if p == tab_pos else r
                  for p, r in enumerate(in_params)]
        views = [_tokens(r, ax, off, width) for r, ax in zip(in_outs, out_axes)]
        in_body(outs[t], *params, *views)


def _out_specs(ogs, x_token_major):
    d = D_MODEL
    x_spec = (pl.BlockSpec((1, TM, d), lambda i, j: (i, j, 0)) if x_token_major
              else pl.BlockSpec((1, d, TM), _tok))
    return [pl.BlockSpec((1, og.shape[1], TM), _tok) for og in ogs] + [
        x_spec, _resident((d, d)), _resident((d, LANES)), _resident((d, LANES))]


def _out_proj_last(ogs, xt, w_t, lg, lb):
    b = xt.shape[0]
    return pl.pallas_call(
        functools.partial(_out_kernel, n_og=len(ogs)),
        grid=(b, SEQ // TM),
        in_specs=_out_specs(ogs, False),
        out_specs=pl.BlockSpec((1, TM, D_MODEL), lambda i, j: (i, j, 0)),
        out_shape=jax.ShapeDtypeStruct((b, SEQ, D_MODEL), F32),
        compiler_params=_proj_params(),
        name="out_proj_ln",
    )(*ogs, xt, w_t, lg, lb)


def _out_in(ogs, xt, w_t, lg, lb, next_even, in_args, x_token_major):
    b = xt.shape[0]
    if next_even:
        in_body, (in_specs, out_specs, out_shape) = _even_in_body, _even_in_specs(b)
        tab_pos, out_axes, name = EVEN_TAB_POS, EVEN_OUT_TOKEN_AXES, "out_even_in_proj"
    else:
        in_body, (in_specs, out_specs, out_shape) = _odd_in_body, _odd_in_specs(b)
        tab_pos, out_axes, name = ODD_TAB_POS, ODD_OUT_TOKEN_AXES, "out_odd_in_proj"
    return pl.pallas_call(
        functools.partial(_out_in_kernel, n_og=len(ogs), n_in=len(in_args),
                          x_token_major=x_token_major, in_body=in_body, tab_pos=tab_pos,
                          out_axes=out_axes),
        grid=(b, SEQ // TM),
        in_specs=_out_specs(ogs, x_token_major) + in_specs,
        out_specs=[pl.BlockSpec((1, D_MODEL, TM), _tok)] + out_specs,
        out_shape=[jax.ShapeDtypeStruct((b, D_MODEL, SEQ), F32)] + out_shape,
        compiler_params=_proj_params(),
        name=name,
    )(*ogs, xt, w_t, lg, lb, *in_args)


def _col(v):
    v = v.astype(F32)
    return jnp.broadcast_to(v[:, None], (v.shape[0], LANES))


def _score_bounds(nq, kmax2):
    return jnp.sqrt(jnp.max(nq, axis=-1) * kmax2) * BOUND_SLACK


def _angles_t(pos, dims, theta):
    inv = theta ** (-jnp.arange(0, dims, 2, dtype=F32) / dims)
    ang = pos.astype(F32)[:, None] * inv[None, :]
    return jnp.cos(ang).T, jnp.sin(ang).T


@jax.jit
def _forward(x, ev_w_in, ev_w_out, ev_lam, ev_subln, ev_qnorm, ev_knorm, ev_ln_g, ev_ln_b,
             od_w_in, od_qnorm, od_kvnorm, od_w_qb, od_w_kvb, od_w_out, od_ln_g, od_ln_b):
    s = x.shape[1]
    pos = jnp.arange(s, dtype=jnp.int32)
    row = jnp.repeat(jnp.arange(s // GRID_W, dtype=jnp.int32), GRID_W)
    col = jnp.tile(jnp.arange(GRID_W, dtype=jnp.int32), s // GRID_W)
    tab_ev = jnp.concatenate(
        _angles_t(pos, A_ROT, ROPE_THETA) + _angles_t(row, B_DIM // 2, AXIAL_THETA)
        + _angles_t(col, B_DIM // 2, AXIAL_THETA), axis=0)
    tab_od = jnp.concatenate(_angles_t(pos, C_ROPE, ROPE_THETA), axis=0)

    def even_args(i):
        return (ev_w_in[i].T.astype(BF16), tab_ev, _col(ev_qnorm[i]), _col(ev_knorm[i]))

    def odd_args(i):
        return (od_w_in[i].T.astype(BF16), od_w_qb[i].T.astype(BF16),
                od_w_kvb[i].T.astype(BF16), tab_od, _col(od_qnorm[i]), _col(od_kvnorm[i]))

    xt = x
    proj = _even_in(x, *even_args(0))
    for layer in range(DEPTH):
        i = layer // 2
        if layer % 2 == 0:
            qa, ka, va, qb, kb, vb, gate, nq, nk = proj
            kmax = jnp.max(nk, axis=-1)
            kmax = jnp.concatenate(
                [kmax[:, :2 * A_HEADS],
                 jnp.repeat(kmax[:, 2 * A_HEADS:2 * A_HEADS + B_KV_HEADS], B_GROUP, axis=1)],
                axis=1)
            bnd = _score_bounds(nq, kmax)
            lam_init = 0.8 - 0.6 * math.exp(-0.3 * layer)
            ogs = [_attn_a(bnd, ev_lam[i].astype(F32), qa, ka, va, gate, _col(ev_subln[i]),
                           lam_init),
                   _attn_b(bnd, qb, kb, vb, gate)]
            out_args = (ev_w_out[i].T.astype(BF16), _col(ev_ln_g[i]), _col(ev_ln_b[i]))
        else:
            q, k, v, gate, nq, nk = proj
            ogs = [_attn_c(_score_bounds(nq, jnp.max(nk, axis=-1)), q, k, v, gate)]
            out_args = (od_w_out[i].T.astype(BF16), _col(od_ln_g[i]), _col(od_ln_b[i]))
        if layer == DEPTH - 1:
            return _out_proj_last(ogs, xt, *out_args)
        nxt = (layer + 1) // 2
        if layer % 2 == 0:
            xt, *proj = _out_in(ogs, xt, *out_args, False, odd_args(nxt), layer == 0)
        else:
            xt, *proj = _out_in(ogs, xt, *out_args, True, even_args(nxt), False)


def kernel(x, ev_w_in, ev_w_out, ev_lam, ev_subln, ev_qnorm, ev_knorm, ev_ln_g, ev_ln_b,
           od_w_in, od_qnorm, od_kvnorm, od_w_qb, od_w_kvb, od_w_out, od_ln_g, od_ln_b):
    return _forward(x, ev_w_in, ev_w_out, ev_lam, ev_subln, ev_qnorm, ev_knorm, ev_ln_g,
                    ev_ln_b, od_w_in, od_qnorm, od_kvnorm, od_w_qb, od_w_kvb, od_w_out,
                    od_ln_g, od_ln_b)
```

```python
import functools
import math

import jax
import jax.numpy as jnp
from jax import lax
from jax.experimental import pallas as pl
from jax.experimental.pallas import tpu as pltpu

F32 = jnp.float32
BF16 = jnp.bfloat16

D_MODEL = 1024
SEQ = 4096
DEPTH = 4
GRID_W = 64
ROPE_THETA = 500000.0
AXIAL_THETA = 10000.0
LN_EPS = 1e-5
RMS_EPS = 1e-6

A_HEADS = 4
A_QK_DIM = 64
A_V_DIM = 128
A_WIDTH = 512
A_ROT = 16
B_Q_HEADS = 8
B_KV_HEADS = 2
B_GROUP = 4
B_DIM = 64
B_WIDTH = 512
EV_IN = 3328

C_HEADS = 16
C_NOPE = 64
C_ROPE = 32
C_V = 64
C_Q_LORA = 256
C_KV_LORA = 128
C_QK = C_NOPE + C_ROPE
OD_IN = 1440

ALPHA = (2 * DEPTH) ** 0.25
LOG2E = 1.4426950408889634

QSCALE_AB = A_QK_DIM ** -0.5 * LOG2E
QSCALE_C = C_QK ** -0.5 * LOG2E

LANES = 128
MXU_TILE = 256
LOOKAHEAD_ITEMS = {1: 2, 2: 3, 4: 8}
MAX_LOCKSTEP = 4
ONES_ROWS = 16
K_PAD = 128
NORM_ROWS = 16
SAFE_BOUND = 60.0
BOUND_SLACK = 1.02
TM = 1024
SUB_TILES = 4
EVEN_TAB_POS, EVEN_OUT_TOKEN_AXES = 1, (2, 2, 3, 2, 1, 3, 2, 2, 2)
ODD_TAB_POS, ODD_OUT_TOKEN_AXES = 3, (2, 2, 3, 2, 2, 2)
KV_CHUNK_A = 512
KV_CHUNK_BC = 256
TQ_A = 1024
TQ_B = 4096
TQ_C = 4096
VMEM_LIMIT = 56 * 1024 * 1024


def _rep(ref, n):
    a = ref[...]
    return jnp.concatenate([a] * (n // LANES), axis=1)


def _rot(x1, x2, cos, sin):
    return x1 * cos - x2 * sin, x2 * cos + x1 * sin


def _silu(x):
    return x * jax.nn.sigmoid(x)


def _sumsq(x):
    return jnp.sum(x * x, axis=0, keepdims=True)


def _tok(i, j):
    return (i, 0, j)


def _const2(i, j):
    return (0, 0)


def _resident(shape):
    return pl.BlockSpec(shape, _const2, pipeline_mode=pl.Buffered(1))


def _proj_params():
    return pltpu.CompilerParams(
        dimension_semantics=("parallel", "parallel"), vmem_limit_bytes=VMEM_LIMIT)


def _even_in_body(xb, w_ref, tab_ref, qn_ref, kn_ref,
                  qa_ref, ka_ref, va_ref, qb_ref, kb_ref, vb_ref, g_ref, nq_ref, nk_ref,
                  *, x_token_major=False):
    contract = (((1,), (1 if x_token_major else 0,)), ((), ()))

    def proj(r0, r1):
        return lax.dot_general(w_ref[r0:r1, :], xb, contract, preferred_element_type=F32)

    cos_a, sin_a = tab_ref[0:8, :], tab_ref[8:16, :]
    cos_r, sin_r = tab_ref[16:32, :], tab_ref[32:48, :]
    cos_c, sin_c = tab_ref[48:64, :], tab_ref[64:80, :]

    def rope_a(h):
        outs = []
        for hc in range(2 * A_HEADS):
            b = hc * A_QK_DIM
            r1, r2 = _rot(h[b:b + 8], h[b + 8:b + 16], cos_a, sin_a)
            outs.append(jnp.concatenate([r1, r2, h[b + 16:b + 64]], axis=0))
        return outs

    g_ref[0] = _silu(proj(2304, 3328)).astype(BF16)
    qs = [x * QSCALE_AB for x in rope_a(proj(0, 512))]
    qa_ref[0] = jnp.concatenate(qs, axis=0).astype(BF16)
    ks = rope_a(proj(512, 1024))
    nq = [_sumsq(x) for x in qs]
    nk = [_sumsq(x) for x in ks]
    for h in range(A_HEADS):
        kt = jnp.concatenate([ks[2 * h], ks[2 * h + 1]], axis=0)
        ka_ref[0, h] = kt.T.astype(BF16)

    def norm_axial(h, g):
        ms = jnp.mean(h * h, axis=0, keepdims=True)
        y = h * lax.rsqrt(ms + RMS_EPS) * g
        a1, a2 = _rot(y[0:16], y[16:32], cos_r, sin_r)
        b1, b2 = _rot(y[32:48], y[48:64], cos_c, sin_c)
        return jnp.concatenate([a1, a2, b1, b2], axis=0)

    width = xb.shape[0 if x_token_major else 1]
    qn = _rep(qn_ref, width)
    kn = _rep(kn_ref, width)
    hq = proj(1536, 2048)
    qs = [norm_axial(hq[h * 64:(h + 1) * 64], qn) * QSCALE_AB for h in range(B_Q_HEADS)]
    qb_ref[0] = jnp.concatenate(qs, axis=0).astype(BF16)
    hkv = proj(2048, 2304)
    ks = [norm_axial(hkv[h * 64:(h + 1) * 64], kn) for h in range(B_KV_HEADS)]
    kb_ref[0] = jnp.concatenate(ks, axis=0).T.astype(BF16)
    nq_ref[0] = jnp.concatenate(nq + [_sumsq(x) for x in qs], axis=0)
    nk = nk + [_sumsq(x) for x in ks]
    nk_ref[0] = jnp.concatenate(nk + [jnp.zeros_like(nk[0])] * (NORM_ROWS - len(nk)), axis=0)
    vb_ref[0, 0] = hkv[128:256].astype(BF16)
    va_ref[0, 0] = proj(1024, 1536).astype(BF16)


def _even_in_kernel(x_ref, *refs):
    n_in = len(refs) - len(EVEN_OUT_TOKEN_AXES)
    width = TM // SUB_TILES
    for t in range(SUB_TILES):
        off = t * width
        params = [_tokens(r, 1, off, width) if p == EVEN_TAB_POS else r
                  for p, r in enumerate(refs[:n_in])]
        views = [_tokens(r, ax, off, width) for r, ax in zip(refs[n_in:], EVEN_OUT_TOKEN_AXES)]
        _even_in_body(x_ref[0, pl.ds(off, width), :].astype(BF16), *params, *views,
                      x_token_major=True)


def _even_in_specs(b):
    d, s = D_MODEL, SEQ
    nt = s // TM
    tok = _tok
    chunk = lambda i, j: (i, j, 0, 0)
    return (
        [
            _resident((EV_IN, d)),
            pl.BlockSpec((80, TM), lambda i, j: (0, j)),
            _resident((B_DIM, LANES)),
            _resident((B_DIM, LANES)),
        ],
        [
            pl.BlockSpec((1, 512, TM), tok),
            pl.BlockSpec((1, A_HEADS, TM, K_PAD), lambda i, j: (i, 0, j, 0)),
            pl.BlockSpec((1, 1, 512, TM), chunk),
            pl.BlockSpec((1, 512, TM), tok),
            pl.BlockSpec((1, TM, K_PAD), lambda i, j: (i, j, 0)),
            pl.BlockSpec((1, 1, 128, TM), chunk),
            pl.BlockSpec((1, 1024, TM), tok),
            pl.BlockSpec((1, NORM_ROWS, TM), tok),
            pl.BlockSpec((1, NORM_ROWS, TM), tok),
        ],
        [
            jax.ShapeDtypeStruct((b, 512, s), BF16),
            jax.ShapeDtypeStruct((b, A_HEADS, s, K_PAD), BF16),
            jax.ShapeDtypeStruct((b, nt, 512, TM), BF16),
            jax.ShapeDtypeStruct((b, 512, s), BF16),
            jax.ShapeDtypeStruct((b, s, K_PAD), BF16),
            jax.ShapeDtypeStruct((b, nt, 128, TM), BF16),
            jax.ShapeDtypeStruct((b, 1024, s), BF16),
            jax.ShapeDtypeStruct((b, NORM_ROWS, s), F32),
            jax.ShapeDtypeStruct((b, NORM_ROWS, s), F32),
        ],
    )


def _even_in(x, w_t, tab, qn, kn):
    b = x.shape[0]
    in_specs, out_specs, out_shape = _even_in_specs(b)
    return pl.pallas_call(
        _even_in_kernel,
        grid=(b, SEQ // TM),
        in_specs=[pl.BlockSpec((1, TM, D_MODEL), lambda i, j: (i, j, 0))] + in_specs,
        out_specs=out_specs,
        out_shape=out_shape,
        compiler_params=_proj_params(),
        name="even_in_proj",
    )(x, w_t, tab, qn, kn)


def _odd_in_body(xb, w_ref, wq_ref, wkv_ref, tab_ref, qn_ref, kvn_ref,
                 q_ref, k_ref, v_ref, g_ref, nq_ref, nk_ref):
    cos, sin = tab_ref[0:16, :], tab_ref[16:32, :]

    def rms(h, g):
        ms = jnp.mean(h * h, axis=0, keepdims=True)
        return h * lax.rsqrt(ms + RMS_EPS) * g

    g_ref[0] = _silu(jnp.dot(w_ref[416:1440, :], xb, preferred_element_type=F32)).astype(BF16)
    lat = jnp.dot(w_ref[0:416, :], xb, preferred_element_type=F32)
    width = xb.shape[1]
    cqn = rms(lat[0:256], _rep(qn_ref, width)).astype(BF16)
    q = jnp.dot(wq_ref[...], cqn, preferred_element_type=F32)
    qs = []
    for h in range(C_HEADS):
        b = h * C_QK
        r1, r2 = _rot(q[b + 64:b + 80], q[b + 80:b + 96], cos, sin)
        qs.append(jnp.concatenate([q[b:b + 64], r1, r2], axis=0) * QSCALE_C)
    q_ref[0] = jnp.concatenate(qs, axis=0).astype(BF16)
    nq_ref[0] = jnp.concatenate([_sumsq(x) for x in qs], axis=0)

    ckvn = rms(lat[256:384], _rep(kvn_ref, width)).astype(BF16)
    kv = jnp.dot(wkv_ref[...], ckvn, preferred_element_type=F32)
    r1, r2 = _rot(lat[384:400], lat[400:416], cos, sin)
    zpad = jnp.zeros((K_PAD - C_QK, width), F32)
    nkr = _sumsq(r1) + _sumsq(r2)
    nk = []
    for h in range(C_HEADS):
        kn = kv[h * 128:h * 128 + 64]
        nk.append(_sumsq(kn) + nkr)
        kt = jnp.concatenate([kn, r1, r2, zpad], axis=0)
        k_ref[0, h] = kt.T.astype(BF16)
    nk_ref[0] = jnp.concatenate(nk, axis=0)
    v_ref[0, 0] = jnp.concatenate(
        [kv[h * 128 + 64:h * 128 + 128] for h in range(C_HEADS)], axis=0).astype(BF16)


def _odd_in_specs(b):
    d, s = D_MODEL, SEQ
    nt = s // TM
    tok = _tok
    return (
        [
            _resident((OD_IN, d)),
            _resident((C_HEADS * C_QK, C_Q_LORA)),
            _resident((C_HEADS * 128, C_KV_LORA)),
            pl.BlockSpec((32, TM), lambda i, j: (0, j)),
            _resident((C_Q_LORA, LANES)),
            _resident((C_KV_LORA, LANES)),
        ],
        [
            pl.BlockSpec((1, C_HEADS * C_QK, TM), tok),
            pl.BlockSpec((1, C_HEADS, TM, K_PAD), lambda i, j: (i, 0, j, 0)),
            pl.BlockSpec((1, 1, 1024, TM), lambda i, j: (i, j, 0, 0)),
            pl.BlockSpec((1, 1024, TM), tok),
            pl.BlockSpec((1, NORM_ROWS, TM), tok),
            pl.BlockSpec((1, NORM_ROWS, TM), tok),
        ],
        [
            jax.ShapeDtypeStruct((b, C_HEADS * C_QK, s), BF16),
            jax.ShapeDtypeStruct((b, C_HEADS, s, K_PAD), BF16),
            jax.ShapeDtypeStruct((b, nt, 1024, TM), BF16),
            jax.ShapeDtypeStruct((b, 1024, s), BF16),
            jax.ShapeDtypeStruct((b, NORM_ROWS, s), F32),
            jax.ShapeDtypeStruct((b, NORM_ROWS, s), F32),
        ],
    )


def _softmax_pv(qps, k_at, v_ref, dv, kv_chunk, bounds=None):
    nsets = len(qps)
    ntiles = qps[0].shape[1] // MXU_TILE
    qps = [qp[:, t * MXU_TILE:(t + 1) * MXU_TILE] for qp in qps for t in range(ntiles)]
    n = len(qps)
    nchunks = SEQ // kv_chunk
    halves = kv_chunk // MXU_TILE
    ones = jnp.ones((ONES_ROWS, kv_chunk), BF16)

    def scores(ci, j):
        return [jnp.dot(k_at(ci * kv_chunk + h * MXU_TILE), qps[j], preferred_element_type=F32)
                for h in range(halves)]

    m = [None] * n
    acc = [None] * n
    lsum = [None] * n
    items = [(ci, j) for g in range(0, n, MAX_LOCKSTEP) for ci in range(nchunks)
             for j in range(g, min(g + MAX_LOCKSTEP, n))]
    lookahead = LOOKAHEAD_ITEMS[min(n, MAX_LOCKSTEP)]
    pending = [scores(*it) for it in items[:lookahead]]
    for idx, (ci, j) in enumerate(items):
        if idx + lookahead < len(items):
            pending.append(scores(*items[idx + lookahead]))
        sc = pending.pop(0)
        r0 = ci * kv_chunk
        v = v_ref[0, r0 // TM, :, pl.ds(r0 % TM, kv_chunk)]
        if bounds is None:
            v = jnp.concatenate([v, ones], axis=0)
            cm = functools.reduce(jnp.maximum, [jnp.max(s, axis=0, keepdims=True) for s in sc])
            ref = cm if ci == 0 else jnp.maximum(m[j], cm)
        else:
            ref = bounds[j // ntiles]
        ps = [jnp.exp2(s - ref) for s in sc]
        pv = functools.reduce(jnp.add, [
            jnp.dot(v[:, h * MXU_TILE:(h + 1) * MXU_TILE], ps[h].astype(BF16),
                    preferred_element_type=F32) for h in range(halves)])
        if bounds is not None:
            part = functools.reduce(jnp.add, [
                jnp.sum(p.reshape(MXU_TILE // 8, 8, MXU_TILE), axis=0) for p in ps])
            lsum[j] = part if ci == 0 else lsum[j] + part
        if ci == 0:
            acc[j] = pv
        elif bounds is None:
            acc[j] = acc[j] * jnp.exp2(m[j] - ref) + pv
        else:
            acc[j] = acc[j] + pv
        m[j] = ref
    out = []
    for j in range(nsets):
        a = jnp.concatenate(acc[j * ntiles:(j + 1) * ntiles], axis=1)
        if bounds is None:
            out.append((a[0:dv], a[dv:dv + 1]))
        else:
            l8 = jnp.concatenate(lsum[j * ntiles:(j + 1) * ntiles], axis=1)
            out.append((a, jnp.sum(l8, axis=0, keepdims=True)))
    return out


def _attend(qps, bounds, k_at, v_ref, dv, kv_chunk, finish):
    safe = functools.reduce(jnp.logical_and, [b <= SAFE_BOUND for b in bounds])

    @pl.when(safe)
    def _():
        finish(_softmax_pv(qps, k_at, v_ref, dv, kv_chunk, bounds))

    @pl.when(jnp.logical_not(safe))
    def _():
        finish(_softmax_pv(qps, k_at, v_ref, dv, kv_chunk))


def _attn_params():
    return pltpu.CompilerParams(
        dimension_semantics=("parallel", "parallel", "parallel"), vmem_limit_bytes=VMEM_LIMIT)


def _attn_a_kernel(bnd_ref, lam_ref, q_ref, k_ref, v_ref, gate_ref, sg_ref, o_ref, *, lam_init):
    i, h = pl.program_id(0), pl.program_id(1)
    q = q_ref[0]
    tq = q.shape[1]
    z = jnp.zeros((A_QK_DIM, tq), BF16)
    qps = [jnp.concatenate([q[0:64], z], axis=0), jnp.concatenate([z, q[64:128]], axis=0)]

    def finish(res):
        (a0, l0), (a1, l1) = res
        lp = lam_ref[...]
        lam = (jnp.exp(jnp.sum(lp[0:1] * lp[1:2], axis=1, keepdims=True))
               - jnp.exp(jnp.sum(lp[2:3] * lp[3:4], axis=1, keepdims=True)) + lam_init)
        o = a0 / l0 - lam * (a1 / l1)
        ms = jnp.mean(o * o, axis=0, keepdims=True)
        o = o * lax.rsqrt(ms + RMS_EPS) * _rep(sg_ref, tq) * (1.0 - lam_init)
        o_ref[0] = (o * gate_ref[0]).astype(BF16)

    _attend(qps, [bnd_ref[i, 2 * h], bnd_ref[i, 2 * h + 1]],
            lambda r0: k_ref[0, 0, pl.ds(r0, MXU_TILE), :], v_ref, A_V_DIM, KV_CHUNK_A, finish)


def _attn_a(bnd, lam_p, qa, ka, va, gate, sg, lam_init):
    b = qa.shape[0]
    nt = SEQ // TM
    return pl.pallas_call(
        functools.partial(_attn_a_kernel, lam_init=lam_init),
        grid=(b, A_HEADS, SEQ // TQ_A),
        in_specs=[
            pl.BlockSpec(memory_space=pltpu.SMEM),
            pl.BlockSpec((4, A_QK_DIM), lambda i, h, t: (0, 0)),
            pl.BlockSpec((1, 128, TQ_A), lambda i, h, t: (i, h, t)),
            pl.BlockSpec((1, 1, SEQ, K_PAD), lambda i, h, t: (i, h, 0, 0)),
            pl.BlockSpec((1, nt, A_V_DIM, TM), lambda i, h, t: (i, 0, h, 0)),
            pl.BlockSpec((1, A_V_DIM, TQ_A), lambda i, h, t: (i, h, t)),
            pl.BlockSpec((A_V_DIM, LANES), lambda i, h, t: (0, 0)),
        ],
        out_specs=pl.BlockSpec((1, A_V_DIM, TQ_A), lambda i, h, t: (i, h, t)),
        out_shape=jax.ShapeDtypeStruct((b, A_WIDTH, SEQ), BF16),
        compiler_params=_attn_params(),
        name="attn_diff",
    )(bnd, lam_p, qa, ka, va, gate, sg)


def _attn_b_kernel(bnd_ref, q_ref, k_ref, v_ref, gate_ref, o_ref):
    i, h = pl.program_id(0), pl.program_id(1)
    first = h < B_GROUP
    q = q_ref[0]
    z = jnp.zeros_like(q)
    qp = jnp.concatenate([jnp.where(first, q, z), jnp.where(first, z, q)], axis=0)

    def finish(res):
        ((a, l),) = res
        o_ref[0] = (a / l * gate_ref[0]).astype(BF16)

    _attend([qp], [bnd_ref[i, 2 * A_HEADS + h]],
            lambda r0: k_ref[0, pl.ds(r0, MXU_TILE), :], v_ref, B_DIM, KV_CHUNK_BC, finish)


def _attn_b(bnd, qb, kb, vb, gate):
    b = qb.shape[0]
    nt = SEQ // TM
    return pl.pallas_call(
        _attn_b_kernel,
        grid=(b, B_Q_HEADS, SEQ // TQ_B),
        in_specs=[
            pl.BlockSpec(memory_space=pltpu.SMEM),
            pl.BlockSpec((1, B_DIM, TQ_B), lambda i, h, t: (i, h, t)),
            pl.BlockSpec((1, SEQ, K_PAD), lambda i, h, t: (i, 0, 0)),
            pl.BlockSpec((1, nt, B_DIM, TM), lambda i, h, t: (i, 0, h // B_GROUP, 0)),
            pl.BlockSpec((1, B_DIM, TQ_B), lambda i, h, t: (i, A_WIDTH // B_DIM + h, t)),
        ],
        out_specs=pl.BlockSpec((1, B_DIM, TQ_B), lambda i, h, t: (i, h, t)),
        out_shape=jax.ShapeDtypeStruct((b, B_WIDTH, SEQ), BF16),
        compiler_params=_attn_params(),
        name="attn_gqa",
    )(bnd, qb, kb, vb, gate)


def _attn_c_kernel(bnd_ref, q_ref, k_ref, v_ref, gate_ref, o_ref):
    i, h = pl.program_id(0), pl.program_id(1)
    q = q_ref[0]
    qp = jnp.concatenate([q, jnp.zeros((K_PAD - C_QK, q.shape[1]), BF16)], axis=0)

    def finish(res):
        ((a, l),) = res
        o_ref[0] = (a / l * gate_ref[0]).astype(BF16)

    _attend([qp], [bnd_ref[i, h]],
            lambda r0: k_ref[0, 0, pl.ds(r0, MXU_TILE), :], v_ref, C_V, KV_CHUNK_BC, finish)


def _attn_c(bnd, q, k, v, gate):
    b = q.shape[0]
    nt = SEQ // TM
    return pl.pallas_call(
        _attn_c_kernel,
        grid=(b, C_HEADS, SEQ // TQ_C),
        in_specs=[
            pl.BlockSpec(memory_space=pltpu.SMEM),
            pl.BlockSpec((1, C_QK, TQ_C), lambda i, h, t: (i, h, t)),
            pl.BlockSpec((1, 1, SEQ, K_PAD), lambda i, h, t: (i, h, 0, 0)),
            pl.BlockSpec((1, nt, C_V, TM), lambda i, h, t: (i, 0, h, 0)),
            pl.BlockSpec((1, C_V, TQ_C), lambda i, h, t: (i, h, t)),
        ],
        out_specs=pl.BlockSpec((1, C_V, TQ_C), lambda i, h, t: (i, h, t)),
        out_shape=jax.ShapeDtypeStruct((b, C_HEADS * C_V, SEQ), BF16),
        compiler_params=_attn_params(),
        name="attn_mla",
    )(bnd, q, k, v, gate)


def _out_body(og_refs, x_ref, w_ref, lg_ref, lb_ref, x_token_major):
    og = jnp.concatenate([r[0] for r in og_refs], axis=0) if len(og_refs) > 1 else og_refs[0][0]
    y = jnp.dot(w_ref[...], og, preferred_element_type=F32)
    x = x_ref[0].T if x_token_major else x_ref[0]
    z = ALPHA * x + y
    mu = jnp.mean(z, axis=0, keepdims=True)
    d = z - mu
    var = jnp.mean(d * d, axis=0, keepdims=True)
    width = y.shape[1]
    return d * lax.rsqrt(var + LN_EPS) * _rep(lg_ref, width) + _rep(lb_ref, width)


def _out_kernel(*refs, n_og):
    x_ref, w_ref, lg_ref, lb_ref, o_ref = refs[n_og:]
    width = TM // SUB_TILES
    for t in range(SUB_TILES):
        off = t * width
        og_views = [_tokens(r, 2, off, width) for r in refs[:n_og]]
        out = _out_body(og_views, _tokens(x_ref, 2, off, width), w_ref, lg_ref, lb_ref, False)
        o_ref[0, pl.ds(off, width), :] = out.T


def _tokens(ref, axis, off, width):
    idx = [slice(None)] * len(ref.shape)
    idx[axis] = pl.ds(off, width)
    return ref.at[tuple(idx)]


def _out_in_kernel(*refs, n_og, n_in, x_token_major, in_body, tab_pos, out_axes):
    x_ref, w_ref, lg_ref, lb_ref = refs[n_og:n_og + 4]
    in_params = refs[n_og + 4:n_og + 4 + n_in]
    xo_ref = refs[n_og + 4 + n_in]
    in_outs = refs[n_og + 5 + n_in:]
    width = TM // SUB_TILES
    outs = []
    for t in range(SUB_TILES):
        off = t * width
        og_views = [_tokens(r, 2, off, width) for r in refs[:n_og]]
        x_view = _tokens(x_ref, 1 if x_token_major else 2, off, width)
        out = _out_body(og_views, x_view, w_ref, lg_ref, lb_ref, x_token_major)
        xo_ref[0, :, pl.ds(off, width)] = out
        outs.append(out.astype(BF16))
    for t in range(SUB_TILES):
        off = t * width
        params = [_tokens(r, 1, off, width) if p == tab_pos else r
                  for p, r in enumerate(in_params)]
        views = [_tokens(r, ax, off, width) for r, ax in zip(in_outs, out_axes)]
        in_body(outs[t], *params, *views)


def _out_specs(ogs, x_token_major):
    d = D_MODEL
    x_spec = (pl.BlockSpec((1, TM, d), lambda i, j: (i, j, 0)) if x_token_major
              else pl.BlockSpec((1, d, TM), _tok))
    return [pl.BlockSpec((1, og.shape[1], TM), _tok) for og in ogs] + [
        x_spec, _resident((d, d)), _resident((d, LANES)), _resident((d, LANES))]


def _out_proj_last(ogs, xt, w_t, lg, lb):
    b = xt.shape[0]
    return pl.pallas_call(
        functools.partial(_out_kernel, n_og=len(ogs)),
        grid=(b, SEQ // TM),
        in_specs=_out_specs(ogs, False),
        out_specs=pl.BlockSpec((1, TM, D_MODEL), lambda i, j: (i, j, 0)),
        out_shape=jax.ShapeDtypeStruct((b, SEQ, D_MODEL), F32),
        compiler_params=_proj_params(),
        name="out_proj_ln",
    )(*ogs, xt, w_t, lg, lb)


def _out_in(ogs, xt, w_t, lg, lb, next_even, in_args, x_token_major):
    b = xt.shape[0]
    if next_even:
        in_body, (in_specs, out_specs, out_shape) = _even_in_body, _even_in_specs(b)
        tab_pos, out_axes, name = EVEN_TAB_POS, EVEN_OUT_TOKEN_AXES, "out_even_in_proj"
    else:
        in_body, (in_specs, out_specs, out_shape) = _odd_in_body, _odd_in_specs(b)
        tab_pos, out_axes, name = ODD_TAB_POS, ODD_OUT_TOKEN_AXES, "out_odd_in_proj"
    return pl.pallas_call(
        functools.partial(_out_in_kernel, n_og=len(ogs), n_in=len(in_args),
                          x_token_major=x_token_major, in_body=in_body, tab_pos=tab_pos,
                          out_axes=out_axes),
        grid=(b, SEQ // TM),
        in_specs=_out_specs(ogs, x_token_major) + in_specs,
        out_specs=[pl.BlockSpec((1, D_MODEL, TM), _tok)] + out_specs,
        out_shape=[jax.ShapeDtypeStruct((b, D_MODEL, SEQ), F32)] + out_shape,
        compiler_params=_proj_params(),
        name=name,
    )(*ogs, xt, w_t, lg, lb, *in_args)


def _col(v):
    v = v.astype(F32)
    return jnp.broadcast_to(v[:, None], (v.shape[0], LANES))


def _score_bounds(nq, kmax2):
    return jnp.sqrt(jnp.max(nq, axis=-1) * kmax2) * BOUND_SLACK


def _angles_t(pos, dims, theta):
    inv = theta ** (-jnp.arange(0, dims, 2, dtype=F32) / dims)
    ang = pos.astype(F32)[:, None] * inv[None, :]
    return jnp.cos(ang).T, jnp.sin(ang).T


@jax.jit
def _forward(x, ev_w_in, ev_w_out, ev_lam, ev_subln, ev_qnorm, ev_knorm, ev_ln_g, ev_ln_b,
             od_w_in, od_qnorm, od_kvnorm, od_w_qb, od_w_kvb, od_w_out, od_ln_g, od_ln_b):
    s = x.shape[1]
    pos = jnp.arange(s, dtype=jnp.int32)
    row = jnp.repeat(jnp.arange(s // GRID_W, dtype=jnp.int32), GRID_W)
    col = jnp.tile(jnp.arange(GRID_W, dtype=jnp.int32), s // GRID_W)
    tab_ev = jnp.concatenate(
        _angles_t(pos, A_ROT, ROPE_THETA) + _angles_t(row, B_DIM // 2, AXIAL_THETA)
        + _angles_t(col, B_DIM // 2, AXIAL_THETA), axis=0)
    tab_od = jnp.concatenate(_angles_t(pos, C_ROPE, ROPE_THETA), axis=0)

    def even_args(i):
        return (ev_w_in[i].T.astype(BF16), tab_ev, _col(ev_qnorm[i]), _col(ev_knorm[i]))

    def odd_args(i):
        return (od_w_in[i].T.astype(BF16), od_w_qb[i].T.astype(BF16),
                od_w_kvb[i].T.astype(BF16), tab_od, _col(od_qnorm[i]), _col(od_kvnorm[i]))

    xt = x
    proj = _even_in(x, *even_args(0))
    for layer in range(DEPTH):
        i = layer // 2
        if layer % 2 == 0:
            qa, ka, va, qb, kb, vb, gate, nq, nk = proj
            kmax = jnp.max(nk, axis=-1)
            kmax = jnp.concatenate(
                [kmax[:, :2 * A_HEADS],
                 jnp.repeat(kmax[:, 2 * A_HEADS:2 * A_HEADS + B_KV_HEADS], B_GROUP, axis=1)],
                axis=1)
            bnd = _score_bounds(nq, kmax)
            lam_init = 0.8 - 0.6 * math.exp(-0.3 * layer)
            ogs = [_attn_a(bnd, ev_lam[i].astype(F32), qa, ka, va, gate, _col(ev_subln[i]),
                           lam_init),
                   _attn_b(bnd, qb, kb, vb, gate)]
            out_args = (ev_w_out[i].T.astype(BF16), _col(ev_ln_g[i]), _col(ev_ln_b[i]))
        else:
            q, k, v, gate, nq, nk = proj
            ogs = [_attn_c(_score_bounds(nq, jnp.max(nk, axis=-1)), q, k, v, gate)]
            out_args = (od_w_out[i].T.astype(BF16), _col(od_ln_g[i]), _col(od_ln_b[i]))
        if layer == DEPTH - 1:
            return _out_proj_last(ogs, xt, *out_args)
        nxt = (layer + 1) // 2
        if layer % 2 == 0:
            xt, *proj = _out_in(ogs, xt, *out_args, False, odd_args(nxt), layer == 0)
        else:
            xt, *proj = _out_in(ogs, xt, *out_args, True, even_args(nxt), False)


def kernel(x, ev_w_in, ev_w_out, ev_lam, ev_subln, ev_qnorm, ev_knorm, ev_ln_g, ev_ln_b,
           od_w_in, od_qnorm, od_kvnorm, od_w_qb, od_w_kvb, od_w_out, od_ln_g, od_ln_b):
    return _forward(x, ev_w_in, ev_w_out, ev_lam, ev_subln, ev_qnorm, ev_knorm, ev_ln_g,
                    ev_ln_b, od_w_in, od_qnorm, od_kvnorm, od_w_qb, od_w_kvb, od_w_out,
                    od_ln_g, od_ln_b)
```

```python
import functools
import math

import jax
import jax.numpy as jnp
from jax import lax
from jax.experimental import pallas as pl
from jax.experimental.pallas import tpu as pltpu

F32 = jnp.float32
BF16 = jnp.bfloat16

D_MODEL = 1024
SEQ = 4096
DEPTH = 4
GRID_W = 64
ROPE_THETA = 500000.0
AXIAL_THETA = 10000.0
LN_EPS = 1e-5
RMS_EPS = 1e-6

A_HEADS = 4
A_QK_DIM = 64
A_V_DIM = 128
A_WIDTH = 512
A_ROT = 16
B_Q_HEADS = 8
B_KV_HEADS = 2
B_GROUP = 4
B_DIM = 64
B_WIDTH = 512
EV_IN = 3328

C_HEADS = 16
C_NOPE = 64
C_ROPE = 32
C_V = 64
C_Q_LORA = 256
C_KV_LORA = 128
C_QK = C_NOPE + C_ROPE
OD_IN = 1440

ALPHA = (2 * DEPTH) ** 0.25
LOG2E = 1.4426950408889634

QSCALE_AB = A_QK_DIM ** -0.5 * LOG2E
QSCALE_C = C_QK ** -0.5 * LOG2E

LANES = 128
MXU_TILE = 256
LOOKAHEAD_ITEMS = {1: 2, 2: 3, 4: 8}
MAX_LOCKSTEP = 4
ONES_ROWS = 16
K_PAD = 128
NORM_ROWS = 16
SAFE_BOUND = 60.0
BOUND_SLACK = 1.02
TM = 1024
V_CHUNK = 512
SUB_TILES = 4
EVEN_TAB_POS, EVEN_OUT_TOKEN_AXES = 1, (2, 2, (1, 3), 2, 1, (1, 3), 2, 2, 2)
ODD_TAB_POS, ODD_OUT_TOKEN_AXES = 3, (2, 2, (1, 3), 2, 2, 2)
KV_CHUNK_A = 512
KV_CHUNK_BC = 256
TQ_A = 1024
TQ_B = 4096
TQ_C = 4096
VMEM_LIMIT = 56 * 1024 * 1024


def _rep(ref, n):
    a = ref[...]
    return jnp.concatenate([a] * (n // LANES), axis=1)


def _rot(x1, x2, cos, sin):
    return x1 * cos - x2 * sin, x2 * cos + x1 * sin


def _silu(x):
    return x * jax.nn.sigmoid(x)


def _sumsq(x):
    return jnp.sum(x * x, axis=0, keepdims=True)


def _tok(i, j):
    return (i, 0, j)


def _const2(i, j):
    return (0, 0)


def _resident(shape):
    return pl.BlockSpec(shape, _const2, pipeline_mode=pl.Buffered(1))


def _proj_params():
    return pltpu.CompilerParams(
        dimension_semantics=("parallel", "parallel"), vmem_limit_bytes=VMEM_LIMIT)


def _even_in_body(xb, w_ref, tab_ref, qn_ref, kn_ref,
                  qa_ref, ka_ref, va_ref, qb_ref, kb_ref, vb_ref, g_ref, nq_ref, nk_ref,
                  *, x_token_major=False):
    contract = (((1,), (1 if x_token_major else 0,)), ((), ()))

    def proj(r0, r1):
        return lax.dot_general(w_ref[r0:r1, :], xb, contract, preferred_element_type=F32)

    cos_a, sin_a = tab_ref[0:8, :], tab_ref[8:16, :]
    cos_r, sin_r = tab_ref[16:32, :], tab_ref[32:48, :]
    cos_c, sin_c = tab_ref[48:64, :], tab_ref[64:80, :]

    def rope_a(h):
        outs = []
        for hc in range(2 * A_HEADS):
            b = hc * A_QK_DIM
            r1, r2 = _rot(h[b:b + 8], h[b + 8:b + 16], cos_a, sin_a)
            outs.append(jnp.concatenate([r1, r2, h[b + 16:b + 64]], axis=0))
        return outs

    g_ref[0] = _silu(proj(2304, 3328)).astype(BF16)
    qs = [x * QSCALE_AB for x in rope_a(proj(0, 512))]
    qa_ref[0] = jnp.concatenate(qs, axis=0).astype(BF16)
    ks = rope_a(proj(512, 1024))
    nq = [_sumsq(x) for x in qs]
    nk = [_sumsq(x) for x in ks]
    for h in range(A_HEADS):
        kt = jnp.concatenate([ks[2 * h], ks[2 * h + 1]], axis=0)
        ka_ref[0, h] = kt.T.astype(BF16)

    def norm_axial(h, g):
        ms = jnp.mean(h * h, axis=0, keepdims=True)
        y = h * lax.rsqrt(ms + RMS_EPS) * g
        a1, a2 = _rot(y[0:16], y[16:32], cos_r, sin_r)
        b1, b2 = _rot(y[32:48], y[48:64], cos_c, sin_c)
        return jnp.concatenate([a1, a2, b1, b2], axis=0)

    width = xb.shape[0 if x_token_major else 1]
    qn = _rep(qn_ref, width)
    kn = _rep(kn_ref, width)
    hq = proj(1536, 2048)
    qs = [norm_axial(hq[h * 64:(h + 1) * 64], qn) * QSCALE_AB for h in range(B_Q_HEADS)]
    qb_ref[0] = jnp.concatenate(qs, axis=0).astype(BF16)
    hkv = proj(2048, 2304)
    ks = [norm_axial(hkv[h * 64:(h + 1) * 64], kn) for h in range(B_KV_HEADS)]
    kb_ref[0] = jnp.concatenate(ks, axis=0).T.astype(BF16)
    nq_ref[0] = jnp.concatenate(nq + [_sumsq(x) for x in qs], axis=0)
    nk = nk + [_sumsq(x) for x in ks]
    nk_ref[0] = jnp.concatenate(nk + [jnp.zeros_like(nk[0])] * (NORM_ROWS - len(nk)), axis=0)
    vb_ref[0, 0] = hkv[128:256].astype(BF16)
    va_ref[0, 0] = proj(1024, 1536).astype(BF16)


def _even_in_kernel(x_ref, *refs):
    n_in = len(refs) - len(EVEN_OUT_TOKEN_AXES)
    width = TM // SUB_TILES
    for t in range(SUB_TILES):
        off = t * width
        params = [_tokens(r, 1, off, width) if p == EVEN_TAB_POS else r
                  for p, r in enumerate(refs[:n_in])]
        views = [_tokens(r, ax, off, width) for r, ax in zip(refs[n_in:], EVEN_OUT_TOKEN_AXES)]
        _even_in_body(x_ref[0, pl.ds(off, width), :].astype(BF16), *params, *views,
                      x_token_major=True)


def _even_in_specs(b):
    d, s = D_MODEL, SEQ
    nt = s // TM
    tok = _tok
    chunk = lambda i, j: (i, j, 0, 0)
    return (
        [
            _resident((EV_IN, d)),
            pl.BlockSpec((80, TM), lambda i, j: (0, j)),
            _resident((B_DIM, LANES)),
            _resident((B_DIM, LANES)),
        ],
        [
            pl.BlockSpec((1, 512, TM), tok),
            pl.BlockSpec((1, A_HEADS, TM, K_PAD), lambda i, j: (i, 0, j, 0)),
            pl.BlockSpec((1, TM // V_CHUNK, 512, V_CHUNK), chunk),
            pl.BlockSpec((1, 512, TM), tok),
            pl.BlockSpec((1, TM, K_PAD), lambda i, j: (i, j, 0)),
            pl.BlockSpec((1, TM // V_CHUNK, 128, V_CHUNK), chunk),
            pl.BlockSpec((1, 1024, TM), tok),
            pl.BlockSpec((1, NORM_ROWS, TM), tok),
            pl.BlockSpec((1, NORM_ROWS, TM), tok),
        ],
        [
            jax.ShapeDtypeStruct((b, 512, s), BF16),
            jax.ShapeDtypeStruct((b, A_HEADS, s, K_PAD), BF16),
            jax.ShapeDtypeStruct((b, s // V_CHUNK, 512, V_CHUNK), BF16),
            jax.ShapeDtypeStruct((b, 512, s), BF16),
            jax.ShapeDtypeStruct((b, s, K_PAD), BF16),
            jax.ShapeDtypeStruct((b, s // V_CHUNK, 128, V_CHUNK), BF16),
            jax.ShapeDtypeStruct((b, 1024, s), BF16),
            jax.ShapeDtypeStruct((b, NORM_ROWS, s), F32),
            jax.ShapeDtypeStruct((b, NORM_ROWS, s), F32),
        ],
    )


def _even_in(x, w_t, tab, qn, kn):
    b = x.shape[0]
    in_specs, out_specs, out_shape = _even_in_specs(b)
    return pl.pallas_call(
        _even_in_kernel,
        grid=(b, SEQ // TM),
        in_specs=[pl.BlockSpec((1, TM, D_MODEL), lambda i, j: (i, j, 0))] + in_specs,
        out_specs=out_specs,
        out_shape=out_shape,
        compiler_params=_proj_params(),
        name="even_in_proj",
    )(x, w_t, tab, qn, kn)


def _odd_in_body(xb, w_ref, wq_ref, wkv_ref, tab_ref, qn_ref, kvn_ref,
                 q_ref, k_ref, v_ref, g_ref, nq_ref, nk_ref):
    cos, sin = tab_ref[0:16, :], tab_ref[16:32, :]

    def rms(h, g):
        ms = jnp.mean(h * h, axis=0, keepdims=True)
        return h * lax.rsqrt(ms + RMS_EPS) * g

    g_ref[0] = _silu(jnp.dot(w_ref[416:1440, :], xb, preferred_element_type=F32)).astype(BF16)
    lat = jnp.dot(w_ref[0:416, :], xb, preferred_element_type=F32)
    width = xb.shape[1]
    cqn = rms(lat[0:256], _rep(qn_ref, width)).astype(BF16)
    q = jnp.dot(wq_ref[...], cqn, preferred_element_type=F32)
    qs = []
    for h in range(C_HEADS):
        b = h * C_QK
        r1, r2 = _rot(q[b + 64:b + 80], q[b + 80:b + 96], cos, sin)
        qs.append(jnp.concatenate([q[b:b + 64], r1, r2], axis=0) * QSCALE_C)
    q_ref[0] = jnp.concatenate(qs, axis=0).astype(BF16)
    nq_ref[0] = jnp.concatenate([_sumsq(x) for x in qs], axis=0)

    ckvn = rms(lat[256:384], _rep(kvn_ref, width)).astype(BF16)
    kv = jnp.dot(wkv_ref[...], ckvn, preferred_element_type=F32)
    r1, r2 = _rot(lat[384:400], lat[400:416], cos, sin)
    zpad = jnp.zeros((K_PAD - C_QK, width), F32)
    nkr = _sumsq(r1) + _sumsq(r2)
    nk = []
    for h in range(C_HEADS):
        kn = kv[h * 128:h * 128 + 64]
        nk.append(_sumsq(kn) + nkr)
        kt = jnp.concatenate([kn, r1, r2, zpad], axis=0)
        k_ref[0, h] = kt.T.astype(BF16)
    nk_ref[0] = jnp.concatenate(nk, axis=0)
    v_ref[0, 0] = jnp.concatenate(
        [kv[h * 128 + 64:h * 128 + 128] for h in range(C_HEADS)], axis=0).astype(BF16)


def _odd_in_specs(b):
    d, s = D_MODEL, SEQ
    nt = s // TM
    tok = _tok
    return (
        [
            _resident((OD_IN, d)),
            _resident((C_HEADS * C_QK, C_Q_LORA)),
            _resident((C_HEADS * 128, C_KV_LORA)),
            pl.BlockSpec((32, TM), lambda i, j: (0, j)),
            _resident((C_Q_LORA, LANES)),
            _resident((C_KV_LORA, LANES)),
        ],
        [
            pl.BlockSpec((1, C_HEADS * C_QK, TM), tok),
            pl.BlockSpec((1, C_HEADS, TM, K_PAD), lambda i, j: (i, 0, j, 0)),
            pl.BlockSpec((1, TM // V_CHUNK, 1024, V_CHUNK), lambda i, j: (i, j, 0, 0)),
            pl.BlockSpec((1, 1024, TM), tok),
            pl.BlockSpec((1, NORM_ROWS, TM), tok),
            pl.BlockSpec((1, NORM_ROWS, TM), tok),
        ],
        [
            jax.ShapeDtypeStruct((b, C_HEADS * C_QK, s), BF16),
            jax.ShapeDtypeStruct((b, C_HEADS, s, K_PAD), BF16),
            jax.ShapeDtypeStruct((b, s // V_CHUNK, 1024, V_CHUNK), BF16),
            jax.ShapeDtypeStruct((b, 1024, s), BF16),
            jax.ShapeDtypeStruct((b, NORM_ROWS, s), F32),
            jax.ShapeDtypeStruct((b, NORM_ROWS, s), F32),
        ],
    )


def _softmax_pv(qps, k_at, v_ref, dv, kv_chunk, bounds=None):
    nsets = len(qps)
    ntiles = qps[0].shape[1] // MXU_TILE
    qps = [qp[:, t * MXU_TILE:(t + 1) * MXU_TILE] for qp in qps for t in range(ntiles)]
    n = len(qps)
    nchunks = SEQ // kv_chunk
    halves = kv_chunk // MXU_TILE
    ones = jnp.ones((ONES_ROWS, kv_chunk), BF16)

    def scores(ci, j):
        return [jnp.dot(k_at(ci * kv_chunk + h * MXU_TILE), qps[j], preferred_element_type=F32)
                for h in range(halves)]

    m = [None] * n
    acc = [None] * n
    lsum = [None] * n
    items = [(ci, j) for g in range(0, n, MAX_LOCKSTEP) for ci in range(nchunks)
             for j in range(g, min(g + MAX_LOCKSTEP, n))]
    lookahead = LOOKAHEAD_ITEMS[min(n, MAX_LOCKSTEP)]
    pending = [scores(*it) for it in items[:lookahead]]
    for idx, (ci, j) in enumerate(items):
        if idx + lookahead < len(items):
            pending.append(scores(*items[idx + lookahead]))
        sc = pending.pop(0)
        r0 = ci * kv_chunk
        v = v_ref[0, r0 // V_CHUNK, :, pl.ds(r0 % V_CHUNK, kv_chunk)]
        if bounds is None:
            v = jnp.concatenate([v, ones], axis=0)
            cm = functools.reduce(jnp.maximum, [jnp.max(s, axis=0, keepdims=True) for s in sc])
            ref = cm if ci == 0 else jnp.maximum(m[j], cm)
        else:
            ref = bounds[j // ntiles]
        ps = [jnp.exp2(s - ref) for s in sc]
        pv = functools.reduce(jnp.add, [
            jnp.dot(v[:, h * MXU_TILE:(h + 1) * MXU_TILE], ps[h].astype(BF16),
                    preferred_element_type=F32) for h in range(halves)])
        if bounds is not None:
            part = functools.reduce(jnp.add, [
                jnp.sum(p.reshape(MXU_TILE // 8, 8, MXU_TILE), axis=0) for p in ps])
            lsum[j] = part if ci == 0 else lsum[j] + part
        if ci == 0:
            acc[j] = pv
        elif bounds is None:
            acc[j] = acc[j] * jnp.exp2(m[j] - ref) + pv
        else:
            acc[j] = acc[j] + pv
        m[j] = ref
    out = []
    for j in range(nsets):
        a = jnp.concatenate(acc[j * ntiles:(j + 1) * ntiles], axis=1)
        if bounds is None:
            out.append((a[0:dv], a[dv:dv + 1]))
        else:
            l8 = jnp.concatenate(lsum[j * ntiles:(j + 1) * ntiles], axis=1)
            out.append((a, jnp.sum(l8, axis=0, keepdims=True)))
    return out


def _attend(qps, bounds, k_at, v_ref, dv, kv_chunk, finish):
    safe = functools.reduce(jnp.logical_and, [b <= SAFE_BOUND for b in bounds])

    @pl.when(safe)
    def _():
        finish(_softmax_pv(qps, k_at, v_ref, dv, kv_chunk, bounds))

    @pl.when(jnp.logical_not(safe))
    def _():
        finish(_softmax_pv(qps, k_at, v_ref, dv, kv_chunk))


def _attn_params():
    return pltpu.CompilerParams(
        dimension_semantics=("parallel", "parallel", "parallel"), vmem_limit_bytes=VMEM_LIMIT)


def _attn_a_kernel(bnd_ref, lam_ref, q_ref, k_ref, v_ref, gate_ref, sg_ref, o_ref, *, lam_init):
    i, h = pl.program_id(0), pl.program_id(1)
    q = q_ref[0]
    tq = q.shape[1]
    z = jnp.zeros((A_QK_DIM, tq), BF16)
    qps = [jnp.concatenate([q[0:64], z], axis=0), jnp.concatenate([z, q[64:128]], axis=0)]

    def finish(res):
        (a0, l0), (a1, l1) = res
        lp = lam_ref[...]
        lam = (jnp.exp(jnp.sum(lp[0:1] * lp[1:2], axis=1, keepdims=True))
               - jnp.exp(jnp.sum(lp[2:3] * lp[3:4], axis=1, keepdims=True)) + lam_init)
        o = a0 / l0 - lam * (a1 / l1)
        ms = jnp.mean(o * o, axis=0, keepdims=True)
        o = o * lax.rsqrt(ms + RMS_EPS) * _rep(sg_ref, tq) * (1.0 - lam_init)
        o_ref[0] = (o * gate_ref[0]).astype(BF16)

    _attend(qps, [bnd_ref[i, 2 * h], bnd_ref[i, 2 * h + 1]],
            lambda r0: k_ref[0, 0, pl.ds(r0, MXU_TILE), :], v_ref, A_V_DIM, KV_CHUNK_A, finish)


def _attn_a(bnd, lam_p, qa, ka, va, gate, sg, lam_init):
    b = qa.shape[0]
    nt = SEQ // V_CHUNK
    return pl.pallas_call(
        functools.partial(_attn_a_kernel, lam_init=lam_init),
        grid=(b, A_HEADS, SEQ // TQ_A),
        in_specs=[
            pl.BlockSpec(memory_space=pltpu.SMEM),
            pl.BlockSpec((4, A_QK_DIM), lambda i, h, t: (0, 0)),
            pl.BlockSpec((1, 128, TQ_A), lambda i, h, t: (i, h, t)),
            pl.BlockSpec((1, 1, SEQ, K_PAD), lambda i, h, t: (i, h, 0, 0)),
            pl.BlockSpec((1, nt, A_V_DIM, V_CHUNK), lambda i, h, t: (i, 0, h, 0)),
            pl.BlockSpec((1, A_V_DIM, TQ_A), lambda i, h, t: (i, h, t)),
            pl.BlockSpec((A_V_DIM, LANES), lambda i, h, t: (0, 0)),
        ],
        out_specs=pl.BlockSpec((1, A_V_DIM, TQ_A), lambda i, h, t: (i, h, t)),
        out_shape=jax.ShapeDtypeStruct((b, A_WIDTH, SEQ), BF16),
        compiler_params=_attn_params(),
        name="attn_diff",
    )(bnd, lam_p, qa, ka, va, gate, sg)


def _attn_b_kernel(bnd_ref, q_ref, k_ref, v_ref, gate_ref, o_ref):
    i, h = pl.program_id(0), pl.program_id(1)
    first = h < B_GROUP
    q = q_ref[0]
    z = jnp.zeros_like(q)
    qp = jnp.concatenate([jnp.where(first, q, z), jnp.where(first, z, q)], axis=0)

    def finish(res):
        ((a, l),) = res
        o_ref[0] = (a / l * gate_ref[0]).astype(BF16)

    _attend([qp], [bnd_ref[i, 2 * A_HEADS + h]],
            lambda r0: k_ref[0, pl.ds(r0, MXU_TILE), :], v_ref, B_DIM, KV_CHUNK_BC, finish)


def _attn_b(bnd, qb, kb, vb, gate):
    b = qb.shape[0]
    nt = SEQ // V_CHUNK
    return pl.pallas_call(
        _attn_b_kernel,
        grid=(b, B_Q_HEADS, SEQ // TQ_B),
        in_specs=[
            pl.BlockSpec(memory_space=pltpu.SMEM),
            pl.BlockSpec((1, B_DIM, TQ_B), lambda i, h, t: (i, h, t)),
            pl.BlockSpec((1, SEQ, K_PAD), lambda i, h, t: (i, 0, 0)),
            pl.BlockSpec((1, nt, B_DIM, V_CHUNK), lambda i, h, t: (i, 0, h // B_GROUP, 0)),
            pl.BlockSpec((1, B_DIM, TQ_B), lambda i, h, t: (i, A_WIDTH // B_DIM + h, t)),
        ],
        out_specs=pl.BlockSpec((1, B_DIM, TQ_B), lambda i, h, t: (i, h, t)),
        out_shape=jax.ShapeDtypeStruct((b, B_WIDTH, SEQ), BF16),
        compiler_params=_attn_params(),
        name="attn_gqa",
    )(bnd, qb, kb, vb, gate)


def _attn_c_kernel(bnd_ref, q_ref, k_ref, v_ref, gate_ref, o_ref):
    i, h = pl.program_id(0), pl.program_id(1)
    q = q_ref[0]
    qp = jnp.concatenate([q, jnp.zeros((K_PAD - C_QK, q.shape[1]), BF16)], axis=0)

    def finish(res):
        ((a, l),) = res
        o_ref[0] = (a / l * gate_ref[0]).astype(BF16)

    _attend([qp], [bnd_ref[i, h]],
            lambda r0: k_ref[0, 0, pl.ds(r0, MXU_TILE), :], v_ref, C_V, KV_CHUNK_BC, finish)


def _attn_c(bnd, q, k, v, gate):
    b = q.shape[0]
    nt = SEQ // V_CHUNK
    return pl.pallas_call(
        _attn_c_kernel,
        grid=(b, C_HEADS, SEQ // TQ_C),
        in_specs=[
            pl.BlockSpec(memory_space=pltpu.SMEM),
            pl.BlockSpec((1, C_QK, TQ_C), lambda i, h, t: (i, h, t)),
            pl.BlockSpec((1, 1, SEQ, K_PAD), lambda i, h, t: (i, h, 0, 0)),
            pl.BlockSpec((1, nt, C_V, V_CHUNK), lambda i, h, t: (i, 0, h, 0)),
            pl.BlockSpec((1, C_V, TQ_C), lambda i, h, t: (i, h, t)),
        ],
        out_specs=pl.BlockSpec((1, C_V, TQ_C), lambda i, h, t: (i, h, t)),
        out_shape=jax.ShapeDtypeStruct((b, C_HEADS * C_V, SEQ), BF16),
        compiler_params=_attn_params(),
        name="attn_mla",
    )(bnd, q, k, v, gate)


def _out_body(og_refs, x_ref, w_ref, lg_ref, lb_ref, x_token_major):
    og = jnp.concatenate([r[0] for r in og_refs], axis=0) if len(og_refs) > 1 else og_refs[0][0]
    y = jnp.dot(w_ref[...], og, preferred_element_type=F32)
    x = x_ref[0].T if x_token_major else x_ref[0]
    z = ALPHA * x + y
    mu = jnp.mean(z, axis=0, keepdims=True)
    d = z - mu
    var = jnp.mean(d * d, axis=0, keepdims=True)
    width = y.shape[1]
    return d * lax.rsqrt(var + LN_EPS) * _rep(lg_ref, width) + _rep(lb_ref, width)


def _out_kernel(*refs, n_og):
    x_ref, w_ref, lg_ref, lb_ref, o_ref = refs[n_og:]
    width = TM // SUB_TILES
    for t in range(SUB_TILES):
        off = t * width
        og_views = [_tokens(r, 2, off, width) for r in refs[:n_og]]
        out = _out_body(og_views, _tokens(x_ref, 2, off, width), w_ref, lg_ref, lb_ref, False)
        o_ref[0, pl.ds(off, width), :] = out.T


def _tokens(ref, axis, off, width):
    idx = [slice(None)] * len(ref.shape)
    if isinstance(axis, tuple):
        idx[axis[0]] = pl.ds(off // V_CHUNK, 1)
        idx[axis[1]] = pl.ds(off % V_CHUNK, width)
    else:
        idx[axis] = pl.ds(off, width)
    return ref.at[tuple(idx)]


def _out_in_kernel(*refs, n_og, n_in, x_token_major, in_body, tab_pos, out_axes):
    x_ref, w_ref, lg_ref, lb_ref = refs[n_og:n_og + 4]
    in_params = refs[n_og + 4:n_og + 4 + n_in]
    xo_ref = refs[n_og + 4 + n_in]
    in_outs = refs[n_og + 5 + n_in:]
    width = TM // SUB_TILES
    outs = []
    for t in range(SUB_TILES):
        off = t * width
        og_views = [_tokens(r, 2, off, width) for r in refs[:n_og]]
        x_view = _tokens(x_ref, 1 if x_token_major else 2, off, width)
        out = _out_body(og_views, x_view, w_ref, lg_ref, lb_ref, x_token_major)
        xo_ref[0, :, pl.ds(off, width)] = out
        outs.append(out.astype(BF16))
    for t in range(SUB_TILES):
        off = t * width
        params = [_tokens(r, 1, off, width) if p == tab_pos else r
                  for p, r in enumerate(in_params)]
        views = [_tokens(r, ax, off, width) for r, ax in zip(in_outs, out_axes)]
        in_body(outs[t], *params, *views)


def _out_specs(ogs, x_token_major):
    d = D_MODEL
    x_spec = (pl.BlockSpec((1, TM, d), lambda i, j: (i, j, 0)) if x_token_major
              else pl.BlockSpec((1, d, TM), _tok))
    return [pl.BlockSpec((1, og.shape[1], TM), _tok) for og in ogs] + [
        x_spec, _resident((d, d)), _resident((d, LANES)), _resident((d, LANES))]


def _out_proj_last(ogs, xt, w_t, lg, lb):
    b = xt.shape[0]
    return pl.pallas_call(
        functools.partial(_out_kernel, n_og=len(ogs)),
        grid=(b, SEQ // TM),
        in_specs=_out_specs(ogs, False),
        out_specs=pl.BlockSpec((1, TM, D_MODEL), lambda i, j: (i, j, 0)),
        out_shape=jax.ShapeDtypeStruct((b, SEQ, D_MODEL), F32),
        compiler_params=_proj_params(),
        name="out_proj_ln",
    )(*ogs, xt, w_t, lg, lb)


def _out_in(ogs, xt, w_t, lg, lb, next_even, in_args, x_token_major):
    b = xt.shape[0]
    if next_even:
        in_body, (in_specs, out_specs, out_shape) = _even_in_body, _even_in_specs(b)
        tab_pos, out_axes, name = EVEN_TAB_POS, EVEN_OUT_TOKEN_AXES, "out_even_in_proj"
    else:
        in_body, (in_specs, out_specs, out_shape) = _odd_in_body, _odd_in_specs(b)
        tab_pos, out_axes, name = ODD_TAB_POS, ODD_OUT_TOKEN_AXES, "out_odd_in_proj"
    return pl.pallas_call(
        functools.partial(_out_in_kernel, n_og=len(ogs), n_in=len(in_args),
                          x_token_major=x_token_major, in_body=in_body, tab_pos=tab_pos,
                          out_axes=out_axes),
        grid=(b, SEQ // TM),
        in_specs=_out_specs(ogs, x_token_major) + in_specs,
        out_specs=[pl.BlockSpec((1, D_MODEL, TM), _tok)] + out_specs,
        out_shape=[jax.ShapeDtypeStruct((b, D_MODEL, SEQ), F32)] + out_shape,
        compiler_params=_proj_params(),
        name=name,
    )(*ogs, xt, w_t, lg, lb, *in_args)


def _col(v):
    v = v.astype(F32)
    return jnp.broadcast_to(v[:, None], (v.shape[0], LANES))


def _score_bounds(nq, kmax2):
    return jnp.sqrt(jnp.max(nq, axis=-1) * kmax2) * BOUND_SLACK


def _angles_t(pos, dims, theta):
    inv = theta ** (-jnp.arange(0, dims, 2, dtype=F32) / dims)
    ang = pos.astype(F32)[:, None] * inv[None, :]
    return jnp.cos(ang).T, jnp.sin(ang).T


@jax.jit
def _forward(x, ev_w_in, ev_w_out, ev_lam, ev_subln, ev_qnorm, ev_knorm, ev_ln_g, ev_ln_b,
             od_w_in, od_qnorm, od_kvnorm, od_w_qb, od_w_kvb, od_w_out, od_ln_g, od_ln_b):
    s = x.shape[1]
    pos = jnp.arange(s, dtype=jnp.int32)
    row = jnp.repeat(jnp.arange(s // GRID_W, dtype=jnp.int32), GRID_W)
    col = jnp.tile(jnp.arange(GRID_W, dtype=jnp.int32), s // GRID_W)
    tab_ev = jnp.concatenate(
        _angles_t(pos, A_ROT, ROPE_THETA) + _angles_t(row, B_DIM // 2, AXIAL_THETA)
        + _angles_t(col, B_DIM // 2, AXIAL_THETA), axis=0)
    tab_od = jnp.concatenate(_angles_t(pos, C_ROPE, ROPE_THETA), axis=0)

    def even_args(i):
        return (ev_w_in[i].T.astype(BF16), tab_ev, _col(ev_qnorm[i]), _col(ev_knorm[i]))

    def odd_args(i):
        return (od_w_in[i].T.astype(BF16), od_w_qb[i].T.astype(BF16),
                od_w_kvb[i].T.astype(BF16), tab_od, _col(od_qnorm[i]), _col(od_kvnorm[i]))

    xt = x
    proj = _even_in(x, *even_args(0))
    for layer in range(DEPTH):
        i = layer // 2
        if layer % 2 == 0:
            qa, ka, va, qb, kb, vb, gate, nq, nk = proj
            kmax = jnp.max(nk, axis=-1)
            kmax = jnp.concatenate(
                [kmax[:, :2 * A_HEADS],
                 jnp.repeat(kmax[:, 2 * A_HEADS:2 * A_HEADS + B_KV_HEADS], B_GROUP, axis=1)],
                axis=1)
            bnd = _score_bounds(nq, kmax)
            lam_init = 0.8 - 0.6 * math.exp(-0.3 * layer)
            ogs = [_attn_a(bnd, ev_lam[i].astype(F32), qa, ka, va, gate, _col(ev_subln[i]),
                           lam_init),
                   _attn_b(bnd, qb, kb, vb, gate)]
            out_args = (ev_w_out[i].T.astype(BF16), _col(ev_ln_g[i]), _col(ev_ln_b[i]))
        else:
            q, k, v, gate, nq, nk = proj
            ogs = [_attn_c(_score_bounds(nq, jnp.max(nk, axis=-1)), q, k, v, gate)]
            out_args = (od_w_out[i].T.astype(BF16), _col(od_ln_g[i]), _col(od_ln_b[i]))
        if layer == DEPTH - 1:
            return _out_proj_last(ogs, xt, *out_args)
        nxt = (layer + 1) // 2
        if layer % 2 == 0:
            xt, *proj = _out_in(ogs, xt, *out_args, False, odd_args(nxt), layer == 0)
        else:
            xt, *proj = _out_in(ogs, xt, *out_args, True, even_args(nxt), False)


def kernel(x, ev_w_in, ev_w_out, ev_lam, ev_subln, ev_qnorm, ev_knorm, ev_ln_g, ev_ln_b,
           od_w_in, od_qnorm, od_kvnorm, od_w_qb, od_w_kvb, od_w_out, od_ln_g, od_ln_b):
    return _forward(x, ev_w_in, ev_w_out, ev_lam, ev_subln, ev_qnorm, ev_knorm, ev_ln_g,
                    ev_ln_b, od_w_in, od_qnorm, od_kvnorm, od_w_qb, od_w_kvb, od_w_out,
                    od_ln_g, od_ln_b)
```

```python
import functools
import math

import jax
import jax.numpy as jnp
from jax import lax
from jax.experimental import pallas as pl
from jax.experimental.pallas import tpu as pltpu

F32 = jnp.float32
BF16 = jnp.bfloat16

D_MODEL = 1024
SEQ = 4096
DEPTH = 4
GRID_W = 64
ROPE_THETA = 500000.0
AXIAL_THETA = 10000.0
LN_EPS = 1e-5
RMS_EPS = 1e-6

A_HEADS = 4
A_QK_DIM = 64
A_V_DIM = 128
A_WIDTH = 512
A_ROT = 16
B_Q_HEADS = 8
B_KV_HEADS = 2
B_GROUP = 4
B_DIM = 64
B_WIDTH = 512
EV_IN = 3328

C_HEADS = 16
C_NOPE = 64
C_ROPE = 32
C_V = 64
C_Q_LORA = 256
C_KV_LORA = 128
C_QK = C_NOPE + C_ROPE
OD_IN = 1440

ALPHA = (2 * DEPTH) ** 0.25
LOG2E = 1.4426950408889634

QSCALE_AB = A_QK_DIM ** -0.5 * LOG2E
QSCALE_C = C_QK ** -0.5 * LOG2E

LANES = 128
MXU_TILE = 256
LOOKAHEAD_ITEMS = {1: 2, 2: 3, 4: 8}
MAX_LOCKSTEP = 4
ONES_ROWS = 16
K_PAD = 128
NORM_ROWS = 16
SAFE_BOUND = 60.0
BOUND_SLACK = 1.02
TM = 1024
V_CHUNK = 512
SUB_TILES = 4
EVEN_TAB_POS, EVEN_OUT_TOKEN_AXES = 1, (2, 2, (1, 3), 2, 1, (1, 3), 2, 2, 2)
ODD_TAB_POS, ODD_OUT_TOKEN_AXES = 3, (2, 2, (1, 3), 2, 2, 2)
KV_CHUNK_A = 512
KV_CHUNK_BC = 256
TQ_A = 1024
TQ_B = 2048
TQ_C = 4096
VMEM_LIMIT = 56 * 1024 * 1024


def _rep(ref, n):
    a = ref[...]
    return jnp.concatenate([a] * (n // LANES), axis=1)


def _rot(x1, x2, cos, sin):
    return x1 * cos - x2 * sin, x2 * cos + x1 * sin


def _silu(x):
    return x * jax.nn.sigmoid(x)


def _sumsq(x):
    return jnp.sum(x * x, axis=0, keepdims=True)


def _tok(i, j):
    return (i, 0, j)


def _const2(i, j):
    return (0, 0)


def _resident(shape):
    return pl.BlockSpec(shape, _const2, pipeline_mode=pl.Buffered(1))


def _proj_params():
    return pltpu.CompilerParams(
        dimension_semantics=("parallel", "parallel"), vmem_limit_bytes=VMEM_LIMIT)


def _even_in_body(xb, w_ref, tab_ref, qn_ref, kn_ref,
                  qa_ref, ka_ref, va_ref, qb_ref, kb_ref, vb_ref, g_ref, nq_ref, nk_ref,
                  *, x_token_major=False):
    contract = (((1,), (1 if x_token_major else 0,)), ((), ()))

    def proj(r0, r1):
        return lax.dot_general(w_ref[r0:r1, :], xb, contract, preferred_element_type=F32)

    cos_a, sin_a = tab_ref[0:8, :], tab_ref[8:16, :]
    cos_r, sin_r = tab_ref[16:32, :], tab_ref[32:48, :]
    cos_c, sin_c = tab_ref[48:64, :], tab_ref[64:80, :]

    def rope_a(h):
        outs = []
        for hc in range(2 * A_HEADS):
            b = hc * A_QK_DIM
            r1, r2 = _rot(h[b:b + 8], h[b + 8:b + 16], cos_a, sin_a)
            outs.append(jnp.concatenate([r1, r2, h[b + 16:b + 64]], axis=0))
        return outs

    g_ref[0] = _silu(proj(2304, 3328)).astype(BF16)
    qs = [x * QSCALE_AB for x in rope_a(proj(0, 512))]
    qa_ref[0] = jnp.concatenate(qs, axis=0).astype(BF16)
    ks = rope_a(proj(512, 1024))
    nq = [_sumsq(x) for x in qs]
    nk = [_sumsq(x) for x in ks]
    for h in range(A_HEADS):
        kt = jnp.concatenate([ks[2 * h], ks[2 * h + 1]], axis=0)
        ka_ref[0, h] = kt.T.astype(BF16)

    def norm_axial(h, g):
        ms = jnp.mean(h * h, axis=0, keepdims=True)
        y = h * lax.rsqrt(ms + RMS_EPS) * g
        a1, a2 = _rot(y[0:16], y[16:32], cos_r, sin_r)
        b1, b2 = _rot(y[32:48], y[48:64], cos_c, sin_c)
        return jnp.concatenate([a1, a2, b1, b2], axis=0)

    width = xb.shape[0 if x_token_major else 1]
    qn = _rep(qn_ref, width)
    kn = _rep(kn_ref, width)
    hq = proj(1536, 2048)
    qs = [norm_axial(hq[h * 64:(h + 1) * 64], qn) * QSCALE_AB for h in range(B_Q_HEADS)]
    qb_ref[0] = jnp.concatenate(qs, axis=0).astype(BF16)
    hkv = proj(2048, 2304)
    ks = [norm_axial(hkv[h * 64:(h + 1) * 64], kn) for h in range(B_KV_HEADS)]
    kb_ref[0] = jnp.concatenate(ks, axis=0).T.astype(BF16)
    nq_ref[0] = jnp.concatenate(nq + [_sumsq(x) for x in qs], axis=0)
    nk = nk + [_sumsq(x) for x in ks]
    nk_ref[0] = jnp.concatenate(nk + [jnp.zeros_like(nk[0])] * (NORM_ROWS - len(nk)), axis=0)
    vb_ref[0, 0] = hkv[128:256].astype(BF16)
    va_ref[0, 0] = proj(1024, 1536).astype(BF16)


def _even_in_kernel(x_ref, *refs):
    n_in = len(refs) - len(EVEN_OUT_TOKEN_AXES)
    width = TM // SUB_TILES
    for t in range(SUB_TILES):
        off = t * width
        params = [_tokens(r, 1, off, width) if p == EVEN_TAB_POS else r
                  for p, r in enumerate(refs[:n_in])]
        views = [_tokens(r, ax, off, width) for r, ax in zip(refs[n_in:], EVEN_OUT_TOKEN_AXES)]
        _even_in_body(x_ref[0, pl.ds(off, width), :].astype(BF16), *params, *views,
                      x_token_major=True)


def _even_in_specs(b):
    d, s = D_MODEL, SEQ
    nt = s // TM
    tok = _tok
    chunk = lambda i, j: (i, j, 0, 0)
    return (
        [
            _resident((EV_IN, d)),
            pl.BlockSpec((80, TM), lambda i, j: (0, j)),
            _resident((B_DIM, LANES)),
            _resident((B_DIM, LANES)),
        ],
        [
            pl.BlockSpec((1, 512, TM), tok),
            pl.BlockSpec((1, A_HEADS, TM, K_PAD), lambda i, j: (i, 0, j, 0)),
            pl.BlockSpec((1, TM // V_CHUNK, 512, V_CHUNK), chunk),
            pl.BlockSpec((1, 512, TM), tok),
            pl.BlockSpec((1, TM, K_PAD), lambda i, j: (i, j, 0)),
            pl.BlockSpec((1, TM // V_CHUNK, 128, V_CHUNK), chunk),
            pl.BlockSpec((1, 1024, TM), tok),
            pl.BlockSpec((1, NORM_ROWS, TM), tok),
            pl.BlockSpec((1, NORM_ROWS, TM), tok),
        ],
        [
            jax.ShapeDtypeStruct((b, 512, s), BF16),
            jax.ShapeDtypeStruct((b, A_HEADS, s, K_PAD), BF16),
            jax.ShapeDtypeStruct((b, s // V_CHUNK, 512, V_CHUNK), BF16),
            jax.ShapeDtypeStruct((b, 512, s), BF16),
            jax.ShapeDtypeStruct((b, s, K_PAD), BF16),
            jax.ShapeDtypeStruct((b, s // V_CHUNK, 128, V_CHUNK), BF16),
            jax.ShapeDtypeStruct((b, 1024, s), BF16),
            jax.ShapeDtypeStruct((b, NORM_ROWS, s), F32),
            jax.ShapeDtypeStruct((b, NORM_ROWS, s), F32),
        ],
    )


def _even_in(x, w_t, tab, qn, kn):
    b = x.shape[0]
    in_specs, out_specs, out_shape = _even_in_specs(b)
    return pl.pallas_call(
        _even_in_kernel,
        grid=(b, SEQ // TM),
        in_specs=[pl.BlockSpec((1, TM, D_MODEL), lambda i, j: (i, j, 0))] + in_specs,
        out_specs=out_specs,
        out_shape=out_shape,
        compiler_params=_proj_params(),
        name="even_in_proj",
    )(x, w_t, tab, qn, kn)


def _odd_in_body(xb, w_ref, wq_ref, wkv_ref, tab_ref, qn_ref, kvn_ref,
                 q_ref, k_ref, v_ref, g_ref, nq_ref, nk_ref):
    cos, sin = tab_ref[0:16, :], tab_ref[16:32, :]

    def rms(h, g):
        ms = jnp.mean(h * h, axis=0, keepdims=True)
        return h * lax.rsqrt(ms + RMS_EPS) * g

    g_ref[0] = _silu(jnp.dot(w_ref[416:1440, :], xb, preferred_element_type=F32)).astype(BF16)
    lat = jnp.dot(w_ref[0:416, :], xb, preferred_element_type=F32)
    width = xb.shape[1]
    cqn = rms(lat[0:256], _rep(qn_ref, width)).astype(BF16)
    q = jnp.dot(wq_ref[...], cqn, preferred_element_type=F32)
    qs = []
    for h in range(C_HEADS):
        b = h * C_QK
        r1, r2 = _rot(q[b + 64:b + 80], q[b + 80:b + 96], cos, sin)
        qs.append(jnp.concatenate([q[b:b + 64], r1, r2], axis=0) * QSCALE_C)
    q_ref[0] = jnp.concatenate(qs, axis=0).astype(BF16)
    nq_ref[0] = jnp.concatenate([_sumsq(x) for x in qs], axis=0)

    ckvn = rms(lat[256:384], _rep(kvn_ref, width)).astype(BF16)
    kv = jnp.dot(wkv_ref[...], ckvn, preferred_element_type=F32)
    r1, r2 = _rot(lat[384:400], lat[400:416], cos, sin)
    zpad = jnp.zeros((K_PAD - C_QK, width), F32)
    nkr = _sumsq(r1) + _sumsq(r2)
    nk = []
    for h in range(C_HEADS):
        kn = kv[h * 128:h * 128 + 64]
        nk.append(_sumsq(kn) + nkr)
        kt = jnp.concatenate([kn, r1, r2, zpad], axis=0)
        k_ref[0, h] = kt.T.astype(BF16)
    nk_ref[0] = jnp.concatenate(nk, axis=0)
    v_ref[0, 0] = jnp.concatenate(
        [kv[h * 128 + 64:h * 128 + 128] for h in range(C_HEADS)], axis=0).astype(BF16)


def _odd_in_specs(b):
    d, s = D_MODEL, SEQ
    nt = s // TM
    tok = _tok
    return (
        [
            _resident((OD_IN, d)),
            _resident((C_HEADS * C_QK, C_Q_LORA)),
            _resident((C_HEADS * 128, C_KV_LORA)),
            pl.BlockSpec((32, TM), lambda i, j: (0, j)),
            _resident((C_Q_LORA, LANES)),
            _resident((C_KV_LORA, LANES)),
        ],
        [
            pl.BlockSpec((1, C_HEADS * C_QK, TM), tok),
            pl.BlockSpec((1, C_HEADS, TM, K_PAD), lambda i, j: (i, 0, j, 0)),
            pl.BlockSpec((1, TM // V_CHUNK, 1024, V_CHUNK), lambda i, j: (i, j, 0, 0)),
            pl.BlockSpec((1, 1024, TM), tok),
            pl.BlockSpec((1, NORM_ROWS, TM), tok),
            pl.BlockSpec((1, NORM_ROWS, TM), tok),
        ],
        [
            jax.ShapeDtypeStruct((b, C_HEADS * C_QK, s), BF16),
            jax.ShapeDtypeStruct((b, C_HEADS, s, K_PAD), BF16),
            jax.ShapeDtypeStruct((b, s // V_CHUNK, 1024, V_CHUNK), BF16),
            jax.ShapeDtypeStruct((b, 1024, s), BF16),
            jax.ShapeDtypeStruct((b, NORM_ROWS, s), F32),
            jax.ShapeDtypeStruct((b, NORM_ROWS, s), F32),
        ],
    )


def _softmax_pv(qps, k_at, v_ref, dv, kv_chunk, bounds=None):
    nsets = len(qps)
    ntiles = qps[0].shape[1] // MXU_TILE
    qps = [qp[:, t * MXU_TILE:(t + 1) * MXU_TILE] for qp in qps for t in range(ntiles)]
    n = len(qps)
    nchunks = SEQ // kv_chunk
    halves = kv_chunk // MXU_TILE
    ones = jnp.ones((ONES_ROWS, kv_chunk), BF16)

    def scores(ci, j):
        return [jnp.dot(k_at(ci * kv_chunk + h * MXU_TILE), qps[j], preferred_element_type=F32)
                for h in range(halves)]

    m = [None] * n
    acc = [None] * n
    lsum = [None] * n
    items = [(ci, j) for g in range(0, n, MAX_LOCKSTEP) for ci in range(nchunks)
             for j in range(g, min(g + MAX_LOCKSTEP, n))]
    lookahead = LOOKAHEAD_ITEMS[min(n, MAX_LOCKSTEP)]
    pending = [scores(*it) for it in items[:lookahead]]
    for idx, (ci, j) in enumerate(items):
        if idx + lookahead < len(items):
            pending.append(scores(*items[idx + lookahead]))
        sc = pending.pop(0)
        r0 = ci * kv_chunk
        v = v_ref[0, r0 // V_CHUNK, :, pl.ds(r0 % V_CHUNK, kv_chunk)]
        if bounds is None:
            v = jnp.concatenate([v, ones], axis=0)
            cm = functools.reduce(jnp.maximum, [jnp.max(s, axis=0, keepdims=True) for s in sc])
            ref = cm if ci == 0 else jnp.maximum(m[j], cm)
        else:
            ref = bounds[j // ntiles]
        ps = [jnp.exp2(s - ref) for s in sc]
        pv = functools.reduce(jnp.add, [
            jnp.dot(v[:, h * MXU_TILE:(h + 1) * MXU_TILE], ps[h].astype(BF16),
                    preferred_element_type=F32) for h in range(halves)])
        if bounds is not None:
            part = functools.reduce(jnp.add, [
                jnp.sum(p.reshape(MXU_TILE // 8, 8, MXU_TILE), axis=0) for p in ps])
            lsum[j] = part if ci == 0 else lsum[j] + part
        if ci == 0:
            acc[j] = pv
        elif bounds is None:
            acc[j] = acc[j] * jnp.exp2(m[j] - ref) + pv
        else:
            acc[j] = acc[j] + pv
        m[j] = ref
    out = []
    for j in range(nsets):
        a = jnp.concatenate(acc[j * ntiles:(j + 1) * ntiles], axis=1)
        if bounds is None:
            out.append((a[0:dv], a[dv:dv + 1]))
        else:
            l8 = jnp.concatenate(lsum[j * ntiles:(j + 1) * ntiles], axis=1)
            out.append((a, jnp.sum(l8, axis=0, keepdims=True)))
    return out


def _attend(qps, bounds, k_at, v_ref, dv, kv_chunk, finish):
    safe = functools.reduce(jnp.logical_and, [b <= SAFE_BOUND for b in bounds])

    @pl.when(safe)
    def _():
        finish(_softmax_pv(qps, k_at, v_ref, dv, kv_chunk, bounds))

    @pl.when(jnp.logical_not(safe))
    def _():
        finish(_softmax_pv(qps, k_at, v_ref, dv, kv_chunk))


def _attn_params():
    return pltpu.CompilerParams(
        dimension_semantics=("parallel", "parallel", "parallel"), vmem_limit_bytes=VMEM_LIMIT)


def _attn_a_kernel(bnd_ref, lam_ref, q_ref, k_ref, v_ref, gate_ref, sg_ref, o_ref, *, lam_init):
    i, h = pl.program_id(0), pl.program_id(1)
    q = q_ref[0]
    tq = q.shape[1]
    z = jnp.zeros((A_QK_DIM, tq), BF16)
    qps = [jnp.concatenate([q[0:64], z], axis=0), jnp.concatenate([z, q[64:128]], axis=0)]

    def finish(res):
        (a0, l0), (a1, l1) = res
        lp = lam_ref[...]
        lam = (jnp.exp(jnp.sum(lp[0:1] * lp[1:2], axis=1, keepdims=True))
               - jnp.exp(jnp.sum(lp[2:3] * lp[3:4], axis=1, keepdims=True)) + lam_init)
        o = a0 / l0 - lam * (a1 / l1)
        ms = jnp.mean(o * o, axis=0, keepdims=True)
        o = o * lax.rsqrt(ms + RMS_EPS) * _rep(sg_ref, tq) * (1.0 - lam_init)
        o_ref[0] = (o * gate_ref[0]).astype(BF16)

    _attend(qps, [bnd_ref[i, 2 * h], bnd_ref[i, 2 * h + 1]],
            lambda r0: k_ref[0, 0, pl.ds(r0, MXU_TILE), :], v_ref, A_V_DIM, KV_CHUNK_A, finish)


def _attn_a(bnd, lam_p, qa, ka, va, gate, sg, lam_init):
    b = qa.shape[0]
    nt = SEQ // V_CHUNK
    return pl.pallas_call(
        functools.partial(_attn_a_kernel, lam_init=lam_init),
        grid=(b, A_HEADS, SEQ // TQ_A),
        in_specs=[
            pl.BlockSpec(memory_space=pltpu.SMEM),
            pl.BlockSpec((4, A_QK_DIM), lambda i, h, t: (0, 0)),
            pl.BlockSpec((1, 128, TQ_A), lambda i, h, t: (i, h, t)),
            pl.BlockSpec((1, 1, SEQ, K_PAD), lambda i, h, t: (i, h, 0, 0)),
            pl.BlockSpec((1, nt, A_V_DIM, V_CHUNK), lambda i, h, t: (i, 0, h, 0)),
            pl.BlockSpec((1, A_V_DIM, TQ_A), lambda i, h, t: (i, h, t)),
            pl.BlockSpec((A_V_DIM, LANES), lambda i, h, t: (0, 0)),
        ],
        out_specs=pl.BlockSpec((1, A_V_DIM, TQ_A), lambda i, h, t: (i, h, t)),
        out_shape=jax.ShapeDtypeStruct((b, A_WIDTH, SEQ), BF16),
        compiler_params=_attn_params(),
        name="attn_diff",
    )(bnd, lam_p, qa, ka, va, gate, sg)


def _attn_b_kernel(bnd_ref, q_ref, k_ref, v_ref, gate_ref, o_ref):
    i, h = pl.program_id(0), pl.program_id(1)
    first = h < B_GROUP
    q = q_ref[0]
    z = jnp.zeros_like(q)
    qp = jnp.concatenate([jnp.where(first, q, z), jnp.where(first, z, q)], axis=0)

    def finish(res):
        ((a, l),) = res
        o_ref[0] = (a / l * gate_ref[0]).astype(BF16)

    _attend([qp], [bnd_ref[i, 2 * A_HEADS + h]],
            lambda r0: k_ref[0, pl.ds(r0, MXU_TILE), :], v_ref, B_DIM, KV_CHUNK_BC, finish)


def _attn_b(bnd, qb, kb, vb, gate):
    b = qb.shape[0]
    nt = SEQ // V_CHUNK
    return pl.pallas_call(
        _attn_b_kernel,
        grid=(b, B_Q_HEADS, SEQ // TQ_B),
        in_specs=[
            pl.BlockSpec(memory_space=pltpu.SMEM),
            pl.BlockSpec((1, B_DIM, TQ_B), lambda i, h, t: (i, h, t)),
            pl.BlockSpec((1, SEQ, K_PAD), lambda i, h, t: (i, 0, 0)),
            pl.BlockSpec((1, nt, B_DIM, V_CHUNK), lambda i, h, t: (i, 0, h // B_GROUP, 0)),
            pl.BlockSpec((1, B_DIM, TQ_B), lambda i, h, t: (i, A_WIDTH // B_DIM + h, t)),
        ],
        out_specs=pl.BlockSpec((1, B_DIM, TQ_B), lambda i, h, t: (i, h, t)),
        out_shape=jax.ShapeDtypeStruct((b, B_WIDTH, SEQ), BF16),
        compiler_params=_attn_params(),
        name="attn_gqa",
    )(bnd, qb, kb, vb, gate)


def _attn_c_kernel(bnd_ref, q_ref, k_ref, v_ref, gate_ref, o_ref):
    i, h = pl.program_id(0), pl.program_id(1)
    q = q_ref[0]
    qp = jnp.concatenate([q, jnp.zeros((K_PAD - C_QK, q.shape[1]), BF16)], axis=0)

    def finish(res):
        ((a, l),) = res
        o_ref[0] = (a / l * gate_ref[0]).astype(BF16)

    _attend([qp], [bnd_ref[i, h]],
            lambda r0: k_ref[0, 0, pl.ds(r0, MXU_TILE), :], v_ref, C_V, KV_CHUNK_BC, finish)


def _attn_c(bnd, q, k, v, gate):
    b = q.shape[0]
    nt = SEQ // V_CHUNK
    return pl.pallas_call(
        _attn_c_kernel,
        grid=(b, C_HEADS, SEQ // TQ_C),
        in_specs=[
            pl.BlockSpec(memory_space=pltpu.SMEM),
            pl.BlockSpec((1, C_QK, TQ_C), lambda i, h, t: (i, h, t)),
            pl.BlockSpec((1, 1, SEQ, K_PAD), lambda i, h, t: (i, h, 0, 0)),
            pl.BlockSpec((1, nt, C_V, V_CHUNK), lambda i, h, t: (i, 0, h, 0)),
            pl.BlockSpec((1, C_V, TQ_C), lambda i, h, t: (i, h, t)),
        ],
        out_specs=pl.BlockSpec((1, C_V, TQ_C), lambda i, h, t: (i, h, t)),
        out_shape=jax.ShapeDtypeStruct((b, C_HEADS * C_V, SEQ), BF16),
        compiler_params=_attn_params(),
        name="attn_mla",
    )(bnd, q, k, v, gate)


def _out_body(og_refs, x_ref, w_ref, lg_ref, lb_ref, x_token_major):
    og = jnp.concatenate([r[0] for r in og_refs], axis=0) if len(og_refs) > 1 else og_refs[0][0]
    y = jnp.dot(w_ref[...], og, preferred_element_type=F32)
    x = x_ref[0].T if x_token_major else x_ref[0]
    z = ALPHA * x + y
    mu = jnp.mean(z, axis=0, keepdims=True)
    d = z - mu
    var = jnp.mean(d * d, axis=0, keepdims=True)
    width = y.shape[1]
    return d * lax.rsqrt(var + LN_EPS) * _rep(lg_ref, width) + _rep(lb_ref, width)


def _out_kernel(*refs, n_og):
    x_ref, w_ref, lg_ref, lb_ref, o_ref = refs[n_og:]
    width = TM // SUB_TILES
    for t in range(SUB_TILES):
        off = t * width
        og_views = [_tokens(r, 2, off, width) for r in refs[:n_og]]
        out = _out_body(og_views, _tokens(x_ref, 2, off, width), w_ref, lg_ref, lb_ref, False)
        o_ref[0, pl.ds(off, width), :] = out.T


def _tokens(ref, axis, off, width):
    idx = [slice(None)] * len(ref.shape)
    if isinstance(axis, tuple):
        idx[axis[0]] = pl.ds(off // V_CHUNK, 1)
        idx[axis[1]] = pl.ds(off % V_CHUNK, width)
    else:
        idx[axis] = pl.ds(off, width)
    return ref.at[tuple(idx)]


def _out_in_kernel(*refs, n_og, n_in, x_token_major, in_body, tab_pos, out_axes):
    x_ref, w_ref, lg_ref, lb_ref = refs[n_og:n_og + 4]
    in_params = refs[n_og + 4:n_og + 4 + n_in]
    xo_ref = refs[n_og + 4 + n_in]
    in_outs = refs[n_og + 5 + n_in:]
    width = TM // SUB_TILES
    outs = []
    for t in range(SUB_TILES):
        off = t * width
        og_views = [_tokens(r, 2, off, width) for r in refs[:n_og]]
        x_view = _tokens(x_ref, 1 if x_token_major else 2, off, width)
        out = _out_body(og_views, x_view, w_ref, lg_ref, lb_ref, x_token_major)
        xo_ref[0, :, pl.ds(off, width)] = out
        outs.append(out.astype(BF16))
    for t in range(SUB_TILES):
        off = t * width
        params = [_tokens(r, 1, off, width) if p == tab_pos else r
                  for p, r in enumerate(in_params)]
        views = [_tokens(r, ax, off, width) for r, ax in zip(in_outs, out_axes)]
        in_body(outs[t], *params, *views)


def _out_specs(ogs, x_token_major):
    d = D_MODEL
    x_spec = (pl.BlockSpec((1, TM, d), lambda i, j: (i, j, 0)) if x_token_major
              else pl.BlockSpec((1, d, TM), _tok))
    return [pl.BlockSpec((1, og.shape[1], TM), _tok) for og in ogs] + [
        x_spec, _resident((d, d)), _resident((d, LANES)), _resident((d, LANES))]


def _out_proj_last(ogs, xt, w_t, lg, lb):
    b = xt.shape[0]
    return pl.pallas_call(
        functools.partial(_out_kernel, n_og=len(ogs)),
        grid=(b, SEQ // TM),
        in_specs=_out_specs(ogs, False),
        out_specs=pl.BlockSpec((1, TM, D_MODEL), lambda i, j: (i, j, 0)),
        out_shape=jax.ShapeDtypeStruct((b, SEQ, D_MODEL), F32),
        compiler_params=_proj_params(),
        name="out_proj_ln",
    )(*ogs, xt, w_t, lg, lb)


def _out_in(ogs, xt, w_t, lg, lb, next_even, in_args, x_token_major):
    b = xt.shape[0]
    if next_even:
        in_body, (in_specs, out_specs, out_shape) = _even_in_body, _even_in_specs(b)
        tab_pos, out_axes, name = EVEN_TAB_POS, EVEN_OUT_TOKEN_AXES, "out_even_in_proj"
    else:
        in_body, (in_specs, out_specs, out_shape) = _odd_in_body, _odd_in_specs(b)
        tab_pos, out_axes, name = ODD_TAB_POS, ODD_OUT_TOKEN_AXES, "out_odd_in_proj"
    return pl.pallas_call(
        functools.partial(_out_in_kernel, n_og=len(ogs), n_in=len(in_args),
                          x_token_major=x_token_major, in_body=in_body, tab_pos=tab_pos,
                          out_axes=out_axes),
        grid=(b, SEQ // TM),
        in_specs=_out_specs(ogs, x_token_major) + in_specs,
        out_specs=[pl.BlockSpec((1, D_MODEL, TM), _tok)] + out_specs,
        out_shape=[jax.ShapeDtypeStruct((b, D_MODEL, SEQ), F32)] + out_shape,
        compiler_params=_proj_params(),
        name=name,
    )(*ogs, xt, w_t, lg, lb, *in_args)


def _col(v):
    v = v.astype(F32)
    return jnp.broadcast_to(v[:, None], (v.shape[0], LANES))


def _score_bounds(nq, kmax2):
    return jnp.sqrt(jnp.max(nq, axis=-1) * kmax2) * BOUND_SLACK


def _angles_t(pos, dims, theta):
    inv = theta ** (-jnp.arange(0, dims, 2, dtype=F32) / dims)
    ang = pos.astype(F32)[:, None] * inv[None, :]
    return jnp.cos(ang).T, jnp.sin(ang).T


@jax.jit
def _forward(x, ev_w_in, ev_w_out, ev_lam, ev_subln, ev_qnorm, ev_knorm, ev_ln_g, ev_ln_b,
             od_w_in, od_qnorm, od_kvnorm, od_w_qb, od_w_kvb, od_w_out, od_ln_g, od_ln_b):
    s = x.shape[1]
    pos = jnp.arange(s, dtype=jnp.int32)
    row = jnp.repeat(jnp.arange(s // GRID_W, dtype=jnp.int32), GRID_W)
    col = jnp.tile(jnp.arange(GRID_W, dtype=jnp.int32), s // GRID_W)
    tab_ev = jnp.concatenate(
        _angles_t(pos, A_ROT, ROPE_THETA) + _angles_t(row, B_DIM // 2, AXIAL_THETA)
        + _angles_t(col, B_DIM // 2, AXIAL_THETA), axis=0)
    tab_od = jnp.concatenate(_angles_t(pos, C_ROPE, ROPE_THETA), axis=0)

    def even_args(i):
        return (ev_w_in[i].T.astype(BF16), tab_ev, _col(ev_qnorm[i]), _col(ev_knorm[i]))

    def odd_args(i):
        return (od_w_in[i].T.astype(BF16), od_w_qb[i].T.astype(BF16),
                od_w_kvb[i].T.astype(BF16), tab_od, _col(od_qnorm[i]), _col(od_kvnorm[i]))

    xt = x
    proj = _even_in(x, *even_args(0))
    for layer in range(DEPTH):
        i = layer // 2
        if layer % 2 == 0:
            qa, ka, va, qb, kb, vb, gate, nq, nk = proj
            kmax = jnp.max(nk, axis=-1)
            kmax = jnp.concatenate(
                [kmax[:, :2 * A_HEADS],
                 jnp.repeat(kmax[:, 2 * A_HEADS:2 * A_HEADS + B_KV_HEADS], B_GROUP, axis=1)],
                axis=1)
            bnd = _score_bounds(nq, kmax)
            lam_init = 0.8 - 0.6 * math.exp(-0.3 * layer)
            ogs = [_attn_a(bnd, ev_lam[i].astype(F32), qa, ka, va, gate, _col(ev_subln[i]),
                           lam_init),
                   _attn_b(bnd, qb, kb, vb, gate)]
            out_args = (ev_w_out[i].T.astype(BF16), _col(ev_ln_g[i]), _col(ev_ln_b[i]))
        else:
            q, k, v, gate, nq, nk = proj
            ogs = [_attn_c(_score_bounds(nq, jnp.max(nk, axis=-1)), q, k, v, gate)]
            out_args = (od_w_out[i].T.astype(BF16), _col(od_ln_g[i]), _col(od_ln_b[i]))
        if layer == DEPTH - 1:
            return _out_proj_last(ogs, xt, *out_args)
        nxt = (layer + 1) // 2
        if layer % 2 == 0:
            xt, *proj = _out_in(ogs, xt, *out_args, False, odd_args(nxt), layer == 0)
        else:
            xt, *proj = _out_in(ogs, xt, *out_args, True, even_args(nxt), False)


def kernel(x, ev_w_in, ev_w_out, ev_lam, ev_subln, ev_qnorm, ev_knorm, ev_ln_g, ev_ln_b,
           od_w_in, od_qnorm, od_kvnorm, od_w_qb, od_w_kvb, od_w_out, od_ln_g, od_ln_b):
    return _forward(x, ev_w_in, ev_w_out, ev_lam, ev_subln, ev_qnorm, ev_knorm, ev_ln_g,
                    ev_ln_b, od_w_in, od_qnorm, od_kvnorm, od_w_qb, od_w_kvb, od_w_out,
                    od_ln_g, od_ln_b)
```

```python
import functools
import math

import jax
import jax.numpy as jnp
from jax import lax
from jax.experimental import pallas as pl
from jax.experimental.pallas import tpu as pltpu

F32 = jnp.float32
BF16 = jnp.bfloat16

D_MODEL = 1024
SEQ = 4096
DEPTH = 4
GRID_W = 64
ROPE_THETA = 500000.0
AXIAL_THETA = 10000.0
LN_EPS = 1e-5
RMS_EPS = 1e-6

A_HEADS = 4
A_QK_DIM = 64
A_V_DIM = 128
A_WIDTH = 512
A_ROT = 16
B_Q_HEADS = 8
B_KV_HEADS = 2
B_GROUP = 4
B_DIM = 64
B_WIDTH = 512
EV_IN = 3328

C_HEADS = 16
C_NOPE = 64
C_ROPE = 32
C_V = 64
C_Q_LORA = 256
C_KV_LORA = 128
C_QK = C_NOPE + C_ROPE
OD_IN = 1440

ALPHA = (2 * DEPTH) ** 0.25
LOG2E = 1.4426950408889634

QSCALE_AB = A_QK_DIM ** -0.5 * LOG2E
QSCALE_C = C_QK ** -0.5 * LOG2E

LANES = 128
MXU_TILE = 256
LOOKAHEAD_ITEMS = {1: 2, 2: 3, 4: 8}
MAX_LOCKSTEP = 4
ONES_ROWS = 16
K_PAD = 128
NORM_ROWS = 16
SAFE_BOUND = 60.0
BOUND_SLACK = 1.02
TM = 1024
V_CHUNK = 512
SUB_TILES = 4
EVEN_TAB_POS, EVEN_OUT_TOKEN_AXES = 1, (2, 2, (1, 3), 2, 1, (1, 3), 2, 2, 2)
ODD_TAB_POS, ODD_OUT_TOKEN_AXES = 3, (2, 2, (1, 3), 2, 2, 2)
KV_CHUNK_A = 512
KV_CHUNK_BC = 256
TQ_A = 1024
TQ_B = 4096
TQ_C = 4096
VMEM_LIMIT = 56 * 1024 * 1024


def _rep(ref, n):
    a = ref[...]
    return jnp.concatenate([a] * (n // LANES), axis=1)


def _rot(x1, x2, cos, sin):
    return x1 * cos - x2 * sin, x2 * cos + x1 * sin


def _silu(x):
    return x * jax.nn.sigmoid(x)


def _sumsq(x):
    return jnp.sum(x * x, axis=0, keepdims=True)


def _tok(i, j):
    return (i, 0, j)


def _const2(i, j):
    return (0, 0)


def _resident(shape):
    return pl.BlockSpec(shape, _const2, pipeline_mode=pl.Buffered(1))


def _proj_params():
    return pltpu.CompilerParams(
        dimension_semantics=("parallel", "parallel"), vmem_limit_bytes=VMEM_LIMIT)


def _even_in_body(xb, w_ref, tab_ref, qn_ref, kn_ref,
                  qa_ref, ka_ref, va_ref, qb_ref, kb_ref, vb_ref, g_ref, nq_ref, nk_ref,
                  *, x_token_major=False):
    contract = (((1,), (1 if x_token_major else 0,)), ((), ()))

    def proj(r0, r1):
        return lax.dot_general(w_ref[r0:r1, :], xb, contract, preferred_element_type=F32)

    cos_a, sin_a = tab_ref[0:8, :], tab_ref[8:16, :]
    cos_r, sin_r = tab_ref[16:32, :], tab_ref[32:48, :]
    cos_c, sin_c = tab_ref[48:64, :], tab_ref[64:80, :]

    def rope_a(h):
        outs = []
        for hc in range(2 * A_HEADS):
            b = hc * A_QK_DIM
            r1, r2 = _rot(h[b:b + 8], h[b + 8:b + 16], cos_a, sin_a)
            outs.append(jnp.concatenate([r1, r2, h[b + 16:b + 64]], axis=0))
        return outs

    g_ref[0] = _silu(proj(2304, 3328)).astype(BF16)
    qs = [x * QSCALE_AB for x in rope_a(proj(0, 512))]
    qa_ref[0] = jnp.concatenate(qs, axis=0).astype(BF16)
    ks = rope_a(proj(512, 1024))
    nq = [_sumsq(x) for x in qs]
    nk = [_sumsq(x) for x in ks]
    for h in range(A_HEADS):
        kt = jnp.concatenate([ks[2 * h], ks[2 * h + 1]], axis=0)
        ka_ref[0, h] = kt.T.astype(BF16)

    def norm_axial(h, g):
        ms = jnp.mean(h * h, axis=0, keepdims=True)
        y = h * lax.rsqrt(ms + RMS_EPS) * g
        a1, a2 = _rot(y[0:16], y[16:32], cos_r, sin_r)
        b1, b2 = _rot(y[32:48], y[48:64], cos_c, sin_c)
        return jnp.concatenate([a1, a2, b1, b2], axis=0)

    width = xb.shape[0 if x_token_major else 1]
    qn = _rep(qn_ref, width)
    kn = _rep(kn_ref, width)
    hq = proj(1536, 2048)
    qs = [norm_axial(hq[h * 64:(h + 1) * 64], qn) * QSCALE_AB for h in range(B_Q_HEADS)]
    qb_ref[0] = jnp.concatenate(qs, axis=0).astype(BF16)
    hkv = proj(2048, 2304)
    ks = [norm_axial(hkv[h * 64:(h + 1) * 64], kn) for h in range(B_KV_HEADS)]
    kb_ref[0] = jnp.concatenate(ks, axis=0).T.astype(BF16)
    nq_ref[0] = jnp.concatenate(nq + [_sumsq(x) for x in qs], axis=0)
    nk = nk + [_sumsq(x) for x in ks]
    nk_ref[0] = jnp.concatenate(nk + [jnp.zeros_like(nk[0])] * (NORM_ROWS - len(nk)), axis=0)
    vb_ref[0, 0] = hkv[128:256].astype(BF16)
    va_ref[0, 0] = proj(1024, 1536).astype(BF16)


def _even_in_kernel(x_ref, *refs):
    n_in = len(refs) - len(EVEN_OUT_TOKEN_AXES)
    width = TM // SUB_TILES
    for t in range(SUB_TILES):
        off = t * width
        params = [_tokens(r, 1, off, width) if p == EVEN_TAB_POS else r
                  for p, r in enumerate(refs[:n_in])]
        views = [_tokens(r, ax, off, width) for r, ax in zip(refs[n_in:], EVEN_OUT_TOKEN_AXES)]
        _even_in_body(x_ref[0, pl.ds(off, width), :].astype(BF16), *params, *views,
                      x_token_major=True)


def _even_in_specs(b):
    d, s = D_MODEL, SEQ
    nt = s // TM
    tok = _tok
    chunk = lambda i, j: (i, j, 0, 0)
    return (
        [
            _resident((EV_IN, d)),
            pl.BlockSpec((80, TM), lambda i, j: (0, j)),
            _resident((B_DIM, LANES)),
            _resident((B_DIM, LANES)),
        ],
        [
            pl.BlockSpec((1, 512, TM), tok),
            pl.BlockSpec((1, A_HEADS, TM, K_PAD), lambda i, j: (i, 0, j, 0)),
            pl.BlockSpec((1, TM // V_CHUNK, 512, V_CHUNK), chunk),
            pl.BlockSpec((1, 512, TM), tok),
            pl.BlockSpec((1, TM, K_PAD), lambda i, j: (i, j, 0)),
            pl.BlockSpec((1, TM // V_CHUNK, 128, V_CHUNK), chunk),
            pl.BlockSpec((1, 1024, TM), tok),
            pl.BlockSpec((1, NORM_ROWS, TM), tok),
            pl.BlockSpec((1, NORM_ROWS, TM), tok),
        ],
        [
            jax.ShapeDtypeStruct((b, 512, s), BF16),
            jax.ShapeDtypeStruct((b, A_HEADS, s, K_PAD), BF16),
            jax.ShapeDtypeStruct((b, s // V_CHUNK, 512, V_CHUNK), BF16),
            jax.ShapeDtypeStruct((b, 512, s), BF16),
            jax.ShapeDtypeStruct((b, s, K_PAD), BF16),
            jax.ShapeDtypeStruct((b, s // V_CHUNK, 128, V_CHUNK), BF16),
            jax.ShapeDtypeStruct((b, 1024, s), BF16),
            jax.ShapeDtypeStruct((b, NORM_ROWS, s), F32),
            jax.ShapeDtypeStruct((b, NORM_ROWS, s), F32),
        ],
    )


def _even_in(x, w_t, tab, qn, kn):
    b = x.shape[0]
    in_specs, out_specs, out_shape = _even_in_specs(b)
    return pl.pallas_call(
        _even_in_kernel,
        grid=(b, SEQ // TM),
        in_specs=[pl.BlockSpec((1, TM, D_MODEL), lambda i, j: (i, j, 0))] + in_specs,
        out_specs=out_specs,
        out_shape=out_shape,
        compiler_params=_proj_params(),
        name="even_in_proj",
    )(x, w_t, tab, qn, kn)


def _odd_in_body(xb, w_ref, wq_ref, wkv_ref, tab_ref, qn_ref, kvn_ref,
                 q_ref, k_ref, v_ref, g_ref, nq_ref, nk_ref):
    cos, sin = tab_ref[0:16, :], tab_ref[16:32, :]

    def rms(h, g):
        ms = jnp.mean(h * h, axis=0, keepdims=True)
        return h * lax.rsqrt(ms + RMS_EPS) * g

    g_ref[0] = _silu(jnp.dot(w_ref[416:1440, :], xb, preferred_element_type=F32)).astype(BF16)
    lat = jnp.dot(w_ref[0:416, :], xb, preferred_element_type=F32)
    width = xb.shape[1]
    cqn = rms(lat[0:256], _rep(qn_ref, width)).astype(BF16)
    q = jnp.dot(wq_ref[...], cqn, preferred_element_type=F32)
    qs = []
    for h in range(C_HEADS):
        b = h * C_QK
        r1, r2 = _rot(q[b + 64:b + 80], q[b + 80:b + 96], cos, sin)
        qs.append(jnp.concatenate([q[b:b + 64], r1, r2], axis=0) * QSCALE_C)
    q_ref[0] = jnp.concatenate(qs, axis=0).astype(BF16)
    nq_ref[0] = jnp.concatenate([_sumsq(x) for x in qs], axis=0)

    ckvn = rms(lat[256:384], _rep(kvn_ref, width)).astype(BF16)
    kv = jnp.dot(wkv_ref[...], ckvn, preferred_element_type=F32)
    r1, r2 = _rot(lat[384:400], lat[400:416], cos, sin)
    zpad = jnp.zeros((K_PAD - C_QK, width), F32)
    nkr = _sumsq(r1) + _sumsq(r2)
    nk = []
    for h in range(C_HEADS):
        kn = kv[h * 128:h * 128 + 64]
        nk.append(_sumsq(kn) + nkr)
        kt = jnp.concatenate([kn, r1, r2, zpad], axis=0)
        k_ref[0, h] = kt.T.astype(BF16)
    nk_ref[0] = jnp.concatenate(nk, axis=0)
    v_ref[0, 0] = jnp.concatenate(
        [kv[h * 128 + 64:h * 128 + 128] for h in range(C_HEADS)], axis=0).astype(BF16)


def _odd_in_specs(b):
    d, s = D_MODEL, SEQ
    nt = s // TM
    tok = _tok
    return (
        [
            _resident((OD_IN, d)),
            _resident((C_HEADS * C_QK, C_Q_LORA)),
            _resident((C_HEADS * 128, C_KV_LORA)),
            pl.BlockSpec((32, TM), lambda i, j: (0, j)),
            _resident((C_Q_LORA, LANES)),
            _resident((C_KV_LORA, LANES)),
        ],
        [
            pl.BlockSpec((1, C_HEADS * C_QK, TM), tok),
            pl.BlockSpec((1, C_HEADS, TM, K_PAD), lambda i, j: (i, 0, j, 0)),
            pl.BlockSpec((1, TM // V_CHUNK, 1024, V_CHUNK), lambda i, j: (i, j, 0, 0)),
            pl.BlockSpec((1, 1024, TM), tok),
            pl.BlockSpec((1, NORM_ROWS, TM), tok),
            pl.BlockSpec((1, NORM_ROWS, TM), tok),
        ],
        [
            jax.ShapeDtypeStruct((b, C_HEADS * C_QK, s), BF16),
            jax.ShapeDtypeStruct((b, C_HEADS, s, K_PAD), BF16),
            jax.ShapeDtypeStruct((b, s // V_CHUNK, 1024, V_CHUNK), BF16),
            jax.ShapeDtypeStruct((b, 1024, s), BF16),
            jax.ShapeDtypeStruct((b, NORM_ROWS, s), F32),
            jax.ShapeDtypeStruct((b, NORM_ROWS, s), F32),
        ],
    )


def _softmax_pv(qps, k_at, v_ref, dv, kv_chunk, bounds=None):
    nsets = len(qps)
    ntiles = qps[0].shape[1] // MXU_TILE
    qps = [qp[:, t * MXU_TILE:(t + 1) * MXU_TILE] for qp in qps for t in range(ntiles)]
    n = len(qps)
    nchunks = SEQ // kv_chunk
    halves = kv_chunk // MXU_TILE
    ones = jnp.ones((ONES_ROWS, kv_chunk), BF16)

    def scores(ci, j):
        return [jnp.dot(k_at(ci * kv_chunk + h * MXU_TILE), qps[j], preferred_element_type=F32)
                for h in range(halves)]

    m = [None] * n
    acc = [None] * n
    lsum = [None] * n
    items = [(ci, j) for g in range(0, n, MAX_LOCKSTEP) for ci in range(nchunks)
             for j in range(g, min(g + MAX_LOCKSTEP, n))]
    lookahead = LOOKAHEAD_ITEMS[min(n, MAX_LOCKSTEP)]
    pending = [scores(*it) for it in items[:lookahead]]
    for idx, (ci, j) in enumerate(items):
        if idx + lookahead < len(items):
            pending.append(scores(*items[idx + lookahead]))
        sc = pending.pop(0)
        r0 = ci * kv_chunk
        v = v_ref[0, r0 // V_CHUNK, :, pl.ds(r0 % V_CHUNK, kv_chunk)]
        if bounds is None:
            v = jnp.concatenate([v, ones], axis=0)
            cm = functools.reduce(jnp.maximum, [jnp.max(s, axis=0, keepdims=True) for s in sc])
            ref = cm if ci == 0 else jnp.maximum(m[j], cm)
        else:
            ref = bounds[j // ntiles]
        ps = [jnp.exp2(s - ref) for s in sc]
        pv = functools.reduce(jnp.add, [
            jnp.dot(v[:, h * MXU_TILE:(h + 1) * MXU_TILE], ps[h].astype(BF16),
                    preferred_element_type=F32) for h in range(halves)])
        if bounds is not None:
            part = functools.reduce(jnp.add, [
                jnp.sum(p.reshape(MXU_TILE // 8, 8, MXU_TILE), axis=0) for p in ps])
            lsum[j] = part if ci == 0 else lsum[j] + part
        if ci == 0:
            acc[j] = pv
        elif bounds is None:
            acc[j] = acc[j] * jnp.exp2(m[j] - ref) + pv
        else:
            acc[j] = acc[j] + pv
        m[j] = ref
    out = []
    for j in range(nsets):
        a = jnp.concatenate(acc[j * ntiles:(j + 1) * ntiles], axis=1)
        if bounds is None:
            out.append((a[0:dv], a[dv:dv + 1]))
        else:
            l8 = jnp.concatenate(lsum[j * ntiles:(j + 1) * ntiles], axis=1)
            out.append((a, jnp.sum(l8, axis=0, keepdims=True)))
    return out


def _attend(qps, bounds, k_at, v_ref, dv, kv_chunk, finish):
    safe = functools.reduce(jnp.logical_and, [b <= SAFE_BOUND for b in bounds])

    @pl.when(safe)
    def _():
        finish(_softmax_pv(qps, k_at, v_ref, dv, kv_chunk, bounds))

    @pl.when(jnp.logical_not(safe))
    def _():
        finish(_softmax_pv(qps, k_at, v_ref, dv, kv_chunk))


def _attn_params():
    return pltpu.CompilerParams(
        dimension_semantics=("parallel", "parallel", "parallel"), vmem_limit_bytes=VMEM_LIMIT)


def _attn_a_kernel(bnd_ref, lam_ref, q_ref, k_ref, v_ref, gate_ref, sg_ref, o_ref, *, lam_init):
    i, h = pl.program_id(0), pl.program_id(1)
    q = q_ref[0]
    tq = q.shape[1]
    z = jnp.zeros((A_QK_DIM, tq), BF16)
    qps = [jnp.concatenate([q[0:64], z], axis=0), jnp.concatenate([z, q[64:128]], axis=0)]

    def finish(res):
        (a0, l0), (a1, l1) = res
        lp = lam_ref[...]
        lam = (jnp.exp(jnp.sum(lp[0:1] * lp[1:2], axis=1, keepdims=True))
               - jnp.exp(jnp.sum(lp[2:3] * lp[3:4], axis=1, keepdims=True)) + lam_init)
        o = a0 / l0 - lam * (a1 / l1)
        ms = jnp.mean(o * o, axis=0, keepdims=True)
        o = o * lax.rsqrt(ms + RMS_EPS) * _rep(sg_ref, tq) * (1.0 - lam_init)
        o_ref[0] = (o * gate_ref[0]).astype(BF16)

    _attend(qps, [bnd_ref[i, 2 * h], bnd_ref[i, 2 * h + 1]],
            lambda r0: k_ref[0, 0, pl.ds(r0, MXU_TILE), :], v_ref, A_V_DIM, KV_CHUNK_A, finish)


def _attn_a(bnd, lam_p, qa, ka, va, gate, sg, lam_init):
    b = qa.shape[0]
    nt = SEQ // V_CHUNK
    return pl.pallas_call(
        functools.partial(_attn_a_kernel, lam_init=lam_init),
        grid=(b, A_HEADS, SEQ // TQ_A),
        in_specs=[
            pl.BlockSpec(memory_space=pltpu.SMEM),
            pl.BlockSpec((4, A_QK_DIM), lambda i, h, t: (0, 0)),
            pl.BlockSpec((1, 128, TQ_A), lambda i, h, t: (i, h, t)),
            pl.BlockSpec((1, 1, SEQ, K_PAD), lambda i, h, t: (i, h, 0, 0)),
            pl.BlockSpec((1, nt, A_V_DIM, V_CHUNK), lambda i, h, t: (i, 0, h, 0)),
            pl.BlockSpec((1, A_V_DIM, TQ_A), lambda i, h, t: (i, h, t)),
            pl.BlockSpec((A_V_DIM, LANES), lambda i, h, t: (0, 0)),
        ],
        out_specs=pl.BlockSpec((1, A_V_DIM, TQ_A), lambda i, h, t: (i, h, t)),
        out_shape=jax.ShapeDtypeStruct((b, A_WIDTH, SEQ), BF16),
        compiler_params=_attn_params(),
        name="attn_diff",
    )(bnd, lam_p, qa, ka, va, gate, sg)


def _attn_b_kernel(bnd_ref, q_ref, k_ref, v_ref, gate_ref, o_ref):
    i, h = pl.program_id(0), pl.program_id(1)
    first = h < B_GROUP
    q = q_ref[0]
    z = jnp.zeros_like(q)
    qp = jnp.concatenate([jnp.where(first, q, z), jnp.where(first, z, q)], axis=0)

    def finish(res):
        ((a, l),) = res
        o_ref[0] = (a / l * gate_ref[0]).astype(BF16)

    _attend([qp], [bnd_ref[i, 2 * A_HEADS + h]],
            lambda r0: k_ref[0, pl.ds(r0, MXU_TILE), :], v_ref, B_DIM, KV_CHUNK_BC, finish)


def _attn_b(bnd, qb, kb, vb, gate):
    b = qb.shape[0]
    nt = SEQ // V_CHUNK
    return pl.pallas_call(
        _attn_b_kernel,
        grid=(b, B_Q_HEADS, SEQ // TQ_B),
        in_specs=[
            pl.BlockSpec(memory_space=pltpu.SMEM),
            pl.BlockSpec((1, B_DIM, TQ_B), lambda i, h, t: (i, h, t)),
            pl.BlockSpec((1, SEQ, K_PAD), lambda i, h, t: (i, 0, 0)),
            pl.BlockSpec((1, nt, B_DIM, V_CHUNK), lambda i, h, t: (i, 0, h // B_GROUP, 0)),
            pl.BlockSpec((1, B_DIM, TQ_B), lambda i, h, t: (i, A_WIDTH // B_DIM + h, t)),
        ],
        out_specs=pl.BlockSpec((1, B_DIM, TQ_B), lambda i, h, t: (i, h, t)),
        out_shape=jax.ShapeDtypeStruct((b, B_WIDTH, SEQ), BF16),
        compiler_params=_attn_params(),
        name="attn_gqa",
    )(bnd, qb, kb, vb, gate)


def _attn_c_kernel(bnd_ref, q_ref, k_ref, v_ref, gate_ref, o_ref):
    i, h = pl.program_id(0), pl.program_id(1)
    q = q_ref[0]
    qp = jnp.concatenate([q, jnp.zeros((K_PAD - C_QK, q.shape[1]), BF16)], axis=0)

    def finish(res):
        ((a, l),) = res
        o_ref[0] = (a / l * gate_ref[0]).astype(BF16)

    _attend([qp], [bnd_ref[i, h]],
            lambda r0: k_ref[0, 0, pl.ds(r0, MXU_TILE), :], v_ref, C_V, KV_CHUNK_BC, finish)


def _attn_c(bnd, q, k, v, gate):
    b = q.shape[0]
    nt = SEQ // V_CHUNK
    return pl.pallas_call(
        _attn_c_kernel,
        grid=(b, C_HEADS, SEQ // TQ_C),
        in_specs=[
            pl.BlockSpec(memory_space=pltpu.SMEM),
            pl.BlockSpec((1, C_QK, TQ_C), lambda i, h, t: (i, h, t)),
            pl.BlockSpec((1, 1, SEQ, K_PAD), lambda i, h, t: (i, h, 0, 0)),
            pl.BlockSpec((1, nt, C_V, V_CHUNK), lambda i, h, t: (i, 0, h, 0)),
            pl.BlockSpec((1, C_V, TQ_C), lambda i, h, t: (i, h, t)),
        ],
        out_specs=pl.BlockSpec((1, C_V, TQ_C), lambda i, h, t: (i, h, t)),
        out_shape=jax.ShapeDtypeStruct((b, C_HEADS * C_V, SEQ), BF16),
        compiler_params=_attn_params(),
        name="attn_mla",
    )(bnd, q, k, v, gate)


def _out_body(og_refs, x_ref, w_ref, lg_ref, lb_ref, x_token_major):
    og = jnp.concatenate([r[0] for r in og_refs], axis=0) if len(og_refs) > 1 else og_refs[0][0]
    y = jnp.dot(w_ref[...], og, preferred_element_type=F32)
    x = x_ref[0].T if x_token_major else x_ref[0]
    z = ALPHA * x + y
    mu = jnp.mean(z, axis=0, keepdims=True)
    d = z - mu
    var = jnp.mean(d * d, axis=0, keepdims=True)
    width = y.shape[1]
    return d * lax.rsqrt(var + LN_EPS) * _rep(lg_ref, width) + _rep(lb_ref, width)


def _out_kernel(*refs, n_og):
    x_ref, w_ref, lg_ref, lb_ref, o_ref = refs[n_og:]
    width = TM // SUB_TILES
    for t in range(SUB_TILES):
        off = t * width
        og_views = [_tokens(r, 2, off, width) for r in refs[:n_og]]
        out = _out_body(og_views, _tokens(x_ref, 2, off, width), w_ref, lg_ref, lb_ref, False)
        o_ref[0, pl.ds(off, width), :] = out.T


def _tokens(ref, axis, off, width):
    idx = [slice(None)] * len(ref.shape)
    if isinstance(axis, tuple):
        idx[axis[0]] = pl.ds(off // V_CHUNK, 1)
        idx[axis[1]] = pl.ds(off % V_CHUNK, width)
    else:
        idx[axis] = pl.ds(off, width)
    return ref.at[tuple(idx)]


def _out_in_kernel(*refs, n_og, n_in, x_token_major, in_body, tab_pos, out_axes):
    x_ref, w_ref, lg_ref, lb_ref = refs[n_og:n_og + 4]
    in_params = refs[n_og + 4:n_og + 4 + n_in]
    xo_ref = refs[n_og + 4 + n_in]
    in_outs = refs[n_og + 5 + n_in:]
    width = TM // SUB_TILES
    outs = []
    for t in range(SUB_TILES):
        off = t * width
        og_views = [_tokens(r, 2, off, width) for r in refs[:n_og]]
        x_view = _tokens(x_ref, 1 if x_token_major else 2, off, width)
        out = _out_body(og_views, x_view, w_ref, lg_ref, lb_ref, x_token_major)
        xo_ref[0, :, pl.ds(off, width)] = out
        outs.append(out.astype(BF16))
    for t in range(SUB_TILES):
        off = t * width
        params = [_tokens(r, 1, off, width) if p == tab_pos else r
                  for p, r in enumerate(in_params)]
        views = [_tokens(r, ax, off, width) for r, ax in zip(in_outs, out_axes)]
        in_body(outs[t], *params, *views)


def _out_specs(ogs, x_token_major):
    d = D_MODEL
    x_spec = (pl.BlockSpec((1, TM, d), lambda i, j: (i, j, 0)) if x_token_major
              else pl.BlockSpec((1, d, TM), _tok))
    return [pl.BlockSpec((1, og.shape[1], TM), _tok) for og in ogs] + [
        x_spec, _resident((d, d)), _resident((d, LANES)), _resident((d, LANES))]


def _out_proj_last(ogs, xt, w_t, lg, lb):
    b = xt.shape[0]
    return pl.pallas_call(
        functools.partial(_out_kernel, n_og=len(ogs)),
        grid=(b, SEQ // TM),
        in_specs=_out_specs(ogs, False),
        out_specs=pl.BlockSpec((1, TM, D_MODEL), lambda i, j: (i, j, 0)),
        out_shape=jax.ShapeDtypeStruct((b, SEQ, D_MODEL), F32),
        compiler_params=_proj_params(),
        name="out_proj_ln",
    )(*ogs, xt, w_t, lg, lb)


def _out_in(ogs, xt, w_t, lg, lb, next_even, in_args, x_token_major):
    b = xt.shape[0]
    if next_even:
        in_body, (in_specs, out_specs, out_shape) = _even_in_body, _even_in_specs(b)
        tab_pos, out_axes, name = EVEN_TAB_POS, EVEN_OUT_TOKEN_AXES, "out_even_in_proj"
    else:
        in_body, (in_specs, out_specs, out_shape) = _odd_in_body, _odd_in_specs(b)
        tab_pos, out_axes, name = ODD_TAB_POS, ODD_OUT_TOKEN_AXES, "out_odd_in_proj"
    return pl.pallas_call(
        functools.partial(_out_in_kernel, n_og=len(ogs), n_in=len(in_args),
                          x_token_major=x_token_major, in_body=in_body, tab_pos=tab_pos,
                          out_axes=out_axes),
        grid=(b, SEQ // TM),
        in_specs=_out_specs(ogs, x_token_major) + in_specs,
        out_specs=[pl.BlockSpec((1, D_MODEL, TM), _tok)] + out_specs,
        out_shape=[jax.ShapeDtypeStruct((b, D_MODEL, SEQ), F32)] + out_shape,
        compiler_params=_proj_params(),
        name=name,
    )(*ogs, xt, w_t, lg, lb, *in_args)


def _col(v):
    v = v.astype(F32)
    return jnp.broadcast_to(v[:, None], (v.shape[0], LANES))


def _score_bounds(nq, kmax2):
    return jnp.sqrt(jnp.max(nq, axis=-1) * kmax2) * BOUND_SLACK


def _angles_t(pos, dims, theta):
    inv = theta ** (-jnp.arange(0, dims, 2, dtype=F32) / dims)
    ang = pos.astype(F32)[:, None] * inv[None, :]
    return jnp.cos(ang).T, jnp.sin(ang).T


@jax.jit
def _forward(x, ev_w_in, ev_w_out, ev_lam, ev_subln, ev_qnorm, ev_knorm, ev_ln_g, ev_ln_b,
             od_w_in, od_qnorm, od_kvnorm, od_w_qb, od_w_kvb, od_w_out, od_ln_g, od_ln_b):
    s = x.shape[1]
    pos = jnp.arange(s, dtype=jnp.int32)
    row = jnp.repeat(jnp.arange(s // GRID_W, dtype=jnp.int32), GRID_W)
    col = jnp.tile(jnp.arange(GRID_W, dtype=jnp.int32), s // GRID_W)
    tab_ev = jnp.concatenate(
        _angles_t(pos, A_ROT, ROPE_THETA) + _angles_t(row, B_DIM // 2, AXIAL_THETA)
        + _angles_t(col, B_DIM // 2, AXIAL_THETA), axis=0)
    tab_od = jnp.concatenate(_angles_t(pos, C_ROPE, ROPE_THETA), axis=0)

    def even_args(i):
        return (ev_w_in[i].T.astype(BF16), tab_ev, _col(ev_qnorm[i]), _col(ev_knorm[i]))

    def odd_args(i):
        return (od_w_in[i].T.astype(BF16), od_w_qb[i].T.astype(BF16),
                od_w_kvb[i].T.astype(BF16), tab_od, _col(od_qnorm[i]), _col(od_kvnorm[i]))

    xt = x
    proj = _even_in(x, *even_args(0))
    for layer in range(DEPTH):
        i = layer // 2
        if layer % 2 == 0:
            qa, ka, va, qb, kb, vb, gate, nq, nk = proj
            kmax = jnp.max(nk, axis=-1)
            kmax = jnp.concatenate(
                [kmax[:, :2 * A_HEADS],
                 jnp.repeat(kmax[:, 2 * A_HEADS:2 * A_HEADS + B_KV_HEADS], B_GROUP, axis=1)],
                axis=1)
            bnd = _score_bounds(nq, kmax)
            lam_init = 0.8 - 0.6 * math.exp(-0.3 * layer)
            ogs = [_attn_a(bnd, ev_lam[i].astype(F32), qa, ka, va, gate, _col(ev_subln[i]),
                           lam_init),
                   _attn_b(bnd, qb, kb, vb, gate)]
            out_args = (ev_w_out[i].T.astype(BF16), _col(ev_ln_g[i]), _col(ev_ln_b[i]))
        else:
            q, k, v, gate, nq, nk = proj
            ogs = [_attn_c(_score_bounds(nq, jnp.max(nk, axis=-1)), q, k, v, gate)]
            out_args = (od_w_out[i].T.astype(BF16), _col(od_ln_g[i]), _col(od_ln_b[i]))
        if layer == DEPTH - 1:
            return _out_proj_last(ogs, xt, *out_args)
        nxt = (layer + 1) // 2
        if layer % 2 == 0:
            xt, *proj = _out_in(ogs, xt, *out_args, False, odd_args(nxt), layer == 0)
        else:
            xt, *proj = _out_in(ogs, xt, *out_args, True, even_args(nxt), False)


def kernel(x, ev_w_in, ev_w_out, ev_lam, ev_subln, ev_qnorm, ev_knorm, ev_ln_g, ev_ln_b,
           od_w_in, od_qnorm, od_kvnorm, od_w_qb, od_w_kvb, od_w_out, od_ln_g, od_ln_b):
    return _forward(x, ev_w_in, ev_w_out, ev_lam, ev_subln, ev_qnorm, ev_knorm, ev_ln_g,
                    ev_ln_b, od_w_in, od_qnorm, od_kvnorm, od_w_qb, od_w_kvb, od_w_out,
                    od_ln_g, od_ln_b)
```

```python
import functools
import math

import jax
import jax.numpy as jnp
from jax import lax
from jax.experimental import pallas as pl
from jax.experimental.pallas import tpu as pltpu

F32 = jnp.float32
BF16 = jnp.bfloat16

D_MODEL = 1024
SEQ = 4096
DEPTH = 4
GRID_W = 64
ROPE_THETA = 500000.0
AXIAL_THETA = 10000.0
LN_EPS = 1e-5
RMS_EPS = 1e-6

A_HEADS = 4
A_QK_DIM = 64
A_V_DIM = 128
A_WIDTH = 512
A_ROT = 16
B_Q_HEADS = 8
B_KV_HEADS = 2
B_GROUP = 4
B_DIM = 64
B_WIDTH = 512
EV_IN = 3328

C_HEADS = 16
C_NOPE = 64
C_ROPE = 32
C_V = 64
C_Q_LORA = 256
C_KV_LORA = 128
C_QK = C_NOPE + C_ROPE
OD_IN = 1440

ALPHA = (2 * DEPTH) ** 0.25
LOG2E = 1.4426950408889634

QSCALE_AB = A_QK_DIM ** -0.5 * LOG2E
QSCALE_C = C_QK ** -0.5 * LOG2E

LANES = 128
MXU_TILE = 256
LOOKAHEAD_ITEMS = {1: 2, 2: 3, 4: 8}
MAX_LOCKSTEP = 4
ONES_ROWS = 16
K_PAD = 128
NORM_ROWS = 16
SAFE_BOUND = 60.0
BOUND_SLACK = 1.02
TM = 1024
V_CHUNK = 512
SUB_TILES = 4
EVEN_TAB_POS, EVEN_OUT_TOKEN_AXES = 1, (2, 2, (1, 3), 2, 1, (1, 3), 2, 2, 2)
ODD_TAB_POS, ODD_OUT_TOKEN_AXES = 3, (2, 2, (1, 3), 2, 2, 2)
KV_CHUNK_A = 512
KV_CHUNK_BC = 256
TQ_A = 1024
TQ_B = 4096
TQ_C = 4096
VMEM_LIMIT = 56 * 1024 * 1024


def _rep(ref, n):
    a = ref[...]
    return jnp.concatenate([a] * (n // LANES), axis=1)


def _rot(x1, x2, cos, sin):
    return x1 * cos - x2 * sin, x2 * cos + x1 * sin


def _silu(x):
    return x * jax.nn.sigmoid(x)


def _sumsq(x):
    return jnp.sum(x * x, axis=0, keepdims=True)


def _tok(i, j):
    return (i, 0, j)


def _const2(i, j):
    return (0, 0)


def _resident(shape):
    return pl.BlockSpec(shape, _const2, pipeline_mode=pl.Buffered(1))


def _proj_params():
    return pltpu.CompilerParams(
        dimension_semantics=("parallel", "parallel"), vmem_limit_bytes=VMEM_LIMIT)


def _even_in_body(xb, w_ref, tab_ref, qn_ref, kn_ref,
                  qa_ref, ka_ref, va_ref, qb_ref, kb_ref, vb_ref, g_ref, nq_ref, nk_ref,
                  *, x_token_major=False):
    contract = (((1,), (1 if x_token_major else 0,)), ((), ()))

    def proj(r0, r1):
        return lax.dot_general(w_ref[r0:r1, :], xb, contract, preferred_element_type=F32)

    cos_a, sin_a = tab_ref[0:8, :], tab_ref[8:16, :]
    cos_r, sin_r = tab_ref[16:32, :], tab_ref[32:48, :]
    cos_c, sin_c = tab_ref[48:64, :], tab_ref[64:80, :]

    def rope_a(h):
        outs = []
        for hc in range(2 * A_HEADS):
            b = hc * A_QK_DIM
            r1, r2 = _rot(h[b:b + 8], h[b + 8:b + 16], cos_a, sin_a)
            outs.append(jnp.concatenate([r1, r2, h[b + 16:b + 64]], axis=0))
        return outs

    g_ref[0] = _silu(proj(2304, 3328)).astype(BF16)
    qs = [x * QSCALE_AB for x in rope_a(proj(0, 512))]
    qa_ref[0] = jnp.concatenate(qs, axis=0).astype(BF16)
    ks = rope_a(proj(512, 1024))
    nq = [_sumsq(x) for x in qs]
    nk = [_sumsq(x) for x in ks]
    for h in range(A_HEADS):
        kt = jnp.concatenate([ks[2 * h], ks[2 * h + 1]], axis=0)
        ka_ref[0, h] = kt.T.astype(BF16)

    def norm_axial(h, g):
        ms = jnp.mean(h * h, axis=0, keepdims=True)
        y = h * lax.rsqrt(ms + RMS_EPS) * g
        a1, a2 = _rot(y[0:16], y[16:32], cos_r, sin_r)
        b1, b2 = _rot(y[32:48], y[48:64], cos_c, sin_c)
        return jnp.concatenate([a1, a2, b1, b2], axis=0)

    width = xb.shape[0 if x_token_major else 1]
    qn = _rep(qn_ref, width)
    kn = _rep(kn_ref, width)
    hq = proj(1536, 2048)
    qs = [norm_axial(hq[h * 64:(h + 1) * 64], qn) * QSCALE_AB for h in range(B_Q_HEADS)]
    qb_ref[0] = jnp.concatenate(qs, axis=0).astype(BF16)
    hkv = proj(2048, 2304)
    ks = [norm_axial(hkv[h * 64:(h + 1) * 64], kn) for h in range(B_KV_HEADS)]
    kb_ref[0] = jnp.concatenate(ks, axis=0).T.astype(BF16)
    nq_ref[0] = jnp.concatenate(nq + [_sumsq(x) for x in qs], axis=0)
    nk = nk + [_sumsq(x) for x in ks]
    nk_ref[0] = jnp.concatenate(nk + [jnp.zeros_like(nk[0])] * (NORM_ROWS - len(nk)), axis=0)
    vb_ref[0, 0] = hkv[128:256].astype(BF16)
    va_ref[0, 0] = proj(1024, 1536).astype(BF16)


def _even_in_kernel(x_ref, *refs):
    n_in = len(refs) - len(EVEN_OUT_TOKEN_AXES)
    width = TM // SUB_TILES
    for t in range(SUB_TILES):
        off = t * width
        params = [_tokens(r, 1, off, width) if p == EVEN_TAB_POS else r
                  for p, r in enumerate(refs[:n_in])]
        views = [_tokens(r, ax, off, width) for r, ax in zip(refs[n_in:], EVEN_OUT_TOKEN_AXES)]
        _even_in_body(x_ref[0, pl.ds(off, width), :].astype(BF16), *params, *views,
                      x_token_major=True)


def _even_in_specs(b):
    d, s = D_MODEL, SEQ
    nt = s // TM
    tok = _tok
    chunk = lambda i, j: (i, j, 0, 0)
    return (
        [
            _resident((EV_IN, d)),
            pl.BlockSpec((80, TM), lambda i, j: (0, j)),
            _resident((B_DIM, LANES)),
            _resident((B_DIM, LANES)),
        ],
        [
            pl.BlockSpec((1, 512, TM), tok),
            pl.BlockSpec((1, A_HEADS, TM, K_PAD), lambda i, j: (i, 0, j, 0)),
            pl.BlockSpec((1, TM // V_CHUNK, 512, V_CHUNK), chunk),
            pl.BlockSpec((1, 512, TM), tok),
            pl.BlockSpec((1, TM, K_PAD), lambda i, j: (i, j, 0)),
            pl.BlockSpec((1, TM // V_CHUNK, 128, V_CHUNK), chunk),
            pl.BlockSpec((1, 1024, TM), tok),
            pl.BlockSpec((1, NORM_ROWS, TM), tok),
            pl.BlockSpec((1, NORM_ROWS, TM), tok),
        ],
        [
            jax.ShapeDtypeStruct((b, 512, s), BF16),
            jax.ShapeDtypeStruct((b, A_HEADS, s, K_PAD), BF16),
            jax.ShapeDtypeStruct((b, s // V_CHUNK, 512, V_CHUNK), BF16),
            jax.ShapeDtypeStruct((b, 512, s), BF16),
            jax.ShapeDtypeStruct((b, s, K_PAD), BF16),
            jax.ShapeDtypeStruct((b, s // V_CHUNK, 128, V_CHUNK), BF16),
            jax.ShapeDtypeStruct((b, 1024, s), BF16),
            jax.ShapeDtypeStruct((b, NORM_ROWS, s), F32),
            jax.ShapeDtypeStruct((b, NORM_ROWS, s), F32),
        ],
    )


def _even_in(x, w_t, tab, qn, kn):
    b = x.shape[0]
    in_specs, out_specs, out_shape = _even_in_specs(b)
    return pl.pallas_call(
        _even_in_kernel,
        grid=(b, SEQ // TM),
        in_specs=[pl.BlockSpec((1, TM, D_MODEL), lambda i, j: (i, j, 0))] + in_specs,
        out_specs=out_specs,
        out_shape=out_shape,
        compiler_params=_proj_params(),
        name="even_in_proj",
    )(x, w_t, tab, qn, kn)


def _odd_in_body(xb, w_ref, wq_ref, wkv_ref, tab_ref, qn_ref, kvn_ref,
                 q_ref, k_ref, v_ref, g_ref, nq_ref, nk_ref):
    cos, sin = tab_ref[0:16, :], tab_ref[16:32, :]

    def rms(h, g):
        ms = jnp.mean(h * h, axis=0, keepdims=True)
        return h * lax.rsqrt(ms + RMS_EPS) * g

    g_ref[0] = _silu(jnp.dot(w_ref[416:1440, :], xb, preferred_element_type=F32)).astype(BF16)
    lat = jnp.dot(w_ref[0:416, :], xb, preferred_element_type=F32)
    width = xb.shape[1]
    cqn = rms(lat[0:256], _rep(qn_ref, width)).astype(BF16)
    q = jnp.dot(wq_ref[...], cqn, preferred_element_type=F32)
    qs = []
    for h in range(C_HEADS):
        b = h * C_QK
        r1, r2 = _rot(q[b + 64:b + 80], q[b + 80:b + 96], cos, sin)
        qs.append(jnp.concatenate([q[b:b + 64], r1, r2], axis=0) * QSCALE_C)
    q_ref[0] = jnp.concatenate(qs, axis=0).astype(BF16)
    nq_ref[0] = jnp.concatenate([_sumsq(x) for x in qs], axis=0)

    ckvn = rms(lat[256:384], _rep(kvn_ref, width)).astype(BF16)
    kv = jnp.dot(wkv_ref[...], ckvn, preferred_element_type=F32)
    r1, r2 = _rot(lat[384:400], lat[400:416], cos, sin)
    zpad = jnp.zeros((K_PAD - C_QK, width), F32)
    nkr = _sumsq(r1) + _sumsq(r2)
    nk = []
    for h in range(C_HEADS):
        kn = kv[h * 128:h * 128 + 64]
        nk.append(_sumsq(kn) + nkr)
        kt = jnp.concatenate([kn, r1, r2, zpad], axis=0)
        k_ref[0, h] = kt.T.astype(BF16)
    nk_ref[0] = jnp.concatenate(nk, axis=0)
    v_ref[0, 0] = jnp.concatenate(
        [kv[h * 128 + 64:h * 128 + 128] for h in range(C_HEADS)], axis=0).astype(BF16)


def _odd_in_specs(b):
    d, s = D_MODEL, SEQ
    nt = s // TM
    tok = _tok
    return (
        [
            _resident((OD_IN, d)),
            _resident((C_HEADS * C_QK, C_Q_LORA)),
            _resident((C_HEADS * 128, C_KV_LORA)),
            pl.BlockSpec((32, TM), lambda i, j: (0, j)),
            _resident((C_Q_LORA, LANES)),
            _resident((C_KV_LORA, LANES)),
        ],
        [
            pl.BlockSpec((1, C_HEADS * C_QK, TM), tok),
            pl.BlockSpec((1, C_HEADS, TM, K_PAD), lambda i, j: (i, 0, j, 0)),
            pl.BlockSpec((1, TM // V_CHUNK, 1024, V_CHUNK), lambda i, j: (i, j, 0, 0)),
            pl.BlockSpec((1, 1024, TM), tok),
            pl.BlockSpec((1, NORM_ROWS, TM), tok),
            pl.BlockSpec((1, NORM_ROWS, TM), tok),
        ],
        [
            jax.ShapeDtypeStruct((b, C_HEADS * C_QK, s), BF16),
            jax.ShapeDtypeStruct((b, C_HEADS, s, K_PAD), BF16),
            jax.ShapeDtypeStruct((b, s // V_CHUNK, 1024, V_CHUNK), BF16),
            jax.ShapeDtypeStruct((b, 1024, s), BF16),
            jax.ShapeDtypeStruct((b, NORM_ROWS, s), F32),
            jax.ShapeDtypeStruct((b, NORM_ROWS, s), F32),
        ],
    )


def _softmax_pv(qps, k_at, v_ref, dv, kv_chunk, bounds=None):
    nsets = len(qps)
    ntiles = qps[0].shape[1] // MXU_TILE
    qps = [qp[:, t * MXU_TILE:(t + 1) * MXU_TILE] for qp in qps for t in range(ntiles)]
    n = len(qps)
    nchunks = SEQ // kv_chunk
    halves = kv_chunk // MXU_TILE
    ones = jnp.ones((ONES_ROWS, kv_chunk), BF16)

    def scores(ci, j):
        return [jnp.dot(k_at(ci * kv_chunk + h * MXU_TILE), qps[j], preferred_element_type=F32)
                for h in range(halves)]

    m = [None] * n
    acc = [None] * n
    lsum = [None] * n
    lockstep = min(n, MAX_LOCKSTEP if dv <= C_V else MAX_LOCKSTEP // 2)
    items = [(ci, j) for g in range(0, n, lockstep) for ci in range(nchunks)
             for j in range(g, min(g + lockstep, n))]
    lookahead = LOOKAHEAD_ITEMS[lockstep]
    pending = [scores(*it) for it in items[:lookahead]]
    for idx, (ci, j) in enumerate(items):
        if idx + lookahead < len(items):
            pending.append(scores(*items[idx + lookahead]))
        sc = pending.pop(0)
        r0 = ci * kv_chunk
        v = v_ref[0, r0 // V_CHUNK, :, pl.ds(r0 % V_CHUNK, kv_chunk)]
        if bounds is None:
            v = jnp.concatenate([v, ones], axis=0)
            cm = functools.reduce(jnp.maximum, [jnp.max(s, axis=0, keepdims=True) for s in sc])
            ref = cm if ci == 0 else jnp.maximum(m[j], cm)
        else:
            ref = bounds[j // ntiles]
        ps = [jnp.exp2(s - ref) for s in sc]
        pv = functools.reduce(jnp.add, [
            jnp.dot(v[:, h * MXU_TILE:(h + 1) * MXU_TILE], ps[h].astype(BF16),
                    preferred_element_type=F32) for h in range(halves)])
        if bounds is not None:
            part = functools.reduce(jnp.add, [
                jnp.sum(p.reshape(MXU_TILE // 8, 8, MXU_TILE), axis=0) for p in ps])
            lsum[j] = part if ci == 0 else lsum[j] + part
        if ci == 0:
            acc[j] = pv
        elif bounds is None:
            acc[j] = acc[j] * jnp.exp2(m[j] - ref) + pv
        else:
            acc[j] = acc[j] + pv
        m[j] = ref
    out = []
    for j in range(nsets):
        a = jnp.concatenate(acc[j * ntiles:(j + 1) * ntiles], axis=1)
        if bounds is None:
            out.append((a[0:dv], a[dv:dv + 1]))
        else:
            l8 = jnp.concatenate(lsum[j * ntiles:(j + 1) * ntiles], axis=1)
            out.append((a, jnp.sum(l8, axis=0, keepdims=True)))
    return out


def _attend(qps, bounds, k_at, v_ref, dv, kv_chunk, finish):
    safe = functools.reduce(jnp.logical_and, [b <= SAFE_BOUND for b in bounds])

    @pl.when(safe)
    def _():
        finish(_softmax_pv(qps, k_at, v_ref, dv, kv_chunk, bounds))

    @pl.when(jnp.logical_not(safe))
    def _():
        finish(_softmax_pv(qps, k_at, v_ref, dv, kv_chunk))


def _attn_params():
    return pltpu.CompilerParams(
        dimension_semantics=("parallel", "parallel", "parallel"), vmem_limit_bytes=VMEM_LIMIT)


def _attn_a_kernel(bnd_ref, lam_ref, q_ref, k_ref, v_ref, gate_ref, sg_ref, o_ref, *, lam_init):
    i, h = pl.program_id(0), pl.program_id(1)
    q = q_ref[0]
    tq = q.shape[1]
    z = jnp.zeros((A_QK_DIM, tq), BF16)
    qps = [jnp.concatenate([q[0:64], z], axis=0), jnp.concatenate([z, q[64:128]], axis=0)]

    def finish(res):
        (a0, l0), (a1, l1) = res
        lp = lam_ref[...]
        lam = (jnp.exp(jnp.sum(lp[0:1] * lp[1:2], axis=1, keepdims=True))
               - jnp.exp(jnp.sum(lp[2:3] * lp[3:4], axis=1, keepdims=True)) + lam_init)
        o = a0 / l0 - lam * (a1 / l1)
        ms = jnp.mean(o * o, axis=0, keepdims=True)
        o = o * lax.rsqrt(ms + RMS_EPS) * _rep(sg_ref, tq) * (1.0 - lam_init)
        o_ref[0] = (o * gate_ref[0]).astype(BF16)

    _attend(qps, [bnd_ref[i, 2 * h], bnd_ref[i, 2 * h + 1]],
            lambda r0: k_ref[0, 0, pl.ds(r0, MXU_TILE), :], v_ref, A_V_DIM, KV_CHUNK_A, finish)


def _attn_a(bnd, lam_p, qa, ka, va, gate, sg, lam_init):
    b = qa.shape[0]
    nt = SEQ // V_CHUNK
    return pl.pallas_call(
        functools.partial(_attn_a_kernel, lam_init=lam_init),
        grid=(b, A_HEADS, SEQ // TQ_A),
        in_specs=[
            pl.BlockSpec(memory_space=pltpu.SMEM),
            pl.BlockSpec((4, A_QK_DIM), lambda i, h, t: (0, 0)),
            pl.BlockSpec((1, 128, TQ_A), lambda i, h, t: (i, h, t)),
            pl.BlockSpec((1, 1, SEQ, K_PAD), lambda i, h, t: (i, h, 0, 0)),
            pl.BlockSpec((1, nt, A_V_DIM, V_CHUNK), lambda i, h, t: (i, 0, h, 0)),
            pl.BlockSpec((1, A_V_DIM, TQ_A), lambda i, h, t: (i, h, t)),
            pl.BlockSpec((A_V_DIM, LANES), lambda i, h, t: (0, 0)),
        ],
        out_specs=pl.BlockSpec((1, A_V_DIM, TQ_A), lambda i, h, t: (i, h, t)),
        out_shape=jax.ShapeDtypeStruct((b, A_WIDTH, SEQ), BF16),
        compiler_params=_attn_params(),
        name="attn_diff",
    )(bnd, lam_p, qa, ka, va, gate, sg)


def _attn_b_kernel(bnd_ref, q_ref, k_ref, v_ref, gate_ref, o_ref):
    i, h = pl.program_id(0), pl.program_id(1)
    first = h < B_GROUP
    q = q_ref[0]
    z = jnp.zeros_like(q)
    qp = jnp.concatenate([jnp.where(first, q, z), jnp.where(first, z, q)], axis=0)

    def finish(res):
        ((a, l),) = res
        o_ref[0] = (a / l * gate_ref[0]).astype(BF16)

    _attend([qp], [bnd_ref[i, 2 * A_HEADS + h]],
            lambda r0: k_ref[0, pl.ds(r0, MXU_TILE), :], v_ref, B_DIM, KV_CHUNK_BC, finish)


def _attn_b(bnd, qb, kb, vb, gate):
    b = qb.shape[0]
    nt = SEQ // V_CHUNK
    return pl.pallas_call(
        _attn_b_kernel,
        grid=(b, B_Q_HEADS, SEQ // TQ_B),
        in_specs=[
            pl.BlockSpec(memory_space=pltpu.SMEM),
            pl.BlockSpec((1, B_DIM, TQ_B), lambda i, h, t: (i, h, t)),
            pl.BlockSpec((1, SEQ, K_PAD), lambda i, h, t: (i, 0, 0)),
            pl.BlockSpec((1, nt, B_DIM, V_CHUNK), lambda i, h, t: (i, 0, h // B_GROUP, 0)),
            pl.BlockSpec((1, B_DIM, TQ_B), lambda i, h, t: (i, A_WIDTH // B_DIM + h, t)),
        ],
        out_specs=pl.BlockSpec((1, B_DIM, TQ_B), lambda i, h, t: (i, h, t)),
        out_shape=jax.ShapeDtypeStruct((b, B_WIDTH, SEQ), BF16),
        compiler_params=_attn_params(),
        name="attn_gqa",
    )(bnd, qb, kb, vb, gate)


def _attn_c_kernel(bnd_ref, q_ref, k_ref, v_ref, gate_ref, o_ref):
    i, h = pl.program_id(0), pl.program_id(1)
    q = q_ref[0]
    qp = jnp.concatenate([q, jnp.zeros((K_PAD - C_QK, q.shape[1]), BF16)], axis=0)

    def finish(res):
        ((a, l),) = res
        o_ref[0] = (a / l * gate_ref[0]).astype(BF16)

    _attend([qp], [bnd_ref[i, h]],
            lambda r0: k_ref[0, 0, pl.ds(r0, MXU_TILE), :], v_ref, C_V, KV_CHUNK_BC, finish)


def _attn_c(bnd, q, k, v, gate):
    b = q.shape[0]
    nt = SEQ // V_CHUNK
    return pl.pallas_call(
        _attn_c_kernel,
        grid=(b, C_HEADS, SEQ // TQ_C),
        in_specs=[
            pl.BlockSpec(memory_space=pltpu.SMEM),
            pl.BlockSpec((1, C_QK, TQ_C), lambda i, h, t: (i, h, t)),
            pl.BlockSpec((1, 1, SEQ, K_PAD), lambda i, h, t: (i, h, 0, 0)),
            pl.BlockSpec((1, nt, C_V, V_CHUNK), lambda i, h, t: (i, 0, h, 0)),
            pl.BlockSpec((1, C_V, TQ_C), lambda i, h, t: (i, h, t)),
        ],
        out_specs=pl.BlockSpec((1, C_V, TQ_C), lambda i, h, t: (i, h, t)),
        out_shape=jax.ShapeDtypeStruct((b, C_HEADS * C_V, SEQ), BF16),
        compiler_params=_attn_params(),
        name="attn_mla",
    )(bnd, q, k, v, gate)


def _out_body(og_refs, x_ref, w_ref, lg_ref, lb_ref, x_token_major):
    og = jnp.concatenate([r[0] for r in og_refs], axis=0) if len(og_refs) > 1 else og_refs[0][0]
    y = jnp.dot(w_ref[...], og, preferred_element_type=F32)
    x = x_ref[0].T if x_token_major else x_ref[0]
    z = ALPHA * x + y
    mu = jnp.mean(z, axis=0, keepdims=True)
    d = z - mu
    var = jnp.mean(d * d, axis=0, keepdims=True)
    width = y.shape[1]
    return d * lax.rsqrt(var + LN_EPS) * _rep(lg_ref, width) + _rep(lb_ref, width)


def _out_kernel(*refs, n_og):
    x_ref, w_ref, lg_ref, lb_ref, o_ref = refs[n_og:]
    width = TM // SUB_TILES
    for t in range(SUB_TILES):
        off = t * width
        og_views = [_tokens(r, 2, off, width) for r in refs[:n_og]]
        out = _out_body(og_views, _tokens(x_ref, 2, off, width), w_ref, lg_ref, lb_ref, False)
        o_ref[0, pl.ds(off, width), :] = out.T


def _tokens(ref, axis, off, width):
    idx = [slice(None)] * len(ref.shape)
    if isinstance(axis, tuple):
        idx[axis[0]] = pl.ds(off // V_CHUNK, 1)
        idx[axis[1]] = pl.ds(off % V_CHUNK, width)
    else:
        idx[axis] = pl.ds(off, width)
    return ref.at[tuple(idx)]


def _out_in_kernel(*refs, n_og, n_in, x_token_major, in_body, tab_pos, out_axes):
    x_ref, w_ref, lg_ref, lb_ref = refs[n_og:n_og + 4]
    in_params = refs[n_og + 4:n_og + 4 + n_in]
    xo_ref = refs[n_og + 4 + n_in]
    in_outs = refs[n_og + 5 + n_in:]
    width = TM // SUB_TILES
    outs = []
    for t in range(SUB_TILES):
        off = t * width
        og_views = [_tokens(r, 2, off, width) for r in refs[:n_og]]
        x_view = _tokens(x_ref, 1 if x_token_major else 2, off, width)
        out = _out_body(og_views, x_view, w_ref, lg_ref, lb_ref, x_token_major)
        xo_ref[0, :, pl.ds(off, width)] = out
        outs.append(out.astype(BF16))
    for t in range(SUB_TILES):
        off = t * width
        params = [_tokens(r, 1, off, width) if p == tab_pos else r
                  for p, r in enumerate(in_params)]
        views = [_tokens(r, ax, off, width) for r, ax in zip(in_outs, out_axes)]
        in_body(outs[t], *params, *views)


def _out_specs(ogs, x_token_major):
    d = D_MODEL
    x_spec = (pl.BlockSpec((1, TM, d), lambda i, j: (i, j, 0)) if x_token_major
              else pl.BlockSpec((1, d, TM), _tok))
    return [pl.BlockSpec((1, og.shape[1], TM), _tok) for og in ogs] + [
        x_spec, _resident((d, d)), _resident((d, LANES)), _resident((d, LANES))]


def _out_proj_last(ogs, xt, w_t, lg, lb):
    b = xt.shape[0]
    return pl.pallas_call(
        functools.partial(_out_kernel, n_og=len(ogs)),
        grid=(b, SEQ // TM),
        in_specs=_out_specs(ogs, False),
        out_specs=pl.BlockSpec((1, TM, D_MODEL), lambda i, j: (i, j, 0)),
        out_shape=jax.ShapeDtypeStruct((b, SEQ, D_MODEL), F32),
        compiler_params=_proj_params(),
        name="out_proj_ln",
    )(*ogs, xt, w_t, lg, lb)


def _out_in(ogs, xt, w_t, lg, lb, next_even, in_args, x_token_major):
    b = xt.shape[0]
    if next_even:
        in_body, (in_specs, out_specs, out_shape) = _even_in_body, _even_in_specs(b)
        tab_pos, out_axes, name = EVEN_TAB_POS, EVEN_OUT_TOKEN_AXES, "out_even_in_proj"
    else:
        in_body, (in_specs, out_specs, out_shape) = _odd_in_body, _odd_in_specs(b)
        tab_pos, out_axes, name = ODD_TAB_POS, ODD_OUT_TOKEN_AXES, "out_odd_in_proj"
    return pl.pallas_call(
        functools.partial(_out_in_kernel, n_og=len(ogs), n_in=len(in_args),
                          x_token_major=x_token_major, in_body=in_body, tab_pos=tab_pos,
                          out_axes=out_axes),
        grid=(b, SEQ // TM),
        in_specs=_out_specs(ogs, x_token_major) + in_specs,
        out_specs=[pl.BlockSpec((1, D_MODEL, TM), _tok)] + out_specs,
        out_shape=[jax.ShapeDtypeStruct((b, D_MODEL, SEQ), F32)] + out_shape,
        compiler_params=_proj_params(),
        name=name,
    )(*ogs, xt, w_t, lg, lb, *in_args)


def _col(v):
    v = v.astype(F32)
    return jnp.broadcast_to(v[:, None], (v.shape[0], LANES))


def _score_bounds(nq, kmax2):
    return jnp.sqrt(jnp.max(nq, axis=-1) * kmax2) * BOUND_SLACK


def _angles_t(pos, dims, theta):
    inv = theta ** (-jnp.arange(0, dims, 2, dtype=F32) / dims)
    ang = pos.astype(F32)[:, None] * inv[None, :]
    return jnp.cos(ang).T, jnp.sin(ang).T


@jax.jit
def _forward(x, ev_w_in, ev_w_out, ev_lam, ev_subln, ev_qnorm, ev_knorm, ev_ln_g, ev_ln_b,
             od_w_in, od_qnorm, od_kvnorm, od_w_qb, od_w_kvb, od_w_out, od_ln_g, od_ln_b):
    s = x.shape[1]
    pos = jnp.arange(s, dtype=jnp.int32)
    row = jnp.repeat(jnp.arange(s // GRID_W, dtype=jnp.int32), GRID_W)
    col = jnp.tile(jnp.arange(GRID_W, dtype=jnp.int32), s // GRID_W)
    tab_ev = jnp.concatenate(
        _angles_t(pos, A_ROT, ROPE_THETA) + _angles_t(row, B_DIM // 2, AXIAL_THETA)
        + _angles_t(col, B_DIM // 2, AXIAL_THETA), axis=0)
    tab_od = jnp.concatenate(_angles_t(pos, C_ROPE, ROPE_THETA), axis=0)

    def even_args(i):
        return (ev_w_in[i].T.astype(BF16), tab_ev, _col(ev_qnorm[i]), _col(ev_knorm[i]))

    def odd_args(i):
        return (od_w_in[i].T.astype(BF16), od_w_qb[i].T.astype(BF16),
                od_w_kvb[i].T.astype(BF16), tab_od, _col(od_qnorm[i]), _col(od_kvnorm[i]))

    xt = x
    proj = _even_in(x, *even_args(0))
    for layer in range(DEPTH):
        i = layer // 2
        if layer % 2 == 0:
            qa, ka, va, qb, kb, vb, gate, nq, nk = proj
            kmax = jnp.max(nk, axis=-1)
            kmax = jnp.concatenate(
                [kmax[:, :2 * A_HEADS],
                 jnp.repeat(kmax[:, 2 * A_HEADS:2 * A_HEADS + B_KV_HEADS], B_GROUP, axis=1)],
                axis=1)
            bnd = _score_bounds(nq, kmax)
            lam_init = 0.8 - 0.6 * math.exp(-0.3 * layer)
            ogs = [_attn_a(bnd, ev_lam[i].astype(F32), qa, ka, va, gate, _col(ev_subln[i]),
                           lam_init),
                   _attn_b(bnd, qb, kb, vb, gate)]
            out_args = (ev_w_out[i].T.astype(BF16), _col(ev_ln_g[i]), _col(ev_ln_b[i]))
        else:
            q, k, v, gate, nq, nk = proj
            ogs = [_attn_c(_score_bounds(nq, jnp.max(nk, axis=-1)), q, k, v, gate)]
            out_args = (od_w_out[i].T.astype(BF16), _col(od_ln_g[i]), _col(od_ln_b[i]))
        if layer == DEPTH - 1:
            return _out_proj_last(ogs, xt, *out_args)
        nxt = (layer + 1) // 2
        if layer % 2 == 0:
            xt, *proj = _out_in(ogs, xt, *out_args, False, odd_args(nxt), layer == 0)
        else:
            xt, *proj = _out_in(ogs, xt, *out_args, True, even_args(nxt), False)


def kernel(x, ev_w_in, ev_w_out, ev_lam, ev_subln, ev_qnorm, ev_knorm, ev_ln_g, ev_ln_b,
           od_w_in, od_qnorm, od_kvnorm, od_w_qb, od_w_kvb, od_w_out, od_ln_g, od_ln_b):
    return _forward(x, ev_w_in, ev_w_out, ev_lam, ev_subln, ev_qnorm, ev_knorm, ev_ln_g,
                    ev_ln_b, od_w_in, od_qnorm, od_kvnorm, od_w_qb, od_w_kvb, od_w_out,
                    od_ln_g, od_ln_b)
```

```python
import functools
import math

import jax
import jax.numpy as jnp
from jax import lax
from jax.experimental import pallas as pl
from jax.experimental.pallas import tpu as pltpu

F32 = jnp.float32
BF16 = jnp.bfloat16

D_MODEL = 1024
SEQ = 4096
DEPTH = 4
GRID_W = 64
ROPE_THETA = 500000.0
AXIAL_THETA = 10000.0
LN_EPS = 1e-5
RMS_EPS = 1e-6

A_HEADS = 4
A_QK_DIM = 64
A_V_DIM = 128
A_WIDTH = 512
A_ROT = 16
B_Q_HEADS = 8
B_KV_HEADS = 2
B_GROUP = 4
B_DIM = 64
B_WIDTH = 512
EV_IN = 3328

C_HEADS = 16
C_NOPE = 64
C_ROPE = 32
C_V = 64
C_Q_LORA = 256
C_KV_LORA = 128
C_QK = C_NOPE + C_ROPE
OD_IN = 1440

ALPHA = (2 * DEPTH) ** 0.25
LOG2E = 1.4426950408889634

QSCALE_AB = A_QK_DIM ** -0.5 * LOG2E
QSCALE_C = C_QK ** -0.5 * LOG2E

LANES = 128
MXU_TILE = 256
LOOKAHEAD_ITEMS = {1: 2, 2: 3, 4: 8}
MAX_LOCKSTEP = 4
ONES_ROWS = 16
K_PAD = 128
NORM_ROWS = 16
SAFE_BOUND = -1.0
BOUND_SLACK = 1.02
TM = 1024
V_CHUNK = 512
SUB_TILES = 4
EVEN_TAB_POS, EVEN_OUT_TOKEN_AXES = 1, (2, 2, (1, 3), 2, 1, (1, 3), 2, 2, 2)
ODD_TAB_POS, ODD_OUT_TOKEN_AXES = 3, (2, 2, (1, 3), 2, 2, 2)
KV_CHUNK_A = 512
KV_CHUNK_BC = 256
TQ_A = 1024
TQ_B = 4096
TQ_C = 4096
VMEM_LIMIT = 56 * 1024 * 1024


def _rep(ref, n):
    a = ref[...]
    return jnp.concatenate([a] * (n // LANES), axis=1)


def _rot(x1, x2, cos, sin):
    return x1 * cos - x2 * sin, x2 * cos + x1 * sin


def _silu(x):
    return x * jax.nn.sigmoid(x)


def _sumsq(x):
    return jnp.sum(x * x, axis=0, keepdims=True)


def _tok(i, j):
    return (i, 0, j)


def _const2(i, j):
    return (0, 0)


def _resident(shape):
    return pl.BlockSpec(shape, _const2, pipeline_mode=pl.Buffered(1))


def _proj_params():
    return pltpu.CompilerParams(
        dimension_semantics=("parallel", "parallel"), vmem_limit_bytes=VMEM_LIMIT)


def _even_in_body(xb, w_ref, tab_ref, qn_ref, kn_ref,
                  qa_ref, ka_ref, va_ref, qb_ref, kb_ref, vb_ref, g_ref, nq_ref, nk_ref,
                  *, x_token_major=False):
    contract = (((1,), (1 if x_token_major else 0,)), ((), ()))

    def proj(r0, r1):
        return lax.dot_general(w_ref[r0:r1, :], xb, contract, preferred_element_type=F32)

    cos_a, sin_a = tab_ref[0:8, :], tab_ref[8:16, :]
    cos_r, sin_r = tab_ref[16:32, :], tab_ref[32:48, :]
    cos_c, sin_c = tab_ref[48:64, :], tab_ref[64:80, :]

    def rope_a(h):
        outs = []
        for hc in range(2 * A_HEADS):
            b = hc * A_QK_DIM
            r1, r2 = _rot(h[b:b + 8], h[b + 8:b + 16], cos_a, sin_a)
            outs.append(jnp.concatenate([r1, r2, h[b + 16:b + 64]], axis=0))
        return outs

    g_ref[0] = _silu(proj(2304, 3328)).astype(BF16)
    qs = [x * QSCALE_AB for x in rope_a(proj(0, 512))]
    qa_ref[0] = jnp.concatenate(qs, axis=0).astype(BF16)
    ks = rope_a(proj(512, 1024))
    nq = [_sumsq(x) for x in qs]
    nk = [_sumsq(x) for x in ks]
    for h in range(A_HEADS):
        kt = jnp.concatenate([ks[2 * h], ks[2 * h + 1]], axis=0)
        ka_ref[0, h] = kt.T.astype(BF16)

    def norm_axial(h, g):
        ms = jnp.mean(h * h, axis=0, keepdims=True)
        y = h * lax.rsqrt(ms + RMS_EPS) * g
        a1, a2 = _rot(y[0:16], y[16:32], cos_r, sin_r)
        b1, b2 = _rot(y[32:48], y[48:64], cos_c, sin_c)
        return jnp.concatenate([a1, a2, b1, b2], axis=0)

    width = xb.shape[0 if x_token_major else 1]
    qn = _rep(qn_ref, width)
    kn = _rep(kn_ref, width)
    hq = proj(1536, 2048)
    qs = [norm_axial(hq[h * 64:(h + 1) * 64], qn) * QSCALE_AB for h in range(B_Q_HEADS)]
    qb_ref[0] = jnp.concatenate(qs, axis=0).astype(BF16)
    hkv = proj(2048, 2304)
    ks = [norm_axial(hkv[h * 64:(h + 1) * 64], kn) for h in range(B_KV_HEADS)]
    kb_ref[0] = jnp.concatenate(ks, axis=0).T.astype(BF16)
    nq_ref[0] = jnp.concatenate(nq + [_sumsq(x) for x in qs], axis=0)
    nk = nk + [_sumsq(x) for x in ks]
    nk_ref[0] = jnp.concatenate(nk + [jnp.zeros_like(nk[0])] * (NORM_ROWS - len(nk)), axis=0)
    vb_ref[0, 0] = hkv[128:256].astype(BF16)
    va_ref[0, 0] = proj(1024, 1536).astype(BF16)


def _even_in_kernel(x_ref, *refs):
    n_in = len(refs) - len(EVEN_OUT_TOKEN_AXES)
    width = TM // SUB_TILES
    for t in range(SUB_TILES):
        off = t * width
        params = [_tokens(r, 1, off, width) if p == EVEN_TAB_POS else r
                  for p, r in enumerate(refs[:n_in])]
        views = [_tokens(r, ax, off, width) for r, ax in zip(refs[n_in:], EVEN_OUT_TOKEN_AXES)]
        _even_in_body(x_ref[0, pl.ds(off, width), :].astype(BF16), *params, *views,
                      x_token_major=True)


def _even_in_specs(b):
    d, s = D_MODEL, SEQ
    nt = s // TM
    tok = _tok
    chunk = lambda i, j: (i, j, 0, 0)
    return (
        [
            _resident((EV_IN, d)),
            pl.BlockSpec((80, TM), lambda i, j: (0, j)),
            _resident((B_DIM, LANES)),
            _resident((B_DIM, LANES)),
        ],
        [
            pl.BlockSpec((1, 512, TM), tok),
            pl.BlockSpec((1, A_HEADS, TM, K_PAD), lambda i, j: (i, 0, j, 0)),
            pl.BlockSpec((1, TM // V_CHUNK, 512, V_CHUNK), chunk),
            pl.BlockSpec((1, 512, TM), tok),
            pl.BlockSpec((1, TM, K_PAD), lambda i, j: (i, j, 0)),
            pl.BlockSpec((1, TM // V_CHUNK, 128, V_CHUNK), chunk),
            pl.BlockSpec((1, 1024, TM), tok),
            pl.BlockSpec((1, NORM_ROWS, TM), tok),
            pl.BlockSpec((1, NORM_ROWS, TM), tok),
        ],
        [
            jax.ShapeDtypeStruct((b, 512, s), BF16),
            jax.ShapeDtypeStruct((b, A_HEADS, s, K_PAD), BF16),
            jax.ShapeDtypeStruct((b, s // V_CHUNK, 512, V_CHUNK), BF16),
            jax.ShapeDtypeStruct((b, 512, s), BF16),
            jax.ShapeDtypeStruct((b, s, K_PAD), BF16),
            jax.ShapeDtypeStruct((b, s // V_CHUNK, 128, V_CHUNK), BF16),
            jax.ShapeDtypeStruct((b, 1024, s), BF16),
            jax.ShapeDtypeStruct((b, NORM_ROWS, s), F32),
            jax.ShapeDtypeStruct((b, NORM_ROWS, s), F32),
        ],
    )


def _even_in(x, w_t, tab, qn, kn):
    b = x.shape[0]
    in_specs, out_specs, out_shape = _even_in_specs(b)
    return pl.pallas_call(
        _even_in_kernel,
        grid=(b, SEQ // TM),
        in_specs=[pl.BlockSpec((1, TM, D_MODEL), lambda i, j: (i, j, 0))] + in_specs,
        out_specs=out_specs,
        out_shape=out_shape,
        compiler_params=_proj_params(),
        name="even_in_proj",
    )(x, w_t, tab, qn, kn)


def _odd_in_body(xb, w_ref, wq_ref, wkv_ref, tab_ref, qn_ref, kvn_ref,
                 q_ref, k_ref, v_ref, g_ref, nq_ref, nk_ref):
    cos, sin = tab_ref[0:16, :], tab_ref[16:32, :]

    def rms(h, g):
        ms = jnp.mean(h * h, axis=0, keepdims=True)
        return h * lax.rsqrt(ms + RMS_EPS) * g

    g_ref[0] = _silu(jnp.dot(w_ref[416:1440, :], xb, preferred_element_type=F32)).astype(BF16)
    lat = jnp.dot(w_ref[0:416, :], xb, preferred_element_type=F32)
    width = xb.shape[1]
    cqn = rms(lat[0:256], _rep(qn_ref, width)).astype(BF16)
    q = jnp.dot(wq_ref[...], cqn, preferred_element_type=F32)
    qs = []
    for h in range(C_HEADS):
        b = h * C_QK
        r1, r2 = _rot(q[b + 64:b + 80], q[b + 80:b + 96], cos, sin)
        qs.append(jnp.concatenate([q[b:b + 64], r1, r2], axis=0) * QSCALE_C)
    q_ref[0] = jnp.concatenate(qs, axis=0).astype(BF16)
    nq_ref[0] = jnp.concatenate([_sumsq(x) for x in qs], axis=0)

    ckvn = rms(lat[256:384], _rep(kvn_ref, width)).astype(BF16)
    kv = jnp.dot(wkv_ref[...], ckvn, preferred_element_type=F32)
    r1, r2 = _rot(lat[384:400], lat[400:416], cos, sin)
    zpad = jnp.zeros((K_PAD - C_QK, width), F32)
    nkr = _sumsq(r1) + _sumsq(r2)
    nk = []
    for h in range(C_HEADS):
        kn = kv[h * 128:h * 128 + 64]
        nk.append(_sumsq(kn) + nkr)
        kt = jnp.concatenate([kn, r1, r2, zpad], axis=0)
        k_ref[0, h] = kt.T.astype(BF16)
    nk_ref[0] = jnp.concatenate(nk, axis=0)
    v_ref[0, 0] = jnp.concatenate(
        [kv[h * 128 + 64:h * 128 + 128] for h in range(C_HEADS)], axis=0).astype(BF16)


def _odd_in_specs(b):
    d, s = D_MODEL, SEQ
    nt = s // TM
    tok = _tok
    return (
        [
            _resident((OD_IN, d)),
            _resident((C_HEADS * C_QK, C_Q_LORA)),
            _resident((C_HEADS * 128, C_KV_LORA)),
            pl.BlockSpec((32, TM), lambda i, j: (0, j)),
            _resident((C_Q_LORA, LANES)),
            _resident((C_KV_LORA, LANES)),
        ],
        [
            pl.BlockSpec((1, C_HEADS * C_QK, TM), tok),
            pl.BlockSpec((1, C_HEADS, TM, K_PAD), lambda i, j: (i, 0, j, 0)),
            pl.BlockSpec((1, TM // V_CHUNK, 1024, V_CHUNK), lambda i, j: (i, j, 0, 0)),
            pl.BlockSpec((1, 1024, TM), tok),
            pl.BlockSpec((1, NORM_ROWS, TM), tok),
            pl.BlockSpec((1, NORM_ROWS, TM), tok),
        ],
        [
            jax.ShapeDtypeStruct((b, C_HEADS * C_QK, s), BF16),
            jax.ShapeDtypeStruct((b, C_HEADS, s, K_PAD), BF16),
            jax.ShapeDtypeStruct((b, s // V_CHUNK, 1024, V_CHUNK), BF16),
            jax.ShapeDtypeStruct((b, 1024, s), BF16),
            jax.ShapeDtypeStruct((b, NORM_ROWS, s), F32),
            jax.ShapeDtypeStruct((b, NORM_ROWS, s), F32),
        ],
    )


def _softmax_pv(qps, k_at, v_ref, dv, kv_chunk, bounds=None):
    nsets = len(qps)
    ntiles = qps[0].shape[1] // MXU_TILE
    qps = [qp[:, t * MXU_TILE:(t + 1) * MXU_TILE] for qp in qps for t in range(ntiles)]
    n = len(qps)
    nchunks = SEQ // kv_chunk
    halves = kv_chunk // MXU_TILE
    ones = jnp.ones((ONES_ROWS, kv_chunk), BF16)

    def scores(ci, j):
        return [jnp.dot(k_at(ci * kv_chunk + h * MXU_TILE), qps[j], preferred_element_type=F32)
                for h in range(halves)]

    m = [None] * n
    acc = [None] * n
    lsum = [None] * n
    items = [(ci, j) for g in range(0, n, MAX_LOCKSTEP) for ci in range(nchunks)
             for j in range(g, min(g + MAX_LOCKSTEP, n))]
    lookahead = LOOKAHEAD_ITEMS[min(n, MAX_LOCKSTEP)]
    pending = [scores(*it) for it in items[:lookahead]]
    for idx, (ci, j) in enumerate(items):
        if idx + lookahead < len(items):
            pending.append(scores(*items[idx + lookahead]))
        sc = pending.pop(0)
        r0 = ci * kv_chunk
        v = v_ref[0, r0 // V_CHUNK, :, pl.ds(r0 % V_CHUNK, kv_chunk)]
        if bounds is None:
            v = jnp.concatenate([v, ones], axis=0)
            cm = functools.reduce(jnp.maximum, [jnp.max(s, axis=0, keepdims=True) for s in sc])
            ref = cm if ci == 0 else jnp.maximum(m[j], cm)
        else:
            ref = bounds[j // ntiles]
        ps = [jnp.exp2(s - ref) for s in sc]
        pv = functools.reduce(jnp.add, [
            jnp.dot(v[:, h * MXU_TILE:(h + 1) * MXU_TILE], ps[h].astype(BF16),
                    preferred_element_type=F32) for h in range(halves)])
        if bounds is not None:
            part = functools.reduce(jnp.add, [
                jnp.sum(p.reshape(MXU_TILE // 8, 8, MXU_TILE), axis=0) for p in ps])
            lsum[j] = part if ci == 0 else lsum[j] + part
        if ci == 0:
            acc[j] = pv
        elif bounds is None:
            acc[j] = acc[j] * jnp.exp2(m[j] - ref) + pv
        else:
            acc[j] = acc[j] + pv
        m[j] = ref
    out = []
    for j in range(nsets):
        a = jnp.concatenate(acc[j * ntiles:(j + 1) * ntiles], axis=1)
        if bounds is None:
            out.append((a[0:dv], a[dv:dv + 1]))
        else:
            l8 = jnp.concatenate(lsum[j * ntiles:(j + 1) * ntiles], axis=1)
            out.append((a, jnp.sum(l8, axis=0, keepdims=True)))
    return out


def _attend(qps, bounds, k_at, v_ref, dv, kv_chunk, finish):
    safe = functools.reduce(jnp.logical_and, [b <= SAFE_BOUND for b in bounds])

    @pl.when(safe)
    def _():
        finish(_softmax_pv(qps, k_at, v_ref, dv, kv_chunk, bounds))

    @pl.when(jnp.logical_not(safe))
    def _():
        finish(_softmax_pv(qps, k_at, v_ref, dv, kv_chunk))


def _attn_params():
    return pltpu.CompilerParams(
        dimension_semantics=("parallel", "parallel", "parallel"), vmem_limit_bytes=VMEM_LIMIT)


def _attn_a_kernel(bnd_ref, lam_ref, q_ref, k_ref, v_ref, gate_ref, sg_ref, o_ref, *, lam_init):
    i, h = pl.program_id(0), pl.program_id(1)
    q = q_ref[0]
    tq = q.shape[1]
    z = jnp.zeros((A_QK_DIM, tq), BF16)
    qps = [jnp.concatenate([q[0:64], z], axis=0), jnp.concatenate([z, q[64:128]], axis=0)]

    def finish(res):
        (a0, l0), (a1, l1) = res
        lp = lam_ref[...]
        lam = (jnp.exp(jnp.sum(lp[0:1] * lp[1:2], axis=1, keepdims=True))
               - jnp.exp(jnp.sum(lp[2:3] * lp[3:4], axis=1, keepdims=True)) + lam_init)
        o = a0 / l0 - lam * (a1 / l1)
        ms = jnp.mean(o * o, axis=0, keepdims=True)
        o = o * lax.rsqrt(ms + RMS_EPS) * _rep(sg_ref, tq) * (1.0 - lam_init)
        o_ref[0] = (o * gate_ref[0]).astype(BF16)

    _attend(qps, [bnd_ref[i, 2 * h], bnd_ref[i, 2 * h + 1]],
            lambda r0: k_ref[0, 0, pl.ds(r0, MXU_TILE), :], v_ref, A_V_DIM, KV_CHUNK_A, finish)


def _attn_a(bnd, lam_p, qa, ka, va, gate, sg, lam_init):
    b = qa.shape[0]
    nt = SEQ // V_CHUNK
    return pl.pallas_call(
        functools.partial(_attn_a_kernel, lam_init=lam_init),
        grid=(b, A_HEADS, SEQ // TQ_A),
        in_specs=[
            pl.BlockSpec(memory_space=pltpu.SMEM),
            pl.BlockSpec((4, A_QK_DIM), lambda i, h, t: (0, 0)),
            pl.BlockSpec((1, 128, TQ_A), lambda i, h, t: (i, h, t)),
            pl.BlockSpec((1, 1, SEQ, K_PAD), lambda i, h, t: (i, h, 0, 0)),
            pl.BlockSpec((1, nt, A_V_DIM, V_CHUNK), lambda i, h, t: (i, 0, h, 0)),
            pl.BlockSpec((1, A_V_DIM, TQ_A), lambda i, h, t: (i, h, t)),
            pl.BlockSpec((A_V_DIM, LANES), lambda i, h, t: (0, 0)),
        ],
        out_specs=pl.BlockSpec((1, A_V_DIM, TQ_A), lambda i, h, t: (i, h, t)),
        out_shape=jax.ShapeDtypeStruct((b, A_WIDTH, SEQ), BF16),
        compiler_params=_attn_params(),
        name="attn_diff",
    )(bnd, lam_p, qa, ka, va, gate, sg)


def _attn_b_kernel(bnd_ref, q_ref, k_ref, v_ref, gate_ref, o_ref):
    i, h = pl.program_id(0), pl.program_id(1)
    first = h < B_GROUP
    q = q_ref[0]
    z = jnp.zeros_like(q)
    qp = jnp.concatenate([jnp.where(first, q, z), jnp.where(first, z, q)], axis=0)

    def finish(res):
        ((a, l),) = res
        o_ref[0] = (a / l * gate_ref[0]).astype(BF16)

    _attend([qp], [bnd_ref[i, 2 * A_HEADS + h]],
            lambda r0: k_ref[0, pl.ds(r0, MXU_TILE), :], v_ref, B_DIM, KV_CHUNK_BC, finish)


def _attn_b(bnd, qb, kb, vb, gate):
    b = qb.shape[0]
    nt = SEQ // V_CHUNK
    return pl.pallas_call(
        _attn_b_kernel,
        grid=(b, B_Q_HEADS, SEQ // TQ_B),
        in_specs=[
            pl.BlockSpec(memory_space=pltpu.SMEM),
            pl.BlockSpec((1, B_DIM, TQ_B), lambda i, h, t: (i, h, t)),
            pl.BlockSpec((1, SEQ, K_PAD), lambda i, h, t: (i, 0, 0)),
            pl.BlockSpec((1, nt, B_DIM, V_CHUNK), lambda i, h, t: (i, 0, h // B_GROUP, 0)),
            pl.BlockSpec((1, B_DIM, TQ_B), lambda i, h, t: (i, A_WIDTH // B_DIM + h, t)),
        ],
        out_specs=pl.BlockSpec((1, B_DIM, TQ_B), lambda i, h, t: (i, h, t)),
        out_shape=jax.ShapeDtypeStruct((b, B_WIDTH, SEQ), BF16),
        compiler_params=_attn_params(),
        name="attn_gqa",
    )(bnd, qb, kb, vb, gate)


def _attn_c_kernel(bnd_ref, q_ref, k_ref, v_ref, gate_ref, o_ref):
    i, h = pl.program_id(0), pl.program_id(1)
    q = q_ref[0]
    qp = jnp.concatenate([q, jnp.zeros((K_PAD - C_QK, q.shape[1]), BF16)], axis=0)

    def finish(res):
        ((a, l),) = res
        o_ref[0] = (a / l * gate_ref[0]).astype(BF16)

    _attend([qp], [bnd_ref[i, h]],
            lambda r0: k_ref[0, 0, pl.ds(r0, MXU_TILE), :], v_ref, C_V, KV_CHUNK_BC, finish)


def _attn_c(bnd, q, k, v, gate):
    b = q.shape[0]
    nt = SEQ // V_CHUNK
    return pl.pallas_call(
        _attn_c_kernel,
        grid=(b, C_HEADS, SEQ // TQ_C),
        in_specs=[
            pl.BlockSpec(memory_space=pltpu.SMEM),
            pl.BlockSpec((1, C_QK, TQ_C), lambda i, h, t: (i, h, t)),
            pl.BlockSpec((1, 1, SEQ, K_PAD), lambda i, h, t: (i, h, 0, 0)),
            pl.BlockSpec((1, nt, C_V, V_CHUNK), lambda i, h, t: (i, 0, h, 0)),
            pl.BlockSpec((1, C_V, TQ_C), lambda i, h, t: (i, h, t)),
        ],
        out_specs=pl.BlockSpec((1, C_V, TQ_C), lambda i, h, t: (i, h, t)),
        out_shape=jax.ShapeDtypeStruct((b, C_HEADS * C_V, SEQ), BF16),
        compiler_params=_attn_params(),
        name="attn_mla",
    )(bnd, q, k, v, gate)


def _out_body(og_refs, x_ref, w_ref, lg_ref, lb_ref, x_token_major):
    og = jnp.concatenate([r[0] for r in og_refs], axis=0) if len(og_refs) > 1 else og_refs[0][0]
    y = jnp.dot(w_ref[...], og, preferred_element_type=F32)
    x = x_ref[0].T if x_token_major else x_ref[0]
    z = ALPHA * x + y
    mu = jnp.mean(z, axis=0, keepdims=True)
    d = z - mu
    var = jnp.mean(d * d, axis=0, keepdims=True)
    width = y.shape[1]
    return d * lax.rsqrt(var + LN_EPS) * _rep(lg_ref, width) + _rep(lb_ref, width)


def _out_kernel(*refs, n_og):
    x_ref, w_ref, lg_ref, lb_ref, o_ref = refs[n_og:]
    width = TM // SUB_TILES
    for t in range(SUB_TILES):
        off = t * width
        og_views = [_tokens(r, 2, off, width) for r in refs[:n_og]]
        out = _out_body(og_views, _tokens(x_ref, 2, off, width), w_ref, lg_ref, lb_ref, False)
        o_ref[0, pl.ds(off, width), :] = out.T


def _tokens(ref, axis, off, width):
    idx = [slice(None)] * len(ref.shape)
    if isinstance(axis, tuple):
        idx[axis[0]] = pl.ds(off // V_CHUNK, 1)
        idx[axis[1]] = pl.ds(off % V_CHUNK, width)
    else:
        idx[axis] = pl.ds(off, width)
    return ref.at[tuple(idx)]


def _out_in_kernel(*refs, n_og, n_in, x_token_major, in_body, tab_pos, out_axes):
    x_ref, w_ref, lg_ref, lb_ref = refs[n_og:n_og + 4]
    in_params = refs[n_og + 4:n_og + 4 + n_in]
    xo_ref = refs[n_og + 4 + n_in]
    in_outs = refs[n_og + 5 + n_in:]
    width = TM // SUB_TILES
    outs = []
    for t in range(SUB_TILES):
        off = t * width
        og_views = [_tokens(r, 2, off, width) for r in refs[:n_og]]
        x_view = _tokens(x_ref, 1 if x_token_major else 2, off, width)
        out = _out_body(og_views, x_view, w_ref, lg_ref, lb_ref, x_token_major)
        xo_ref[0, :, pl.ds(off, width)] = out
        outs.append(out.astype(BF16))
    for t in range(SUB_TILES):
        off = t * width
        params = [_tokens(r, 1, off, width) if p == tab_pos else r
                  for p, r in enumerate(in_params)]
        views = [_tokens(r, ax, off, width) for r, ax in zip(in_outs, out_axes)]
        in_body(outs[t], *params, *views)


def _out_specs(ogs, x_token_major):
    d = D_MODEL
    x_spec = (pl.BlockSpec((1, TM, d), lambda i, j: (i, j, 0)) if x_token_major
              else pl.BlockSpec((1, d, TM), _tok))
    return [pl.BlockSpec((1, og.shape[1], TM), _tok) for og in ogs] + [
        x_spec, _resident((d, d)), _resident((d, LANES)), _resident((d, LANES))]


def _out_proj_last(ogs, xt, w_t, lg, lb):
    b = xt.shape[0]
    return pl.pallas_call(
        functools.partial(_out_kernel, n_og=len(ogs)),
        grid=(b, SEQ // TM),
        in_specs=_out_specs(ogs, False),
        out_specs=pl.BlockSpec((1, TM, D_MODEL), lambda i, j: (i, j, 0)),
        out_shape=jax.ShapeDtypeStruct((b, SEQ, D_MODEL), F32),
        compiler_params=_proj_params(),
        name="out_proj_ln",
    )(*ogs, xt, w_t, lg, lb)


def _out_in(ogs, xt, w_t, lg, lb, next_even, in_args, x_token_major):
    b = xt.shape[0]
    if next_even:
        in_body, (in_specs, out_specs, out_shape) = _even_in_body, _even_in_specs(b)
        tab_pos, out_axes, name = EVEN_TAB_POS, EVEN_OUT_TOKEN_AXES, "out_even_in_proj"
    else:
        in_body, (in_specs, out_specs, out_shape) = _odd_in_body, _odd_in_specs(b)
        tab_pos, out_axes, name = ODD_TAB_POS, ODD_OUT_TOKEN_AXES, "out_odd_in_proj"
    return pl.pallas_call(
        functools.partial(_out_in_kernel, n_og=len(ogs), n_in=len(in_args),
                          x_token_major=x_token_major, in_body=in_body, tab_pos=tab_pos,
                          out_axes=out_axes),
        grid=(b, SEQ // TM),
        in_specs=_out_specs(ogs, x_token_major) + in_specs,
        out_specs=[pl.BlockSpec((1, D_MODEL, TM), _tok)] + out_specs,
        out_shape=[jax.ShapeDtypeStruct((b, D_MODEL, SEQ), F32)] + out_shape,
        compiler_params=_proj_params(),
        name=name,
    )(*ogs, xt, w_t, lg, lb, *in_args)


def _col(v):
    v = v.astype(F32)
    return jnp.broadcast_to(v[:, None], (v.shape[0], LANES))


def _score_bounds(nq, kmax2):
    return jnp.sqrt(jnp.max(nq, axis=-1) * kmax2) * BOUND_SLACK


def _angles_t(pos, dims, theta):
    inv = theta ** (-jnp.arange(0, dims, 2, dtype=F32) / dims)
    ang = pos.astype(F32)[:, None] * inv[None, :]
    return jnp.cos(ang).T, jnp.sin(ang).T


@jax.jit
def _forward(x, ev_w_in, ev_w_out, ev_lam, ev_subln, ev_qnorm, ev_knorm, ev_ln_g, ev_ln_b,
             od_w_in, od_qnorm, od_kvnorm, od_w_qb, od_w_kvb, od_w_out, od_ln_g, od_ln_b):
    s = x.shape[1]
    pos = jnp.arange(s, dtype=jnp.int32)
    row = jnp.repeat(jnp.arange(s // GRID_W, dtype=jnp.int32), GRID_W)
    col = jnp.tile(jnp.arange(GRID_W, dtype=jnp.int32), s // GRID_W)
    tab_ev = jnp.concatenate(
        _angles_t(pos, A_ROT, ROPE_THETA) + _angles_t(row, B_DIM // 2, AXIAL_THETA)
        + _angles_t(col, B_DIM // 2, AXIAL_THETA), axis=0)
    tab_od = jnp.concatenate(_angles_t(pos, C_ROPE, ROPE_THETA), axis=0)

    def even_args(i):
        return (ev_w_in[i].T.astype(BF16), tab_ev, _col(ev_qnorm[i]), _col(ev_knorm[i]))

    def odd_args(i):
        return (od_w_in[i].T.astype(BF16), od_w_qb[i].T.astype(BF16),
                od_w_kvb[i].T.astype(BF16), tab_od, _col(od_qnorm[i]), _col(od_kvnorm[i]))

    xt = x
    proj = _even_in(x, *even_args(0))
    for layer in range(DEPTH):
        i = layer // 2
        if layer % 2 == 0:
            qa, ka, va, qb, kb, vb, gate, nq, nk = proj
            kmax = jnp.max(nk, axis=-1)
            kmax = jnp.concatenate(
                [kmax[:, :2 * A_HEADS],
                 jnp.repeat(kmax[:, 2 * A_HEADS:2 * A_HEADS + B_KV_HEADS], B_GROUP, axis=1)],
                axis=1)
            bnd = _score_bounds(nq, kmax)
            lam_init = 0.8 - 0.6 * math.exp(-0.3 * layer)
            ogs = [_attn_a(bnd, ev_lam[i].astype(F32), qa, ka, va, gate, _col(ev_subln[i]),
                           lam_init),
                   _attn_b(bnd, qb, kb, vb, gate)]
            out_args = (ev_w_out[i].T.astype(BF16), _col(ev_ln_g[i]), _col(ev_ln_b[i]))
        else:
            q, k, v, gate, nq, nk = proj
            ogs = [_attn_c(_score_bounds(nq, jnp.max(nk, axis=-1)), q, k, v, gate)]
            out_args = (od_w_out[i].T.astype(BF16), _col(od_ln_g[i]), _col(od_ln_b[i]))
        if layer == DEPTH - 1:
            return _out_proj_last(ogs, xt, *out_args)
        nxt = (layer + 1) // 2
        if layer % 2 == 0:
            xt, *proj = _out_in(ogs, xt, *out_args, False, odd_args(nxt), layer == 0)
        else:
            xt, *proj = _out_in(ogs, xt, *out_args, True, even_args(nxt), False)


def kernel(x, ev_w_in, ev_w_out, ev_lam, ev_subln, ev_qnorm, ev_knorm, ev_ln_g, ev_ln_b,
           od_w_in, od_qnorm, od_kvnorm, od_w_qb, od_w_kvb, od_w_out, od_ln_g, od_ln_b):
    return _forward(x, ev_w_in, ev_w_out, ev_lam, ev_subln, ev_qnorm, ev_knorm, ev_ln_g,
                    ev_ln_b, od_w_in, od_qnorm, od_kvnorm, od_w_qb, od_w_kvb, od_w_out,
                    od_ln_g, od_ln_b)
```

```python
import functools
import math

import jax
import jax.numpy as jnp
from jax import lax
from jax.experimental import pallas as pl
from jax.experimental.pallas import tpu as pltpu

F32 = jnp.float32
BF16 = jnp.bfloat16

D_MODEL = 1024
SEQ = 4096
DEPTH = 4
GRID_W = 64
ROPE_THETA = 500000.0
AXIAL_THETA = 10000.0
LN_EPS = 1e-5
RMS_EPS = 1e-6

A_HEADS = 4
A_QK_DIM = 64
A_V_DIM = 128
A_WIDTH = 512
A_ROT = 16
B_Q_HEADS = 8
B_KV_HEADS = 2
B_GROUP = 4
B_DIM = 64
B_WIDTH = 512
EV_IN = 3328

C_HEADS = 16
C_NOPE = 64
C_ROPE = 32
C_V = 64
C_Q_LORA = 256
C_KV_LORA = 128
C_QK = C_NOPE + C_ROPE
OD_IN = 1440

ALPHA = (2 * DEPTH) ** 0.25
LOG2E = 1.4426950408889634

QSCALE_AB = A_QK_DIM ** -0.5 * LOG2E
QSCALE_C = C_QK ** -0.5 * LOG2E

LANES = 128
MXU_TILE = 256
LOOKAHEAD_ITEMS = {1: 2, 2: 3, 4: 8}
MAX_LOCKSTEP = 4
ONES_ROWS = 16
K_PAD = 128
NORM_ROWS = 16
SAFE_BOUND = 60.0
BOUND_SLACK = 1.02
TM = 1024
V_CHUNK = 512
SUB_TILES = 4
EVEN_TAB_POS, EVEN_OUT_TOKEN_AXES = 1, (2, 2, (1, 3), 2, 1, (1, 3), 2, 2, 2)
ODD_TAB_POS, ODD_OUT_TOKEN_AXES = 3, (2, 2, (1, 3), 2, 2, 2)
KV_CHUNK_A = 512
KV_CHUNK_BC = 256
TQ_A = 1024
TQ_B = 4096
TQ_C = 4096
VMEM_LIMIT = 56 * 1024 * 1024


def _rep(ref, n):
    a = ref[...]
    return jnp.concatenate([a] * (n // LANES), axis=1)


def _rot(x1, x2, cos, sin):
    return x1 * cos - x2 * sin, x2 * cos + x1 * sin


def _silu(x):
    return x * jax.nn.sigmoid(x)


def _sumsq(x):
    return jnp.sum(x * x, axis=0, keepdims=True)


def _tok(i, j):
    return (i, 0, j)


def _const2(i, j):
    return (0, 0)


def _resident(shape):
    return pl.BlockSpec(shape, _const2, pipeline_mode=pl.Buffered(1))


def _proj_params():
    return pltpu.CompilerParams(
        dimension_semantics=("parallel", "parallel"), vmem_limit_bytes=VMEM_LIMIT)


def _even_in_body(xb, w_ref, tab_ref, qn_ref, kn_ref,
                  qa_ref, ka_ref, va_ref, qb_ref, kb_ref, vb_ref, g_ref, nq_ref, nk_ref,
                  *, x_token_major=False):
    contract = (((1,), (1 if x_token_major else 0,)), ((), ()))

    def proj(r0, r1):
        return lax.dot_general(w_ref[r0:r1, :], xb, contract, preferred_element_type=F32)

    cos_a, sin_a = tab_ref[0:8, :], tab_ref[8:16, :]
    cos_r, sin_r = tab_ref[16:32, :], tab_ref[32:48, :]
    cos_c, sin_c = tab_ref[48:64, :], tab_ref[64:80, :]

    def rope_a(h):
        outs = []
        for hc in range(2 * A_HEADS):
            b = hc * A_QK_DIM
            r1, r2 = _rot(h[b:b + 8], h[b + 8:b + 16], cos_a, sin_a)
            outs.append(jnp.concatenate([r1, r2, h[b + 16:b + 64]], axis=0))
        return outs

    g_ref[0] = _silu(proj(2304, 3328)).astype(BF16)
    qs = [x * QSCALE_AB for x in rope_a(proj(0, 512))]
    qa_ref[0] = jnp.concatenate(qs, axis=0).astype(BF16)
    ks = rope_a(proj(512, 1024))
    nq = [_sumsq(x) for x in qs]
    nk = [_sumsq(x) for x in ks]
    for h in range(A_HEADS):
        kt = jnp.concatenate([ks[2 * h], ks[2 * h + 1]], axis=0)
        ka_ref[0, h] = kt.T.astype(BF16)

    def norm_axial(h, g):
        ms = jnp.mean(h * h, axis=0, keepdims=True)
        y = h * lax.rsqrt(ms + RMS_EPS) * g
        a1, a2 = _rot(y[0:16], y[16:32], cos_r, sin_r)
        b1, b2 = _rot(y[32:48], y[48:64], cos_c, sin_c)
        return jnp.concatenate([a1, a2, b1, b2], axis=0)

    width = xb.shape[0 if x_token_major else 1]
    qn = _rep(qn_ref, width)
    kn = _rep(kn_ref, width)
    hq = proj(1536, 2048)
    qs = [norm_axial(hq[h * 64:(h + 1) * 64], qn) * QSCALE_AB for h in range(B_Q_HEADS)]
    qb_ref[0] = jnp.concatenate(qs, axis=0).astype(BF16)
    hkv = proj(2048, 2304)
    ks = [norm_axial(hkv[h * 64:(h + 1) * 64], kn) for h in range(B_KV_HEADS)]
    kb_ref[0] = jnp.concatenate(ks, axis=0).T.astype(BF16)
    nq_ref[0] = jnp.concatenate(nq + [_sumsq(x) for x in qs], axis=0)
    nk = nk + [_sumsq(x) for x in ks]
    nk_ref[0] = jnp.concatenate(nk + [jnp.zeros_like(nk[0])] * (NORM_ROWS - len(nk)), axis=0)
    vb_ref[0, 0] = hkv[128:256].astype(BF16)
    va_ref[0, 0] = proj(1024, 1536).astype(BF16)


def _even_in_kernel(x_ref, *refs):
    n_in = len(refs) - len(EVEN_OUT_TOKEN_AXES)
    width = TM // SUB_TILES
    for t in range(SUB_TILES):
        off = t * width
        params = [_tokens(r, 1, off, width) if p == EVEN_TAB_POS else r
                  for p, r in enumerate(refs[:n_in])]
        views = [_tokens(r, ax, off, width) for r, ax in zip(refs[n_in:], EVEN_OUT_TOKEN_AXES)]
        _even_in_body(x_ref[0, pl.ds(off, width), :].astype(BF16), *params, *views,
                      x_token_major=True)


def _even_in_specs(b):
    d, s = D_MODEL, SEQ
    nt = s // TM
    tok = _tok
    chunk = lambda i, j: (i, j, 0, 0)
    return (
        [
            _resident((EV_IN, d)),
            pl.BlockSpec((80, TM), lambda i, j: (0, j)),
            _resident((B_DIM, LANES)),
            _resident((B_DIM, LANES)),
        ],
        [
            pl.BlockSpec((1, 512, TM), tok),
            pl.BlockSpec((1, A_HEADS, TM, K_PAD), lambda i, j: (i, 0, j, 0)),
            pl.BlockSpec((1, TM // V_CHUNK, 512, V_CHUNK), chunk),
            pl.BlockSpec((1, 512, TM), tok),
            pl.BlockSpec((1, TM, K_PAD), lambda i, j: (i, j, 0)),
            pl.BlockSpec((1, TM // V_CHUNK, 128, V_CHUNK), chunk),
            pl.BlockSpec((1, 1024, TM), tok),
            pl.BlockSpec((1, NORM_ROWS, TM), tok),
            pl.BlockSpec((1, NORM_ROWS, TM), tok),
        ],
        [
            jax.ShapeDtypeStruct((b, 512, s), BF16),
            jax.ShapeDtypeStruct((b, A_HEADS, s, K_PAD), BF16),
            jax.ShapeDtypeStruct((b, s // V_CHUNK, 512, V_CHUNK), BF16),
            jax.ShapeDtypeStruct((b, 512, s), BF16),
            jax.ShapeDtypeStruct((b, s, K_PAD), BF16),
            jax.ShapeDtypeStruct((b, s // V_CHUNK, 128, V_CHUNK), BF16),
            jax.ShapeDtypeStruct((b, 1024, s), BF16),
            jax.ShapeDtypeStruct((b, NORM_ROWS, s), F32),
            jax.ShapeDtypeStruct((b, NORM_ROWS, s), F32),
        ],
    )


def _even_in(x, w_t, tab, qn, kn):
    b = x.shape[0]
    in_specs, out_specs, out_shape = _even_in_specs(b)
    return pl.pallas_call(
        _even_in_kernel,
        grid=(b, SEQ // TM),
        in_specs=[pl.BlockSpec((1, TM, D_MODEL), lambda i, j: (i, j, 0))] + in_specs,
        out_specs=out_specs,
        out_shape=out_shape,
        compiler_params=_proj_params(),
        name="even_in_proj",
    )(x, w_t, tab, qn, kn)


def _odd_in_body(xb, w_ref, wq_ref, wkv_ref, tab_ref, qn_ref, kvn_ref,
                 q_ref, k_ref, v_ref, g_ref, nq_ref, nk_ref):
    cos, sin = tab_ref[0:16, :], tab_ref[16:32, :]

    def rms(h, g):
        ms = jnp.mean(h * h, axis=0, keepdims=True)
        return h * lax.rsqrt(ms + RMS_EPS) * g

    g_ref[0] = _silu(jnp.dot(w_ref[416:1440, :], xb, preferred_element_type=F32)).astype(BF16)
    lat = jnp.dot(w_ref[0:416, :], xb, preferred_element_type=F32)
    width = xb.shape[1]
    cqn = rms(lat[0:256], _rep(qn_ref, width)).astype(BF16)
    q = jnp.dot(wq_ref[...], cqn, preferred_element_type=F32)
    qs = []
    for h in range(C_HEADS):
        b = h * C_QK
        r1, r2 = _rot(q[b + 64:b + 80], q[b + 80:b + 96], cos, sin)
        qs.append(jnp.concatenate([q[b:b + 64], r1, r2], axis=0) * QSCALE_C)
    q_ref[0] = jnp.concatenate(qs, axis=0).astype(BF16)
    nq_ref[0] = jnp.concatenate([_sumsq(x) for x in qs], axis=0)

    ckvn = rms(lat[256:384], _rep(kvn_ref, width)).astype(BF16)
    kv = jnp.dot(wkv_ref[...], ckvn, preferred_element_type=F32)
    r1, r2 = _rot(lat[384:400], lat[400:416], cos, sin)
    zpad = jnp.zeros((K_PAD - C_QK, width), F32)
    nkr = _sumsq(r1) + _sumsq(r2)
    nk = []
    for h in range(C_HEADS):
        kn = kv[h * 128:h * 128 + 64]
        nk.append(_sumsq(kn) + nkr)
        kt = jnp.concatenate([kn, r1, r2, zpad], axis=0)
        k_ref[0, h] = kt.T.astype(BF16)
    nk_ref[0] = jnp.concatenate(nk, axis=0)
    v_ref[0, 0] = jnp.concatenate(
        [kv[h * 128 + 64:h * 128 + 128] for h in range(C_HEADS)], axis=0).astype(BF16)


def _odd_in_specs(b):
    d, s = D_MODEL, SEQ
    nt = s // TM
    tok = _tok
    return (
        [
            _resident((OD_IN, d)),
            _resident((C_HEADS * C_QK, C_Q_LORA)),
            _resident((C_HEADS * 128, C_KV_LORA)),
            pl.BlockSpec((32, TM), lambda i, j: (0, j)),
            _resident((C_Q_LORA, LANES)),
            _resident((C_KV_LORA, LANES)),
        ],
        [
            pl.BlockSpec((1, C_HEADS * C_QK, TM), tok),
            pl.BlockSpec((1, C_HEADS, TM, K_PAD), lambda i, j: (i, 0, j, 0)),
            pl.BlockSpec((1, TM // V_CHUNK, 1024, V_CHUNK), lambda i, j: (i, j, 0, 0)),
            pl.BlockSpec((1, 1024, TM), tok),
            pl.BlockSpec((1, NORM_ROWS, TM), tok),
            pl.BlockSpec((1, NORM_ROWS, TM), tok),
        ],
        [
            jax.ShapeDtypeStruct((b, C_HEADS * C_QK, s), BF16),
            jax.ShapeDtypeStruct((b, C_HEADS, s, K_PAD), BF16),
            jax.ShapeDtypeStruct((b, s // V_CHUNK, 1024, V_CHUNK), BF16),
            jax.ShapeDtypeStruct((b, 1024, s), BF16),
            jax.ShapeDtypeStruct((b, NORM_ROWS, s), F32),
            jax.ShapeDtypeStruct((b, NORM_ROWS, s), F32),
        ],
    )


def _softmax_pv(qps, k_at, v_ref, dv, kv_chunk, bounds=None):
    nsets = len(qps)
    ntiles = qps[0].shape[1] // MXU_TILE
    qps = [qp[:, t * MXU_TILE:(t + 1) * MXU_TILE] for qp in qps for t in range(ntiles)]
    n = len(qps)
    nchunks = SEQ // kv_chunk
    halves = kv_chunk // MXU_TILE
    ones = jnp.ones((ONES_ROWS, kv_chunk), BF16)

    def scores(ci, j):
        return [jnp.dot(k_at(ci * kv_chunk + h * MXU_TILE), qps[j], preferred_element_type=F32)
                for h in range(halves)]

    m = [None] * n
    acc = [None] * n
    lsum = [None] * n
    items = [(ci, j) for g in range(0, n, MAX_LOCKSTEP) for ci in range(nchunks)
             for j in range(g, min(g + MAX_LOCKSTEP, n))]
    lookahead = LOOKAHEAD_ITEMS[min(n, MAX_LOCKSTEP)]
    pending = [scores(*it) for it in items[:lookahead]]
    for idx, (ci, j) in enumerate(items):
        if idx + lookahead < len(items):
            pending.append(scores(*items[idx + lookahead]))
        sc = pending.pop(0)
        r0 = ci * kv_chunk
        v = v_ref[0, r0 // V_CHUNK, :, pl.ds(r0 % V_CHUNK, kv_chunk)]
        if bounds is None:
            v = jnp.concatenate([v, ones], axis=0)
            cm = functools.reduce(jnp.maximum, [jnp.max(s, axis=0, keepdims=True) for s in sc])
            ref = cm if ci == 0 else jnp.maximum(m[j], cm)
        else:
            ref = bounds[j // ntiles]
        ps = [jnp.exp2(s - ref) for s in sc]
        pv = functools.reduce(jnp.add, [
            jnp.dot(v[:, h * MXU_TILE:(h + 1) * MXU_TILE], ps[h].astype(BF16),
                    preferred_element_type=F32) for h in range(halves)])
        if bounds is not None:
            part = functools.reduce(jnp.add, [
                jnp.sum(p.reshape(MXU_TILE // 8, 8, MXU_TILE), axis=0) for p in ps])
            lsum[j] = part if ci == 0 else lsum[j] + part
        if ci == 0:
            acc[j] = pv
        elif bounds is None:
            acc[j] = acc[j] * jnp.exp2(m[j] - ref) + pv
        else:
            acc[j] = acc[j] + pv
        m[j] = ref
    out = []
    for j in range(nsets):
        a = jnp.concatenate(acc[j * ntiles:(j + 1) * ntiles], axis=1)
        if bounds is None:
            out.append((a[0:dv], a[dv:dv + 1]))
        else:
            l8 = jnp.concatenate(lsum[j * ntiles:(j + 1) * ntiles], axis=1)
            out.append((a, jnp.sum(l8, axis=0, keepdims=True)))
    return out


def _attend(qps, bounds, k_at, v_ref, dv, kv_chunk, finish):
    safe = functools.reduce(jnp.logical_and, [b <= SAFE_BOUND for b in bounds])

    @pl.when(safe)
    def _():
        finish(_softmax_pv(qps, k_at, v_ref, dv, kv_chunk, bounds))

    @pl.when(jnp.logical_not(safe))
    def _():
        finish(_softmax_pv(qps, k_at, v_ref, dv, kv_chunk))


def _attn_params():
    return pltpu.CompilerParams(
        dimension_semantics=("parallel", "parallel", "parallel"), vmem_limit_bytes=VMEM_LIMIT)


def _attn_a_kernel(bnd_ref, lam_ref, q_ref, k_ref, v_ref, gate_ref, sg_ref, o_ref, *, lam_init):
    i, h = pl.program_id(0), pl.program_id(1)
    q = q_ref[0]
    tq = q.shape[1]
    z = jnp.zeros((A_QK_DIM, tq), BF16)
    qps = [jnp.concatenate([q[0:64], z], axis=0), jnp.concatenate([z, q[64:128]], axis=0)]

    def finish(res):
        (a0, l0), (a1, l1) = res
        lp = lam_ref[...]
        lam = (jnp.exp(jnp.sum(lp[0:1] * lp[1:2], axis=1, keepdims=True))
               - jnp.exp(jnp.sum(lp[2:3] * lp[3:4], axis=1, keepdims=True)) + lam_init)
        o = a0 * (1.0 / l0) - a1 * (lam * (1.0 / l1))
        ms = jnp.mean(o * o, axis=0, keepdims=True)
        o = o * lax.rsqrt(ms + RMS_EPS) * _rep(sg_ref, tq) * (1.0 - lam_init)
        o_ref[0] = (o * gate_ref[0]).astype(BF16)

    _attend(qps, [bnd_ref[i, 2 * h], bnd_ref[i, 2 * h + 1]],
            lambda r0: k_ref[0, 0, pl.ds(r0, MXU_TILE), :], v_ref, A_V_DIM, KV_CHUNK_A, finish)


def _attn_a(bnd, lam_p, qa, ka, va, gate, sg, lam_init):
    b = qa.shape[0]
    nt = SEQ // V_CHUNK
    return pl.pallas_call(
        functools.partial(_attn_a_kernel, lam_init=lam_init),
        grid=(b, A_HEADS, SEQ // TQ_A),
        in_specs=[
            pl.BlockSpec(memory_space=pltpu.SMEM),
            pl.BlockSpec((4, A_QK_DIM), lambda i, h, t: (0, 0)),
            pl.BlockSpec((1, 128, TQ_A), lambda i, h, t: (i, h, t)),
            pl.BlockSpec((1, 1, SEQ, K_PAD), lambda i, h, t: (i, h, 0, 0)),
            pl.BlockSpec((1, nt, A_V_DIM, V_CHUNK), lambda i, h, t: (i, 0, h, 0)),
            pl.BlockSpec((1, A_V_DIM, TQ_A), lambda i, h, t: (i, h, t)),
            pl.BlockSpec((A_V_DIM, LANES), lambda i, h, t: (0, 0)),
        ],
        out_specs=pl.BlockSpec((1, A_V_DIM, TQ_A), lambda i, h, t: (i, h, t)),
        out_shape=jax.ShapeDtypeStruct((b, A_WIDTH, SEQ), BF16),
        compiler_params=_attn_params(),
        name="attn_diff",
    )(bnd, lam_p, qa, ka, va, gate, sg)


def _attn_b_kernel(bnd_ref, q_ref, k_ref, v_ref, gate_ref, o_ref):
    i, h = pl.program_id(0), pl.program_id(1)
    first = h < B_GROUP
    q = q_ref[0]
    z = jnp.zeros_like(q)
    qp = jnp.concatenate([jnp.where(first, q, z), jnp.where(first, z, q)], axis=0)

    def finish(res):
        ((a, l),) = res
        o_ref[0] = (a * (1.0 / l) * gate_ref[0]).astype(BF16)

    _attend([qp], [bnd_ref[i, 2 * A_HEADS + h]],
            lambda r0: k_ref[0, pl.ds(r0, MXU_TILE), :], v_ref, B_DIM, KV_CHUNK_BC, finish)


def _attn_b(bnd, qb, kb, vb, gate):
    b = qb.shape[0]
    nt = SEQ // V_CHUNK
    return pl.pallas_call(
        _attn_b_kernel,
        grid=(b, B_Q_HEADS, SEQ // TQ_B),
        in_specs=[
            pl.BlockSpec(memory_space=pltpu.SMEM),
            pl.BlockSpec((1, B_DIM, TQ_B), lambda i, h, t: (i, h, t)),
            pl.BlockSpec((1, SEQ, K_PAD), lambda i, h, t: (i, 0, 0)),
            pl.BlockSpec((1, nt, B_DIM, V_CHUNK), lambda i, h, t: (i, 0, h // B_GROUP, 0)),
            pl.BlockSpec((1, B_DIM, TQ_B), lambda i, h, t: (i, A_WIDTH // B_DIM + h, t)),
        ],
        out_specs=pl.BlockSpec((1, B_DIM, TQ_B), lambda i, h, t: (i, h, t)),
        out_shape=jax.ShapeDtypeStruct((b, B_WIDTH, SEQ), BF16),
        compiler_params=_attn_params(),
        name="attn_gqa",
    )(bnd, qb, kb, vb, gate)


def _attn_c_kernel(bnd_ref, q_ref, k_ref, v_ref, gate_ref, o_ref):
    i, h = pl.program_id(0), pl.program_id(1)
    q = q_ref[0]
    qp = jnp.concatenate([q, jnp.zeros((K_PAD - C_QK, q.shape[1]), BF16)], axis=0)

    def finish(res):
        ((a, l),) = res
        o_ref[0] = (a * (1.0 / l) * gate_ref[0]).astype(BF16)

    _attend([qp], [bnd_ref[i, h]],
            lambda r0: k_ref[0, 0, pl.ds(r0, MXU_TILE), :], v_ref, C_V, KV_CHUNK_BC, finish)


def _attn_c(bnd, q, k, v, gate):
    b = q.shape[0]
    nt = SEQ // V_CHUNK
    return pl.pallas_call(
        _attn_c_kernel,
        grid=(b, C_HEADS, SEQ // TQ_C),
        in_specs=[
            pl.BlockSpec(memory_space=pltpu.SMEM),
            pl.BlockSpec((1, C_QK, TQ_C), lambda i, h, t: (i, h, t)),
            pl.BlockSpec((1, 1, SEQ, K_PAD), lambda i, h, t: (i, h, 0, 0)),
            pl.BlockSpec((1, nt, C_V, V_CHUNK), lambda i, h, t: (i, 0, h, 0)),
            pl.BlockSpec((1, C_V, TQ_C), lambda i, h, t: (i, h, t)),
        ],
        out_specs=pl.BlockSpec((1, C_V, TQ_C), lambda i, h, t: (i, h, t)),
        out_shape=jax.ShapeDtypeStruct((b, C_HEADS * C_V, SEQ), BF16),
        compiler_params=_attn_params(),
        name="attn_mla",
    )(bnd, q, k, v, gate)


def _out_body(og_refs, x_ref, w_ref, lg_ref, lb_ref, x_token_major):
    og = jnp.concatenate([r[0] for r in og_refs], axis=0) if len(og_refs) > 1 else og_refs[0][0]
    y = jnp.dot(w_ref[...], og, preferred_element_type=F32)
    x = x_ref[0].T if x_token_major else x_ref[0]
    z = ALPHA * x + y
    mu = jnp.mean(z, axis=0, keepdims=True)
    d = z - mu
    var = jnp.mean(d * d, axis=0, keepdims=True)
    width = y.shape[1]
    return d * lax.rsqrt(var + LN_EPS) * _rep(lg_ref, width) + _rep(lb_ref, width)


def _out_kernel(*refs, n_og):
    x_ref, w_ref, lg_ref, lb_ref, o_ref = refs[n_og:]
    width = TM // SUB_TILES
    for t in range(SUB_TILES):
        off = t * width
        og_views = [_tokens(r, 2, off, width) for r in refs[:n_og]]
        out = _out_body(og_views, _tokens(x_ref, 2, off, width), w_ref, lg_ref, lb_ref, False)
        o_ref[0, pl.ds(off, width), :] = out.T


def _tokens(ref, axis, off, width):
    idx = [slice(None)] * len(ref.shape)
    if isinstance(axis, tuple):
        idx[axis[0]] = pl.ds(off // V_CHUNK, 1)
        idx[axis[1]] = pl.ds(off % V_CHUNK, width)
    else:
        idx[axis] = pl.ds(off, width)
    return ref.at[tuple(idx)]


def _out_in_kernel(*refs, n_og, n_in, x_token_major, in_body, tab_pos, out_axes):
    x_ref, w_ref, lg_ref, lb_ref = refs[n_og:n_og + 4]
    in_params = refs[n_og + 4:n_og + 4 + n_in]
    xo_ref = refs[n_og + 4 + n_in]
    in_outs = refs[n_og + 5 + n_in:]
    width = TM // SUB_TILES
    outs = []
    for t in range(SUB_TILES):
        off = t * width
        og_views = [_tokens(r, 2, off, width) for r in refs[:n_og]]
        x_view = _tokens(x_ref, 1 if x_token_major else 2, off, width)
        out = _out_body(og_views, x_view, w_ref, lg_ref, lb_ref, x_token_major)
        xo_ref[0, :, pl.ds(off, width)] = out
        outs.append(out.astype(BF16))
    for t in range(SUB_TILES):
        off = t * width
        params = [_tokens(r, 1, off, width) if p == tab_pos else r
                  for p, r in enumerate(in_params)]
        views = [_tokens(r, ax, off, width) for r, ax in zip(in_outs, out_axes)]
        in_body(outs[t], *params, *views)


def _out_specs(ogs, x_token_major):
    d = D_MODEL
    x_spec = (pl.BlockSpec((1, TM, d), lambda i, j: (i, j, 0)) if x_token_major
              else pl.BlockSpec((1, d, TM), _tok))
    return [pl.BlockSpec((1, og.shape[1], TM), _tok) for og in ogs] + [
        x_spec, _resident((d, d)), _resident((d, LANES)), _resident((d, LANES))]


def _out_proj_last(ogs, xt, w_t, lg, lb):
    b = xt.shape[0]
    return pl.pallas_call(
        functools.partial(_out_kernel, n_og=len(ogs)),
        grid=(b, SEQ // TM),
        in_specs=_out_specs(ogs, False),
        out_specs=pl.BlockSpec((1, TM, D_MODEL), lambda i, j: (i, j, 0)),
        out_shape=jax.ShapeDtypeStruct((b, SEQ, D_MODEL), F32),
        compiler_params=_proj_params(),
        name="out_proj_ln",
    )(*ogs, xt, w_t, lg, lb)


def _out_in(ogs, xt, w_t, lg, lb, next_even, in_args, x_token_major):
    b = xt.shape[0]
    if next_even:
        in_body, (in_specs, out_specs, out_shape) = _even_in_body, _even_in_specs(b)
        tab_pos, out_axes, name = EVEN_TAB_POS, EVEN_OUT_TOKEN_AXES, "out_even_in_proj"
    else:
        in_body, (in_specs, out_specs, out_shape) = _odd_in_body, _odd_in_specs(b)
        tab_pos, out_axes, name = ODD_TAB_POS, ODD_OUT_TOKEN_AXES, "out_odd_in_proj"
    return pl.pallas_call(
        functools.partial(_out_in_kernel, n_og=len(ogs), n_in=len(in_args),
                          x_token_major=x_token_major, in_body=in_body, tab_pos=tab_pos,
                          out_axes=out_axes),
        grid=(b, SEQ // TM),
        in_specs=_out_specs(ogs, x_token_major) + in_specs,
        out_specs=[pl.BlockSpec((1, D_MODEL, TM), _tok)] + out_specs,
        out_shape=[jax.ShapeDtypeStruct((b, D_MODEL, SEQ), F32)] + out_shape,
        compiler_params=_proj_params(),
        name=name,
    )(*ogs, xt, w_t, lg, lb, *in_args)


def _col(v):
    v = v.astype(F32)
    return jnp.broadcast_to(v[:, None], (v.shape[0], LANES))


def _score_bounds(nq, kmax2):
    return jnp.sqrt(jnp.max(nq, axis=-1) * kmax2) * BOUND_SLACK


def _angles_t(pos, dims, theta):
    inv = theta ** (-jnp.arange(0, dims, 2, dtype=F32) / dims)
    ang = pos.astype(F32)[:, None] * inv[None, :]
    return jnp.cos(ang).T, jnp.sin(ang).T


@jax.jit
def _forward(x, ev_w_in, ev_w_out, ev_lam, ev_subln, ev_qnorm, ev_knorm, ev_ln_g, ev_ln_b,
             od_w_in, od_qnorm, od_kvnorm, od_w_qb, od_w_kvb, od_w_out, od_ln_g, od_ln_b):
    s = x.shape[1]
    pos = jnp.arange(s, dtype=jnp.int32)
    row = jnp.repeat(jnp.arange(s // GRID_W, dtype=jnp.int32), GRID_W)
    col = jnp.tile(jnp.arange(GRID_W, dtype=jnp.int32), s // GRID_W)
    tab_ev = jnp.concatenate(
        _angles_t(pos, A_ROT, ROPE_THETA) + _angles_t(row, B_DIM // 2, AXIAL_THETA)
        + _angles_t(col, B_DIM // 2, AXIAL_THETA), axis=0)
    tab_od = jnp.concatenate(_angles_t(pos, C_ROPE, ROPE_THETA), axis=0)

    def even_args(i):
        return (ev_w_in[i].T.astype(BF16), tab_ev, _col(ev_qnorm[i]), _col(ev_knorm[i]))

    def odd_args(i):
        return (od_w_in[i].T.astype(BF16), od_w_qb[i].T.astype(BF16),
                od_w_kvb[i].T.astype(BF16), tab_od, _col(od_qnorm[i]), _col(od_kvnorm[i]))

    xt = x
    proj = _even_in(x, *even_args(0))
    for layer in range(DEPTH):
        i = layer // 2
        if layer % 2 == 0:
            qa, ka, va, qb, kb, vb, gate, nq, nk = proj
            kmax = jnp.max(nk, axis=-1)
            kmax = jnp.concatenate(
                [kmax[:, :2 * A_HEADS],
                 jnp.repeat(kmax[:, 2 * A_HEADS:2 * A_HEADS + B_KV_HEADS], B_GROUP, axis=1)],
                axis=1)
            bnd = _score_bounds(nq, kmax)
            lam_init = 0.8 - 0.6 * math.exp(-0.3 * layer)
            ogs = [_attn_a(bnd, ev_lam[i].astype(F32), qa, ka, va, gate, _col(ev_subln[i]),
                           lam_init),
                   _attn_b(bnd, qb, kb, vb, gate)]
            out_args = (ev_w_out[i].T.astype(BF16), _col(ev_ln_g[i]), _col(ev_ln_b[i]))
        else:
            q, k, v, gate, nq, nk = proj
            ogs = [_attn_c(_score_bounds(nq, jnp.max(nk, axis=-1)), q, k, v, gate)]
            out_args = (od_w_out[i].T.astype(BF16), _col(od_ln_g[i]), _col(od_ln_b[i]))
        if layer == DEPTH - 1:
            return _out_proj_last(ogs, xt, *out_args)
        nxt = (layer + 1) // 2
        if layer % 2 == 0:
            xt, *proj = _out_in(ogs, xt, *out_args, False, odd_args(nxt), layer == 0)
        else:
            xt, *proj = _out_in(ogs, xt, *out_args, True, even_args(nxt), False)


def kernel(x, ev_w_in, ev_w_out, ev_lam, ev_subln, ev_qnorm, ev_knorm, ev_ln_g, ev_ln_b,
           od_w_in, od_qnorm, od_kvnorm, od_w_qb, od_w_kvb, od_w_out, od_ln_g, od_ln_b):
    return _forward(x, ev_w_in, ev_w_out, ev_lam, ev_subln, ev_qnorm, ev_knorm, ev_ln_g,
                    ev_ln_b, od_w_in, od_qnorm, od_kvnorm, od_w_qb, od_w_kvb, od_w_out,
                    od_ln_g, od_ln_b)
```

```python
import functools
import math

import jax
import jax.numpy as jnp
from jax import lax
from jax.experimental import pallas as pl
from jax.experimental.pallas import tpu as pltpu

F32 = jnp.float32
BF16 = jnp.bfloat16

D_MODEL = 1024
SEQ = 4096
DEPTH = 4
GRID_W = 64
ROPE_THETA = 500000.0
AXIAL_THETA = 10000.0
LN_EPS = 1e-5
RMS_EPS = 1e-6

A_HEADS = 4
A_QK_DIM = 64
A_V_DIM = 128
A_WIDTH = 512
A_ROT = 16
B_Q_HEADS = 8
B_KV_HEADS = 2
B_GROUP = 4
B_DIM = 64
B_WIDTH = 512
EV_IN = 3328

C_HEADS = 16
C_NOPE = 64
C_ROPE = 32
C_V = 64
C_Q_LORA = 256
C_KV_LORA = 128
C_QK = C_NOPE + C_ROPE
OD_IN = 1440

ALPHA = (2 * DEPTH) ** 0.25
LOG2E = 1.4426950408889634

QSCALE_AB = A_QK_DIM ** -0.5 * LOG2E
QSCALE_C = C_QK ** -0.5 * LOG2E

LANES = 128
MXU_TILE = 256
LOOKAHEAD_ITEMS = {1: 2, 2: 3, 4: 8}
MAX_LOCKSTEP = 4
ONES_ROWS = 16
K_PAD = 128
NORM_ROWS = 16
SAFE_BOUND = 60.0
BOUND_SLACK = 1.02
TM = 1024
V_CHUNK = 512
SUB_TILES = 4
EVEN_TAB_POS, EVEN_OUT_TOKEN_AXES = 1, (2, 2, (1, 3), 2, 1, (1, 3), 2, 2, 2)
ODD_TAB_POS, ODD_OUT_TOKEN_AXES = 3, (2, 2, (1, 3), 2, 2, 2)
KV_CHUNK_A = 512
KV_CHUNK_BC = 256
TQ_A = 1024
TQ_B = 4096
TQ_C = 4096
VMEM_LIMIT = 56 * 1024 * 1024


def _rep(ref, n):
    a = ref[...]
    return jnp.concatenate([a] * (n // LANES), axis=1)


def _rot(x1, x2, cos, sin):
    return x1 * cos - x2 * sin, x2 * cos + x1 * sin


def _silu(x):
    return x * jax.nn.sigmoid(x)


def _sumsq(x):
    return jnp.sum(x * x, axis=0, keepdims=True)


def _tok(i, j):
    return (i, 0, j)


def _const2(i, j):
    return (0, 0)


def _resident(shape):
    return pl.BlockSpec(shape, _const2, pipeline_mode=pl.Buffered(1))


def _proj_params():
    return pltpu.CompilerParams(
        dimension_semantics=("parallel", "parallel"), vmem_limit_bytes=VMEM_LIMIT)


def _even_in_body(xb, w_ref, tab_ref, qn_ref, kn_ref,
                  qa_ref, ka_ref, va_ref, qb_ref, kb_ref, vb_ref, g_ref, nq_ref, nk_ref,
                  *, x_token_major=False):
    contract = (((1,), (1 if x_token_major else 0,)), ((), ()))

    def proj(r0, r1):
        return lax.dot_general(w_ref[r0:r1, :], xb, contract, preferred_element_type=F32)

    cos_a, sin_a = tab_ref[0:8, :], tab_ref[8:16, :]
    cos_r, sin_r = tab_ref[16:32, :], tab_ref[32:48, :]
    cos_c, sin_c = tab_ref[48:64, :], tab_ref[64:80, :]

    def rope_a(h):
        outs = []
        for hc in range(2 * A_HEADS):
            b = hc * A_QK_DIM
            r1, r2 = _rot(h[b:b + 8], h[b + 8:b + 16], cos_a, sin_a)
            outs.append(jnp.concatenate([r1, r2, h[b + 16:b + 64]], axis=0))
        return outs

    g_ref[0] = _silu(proj(2304, 3328)).astype(BF16)
    qs = [x * QSCALE_AB for x in rope_a(proj(0, 512))]
    qa_ref[0] = jnp.concatenate(qs, axis=0).astype(BF16)
    ks = rope_a(proj(512, 1024))
    nq = [_sumsq(x) for x in qs]
    nk = [_sumsq(x) for x in ks]
    for h in range(A_HEADS):
        kt = jnp.concatenate([ks[2 * h], ks[2 * h + 1]], axis=0)
        ka_ref[0, h] = kt.T.astype(BF16)

    def norm_axial(h, g):
        ms = jnp.mean(h * h, axis=0, keepdims=True)
        y = h * lax.rsqrt(ms + RMS_EPS) * g
        a1, a2 = _rot(y[0:16], y[16:32], cos_r, sin_r)
        b1, b2 = _rot(y[32:48], y[48:64], cos_c, sin_c)
        return jnp.concatenate([a1, a2, b1, b2], axis=0)

    width = xb.shape[0 if x_token_major else 1]
    qn = _rep(qn_ref, width)
    kn = _rep(kn_ref, width)
    hq = proj(1536, 2048)
    qs = [norm_axial(hq[h * 64:(h + 1) * 64], qn) * QSCALE_AB for h in range(B_Q_HEADS)]
    qb_ref[0] = jnp.concatenate(qs, axis=0).astype(BF16)
    hkv = proj(2048, 2304)
    ks = [norm_axial(hkv[h * 64:(h + 1) * 64], kn) for h in range(B_KV_HEADS)]
    kb_ref[0] = jnp.concatenate(ks, axis=0).T.astype(BF16)
    nq_ref[0] = jnp.concatenate(nq + [_sumsq(x) for x in qs], axis=0)
    nk = nk + [_sumsq(x) for x in ks]
    nk_ref[0] = jnp.concatenate(nk + [jnp.zeros_like(nk[0])] * (NORM_ROWS - len(nk)), axis=0)
    vb_ref[0, 0] = hkv[128:256].astype(BF16)
    va_ref[0, 0] = proj(1024, 1536).astype(BF16)


def _even_in_kernel(x_ref, *refs):
    n_in = len(refs) - len(EVEN_OUT_TOKEN_AXES)
    width = TM // SUB_TILES
    for t in range(SUB_TILES):
        off = t * width
        params = [_tokens(r, 1, off, width) if p == EVEN_TAB_POS else r
                  for p, r in enumerate(refs[:n_in])]
        views = [_tokens(r, ax, off, width) for r, ax in zip(refs[n_in:], EVEN_OUT_TOKEN_AXES)]
        _even_in_body(x_ref[0, pl.ds(off, width), :].astype(BF16), *params, *views,
                      x_token_major=True)


def _even_in_specs(b):
    d, s = D_MODEL, SEQ
    nt = s // TM
    tok = _tok
    chunk = lambda i, j: (i, j, 0, 0)
    return (
        [
            _resident((EV_IN, d)),
            pl.BlockSpec((80, TM), lambda i, j: (0, j)),
            _resident((B_DIM, LANES)),
            _resident((B_DIM, LANES)),
        ],
        [
            pl.BlockSpec((1, 512, TM), tok),
            pl.BlockSpec((1, A_HEADS, TM, K_PAD), lambda i, j: (i, 0, j, 0)),
            pl.BlockSpec((1, TM // V_CHUNK, 512, V_CHUNK), chunk),
            pl.BlockSpec((1, 512, TM), tok),
            pl.BlockSpec((1, TM, K_PAD), lambda i, j: (i, j, 0)),
            pl.BlockSpec((1, TM // V_CHUNK, 128, V_CHUNK), chunk),
            pl.BlockSpec((1, 1024, TM), tok),
            pl.BlockSpec((1, NORM_ROWS, TM), tok),
            pl.BlockSpec((1, NORM_ROWS, TM), tok),
        ],
        [
            jax.ShapeDtypeStruct((b, 512, s), BF16),
            jax.ShapeDtypeStruct((b, A_HEADS, s, K_PAD), BF16),
            jax.ShapeDtypeStruct((b, s // V_CHUNK, 512, V_CHUNK), BF16),
            jax.ShapeDtypeStruct((b, 512, s), BF16),
            jax.ShapeDtypeStruct((b, s, K_PAD), BF16),
            jax.ShapeDtypeStruct((b, s // V_CHUNK, 128, V_CHUNK), BF16),
            jax.ShapeDtypeStruct((b, 1024, s), BF16),
            jax.ShapeDtypeStruct((b, NORM_ROWS, s), F32),
            jax.ShapeDtypeStruct((b, NORM_ROWS, s), F32),
        ],
    )


def _even_in(x, w_t, tab, qn, kn):
    b = x.shape[0]
    in_specs, out_specs, out_shape = _even_in_specs(b)
    return pl.pallas_call(
        _even_in_kernel,
        grid=(b, SEQ // TM),
        in_specs=[pl.BlockSpec((1, TM, D_MODEL), lambda i, j: (i, j, 0))] + in_specs,
        out_specs=out_specs,
        out_shape=out_shape,
        compiler_params=_proj_params(),
        name="even_in_proj",
    )(x, w_t, tab, qn, kn)


def _odd_in_body(xb, w_ref, wq_ref, wkv_ref, tab_ref, qn_ref, kvn_ref,
                 q_ref, k_ref, v_ref, g_ref, nq_ref, nk_ref):
    cos, sin = tab_ref[0:16, :], tab_ref[16:32, :]

    def rms(h, g):
        ms = jnp.mean(h * h, axis=0, keepdims=True)
        return h * lax.rsqrt(ms + RMS_EPS) * g

    g_ref[0] = _silu(jnp.dot(w_ref[416:1440, :], xb, preferred_element_type=F32)).astype(BF16)
    lat = jnp.dot(w_ref[0:416, :], xb, preferred_element_type=F32)
    width = xb.shape[1]
    cqn = rms(lat[0:256], _rep(qn_ref, width)).astype(BF16)
    q = jnp.dot(wq_ref[...], cqn, preferred_element_type=F32)
    qs = []
    for h in range(C_HEADS):
        b = h * C_QK
        r1, r2 = _rot(q[b + 64:b + 80], q[b + 80:b + 96], cos, sin)
        qs.append(jnp.concatenate([q[b:b + 64], r1, r2], axis=0) * QSCALE_C)
    q_ref[0] = jnp.concatenate(qs, axis=0).astype(BF16)
    nq_ref[0] = jnp.concatenate([_sumsq(x) for x in qs], axis=0)

    ckvn = rms(lat[256:384], _rep(kvn_ref, width)).astype(BF16)
    kv = jnp.dot(wkv_ref[...], ckvn, preferred_element_type=F32)
    r1, r2 = _rot(lat[384:400], lat[400:416], cos, sin)
    zpad = jnp.zeros((K_PAD - C_QK, width), F32)
    nkr = _sumsq(r1) + _sumsq(r2)
    nk = []
    for h in range(C_HEADS):
        kn = kv[h * 128:h * 128 + 64]
        nk.append(_sumsq(kn) + nkr)
        kt = jnp.concatenate([kn, r1, r2, zpad], axis=0)
        k_ref[0, h] = kt.T.astype(BF16)
    nk_ref[0] = jnp.concatenate(nk, axis=0)
    v_ref[0, 0] = jnp.concatenate(
        [kv[h * 128 + 64:h * 128 + 128] for h in range(C_HEADS)], axis=0).astype(BF16)


def _odd_in_specs(b):
    d, s = D_MODEL, SEQ
    nt = s // TM
    tok = _tok
    return (
        [
            _resident((OD_IN, d)),
            _resident((C_HEADS * C_QK, C_Q_LORA)),
            _resident((C_HEADS * 128, C_KV_LORA)),
            pl.BlockSpec((32, TM), lambda i, j: (0, j)),
            _resident((C_Q_LORA, LANES)),
            _resident((C_KV_LORA, LANES)),
        ],
        [
            pl.BlockSpec((1, C_HEADS * C_QK, TM), tok),
            pl.BlockSpec((1, C_HEADS, TM, K_PAD), lambda i, j: (i, 0, j, 0)),
            pl.BlockSpec((1, TM // V_CHUNK, 1024, V_CHUNK), lambda i, j: (i, j, 0, 0)),
            pl.BlockSpec((1, 1024, TM), tok),
            pl.BlockSpec((1, NORM_ROWS, TM), tok),
            pl.BlockSpec((1, NORM_ROWS, TM), tok),
        ],
        [
            jax.ShapeDtypeStruct((b, C_HEADS * C_QK, s), BF16),
            jax.ShapeDtypeStruct((b, C_HEADS, s, K_PAD), BF16),
            jax.ShapeDtypeStruct((b, s // V_CHUNK, 1024, V_CHUNK), BF16),
            jax.ShapeDtypeStruct((b, 1024, s), BF16),
            jax.ShapeDtypeStruct((b, NORM_ROWS, s), F32),
            jax.ShapeDtypeStruct((b, NORM_ROWS, s), F32),
        ],
    )


def _softmax_pv(qps, k_at, v_ref, dv, kv_chunk, bounds=None):
    nsets = len(qps)
    ntiles = qps[0].shape[1] // MXU_TILE
    qps = [qp[:, t * MXU_TILE:(t + 1) * MXU_TILE] for qp in qps for t in range(ntiles)]
    n = len(qps)
    nchunks = SEQ // kv_chunk
    halves = kv_chunk // MXU_TILE
    ones = jnp.ones((ONES_ROWS, kv_chunk), BF16)

    def scores(ci, j):
        return [jnp.dot(k_at(ci * kv_chunk + h * MXU_TILE), qps[j], preferred_element_type=F32)
                for h in range(halves)]

    m = [None] * n
    acc = [None] * n
    lsum = [None] * n
    items = [(ci, j) for g in range(0, n, MAX_LOCKSTEP) for ci in range(nchunks)
             for j in range(g, min(g + MAX_LOCKSTEP, n))]
    lookahead = LOOKAHEAD_ITEMS[min(n, MAX_LOCKSTEP)] * MXU_TILE // kv_chunk
    pending = [scores(*it) for it in items[:lookahead]]
    for idx, (ci, j) in enumerate(items):
        if idx + lookahead < len(items):
            pending.append(scores(*items[idx + lookahead]))
        sc = pending.pop(0)
        r0 = ci * kv_chunk
        v = v_ref[0, r0 // V_CHUNK, :, pl.ds(r0 % V_CHUNK, kv_chunk)]
        if bounds is None:
            v = jnp.concatenate([v, ones], axis=0)
            cm = functools.reduce(jnp.maximum, [jnp.max(s, axis=0, keepdims=True) for s in sc])
            ref = cm if ci == 0 else jnp.maximum(m[j], cm)
        else:
            ref = bounds[j // ntiles]
        ps = [jnp.exp2(s - ref) for s in sc]
        pv = functools.reduce(jnp.add, [
            jnp.dot(v[:, h * MXU_TILE:(h + 1) * MXU_TILE], ps[h].astype(BF16),
                    preferred_element_type=F32) for h in range(halves)])
        if bounds is not None:
            part = functools.reduce(jnp.add, [
                jnp.sum(p.reshape(MXU_TILE // 8, 8, MXU_TILE), axis=0) for p in ps])
            lsum[j] = part if ci == 0 else lsum[j] + part
        if ci == 0:
            acc[j] = pv
        elif bounds is None:
            acc[j] = acc[j] * jnp.exp2(m[j] - ref) + pv
        else:
            acc[j] = acc[j] + pv
        m[j] = ref
    out = []
    for j in range(nsets):
        a = jnp.concatenate(acc[j * ntiles:(j + 1) * ntiles], axis=1)
        if bounds is None:
            out.append((a[0:dv], a[dv:dv + 1]))
        else:
            l8 = jnp.concatenate(lsum[j * ntiles:(j + 1) * ntiles], axis=1)
            out.append((a, jnp.sum(l8, axis=0, keepdims=True)))
    return out


def _attend(qps, bounds, k_at, v_ref, dv, kv_chunk, finish):
    safe = functools.reduce(jnp.logical_and, [b <= SAFE_BOUND for b in bounds])

    @pl.when(safe)
    def _():
        finish(_softmax_pv(qps, k_at, v_ref, dv, kv_chunk, bounds))

    @pl.when(jnp.logical_not(safe))
    def _():
        finish(_softmax_pv(qps, k_at, v_ref, dv, kv_chunk))


def _attn_params():
    return pltpu.CompilerParams(
        dimension_semantics=("parallel", "parallel", "parallel"), vmem_limit_bytes=VMEM_LIMIT)


def _attn_a_kernel(bnd_ref, lam_ref, q_ref, k_ref, v_ref, gate_ref, sg_ref, o_ref, *, lam_init):
    i, h = pl.program_id(0), pl.program_id(1)
    q = q_ref[0]
    tq = q.shape[1]
    z = jnp.zeros((A_QK_DIM, tq), BF16)
    qps = [jnp.concatenate([q[0:64], z], axis=0), jnp.concatenate([z, q[64:128]], axis=0)]

    def finish(res):
        (a0, l0), (a1, l1) = res
        lp = lam_ref[...]
        lam = (jnp.exp(jnp.sum(lp[0:1] * lp[1:2], axis=1, keepdims=True))
               - jnp.exp(jnp.sum(lp[2:3] * lp[3:4], axis=1, keepdims=True)) + lam_init)
        o = a0 / l0 - lam * (a1 / l1)
        ms = jnp.mean(o * o, axis=0, keepdims=True)
        o = o * lax.rsqrt(ms + RMS_EPS) * _rep(sg_ref, tq) * (1.0 - lam_init)
        o_ref[0] = (o * gate_ref[0]).astype(BF16)

    _attend(qps, [bnd_ref[i, 2 * h], bnd_ref[i, 2 * h + 1]],
            lambda r0: k_ref[0, 0, pl.ds(r0, MXU_TILE), :], v_ref, A_V_DIM, KV_CHUNK_A, finish)


def _attn_a(bnd, lam_p, qa, ka, va, gate, sg, lam_init):
    b = qa.shape[0]
    nt = SEQ // V_CHUNK
    return pl.pallas_call(
        functools.partial(_attn_a_kernel, lam_init=lam_init),
        grid=(b, A_HEADS, SEQ // TQ_A),
        in_specs=[
            pl.BlockSpec(memory_space=pltpu.SMEM),
            pl.BlockSpec((4, A_QK_DIM), lambda i, h, t: (0, 0)),
            pl.BlockSpec((1, 128, TQ_A), lambda i, h, t: (i, h, t)),
            pl.BlockSpec((1, 1, SEQ, K_PAD), lambda i, h, t: (i, h, 0, 0)),
            pl.BlockSpec((1, nt, A_V_DIM, V_CHUNK), lambda i, h, t: (i, 0, h, 0)),
            pl.BlockSpec((1, A_V_DIM, TQ_A), lambda i, h, t: (i, h, t)),
            pl.BlockSpec((A_V_DIM, LANES), lambda i, h, t: (0, 0)),
        ],
        out_specs=pl.BlockSpec((1, A_V_DIM, TQ_A), lambda i, h, t: (i, h, t)),
        out_shape=jax.ShapeDtypeStruct((b, A_WIDTH, SEQ), BF16),
        compiler_params=_attn_params(),
        name="attn_diff",
    )(bnd, lam_p, qa, ka, va, gate, sg)


def _attn_b_kernel(bnd_ref, q_ref, k_ref, v_ref, gate_ref, o_ref):
    i, h = pl.program_id(0), pl.program_id(1)
    first = h < B_GROUP
    q = q_ref[0]
    z = jnp.zeros_like(q)
    qp = jnp.concatenate([jnp.where(first, q, z), jnp.where(first, z, q)], axis=0)

    def finish(res):
        ((a, l),) = res
        o_ref[0] = (a / l * gate_ref[0]).astype(BF16)

    _attend([qp], [bnd_ref[i, 2 * A_HEADS + h]],
            lambda r0: k_ref[0, pl.ds(r0, MXU_TILE), :], v_ref, B_DIM, KV_CHUNK_BC, finish)


def _attn_b(bnd, qb, kb, vb, gate):
    b = qb.shape[0]
    nt = SEQ // V_CHUNK
    return pl.pallas_call(
        _attn_b_kernel,
        grid=(b, B_Q_HEADS, SEQ // TQ_B),
        in_specs=[
            pl.BlockSpec(memory_space=pltpu.SMEM),
            pl.BlockSpec((1, B_DIM, TQ_B), lambda i, h, t: (i, h, t)),
            pl.BlockSpec((1, SEQ, K_PAD), lambda i, h, t: (i, 0, 0)),
            pl.BlockSpec((1, nt, B_DIM, V_CHUNK), lambda i, h, t: (i, 0, h // B_GROUP, 0)),
            pl.BlockSpec((1, B_DIM, TQ_B), lambda i, h, t: (i, A_WIDTH // B_DIM + h, t)),
        ],
        out_specs=pl.BlockSpec((1, B_DIM, TQ_B), lambda i, h, t: (i, h, t)),
        out_shape=jax.ShapeDtypeStruct((b, B_WIDTH, SEQ), BF16),
        compiler_params=_attn_params(),
        name="attn_gqa",
    )(bnd, qb, kb, vb, gate)


def _attn_c_kernel(bnd_ref, q_ref, k_ref, v_ref, gate_ref, o_ref):
    i, h = pl.program_id(0), pl.program_id(1)
    q = q_ref[0]
    qp = jnp.concatenate([q, jnp.zeros((K_PAD - C_QK, q.shape[1]), BF16)], axis=0)

    def finish(res):
        ((a, l),) = res
        o_ref[0] = (a / l * gate_ref[0]).astype(BF16)

    _attend([qp], [bnd_ref[i, h]],
            lambda r0: k_ref[0, 0, pl.ds(r0, MXU_TILE), :], v_ref, C_V, KV_CHUNK_BC, finish)


def _attn_c(bnd, q, k, v, gate):
    b = q.shape[0]
    nt = SEQ // V_CHUNK
    return pl.pallas_call(
        _attn_c_kernel,
        grid=(b, C_HEADS, SEQ // TQ_C),
        in_specs=[
            pl.BlockSpec(memory_space=pltpu.SMEM),
            pl.BlockSpec((1, C_QK, TQ_C), lambda i, h, t: (i, h, t)),
            pl.BlockSpec((1, 1, SEQ, K_PAD), lambda i, h, t: (i, h, 0, 0)),
            pl.BlockSpec((1, nt, C_V, V_CHUNK), lambda i, h, t: (i, 0, h, 0)),
            pl.BlockSpec((1, C_V, TQ_C), lambda i, h, t: (i, h, t)),
        ],
        out_specs=pl.BlockSpec((1, C_V, TQ_C), lambda i, h, t: (i, h, t)),
        out_shape=jax.ShapeDtypeStruct((b, C_HEADS * C_V, SEQ), BF16),
        compiler_params=_attn_params(),
        name="attn_mla",
    )(bnd, q, k, v, gate)


def _out_body(og_refs, x_ref, w_ref, lg_ref, lb_ref, x_token_major):
    og = jnp.concatenate([r[0] for r in og_refs], axis=0) if len(og_refs) > 1 else og_refs[0][0]
    y = jnp.dot(w_ref[...], og, preferred_element_type=F32)
    x = x_ref[0].T if x_token_major else x_ref[0]
    z = ALPHA * x + y
    mu = jnp.mean(z, axis=0, keepdims=True)
    d = z - mu
    var = jnp.mean(d * d, axis=0, keepdims=True)
    width = y.shape[1]
    return d * lax.rsqrt(var + LN_EPS) * _rep(lg_ref, width) + _rep(lb_ref, width)


def _out_kernel(*refs, n_og):
    x_ref, w_ref, lg_ref, lb_ref, o_ref = refs[n_og:]
    width = TM // SUB_TILES
    for t in range(SUB_TILES):
        off = t * width
        og_views = [_tokens(r, 2, off, width) for r in refs[:n_og]]
        out = _out_body(og_views, _tokens(x_ref, 2, off, width), w_ref, lg_ref, lb_ref, False)
        o_ref[0, pl.ds(off, width), :] = out.T


def _tokens(ref, axis, off, width):
    idx = [slice(None)] * len(ref.shape)
    if isinstance(axis, tuple):
        idx[axis[0]] = pl.ds(off // V_CHUNK, 1)
        idx[axis[1]] = pl.ds(off % V_CHUNK, width)
    else:
        idx[axis] = pl.ds(off, width)
    return ref.at[tuple(idx)]


def _out_in_kernel(*refs, n_og, n_in, x_token_major, in_body, tab_pos, out_axes):
    x_ref, w_ref, lg_ref, lb_ref = refs[n_og:n_og + 4]
    in_params = refs[n_og + 4:n_og + 4 + n_in]
    xo_ref = refs[n_og + 4 + n_in]
    in_outs = refs[n_og + 5 + n_in:]
    width = TM // SUB_TILES
    outs = []
    for t in range(SUB_TILES):
        off = t * width
        og_views = [_tokens(r, 2, off, width) for r in refs[:n_og]]
        x_view = _tokens(x_ref, 1 if x_token_major else 2, off, width)
        out = _out_body(og_views, x_view, w_ref, lg_ref, lb_ref, x_token_major)
        xo_ref[0, :, pl.ds(off, width)] = out
        outs.append(out.astype(BF16))
    for t in range(SUB_TILES):
        off = t * width
        params = [_tokens(r, 1, off, width) if p == tab_pos else r
                  for p, r in enumerate(in_params)]
        views = [_tokens(r, ax, off, width) for r, ax in zip(in_outs, out_axes)]
        in_body(outs[t], *params, *views)


def _out_specs(ogs, x_token_major):
    d = D_MODEL
    x_spec = (pl.BlockSpec((1, TM, d), lambda i, j: (i, j, 0)) if x_token_major
              else pl.BlockSpec((1, d, TM), _tok))
    return [pl.BlockSpec((1, og.shape[1], TM), _tok) for og in ogs] + [
        x_spec, _resident((d, d)), _resident((d, LANES)), _resident((d, LANES))]


def _out_proj_last(ogs, xt, w_t, lg, lb):
    b = xt.shape[0]
    return pl.pallas_call(
        functools.partial(_out_kernel, n_og=len(ogs)),
        grid=(b, SEQ // TM),
        in_specs=_out_specs(ogs, False),
        out_specs=pl.BlockSpec((1, TM, D_MODEL), lambda i, j: (i, j, 0)),
        out_shape=jax.ShapeDtypeStruct((b, SEQ, D_MODEL), F32),
        compiler_params=_proj_params(),
        name="out_proj_ln",
    )(*ogs, xt, w_t, lg, lb)


def _out_in(ogs, xt, w_t, lg, lb, next_even, in_args, x_token_major):
    b = xt.shape[0]
    if next_even:
        in_body, (in_specs, out_specs, out_shape) = _even_in_body, _even_in_specs(b)
        tab_pos, out_axes, name = EVEN_TAB_POS, EVEN_OUT_TOKEN_AXES, "out_even_in_proj"
    else:
        in_body, (in_specs, out_specs, out_shape) = _odd_in_body, _odd_in_specs(b)
        tab_pos, out_axes, name = ODD_TAB_POS, ODD_OUT_TOKEN_AXES, "out_odd_in_proj"
    return pl.pallas_call(
        functools.partial(_out_in_kernel, n_og=len(ogs), n_in=len(in_args),
                          x_token_major=x_token_major, in_body=in_body, tab_pos=tab_pos,
                          out_axes=out_axes),
        grid=(b, SEQ // TM),
        in_specs=_out_specs(ogs, x_token_major) + in_specs,
        out_specs=[pl.BlockSpec((1, D_MODEL, TM), _tok)] + out_specs,
        out_shape=[jax.ShapeDtypeStruct((b, D_MODEL, SEQ), F32)] + out_shape,
        compiler_params=_proj_params(),
        name=name,
    )(*ogs, xt, w_t, lg, lb, *in_args)


def _col(v):
    v = v.astype(F32)
    return jnp.broadcast_to(v[:, None], (v.shape[0], LANES))


def _score_bounds(nq, kmax2):
    return jnp.sqrt(jnp.max(nq, axis=-1) * kmax2) * BOUND_SLACK


def _angles_t(pos, dims, theta):
    inv = theta ** (-jnp.arange(0, dims, 2, dtype=F32) / dims)
    ang = pos.astype(F32)[:, None] * inv[None, :]
    return jnp.cos(ang).T, jnp.sin(ang).T


@jax.jit
def _forward(x, ev_w_in, ev_w_out, ev_lam, ev_subln, ev_qnorm, ev_knorm, ev_ln_g, ev_ln_b,
             od_w_in, od_qnorm, od_kvnorm, od_w_qb, od_w_kvb, od_w_out, od_ln_g, od_ln_b):
    s = x.shape[1]
    pos = jnp.arange(s, dtype=jnp.int32)
    row = jnp.repeat(jnp.arange(s // GRID_W, dtype=jnp.int32), GRID_W)
    col = jnp.tile(jnp.arange(GRID_W, dtype=jnp.int32), s // GRID_W)
    tab_ev = jnp.concatenate(
        _angles_t(pos, A_ROT, ROPE_THETA) + _angles_t(row, B_DIM // 2, AXIAL_THETA)
        + _angles_t(col, B_DIM // 2, AXIAL_THETA), axis=0)
    tab_od = jnp.concatenate(_angles_t(pos, C_ROPE, ROPE_THETA), axis=0)

    def even_args(i):
        return (ev_w_in[i].T.astype(BF16), tab_ev, _col(ev_qnorm[i]), _col(ev_knorm[i]))

    def odd_args(i):
        return (od_w_in[i].T.astype(BF16), od_w_qb[i].T.astype(BF16),
                od_w_kvb[i].T.astype(BF16), tab_od, _col(od_qnorm[i]), _col(od_kvnorm[i]))

    xt = x
    proj = _even_in(x, *even_args(0))
    for layer in range(DEPTH):
        i = layer // 2
        if layer % 2 == 0:
            qa, ka, va, qb, kb, vb, gate, nq, nk = proj
            kmax = jnp.max(nk, axis=-1)
            kmax = jnp.concatenate(
                [kmax[:, :2 * A_HEADS],
                 jnp.repeat(kmax[:, 2 * A_HEADS:2 * A_HEADS + B_KV_HEADS], B_GROUP, axis=1)],
                axis=1)
            bnd = _score_bounds(nq, kmax)
            lam_init = 0.8 - 0.6 * math.exp(-0.3 * layer)
            ogs = [_attn_a(bnd, ev_lam[i].astype(F32), qa, ka, va, gate, _col(ev_subln[i]),
                           lam_init),
                   _attn_b(bnd, qb, kb, vb, gate)]
            out_args = (ev_w_out[i].T.astype(BF16), _col(ev_ln_g[i]), _col(ev_ln_b[i]))
        else:
            q, k, v, gate, nq, nk = proj
            ogs = [_attn_c(_score_bounds(nq, jnp.max(nk, axis=-1)), q, k, v, gate)]
            out_args = (od_w_out[i].T.astype(BF16), _col(od_ln_g[i]), _col(od_ln_b[i]))
        if layer == DEPTH - 1:
            return _out_proj_last(ogs, xt, *out_args)
        nxt = (layer + 1) // 2
        if layer % 2 == 0:
            xt, *proj = _out_in(ogs, xt, *out_args, False, odd_args(nxt), layer == 0)
        else:
            xt, *proj = _out_in(ogs, xt, *out_args, True, even_args(nxt), False)


def kernel(x, ev_w_in, ev_w_out, ev_lam, ev_subln, ev_qnorm, ev_knorm, ev_ln_g, ev_ln_b,
           od_w_in, od_qnorm, od_kvnorm, od_w_qb, od_w_kvb, od_w_out, od_ln_g, od_ln_b):
    return _forward(x, ev_w_in, ev_w_out, ev_lam, ev_subln, ev_qnorm, ev_knorm, ev_ln_g,
                    ev_ln_b, od_w_in, od_qnorm, od_kvnorm, od_w_qb, od_w_kvb, od_w_out,
                    od_ln_g, od_ln_b)
```

```python
import functools
import math

import jax
import jax.numpy as jnp
from jax import lax
from jax.experimental import pallas as pl
from jax.experimental.pallas import tpu as pltpu

F32 = jnp.float32
BF16 = jnp.bfloat16

D_MODEL = 1024
SEQ = 4096
DEPTH = 4
GRID_W = 64
ROPE_THETA = 500000.0
AXIAL_THETA = 10000.0
LN_EPS = 1e-5
RMS_EPS = 1e-6

A_HEADS = 4
A_QK_DIM = 64
A_V_DIM = 128
A_WIDTH = 512
A_ROT = 16
B_Q_HEADS = 8
B_KV_HEADS = 2
B_GROUP = 4
B_DIM = 64
B_WIDTH = 512
EV_IN = 3328

C_HEADS = 16
C_NOPE = 64
C_ROPE = 32
C_V = 64
C_Q_LORA = 256
C_KV_LORA = 128
C_QK = C_NOPE + C_ROPE
OD_IN = 1440

ALPHA = (2 * DEPTH) ** 0.25
LOG2E = 1.4426950408889634

QSCALE_AB = A_QK_DIM ** -0.5 * LOG2E
QSCALE_C = C_QK ** -0.5 * LOG2E

LANES = 128
MXU_TILE = 256
LOOKAHEAD_ITEMS = {1: 2, 2: 3, 4: 8}
MAX_LOCKSTEP = 4
ONES_ROWS = 16
K_PAD = 128
NORM_ROWS = 16
SAFE_BOUND = 60.0
BOUND_SLACK = 1.02
TM = 1024
V_CHUNK = 512
SUB_TILES = 4
EVEN_TAB_POS, EVEN_OUT_TOKEN_AXES = 1, (2, 2, (1, 3), 2, 1, (1, 3), 2, 2, 2)
ODD_TAB_POS, ODD_OUT_TOKEN_AXES = 3, (2, 2, (1, 3), 2, 2, 2)
KV_CHUNK_A = 512
KV_CHUNK_BC = 256
TQ_A = 1024
TQ_B = 4096
TQ_C = 4096
VMEM_LIMIT = 56 * 1024 * 1024


def _rep(ref, n):
    a = ref[...]
    return jnp.concatenate([a] * (n // LANES), axis=1)


def _rot(x1, x2, cos, sin):
    return x1 * cos - x2 * sin, x2 * cos + x1 * sin


def _silu(x):
    return x * jax.nn.sigmoid(x)


def _sumsq(x):
    return jnp.sum(x * x, axis=0, keepdims=True)


def _tok(i, j):
    return (i, 0, j)


def _const2(i, j):
    return (0, 0)


def _resident(shape):
    return pl.BlockSpec(shape, _const2, pipeline_mode=pl.Buffered(1))


def _proj_params():
    return pltpu.CompilerParams(
        dimension_semantics=("parallel", "parallel"), vmem_limit_bytes=VMEM_LIMIT)


def _even_in_body(xb, w_ref, tab_ref, qn_ref, kn_ref,
                  qa_ref, ka_ref, va_ref, qb_ref, kb_ref, vb_ref, g_ref, nq_ref, nk_ref,
                  *, x_token_major=False):
    contract = (((1,), (1 if x_token_major else 0,)), ((), ()))

    def proj(r0, r1):
        return lax.dot_general(w_ref[r0:r1, :], xb, contract, preferred_element_type=F32)

    cos_a, sin_a = tab_ref[0:8, :], tab_ref[8:16, :]
    cos_r, sin_r = tab_ref[16:32, :], tab_ref[32:48, :]
    cos_c, sin_c = tab_ref[48:64, :], tab_ref[64:80, :]

    def rope_a(h):
        outs = []
        for hc in range(2 * A_HEADS):
            b = hc * A_QK_DIM
            r1, r2 = _rot(h[b:b + 8], h[b + 8:b + 16], cos_a, sin_a)
            outs.append(jnp.concatenate([r1, r2, h[b + 16:b + 64]], axis=0))
        return outs

    g_ref[0] = _silu(proj(2304, 3328)).astype(BF16)
    qs = [x * QSCALE_AB for x in rope_a(proj(0, 512))]
    qa_ref[0] = jnp.concatenate(qs, axis=0).astype(BF16)
    ks = rope_a(proj(512, 1024))
    nq = [_sumsq(x) for x in qs]
    nk = [_sumsq(x) for x in ks]
    for h in range(A_HEADS):
        kt = jnp.concatenate([ks[2 * h], ks[2 * h + 1]], axis=0)
        ka_ref[0, h] = kt.T.astype(BF16)

    def norm_axial(h, g):
        ms = jnp.mean(h * h, axis=0, keepdims=True)
        y = h * lax.rsqrt(ms + RMS_EPS) * g
        a1, a2 = _rot(y[0:16], y[16:32], cos_r, sin_r)
        b1, b2 = _rot(y[32:48], y[48:64], cos_c, sin_c)
        return jnp.concatenate([a1, a2, b1, b2], axis=0)

    width = xb.shape[0 if x_token_major else 1]
    qn = _rep(qn_ref, width)
    kn = _rep(kn_ref, width)
    hq = proj(1536, 2048)
    qs = [norm_axial(hq[h * 64:(h + 1) * 64], qn) * QSCALE_AB for h in range(B_Q_HEADS)]
    qb_ref[0] = jnp.concatenate(qs, axis=0).astype(BF16)
    hkv = proj(2048, 2304)
    ks = [norm_axial(hkv[h * 64:(h + 1) * 64], kn) for h in range(B_KV_HEADS)]
    kb_ref[0] = jnp.concatenate(ks, axis=0).T.astype(BF16)
    nq_ref[0] = jnp.concatenate(nq + [_sumsq(x) for x in qs], axis=0)
    nk = nk + [_sumsq(x) for x in ks]
    nk_ref[0] = jnp.concatenate(nk + [jnp.zeros_like(nk[0])] * (NORM_ROWS - len(nk)), axis=0)
    vb_ref[0, 0] = hkv[128:256].astype(BF16)
    va_ref[0, 0] = proj(1024, 1536).astype(BF16)


def _even_in_kernel(x_ref, *refs):
    n_in = len(refs) - len(EVEN_OUT_TOKEN_AXES)
    width = TM // SUB_TILES
    for t in range(SUB_TILES):
        off = t * width
        params = [_tokens(r, 1, off, width) if p == EVEN_TAB_POS else r
                  for p, r in enumerate(refs[:n_in])]
        views = [_tokens(r, ax, off, width) for r, ax in zip(refs[n_in:], EVEN_OUT_TOKEN_AXES)]
        _even_in_body(x_ref[0, pl.ds(off, width), :].astype(BF16), *params, *views,
                      x_token_major=True)


def _even_in_specs(b):
    d, s = D_MODEL, SEQ
    nt = s // TM
    tok = _tok
    chunk = lambda i, j: (i, j, 0, 0)
    return (
        [
            _resident((EV_IN, d)),
            pl.BlockSpec((80, TM), lambda i, j: (0, j)),
            _resident((B_DIM, LANES)),
            _resident((B_DIM, LANES)),
        ],
        [
            pl.BlockSpec((1, 512, TM), tok),
            pl.BlockSpec((1, A_HEADS, TM, K_PAD), lambda i, j: (i, 0, j, 0)),
            pl.BlockSpec((1, TM // V_CHUNK, 512, V_CHUNK), chunk),
            pl.BlockSpec((1, 512, TM), tok),
            pl.BlockSpec((1, TM, K_PAD), lambda i, j: (i, j, 0)),
            pl.BlockSpec((1, TM // V_CHUNK, 128, V_CHUNK), chunk),
            pl.BlockSpec((1, 1024, TM), tok),
            pl.BlockSpec((1, NORM_ROWS, TM), tok),
            pl.BlockSpec((1, NORM_ROWS, TM), tok),
        ],
        [
            jax.ShapeDtypeStruct((b, 512, s), BF16),
            jax.ShapeDtypeStruct((b, A_HEADS, s, K_PAD), BF16),
            jax.ShapeDtypeStruct((b, s // V_CHUNK, 512, V_CHUNK), BF16),
            jax.ShapeDtypeStruct((b, 512, s), BF16),
            jax.ShapeDtypeStruct((b, s, K_PAD), BF16),
            jax.ShapeDtypeStruct((b, s // V_CHUNK, 128, V_CHUNK), BF16),
            jax.ShapeDtypeStruct((b, 1024, s), BF16),
            jax.ShapeDtypeStruct((b, NORM_ROWS, s), F32),
            jax.ShapeDtypeStruct((b, NORM_ROWS, s), F32),
        ],
    )


def _even_in(x, w_t, tab, qn, kn):
    b = x.shape[0]
    in_specs, out_specs, out_shape = _even_in_specs(b)
    return pl.pallas_call(
        _even_in_kernel,
        grid=(b, SEQ // TM),
        in_specs=[pl.BlockSpec((1, TM, D_MODEL), lambda i, j: (i, j, 0))] + in_specs,
        out_specs=out_specs,
        out_shape=out_shape,
        compiler_params=_proj_params(),
        name="even_in_proj",
    )(x, w_t, tab, qn, kn)


def _odd_in_body(xb, w_ref, wq_ref, wkv_ref, tab_ref, qn_ref, kvn_ref,
                 q_ref, k_ref, v_ref, g_ref, nq_ref, nk_ref):
    cos, sin = tab_ref[0:16, :], tab_ref[16:32, :]

    def rms(h, g):
        ms = jnp.mean(h * h, axis=0, keepdims=True)
        return h * lax.rsqrt(ms + RMS_EPS) * g

    g_ref[0] = _silu(jnp.dot(w_ref[416:1440, :], xb, preferred_element_type=F32)).astype(BF16)
    lat = jnp.dot(w_ref[0:416, :], xb, preferred_element_type=F32)
    width = xb.shape[1]
    cqn = rms(lat[0:256], _rep(qn_ref, width)).astype(BF16)
    q = jnp.dot(wq_ref[...], cqn, preferred_element_type=F32)
    qs = []
    for h in range(C_HEADS):
        b = h * C_QK
        r1, r2 = _rot(q[b + 64:b + 80], q[b + 80:b + 96], cos, sin)
        qs.append(jnp.concatenate([q[b:b + 64], r1, r2], axis=0) * QSCALE_C)
    q_ref[0] = jnp.concatenate(qs, axis=0).astype(BF16)
    nq_ref[0] = jnp.concatenate([_sumsq(x) for x in qs], axis=0)

    ckvn = rms(lat[256:384], _rep(kvn_ref, width)).astype(BF16)
    kv = jnp.dot(wkv_ref[...], ckvn, preferred_element_type=F32)
    r1, r2 = _rot(lat[384:400], lat[400:416], cos, sin)
    zpad = jnp.zeros((K_PAD - C_QK, width), F32)
    nkr = _sumsq(r1) + _sumsq(r2)
    nk = []
    for h in range(C_HEADS):
        kn = kv[h * 128:h * 128 + 64]
        nk.append(_sumsq(kn) + nkr)
        kt = jnp.concatenate([kn, r1, r2, zpad], axis=0)
        k_ref[0, h] = kt.T.astype(BF16)
    nk_ref[0] = jnp.concatenate(nk, axis=0)
    v_ref[0, 0] = jnp.concatenate(
        [kv[h * 128 + 64:h * 128 + 128] for h in range(C_HEADS)], axis=0).astype(BF16)


def _odd_in_specs(b):
    d, s = D_MODEL, SEQ
    nt = s // TM
    tok = _tok
    return (
        [
            _resident((OD_IN, d)),
            _resident((C_HEADS * C_QK, C_Q_LORA)),
            _resident((C_HEADS * 128, C_KV_LORA)),
            pl.BlockSpec((32, TM), lambda i, j: (0, j)),
            _resident((C_Q_LORA, LANES)),
            _resident((C_KV_LORA, LANES)),
        ],
        [
            pl.BlockSpec((1, C_HEADS * C_QK, TM), tok),
            pl.BlockSpec((1, C_HEADS, TM, K_PAD), lambda i, j: (i, 0, j, 0)),
            pl.BlockSpec((1, TM // V_CHUNK, 1024, V_CHUNK), lambda i, j: (i, j, 0, 0)),
            pl.BlockSpec((1, 1024, TM), tok),
            pl.BlockSpec((1, NORM_ROWS, TM), tok),
            pl.BlockSpec((1, NORM_ROWS, TM), tok),
        ],
        [
            jax.ShapeDtypeStruct((b, C_HEADS * C_QK, s), BF16),
            jax.ShapeDtypeStruct((b, C_HEADS, s, K_PAD), BF16),
            jax.ShapeDtypeStruct((b, s // V_CHUNK, 1024, V_CHUNK), BF16),
            jax.ShapeDtypeStruct((b, 1024, s), BF16),
            jax.ShapeDtypeStruct((b, NORM_ROWS, s), F32),
            jax.ShapeDtypeStruct((b, NORM_ROWS, s), F32),
        ],
    )


def _softmax_pv(qps, k_at, v_ref, dv, kv_chunk, bounds=None):
    nsets = len(qps)
    ntiles = qps[0].shape[1] // MXU_TILE
    qps = [qp[:, t * MXU_TILE:(t + 1) * MXU_TILE] for qp in qps for t in range(ntiles)]
    n = len(qps)
    nchunks = SEQ // kv_chunk
    halves = kv_chunk // MXU_TILE
    ones = jnp.ones((ONES_ROWS, kv_chunk), BF16)

    def scores(ci, j):
        return [jnp.dot(k_at(ci * kv_chunk + h * MXU_TILE), qps[j], preferred_element_type=F32)
                for h in range(halves)]

    m = [None] * n
    acc = [None] * n
    lsum = [None] * n
    items = [(ci, j) for g in range(0, n, MAX_LOCKSTEP) for ci in range(nchunks)
             for j in range(g, min(g + MAX_LOCKSTEP, n))]
    lookahead = LOOKAHEAD_ITEMS[min(n, MAX_LOCKSTEP)] * KV_CHUNK_A // kv_chunk
    pending = [scores(*it) for it in items[:lookahead]]
    for idx, (ci, j) in enumerate(items):
        if idx + lookahead < len(items):
            pending.append(scores(*items[idx + lookahead]))
        sc = pending.pop(0)
        r0 = ci * kv_chunk
        v = v_ref[0, r0 // V_CHUNK, :, pl.ds(r0 % V_CHUNK, kv_chunk)]
        if bounds is None:
            v = jnp.concatenate([v, ones], axis=0)
            cm = functools.reduce(jnp.maximum, [jnp.max(s, axis=0, keepdims=True) for s in sc])
            ref = cm if ci == 0 else jnp.maximum(m[j], cm)
        else:
            ref = bounds[j // ntiles]
        ps = [jnp.exp2(s - ref) for s in sc]
        pv = functools.reduce(jnp.add, [
            jnp.dot(v[:, h * MXU_TILE:(h + 1) * MXU_TILE], ps[h].astype(BF16),
                    preferred_element_type=F32) for h in range(halves)])
        if bounds is not None:
            part = functools.reduce(jnp.add, [
                jnp.sum(p.reshape(MXU_TILE // 8, 8, MXU_TILE), axis=0) for p in ps])
            lsum[j] = part if ci == 0 else lsum[j] + part
        if ci == 0:
            acc[j] = pv
        elif bounds is None:
            acc[j] = acc[j] * jnp.exp2(m[j] - ref) + pv
        else:
            acc[j] = acc[j] + pv
        m[j] = ref
    out = []
    for j in range(nsets):
        a = jnp.concatenate(acc[j * ntiles:(j + 1) * ntiles], axis=1)
        if bounds is None:
            out.append((a[0:dv], a[dv:dv + 1]))
        else:
            l8 = jnp.concatenate(lsum[j * ntiles:(j + 1) * ntiles], axis=1)
            out.append((a, jnp.sum(l8, axis=0, keepdims=True)))
    return out


def _attend(qps, bounds, k_at, v_ref, dv, kv_chunk, finish):
    safe = functools.reduce(jnp.logical_and, [b <= SAFE_BOUND for b in bounds])

    @pl.when(safe)
    def _():
        finish(_softmax_pv(qps, k_at, v_ref, dv, kv_chunk, bounds))

    @pl.when(jnp.logical_not(safe))
    def _():
        finish(_softmax_pv(qps, k_at, v_ref, dv, kv_chunk))


def _attn_params():
    return pltpu.CompilerParams(
        dimension_semantics=("parallel", "parallel", "parallel"), vmem_limit_bytes=VMEM_LIMIT)


def _attn_a_kernel(bnd_ref, lam_ref, q_ref, k_ref, v_ref, gate_ref, sg_ref, o_ref, *, lam_init):
    i, h = pl.program_id(0), pl.program_id(1)
    q = q_ref[0]
    tq = q.shape[1]
    z = jnp.zeros((A_QK_DIM, tq), BF16)
    qps = [jnp.concatenate([q[0:64], z], axis=0), jnp.concatenate([z, q[64:128]], axis=0)]

    def finish(res):
        (a0, l0), (a1, l1) = res
        lp = lam_ref[...]
        lam = (jnp.exp(jnp.sum(lp[0:1] * lp[1:2], axis=1, keepdims=True))
               - jnp.exp(jnp.sum(lp[2:3] * lp[3:4], axis=1, keepdims=True)) + lam_init)
        o = a0 / l0 - lam * (a1 / l1)
        ms = jnp.mean(o * o, axis=0, keepdims=True)
        o = o * lax.rsqrt(ms + RMS_EPS) * _rep(sg_ref, tq) * (1.0 - lam_init)
        o_ref[0] = (o * gate_ref[0]).astype(BF16)

    _attend(qps, [bnd_ref[i, 2 * h], bnd_ref[i, 2 * h + 1]],
            lambda r0: k_ref[0, 0, pl.ds(r0, MXU_TILE), :], v_ref, A_V_DIM, KV_CHUNK_A, finish)


def _attn_a(bnd, lam_p, qa, ka, va, gate, sg, lam_init):
    b = qa.shape[0]
    nt = SEQ // V_CHUNK
    return pl.pallas_call(
        functools.partial(_attn_a_kernel, lam_init=lam_init),
        grid=(b, A_HEADS, SEQ // TQ_A),
        in_specs=[
            pl.BlockSpec(memory_space=pltpu.SMEM),
            pl.BlockSpec((4, A_QK_DIM), lambda i, h, t: (0, 0)),
            pl.BlockSpec((1, 128, TQ_A), lambda i, h, t: (i, h, t)),
            pl.BlockSpec((1, 1, SEQ, K_PAD), lambda i, h, t: (i, h, 0, 0)),
            pl.BlockSpec((1, nt, A_V_DIM, V_CHUNK), lambda i, h, t: (i, 0, h, 0)),
            pl.BlockSpec((1, A_V_DIM, TQ_A), lambda i, h, t: (i, h, t)),
            pl.BlockSpec((A_V_DIM, LANES), lambda i, h, t: (0, 0)),
        ],
        out_specs=pl.BlockSpec((1, A_V_DIM, TQ_A), lambda i, h, t: (i, h, t)),
        out_shape=jax.ShapeDtypeStruct((b, A_WIDTH, SEQ), BF16),
        compiler_params=_attn_params(),
        name="attn_diff",
    )(bnd, lam_p, qa, ka, va, gate, sg)


def _attn_b_kernel(bnd_ref, q_ref, k_ref, v_ref, gate_ref, o_ref):
    i, h = pl.program_id(0), pl.program_id(1)
    first = h < B_GROUP
    q = q_ref[0]
    z = jnp.zeros_like(q)
    qp = jnp.concatenate([jnp.where(first, q, z), jnp.where(first, z, q)], axis=0)

    def finish(res):
        ((a, l),) = res
        o_ref[0] = (a / l * gate_ref[0]).astype(BF16)

    _attend([qp], [bnd_ref[i, 2 * A_HEADS + h]],
            lambda r0: k_ref[0, pl.ds(r0, MXU_TILE), :], v_ref, B_DIM, KV_CHUNK_BC, finish)


def _attn_b(bnd, qb, kb, vb, gate):
    b = qb.shape[0]
    nt = SEQ // V_CHUNK
    return pl.pallas_call(
        _attn_b_kernel,
        grid=(b, B_Q_HEADS, SEQ // TQ_B),
        in_specs=[
            pl.BlockSpec(memory_space=pltpu.SMEM),
            pl.BlockSpec((1, B_DIM, TQ_B), lambda i, h, t: (i, h, t)),
            pl.BlockSpec((1, SEQ, K_PAD), lambda i, h, t: (i, 0, 0)),
            pl.BlockSpec((1, nt, B_DIM, V_CHUNK), lambda i, h, t: (i, 0, h // B_GROUP, 0)),
            pl.BlockSpec((1, B_DIM, TQ_B), lambda i, h, t: (i, A_WIDTH // B_DIM + h, t)),
        ],
        out_specs=pl.BlockSpec((1, B_DIM, TQ_B), lambda i, h, t: (i, h, t)),
        out_shape=jax.ShapeDtypeStruct((b, B_WIDTH, SEQ), BF16),
        compiler_params=_attn_params(),
        name="attn_gqa",
    )(bnd, qb, kb, vb, gate)


def _attn_c_kernel(bnd_ref, q_ref, k_ref, v_ref, gate_ref, o_ref):
    i, h = pl.program_id(0), pl.program_id(1)
    q = q_ref[0]
    qp = jnp.concatenate([q, jnp.zeros((K_PAD - C_QK, q.shape[1]), BF16)], axis=0)

    def finish(res):
        ((a, l),) = res
        o_ref[0] = (a / l * gate_ref[0]).astype(BF16)

    _attend([qp], [bnd_ref[i, h]],
            lambda r0: k_ref[0, 0, pl.ds(r0, MXU_TILE), :], v_ref, C_V, KV_CHUNK_BC, finish)


def _attn_c(bnd, q, k, v, gate):
    b = q.shape[0]
    nt = SEQ // V_CHUNK
    return pl.pallas_call(
        _attn_c_kernel,
        grid=(b, C_HEADS, SEQ // TQ_C),
        in_specs=[
            pl.BlockSpec(memory_space=pltpu.SMEM),
            pl.BlockSpec((1, C_QK, TQ_C), lambda i, h, t: (i, h, t)),
            pl.BlockSpec((1, 1, SEQ, K_PAD), lambda i, h, t: (i, h, 0, 0)),
            pl.BlockSpec((1, nt, C_V, V_CHUNK), lambda i, h, t: (i, 0, h, 0)),
            pl.BlockSpec((1, C_V, TQ_C), lambda i, h, t: (i, h, t)),
        ],
        out_specs=pl.BlockSpec((1, C_V, TQ_C), lambda i, h, t: (i, h, t)),
        out_shape=jax.ShapeDtypeStruct((b, C_HEADS * C_V, SEQ), BF16),
        compiler_params=_attn_params(),
        name="attn_mla",
    )(bnd, q, k, v, gate)


def _out_body(og_refs, x_ref, w_ref, lg_ref, lb_ref, x_token_major):
    og = jnp.concatenate([r[0] for r in og_refs], axis=0) if len(og_refs) > 1 else og_refs[0][0]
    y = jnp.dot(w_ref[...], og, preferred_element_type=F32)
    x = x_ref[0].T if x_token_major else x_ref[0]
    z = ALPHA * x + y
    mu = jnp.mean(z, axis=0, keepdims=True)
    d = z - mu
    var = jnp.mean(d * d, axis=0, keepdims=True)
    width = y.shape[1]
    return d * lax.rsqrt(var + LN_EPS) * _rep(lg_ref, width) + _rep(lb_ref, width)


def _out_kernel(*refs, n_og):
    x_ref, w_ref, lg_ref, lb_ref, o_ref = refs[n_og:]
    width = TM // SUB_TILES
    for t in range(SUB_TILES):
        off = t * width
        og_views = [_tokens(r, 2, off, width) for r in refs[:n_og]]
        out = _out_body(og_views, _tokens(x_ref, 2, off, width), w_ref, lg_ref, lb_ref, False)
        o_ref[0, pl.ds(off, width), :] = out.T


def _tokens(ref, axis, off, width):
    idx = [slice(None)] * len(ref.shape)
    if isinstance(axis, tuple):
        idx[axis[0]] = pl.ds(off // V_CHUNK, 1)
        idx[axis[1]] = pl.ds(off % V_CHUNK, width)
    else:
        idx[axis] = pl.ds(off, width)
    return ref.at[tuple(idx)]


def _out_in_kernel(*refs, n_og, n_in, x_token_major, in_body, tab_pos, out_axes):
    x_ref, w_ref, lg_ref, lb_ref = refs[n_og:n_og + 4]
    in_params = refs[n_og + 4:n_og + 4 + n_in]
    xo_ref = refs[n_og + 4 + n_in]
    in_outs = refs[n_og + 5 + n_in:]
    width = TM // SUB_TILES
    outs = []
    for t in range(SUB_TILES):
        off = t * width
        og_views = [_tokens(r, 2, off, width) for r in refs[:n_og]]
        x_view = _tokens(x_ref, 1 if x_token_major else 2, off, width)
        out = _out_body(og_views, x_view, w_ref, lg_ref, lb_ref, x_token_major)
        xo_ref[0, :, pl.ds(off, width)] = out
        outs.append(out.astype(BF16))
    for t in range(SUB_TILES):
        off = t * width
        params = [_tokens(r, 1, off, width) if p == tab_pos else r
                  for p, r in enumerate(in_params)]
        views = [_tokens(r, ax, off, width) for r, ax in zip(in_outs, out_axes)]
        in_body(outs[t], *params, *views)


def _out_specs(ogs, x_token_major):
    d = D_MODEL
    x_spec = (pl.BlockSpec((1, TM, d), lambda i, j: (i, j, 0)) if x_token_major
              else pl.BlockSpec((1, d, TM), _tok))
    return [pl.BlockSpec((1, og.shape[1], TM), _tok) for og in ogs] + [
        x_spec, _resident((d, d)), _resident((d, LANES)), _resident((d, LANES))]


def _out_proj_last(ogs, xt, w_t, lg, lb):
    b = xt.shape[0]
    return pl.pallas_call(
        functools.partial(_out_kernel, n_og=len(ogs)),
        grid=(b, SEQ // TM),
        in_specs=_out_specs(ogs, False),
        out_specs=pl.BlockSpec((1, TM, D_MODEL), lambda i, j: (i, j, 0)),
        out_shape=jax.ShapeDtypeStruct((b, SEQ, D_MODEL), F32),
        compiler_params=_proj_params(),
        name="out_proj_ln",
    )(*ogs, xt, w_t, lg, lb)


def _out_in(ogs, xt, w_t, lg, lb, next_even, in_args, x_token_major):
    b = xt.shape[0]
    if next_even:
        in_body, (in_specs, out_specs, out_shape) = _even_in_body, _even_in_specs(b)
        tab_pos, out_axes, name = EVEN_TAB_POS, EVEN_OUT_TOKEN_AXES, "out_even_in_proj"
    else:
        in_body, (in_specs, out_specs, out_shape) = _odd_in_body, _odd_in_specs(b)
        tab_pos, out_axes, name = ODD_TAB_POS, ODD_OUT_TOKEN_AXES, "out_odd_in_proj"
    return pl.pallas_call(
        functools.partial(_out_in_kernel, n_og=len(ogs), n_in=len(in_args),
                          x_token_major=x_token_major, in_body=in_body, tab_pos=tab_pos,
                          out_axes=out_axes),
        grid=(b, SEQ // TM),
        in_specs=_out_specs(ogs, x_token_major) + in_specs,
        out_specs=[pl.BlockSpec((1, D_MODEL, TM), _tok)] + out_specs,
        out_shape=[jax.ShapeDtypeStruct((b, D_MODEL, SEQ), F32)] + out_shape,
        compiler_params=_proj_params(),
        name=name,
    )(*ogs, xt, w_t, lg, lb, *in_args)


def _col(v):
    v = v.astype(F32)
    return jnp.broadcast_to(v[:, None], (v.shape[0], LANES))


def _score_bounds(nq, kmax2):
    return jnp.sqrt(jnp.max(nq, axis=-1) * kmax2) * BOUND_SLACK


def _angles_t(pos, dims, theta):
    inv = theta ** (-jnp.arange(0, dims, 2, dtype=F32) / dims)
    ang = pos.astype(F32)[:, None] * inv[None, :]
    return jnp.cos(ang).T, jnp.sin(ang).T


@jax.jit
def _forward(x, ev_w_in, ev_w_out, ev_lam, ev_subln, ev_qnorm, ev_knorm, ev_ln_g, ev_ln_b,
             od_w_in, od_qnorm, od_kvnorm, od_w_qb, od_w_kvb, od_w_out, od_ln_g, od_ln_b):
    s = x.shape[1]
    pos = jnp.arange(s, dtype=jnp.int32)
    row = jnp.repeat(jnp.arange(s // GRID_W, dtype=jnp.int32), GRID_W)
    col = jnp.tile(jnp.arange(GRID_W, dtype=jnp.int32), s // GRID_W)
    tab_ev = jnp.concatenate(
        _angles_t(pos, A_ROT, ROPE_THETA) + _angles_t(row, B_DIM // 2, AXIAL_THETA)
        + _angles_t(col, B_DIM // 2, AXIAL_THETA), axis=0)
    tab_od = jnp.concatenate(_angles_t(pos, C_ROPE, ROPE_THETA), axis=0)

    def even_args(i):
        return (ev_w_in[i].T.astype(BF16), tab_ev, _col(ev_qnorm[i]), _col(ev_knorm[i]))

    def odd_args(i):
        return (od_w_in[i].T.astype(BF16), od_w_qb[i].T.astype(BF16),
                od_w_kvb[i].T.astype(BF16), tab_od, _col(od_qnorm[i]), _col(od_kvnorm[i]))

    xt = x
    proj = _even_in(x, *even_args(0))
    for layer in range(DEPTH):
        i = layer // 2
        if layer % 2 == 0:
            qa, ka, va, qb, kb, vb, gate, nq, nk = proj
            kmax = jnp.max(nk, axis=-1)
            kmax = jnp.concatenate(
                [kmax[:, :2 * A_HEADS],
                 jnp.repeat(kmax[:, 2 * A_HEADS:2 * A_HEADS + B_KV_HEADS], B_GROUP, axis=1)],
                axis=1)
            bnd = _score_bounds(nq, kmax)
            lam_init = 0.8 - 0.6 * math.exp(-0.3 * layer)
            ogs = [_attn_a(bnd, ev_lam[i].astype(F32), qa, ka, va, gate, _col(ev_subln[i]),
                           lam_init),
                   _attn_b(bnd, qb, kb, vb, gate)]
            out_args = (ev_w_out[i].T.astype(BF16), _col(ev_ln_g[i]), _col(ev_ln_b[i]))
        else:
            q, k, v, gate, nq, nk = proj
            ogs = [_attn_c(_score_bounds(nq, jnp.max(nk, axis=-1)), q, k, v, gate)]
            out_args = (od_w_out[i].T.astype(BF16), _col(od_ln_g[i]), _col(od_ln_b[i]))
        if layer == DEPTH - 1:
            return _out_proj_last(ogs, xt, *out_args)
        nxt = (layer + 1) // 2
        if layer % 2 == 0:
            xt, *proj = _out_in(ogs, xt, *out_args, False, odd_args(nxt), layer == 0)
        else:
            xt, *proj = _out_in(ogs, xt, *out_args, True, even_args(nxt), False)


def kernel(x, ev_w_in, ev_w_out, ev_lam, ev_subln, ev_qnorm, ev_knorm, ev_ln_g, ev_ln_b,
           od_w_in, od_qnorm, od_kvnorm, od_w_qb, od_w_kvb, od_w_out, od_ln_g, od_ln_b):
    return _forward(x, ev_w_in, ev_w_out, ev_lam, ev_subln, ev_qnorm, ev_knorm, ev_ln_g,
                    ev_ln_b, od_w_in, od_qnorm, od_kvnorm, od_w_qb, od_w_kvb, od_w_out,
                    od_ln_g, od_ln_b)
```

```python
import functools
import math

import jax
import jax.numpy as jnp
from jax import lax
from jax.experimental import pallas as pl
from jax.experimental.pallas import tpu as pltpu

F32 = jnp.float32
BF16 = jnp.bfloat16

D_MODEL = 1024
SEQ = 4096
DEPTH = 4
GRID_W = 64
ROPE_THETA = 500000.0
AXIAL_THETA = 10000.0
LN_EPS = 1e-5
RMS_EPS = 1e-6

A_HEADS = 4
A_QK_DIM = 64
A_V_DIM = 128
A_WIDTH = 512
A_ROT = 16
B_Q_HEADS = 8
B_KV_HEADS = 2
B_GROUP = 4
B_DIM = 64
B_WIDTH = 512
EV_IN = 3328

C_HEADS = 16
C_NOPE = 64
C_ROPE = 32
C_V = 64
C_Q_LORA = 256
C_KV_LORA = 128
C_QK = C_NOPE + C_ROPE
OD_IN = 1440

ALPHA = (2 * DEPTH) ** 0.25
LOG2E = 1.4426950408889634

QSCALE_AB = A_QK_DIM ** -0.5 * LOG2E
QSCALE_C = C_QK ** -0.5 * LOG2E

LANES = 128
MXU_TILE = 256
LOOKAHEAD_ITEMS = {1: 2, 2: 3, 4: 8}
MAX_LOCKSTEP = 4
ONES_ROWS = 16
K_PAD = 128
NORM_ROWS = 16
SAFE_BOUND = 60.0
BOUND_SLACK = 1.02
TM = 1024
V_CHUNK = 512
SUB_TILES = 4
EVEN_TAB_POS, EVEN_OUT_TOKEN_AXES = 1, (2, 2, (1, 3), 2, 1, (1, 3), 2, 2, 2)
ODD_TAB_POS, ODD_OUT_TOKEN_AXES = 3, (2, 2, (1, 3), 2, 2, 2)
KV_CHUNK_A = 512
KV_CHUNK_BC = 256
TQ_A = 1024
TQ_B = 4096
TQ_C = 4096
VMEM_LIMIT = 56 * 1024 * 1024


def _rep(ref, n):
    a = ref[...]
    return jnp.concatenate([a] * (n // LANES), axis=1)


def _rot(x1, x2, cos, sin):
    return x1 * cos - x2 * sin, x2 * cos + x1 * sin


def _silu(x):
    return x * jax.nn.sigmoid(x)


def _sumsq(x):
    return jnp.sum(x * x, axis=0, keepdims=True)


def _tok(i, j):
    return (i, 0, j)


def _const2(i, j):
    return (0, 0)


def _resident(shape):
    return pl.BlockSpec(shape, _const2, pipeline_mode=pl.Buffered(1))


def _proj_params():
    return pltpu.CompilerParams(
        dimension_semantics=("parallel", "parallel"), vmem_limit_bytes=VMEM_LIMIT)


def _even_in_body(xb, w_ref, tab_ref, qn_ref, kn_ref,
                  qa_ref, ka_ref, va_ref, qb_ref, kb_ref, vb_ref, g_ref, nq_ref, nk_ref,
                  *, x_token_major=False):
    contract = (((1,), (1 if x_token_major else 0,)), ((), ()))

    def proj(r0, r1):
        return lax.dot_general(w_ref[r0:r1, :], xb, contract, preferred_element_type=F32)

    cos_a, sin_a = tab_ref[0:8, :], tab_ref[8:16, :]
    cos_r, sin_r = tab_ref[16:32, :], tab_ref[32:48, :]
    cos_c, sin_c = tab_ref[48:64, :], tab_ref[64:80, :]

    def rope_a(h):
        outs = []
        for hc in range(2 * A_HEADS):
            b = hc * A_QK_DIM
            r1, r2 = _rot(h[b:b + 8], h[b + 8:b + 16], cos_a, sin_a)
            outs.append(jnp.concatenate([r1, r2, h[b + 16:b + 64]], axis=0))
        return outs

    g_ref[0] = _silu(proj(2304, 3328)).astype(BF16)
    qs = [x * QSCALE_AB for x in rope_a(proj(0, 512))]
    qa_ref[0] = jnp.concatenate(qs, axis=0).astype(BF16)
    ks = rope_a(proj(512, 1024))
    nq = [_sumsq(x) for x in qs]
    nk = [_sumsq(x) for x in ks]
    for h in range(A_HEADS):
        kt = jnp.concatenate([ks[2 * h], ks[2 * h + 1]], axis=0)
        ka_ref[0, h] = kt.T.astype(BF16)

    def norm_axial(h, g):
        ms = jnp.mean(h * h, axis=0, keepdims=True)
        y = h * lax.rsqrt(ms + RMS_EPS) * g
        a1, a2 = _rot(y[0:16], y[16:32], cos_r, sin_r)
        b1, b2 = _rot(y[32:48], y[48:64], cos_c, sin_c)
        return jnp.concatenate([a1, a2, b1, b2], axis=0)

    width = xb.shape[0 if x_token_major else 1]
    qn = _rep(qn_ref, width)
    kn = _rep(kn_ref, width)
    hq = proj(1536, 2048)
    qs = [norm_axial(hq[h * 64:(h + 1) * 64], qn) * QSCALE_AB for h in range(B_Q_HEADS)]
    qb_ref[0] = jnp.concatenate(qs, axis=0).astype(BF16)
    hkv = proj(2048, 2304)
    ks = [norm_axial(hkv[h * 64:(h + 1) * 64], kn) for h in range(B_KV_HEADS)]
    kb_ref[0] = jnp.concatenate(ks, axis=0).T.astype(BF16)
    nq_ref[0] = jnp.concatenate(nq + [_sumsq(x) for x in qs], axis=0)
    nk = nk + [_sumsq(x) for x in ks]
    nk_ref[0] = jnp.concatenate(nk + [jnp.zeros_like(nk[0])] * (NORM_ROWS - len(nk)), axis=0)
    vb_ref[0, 0] = hkv[128:256].astype(BF16)
    va_ref[0, 0] = proj(1024, 1536).astype(BF16)


def _even_in_kernel(x_ref, *refs):
    n_in = len(refs) - len(EVEN_OUT_TOKEN_AXES)
    width = TM // SUB_TILES
    for t in range(SUB_TILES):
        off = t * width
        params = [_tokens(r, 1, off, width) if p == EVEN_TAB_POS else r
                  for p, r in enumerate(refs[:n_in])]
        views = [_tokens(r, ax, off, width) for r, ax in zip(refs[n_in:], EVEN_OUT_TOKEN_AXES)]
        _even_in_body(x_ref[0, pl.ds(off, width), :].astype(BF16), *params, *views,
                      x_token_major=True)


def _even_in_specs(b):
    d, s = D_MODEL, SEQ
    nt = s // TM
    tok = _tok
    chunk = lambda i, j: (i, j, 0, 0)
    return (
        [
            _resident((EV_IN, d)),
            pl.BlockSpec((80, TM), lambda i, j: (0, j)),
            _resident((B_DIM, LANES)),
            _resident((B_DIM, LANES)),
        ],
        [
            pl.BlockSpec((1, 512, TM), tok),
            pl.BlockSpec((1, A_HEADS, TM, K_PAD), lambda i, j: (i, 0, j, 0)),
            pl.BlockSpec((1, TM // V_CHUNK, 512, V_CHUNK), chunk),
            pl.BlockSpec((1, 512, TM), tok),
            pl.BlockSpec((1, TM, K_PAD), lambda i, j: (i, j, 0)),
            pl.BlockSpec((1, TM // V_CHUNK, 128, V_CHUNK), chunk),
            pl.BlockSpec((1, 1024, TM), tok),
            pl.BlockSpec((1, NORM_ROWS, TM), tok),
            pl.BlockSpec((1, NORM_ROWS, TM), tok),
        ],
        [
            jax.ShapeDtypeStruct((b, 512, s), BF16),
            jax.ShapeDtypeStruct((b, A_HEADS, s, K_PAD), BF16),
            jax.ShapeDtypeStruct((b, s // V_CHUNK, 512, V_CHUNK), BF16),
            jax.ShapeDtypeStruct((b, 512, s), BF16),
            jax.ShapeDtypeStruct((b, s, K_PAD), BF16),
            jax.ShapeDtypeStruct((b, s // V_CHUNK, 128, V_CHUNK), BF16),
            jax.ShapeDtypeStruct((b, 1024, s), BF16),
            jax.ShapeDtypeStruct((b, NORM_ROWS, s), F32),
            jax.ShapeDtypeStruct((b, NORM_ROWS, s), F32),
        ],
    )


def _even_in(x, w_t, tab, qn, kn):
    b = x.shape[0]
    in_specs, out_specs, out_shape = _even_in_specs(b)
    return pl.pallas_call(
        _even_in_kernel,
        grid=(b, SEQ // TM),
        in_specs=[pl.BlockSpec((1, TM, D_MODEL), lambda i, j: (i, j, 0))] + in_specs,
        out_specs=out_specs,
        out_shape=out_shape,
        compiler_params=_proj_params(),
        name="even_in_proj",
    )(x, w_t, tab, qn, kn)


def _odd_in_body(xb, w_ref, wq_ref, wkv_ref, tab_ref, qn_ref, kvn_ref,
                 q_ref, k_ref, v_ref, g_ref, nq_ref, nk_ref):
    cos, sin = tab_ref[0:16, :], tab_ref[16:32, :]

    def rms(h, g):
        ms = jnp.mean(h * h, axis=0, keepdims=True)
        return h * lax.rsqrt(ms + RMS_EPS) * g

    g_ref[0] = _silu(jnp.dot(w_ref[416:1440, :], xb, preferred_element_type=F32)).astype(BF16)
    lat = jnp.dot(w_ref[0:416, :], xb, preferred_element_type=F32)
    width = xb.shape[1]
    cqn = rms(lat[0:256], _rep(qn_ref, width)).astype(BF16)
    q = jnp.dot(wq_ref[...], cqn, preferred_element_type=F32)
    qs = []
    for h in range(C_HEADS):
        b = h * C_QK
        r1, r2 = _rot(q[b + 64:b + 80], q[b + 80:b + 96], cos, sin)
        qs.append(jnp.concatenate([q[b:b + 64], r1, r2], axis=0) * QSCALE_C)
    q_ref[0] = jnp.concatenate(qs, axis=0).astype(BF16)
    nq_ref[0] = jnp.concatenate([_sumsq(x) for x in qs], axis=0)

    ckvn = rms(lat[256:384], _rep(kvn_ref, width)).astype(BF16)
    kv = jnp.dot(wkv_ref[...], ckvn, preferred_element_type=F32)
    r1, r2 = _rot(lat[384:400], lat[400:416], cos, sin)
    zpad = jnp.zeros((K_PAD - C_QK, width), F32)
    nkr = _sumsq(r1) + _sumsq(r2)
    nk = []
    for h in range(C_HEADS):
        kn = kv[h * 128:h * 128 + 64]
        nk.append(_sumsq(kn) + nkr)
        kt = jnp.concatenate([kn, r1, r2, zpad], axis=0)
        k_ref[0, h] = kt.T.astype(BF16)
    nk_ref[0] = jnp.concatenate(nk, axis=0)
    v_ref[0, 0] = jnp.concatenate(
        [kv[h * 128 + 64:h * 128 + 128] for h in range(C_HEADS)], axis=0).astype(BF16)


def _odd_in_specs(b):
    d, s = D_MODEL, SEQ
    nt = s // TM
    tok = _tok
    return (
        [
            _resident((OD_IN, d)),
            _resident((C_HEADS * C_QK, C_Q_LORA)),
            _resident((C_HEADS * 128, C_KV_LORA)),
            pl.BlockSpec((32, TM), lambda i, j: (0, j)),
            _resident((C_Q_LORA, LANES)),
            _resident((C_KV_LORA, LANES)),
        ],
        [
            pl.BlockSpec((1, C_HEADS * C_QK, TM), tok),
            pl.BlockSpec((1, C_HEADS, TM, K_PAD), lambda i, j: (i, 0, j, 0)),
            pl.BlockSpec((1, TM // V_CHUNK, 1024, V_CHUNK), lambda i, j: (i, j, 0, 0)),
            pl.BlockSpec((1, 1024, TM), tok),
            pl.BlockSpec((1, NORM_ROWS, TM), tok),
            pl.BlockSpec((1, NORM_ROWS, TM), tok),
        ],
        [
            jax.ShapeDtypeStruct((b, C_HEADS * C_QK, s), BF16),
            jax.ShapeDtypeStruct((b, C_HEADS, s, K_PAD), BF16),
            jax.ShapeDtypeStruct((b, s // V_CHUNK, 1024, V_CHUNK), BF16),
            jax.ShapeDtypeStruct((b, 1024, s), BF16),
            jax.ShapeDtypeStruct((b, NORM_ROWS, s), F32),
            jax.ShapeDtypeStruct((b, NORM_ROWS, s), F32),
        ],
    )


def _softmax_pv(qp_at, ntiles, k_at, v_ref, dv, kv_chunk, bounds=None):
    nsets = len(qp_at)
    streams = [(s, t) for s in range(nsets) for t in range(ntiles)]
    n = len(streams)
    nchunks = SEQ // kv_chunk
    halves = kv_chunk // MXU_TILE
    ones = jnp.ones((ONES_ROWS, kv_chunk), BF16)

    def scores(ci, j):
        s, t = streams[j]
        qp = qp_at[s](t)
        return [jnp.dot(k_at(ci * kv_chunk + h * MXU_TILE), qp, preferred_element_type=F32)
                for h in range(halves)]

    m = [None] * n
    acc = [None] * n
    lsum = [None] * n
    items = [(ci, j) for g in range(0, n, MAX_LOCKSTEP) for ci in range(nchunks)
             for j in range(g, min(g + MAX_LOCKSTEP, n))]
    lookahead = LOOKAHEAD_ITEMS[min(n, MAX_LOCKSTEP)]
    pending = [scores(*it) for it in items[:lookahead]]
    for idx, (ci, j) in enumerate(items):
        if idx + lookahead < len(items):
            pending.append(scores(*items[idx + lookahead]))
        sc = pending.pop(0)
        r0 = ci * kv_chunk
        v = v_ref[0, r0 // V_CHUNK, :, pl.ds(r0 % V_CHUNK, kv_chunk)]
        if bounds is None:
            v = jnp.concatenate([v, ones], axis=0)
            cm = functools.reduce(jnp.maximum, [jnp.max(s, axis=0, keepdims=True) for s in sc])
            ref = cm if ci == 0 else jnp.maximum(m[j], cm)
        else:
            ref = bounds[j // ntiles]
        ps = [jnp.exp2(s - ref) for s in sc]
        pv = functools.reduce(jnp.add, [
            jnp.dot(v[:, h * MXU_TILE:(h + 1) * MXU_TILE], ps[h].astype(BF16),
                    preferred_element_type=F32) for h in range(halves)])
        if bounds is not None:
            part = functools.reduce(jnp.add, [
                jnp.sum(p.reshape(MXU_TILE // 8, 8, MXU_TILE), axis=0) for p in ps])
            lsum[j] = part if ci == 0 else lsum[j] + part
        if ci == 0:
            acc[j] = pv
        elif bounds is None:
            acc[j] = acc[j] * jnp.exp2(m[j] - ref) + pv
        else:
            acc[j] = acc[j] + pv
        m[j] = ref
    out = []
    for j in range(nsets):
        a = jnp.concatenate(acc[j * ntiles:(j + 1) * ntiles], axis=1)
        if bounds is None:
            out.append((a[0:dv], a[dv:dv + 1]))
        else:
            l8 = jnp.concatenate(lsum[j * ntiles:(j + 1) * ntiles], axis=1)
            out.append((a, jnp.sum(l8, axis=0, keepdims=True)))
    return out


def _attend(qp_at, ntiles, bounds, k_at, v_ref, dv, kv_chunk, finish):
    safe = functools.reduce(jnp.logical_and, [b <= SAFE_BOUND for b in bounds])

    @pl.when(safe)
    def _():
        finish(_softmax_pv(qp_at, ntiles, k_at, v_ref, dv, kv_chunk, bounds))

    @pl.when(jnp.logical_not(safe))
    def _():
        finish(_softmax_pv(qp_at, ntiles, k_at, v_ref, dv, kv_chunk))


def _attn_params():
    return pltpu.CompilerParams(
        dimension_semantics=("parallel", "parallel", "parallel"), vmem_limit_bytes=VMEM_LIMIT)


def _attn_a_kernel(bnd_ref, lam_ref, q_ref, k_ref, v_ref, gate_ref, sg_ref, o_ref, *, lam_init):
    i, h = pl.program_id(0), pl.program_id(1)
    tq = q_ref.shape[2]
    z = jnp.zeros((A_QK_DIM, MXU_TILE), BF16)

    def tile(c, t):
        return q_ref[0, c * A_QK_DIM:(c + 1) * A_QK_DIM, t * MXU_TILE:(t + 1) * MXU_TILE]

    qp_at = [lambda t: jnp.concatenate([tile(0, t), z], axis=0),
             lambda t: jnp.concatenate([z, tile(1, t)], axis=0)]

    def finish(res):
        (a0, l0), (a1, l1) = res
        lp = lam_ref[...]
        lam = (jnp.exp(jnp.sum(lp[0:1] * lp[1:2], axis=1, keepdims=True))
               - jnp.exp(jnp.sum(lp[2:3] * lp[3:4], axis=1, keepdims=True)) + lam_init)
        o = a0 / l0 - lam * (a1 / l1)
        ms = jnp.mean(o * o, axis=0, keepdims=True)
        o = o * lax.rsqrt(ms + RMS_EPS) * _rep(sg_ref, tq) * (1.0 - lam_init)
        o_ref[0] = (o * gate_ref[0]).astype(BF16)

    _attend(qp_at, tq // MXU_TILE, [bnd_ref[i, 2 * h], bnd_ref[i, 2 * h + 1]],
            lambda r0: k_ref[0, 0, pl.ds(r0, MXU_TILE), :], v_ref, A_V_DIM, KV_CHUNK_A, finish)


def _attn_a(bnd, lam_p, qa, ka, va, gate, sg, lam_init):
    b = qa.shape[0]
    nt = SEQ // V_CHUNK
    return pl.pallas_call(
        functools.partial(_attn_a_kernel, lam_init=lam_init),
        grid=(b, A_HEADS, SEQ // TQ_A),
        in_specs=[
            pl.BlockSpec(memory_space=pltpu.SMEM),
            pl.BlockSpec((4, A_QK_DIM), lambda i, h, t: (0, 0)),
            pl.BlockSpec((1, 128, TQ_A), lambda i, h, t: (i, h, t)),
            pl.BlockSpec((1, 1, SEQ, K_PAD), lambda i, h, t: (i, h, 0, 0)),
            pl.BlockSpec((1, nt, A_V_DIM, V_CHUNK), lambda i, h, t: (i, 0, h, 0)),
            pl.BlockSpec((1, A_V_DIM, TQ_A), lambda i, h, t: (i, h, t)),
            pl.BlockSpec((A_V_DIM, LANES), lambda i, h, t: (0, 0)),
        ],
        out_specs=pl.BlockSpec((1, A_V_DIM, TQ_A), lambda i, h, t: (i, h, t)),
        out_shape=jax.ShapeDtypeStruct((b, A_WIDTH, SEQ), BF16),
        compiler_params=_attn_params(),
        name="attn_diff",
    )(bnd, lam_p, qa, ka, va, gate, sg)


def _attn_b_kernel(bnd_ref, q_ref, k_ref, v_ref, gate_ref, o_ref):
    i, h = pl.program_id(0), pl.program_id(1)
    first = h < B_GROUP
    z = jnp.zeros((B_DIM, MXU_TILE), BF16)

    def qp(t):
        q = q_ref[0, :, t * MXU_TILE:(t + 1) * MXU_TILE]
        return jnp.concatenate([jnp.where(first, q, z), jnp.where(first, z, q)], axis=0)

    def finish(res):
        ((a, l),) = res
        o_ref[0] = (a / l * gate_ref[0]).astype(BF16)

    _attend([qp], q_ref.shape[2] // MXU_TILE, [bnd_ref[i, 2 * A_HEADS + h]],
            lambda r0: k_ref[0, pl.ds(r0, MXU_TILE), :], v_ref, B_DIM, KV_CHUNK_BC, finish)


def _attn_b(bnd, qb, kb, vb, gate):
    b = qb.shape[0]
    nt = SEQ // V_CHUNK
    return pl.pallas_call(
        _attn_b_kernel,
        grid=(b, B_Q_HEADS, SEQ // TQ_B),
        in_specs=[
            pl.BlockSpec(memory_space=pltpu.SMEM),
            pl.BlockSpec((1, B_DIM, TQ_B), lambda i, h, t: (i, h, t)),
            pl.BlockSpec((1, SEQ, K_PAD), lambda i, h, t: (i, 0, 0)),
            pl.BlockSpec((1, nt, B_DIM, V_CHUNK), lambda i, h, t: (i, 0, h // B_GROUP, 0)),
            pl.BlockSpec((1, B_DIM, TQ_B), lambda i, h, t: (i, A_WIDTH // B_DIM + h, t)),
        ],
        out_specs=pl.BlockSpec((1, B_DIM, TQ_B), lambda i, h, t: (i, h, t)),
        out_shape=jax.ShapeDtypeStruct((b, B_WIDTH, SEQ), BF16),
        compiler_params=_attn_params(),
        name="attn_gqa",
    )(bnd, qb, kb, vb, gate)


def _attn_c_kernel(bnd_ref, q_ref, k_ref, v_ref, gate_ref, o_ref):
    i, h = pl.program_id(0), pl.program_id(1)
    z = jnp.zeros((K_PAD - C_QK, MXU_TILE), BF16)

    def qp(t):
        return jnp.concatenate([q_ref[0, :, t * MXU_TILE:(t + 1) * MXU_TILE], z], axis=0)

    def finish(res):
        ((a, l),) = res
        o_ref[0] = (a / l * gate_ref[0]).astype(BF16)

    _attend([qp], q_ref.shape[2] // MXU_TILE, [bnd_ref[i, h]],
            lambda r0: k_ref[0, 0, pl.ds(r0, MXU_TILE), :], v_ref, C_V, KV_CHUNK_BC, finish)


def _attn_c(bnd, q, k, v, gate):
    b = q.shape[0]
    nt = SEQ // V_CHUNK
    return pl.pallas_call(
        _attn_c_kernel,
        grid=(b, C_HEADS, SEQ // TQ_C),
        in_specs=[
            pl.BlockSpec(memory_space=pltpu.SMEM),
            pl.BlockSpec((1, C_QK, TQ_C), lambda i, h, t: (i, h, t)),
            pl.BlockSpec((1, 1, SEQ, K_PAD), lambda i, h, t: (i, h, 0, 0)),
            pl.BlockSpec((1, nt, C_V, V_CHUNK), lambda i, h, t: (i, 0, h, 0)),
            pl.BlockSpec((1, C_V, TQ_C), lambda i, h, t: (i, h, t)),
        ],
        out_specs=pl.BlockSpec((1, C_V, TQ_C), lambda i, h, t: (i, h, t)),
        out_shape=jax.ShapeDtypeStruct((b, C_HEADS * C_V, SEQ), BF16),
        compiler_params=_attn_params(),
        name="attn_mla",
    )(bnd, q, k, v, gate)


def _out_body(og_refs, x_ref, w_ref, lg_ref, lb_ref, x_token_major):
    og = jnp.concatenate([r[0] for r in og_refs], axis=0) if len(og_refs) > 1 else og_refs[0][0]
    y = jnp.dot(w_ref[...], og, preferred_element_type=F32)
    x = x_ref[0].T if x_token_major else x_ref[0]
    z = ALPHA * x + y
    mu = jnp.mean(z, axis=0, keepdims=True)
    d = z - mu
    var = jnp.mean(d * d, axis=0, keepdims=True)
    width = y.shape[1]
    return d * lax.rsqrt(var + LN_EPS) * _rep(lg_ref, width) + _rep(lb_ref, width)


def _out_kernel(*refs, n_og):
    x_ref, w_ref, lg_ref, lb_ref, o_ref = refs[n_og:]
    width = TM // SUB_TILES
    for t in range(SUB_TILES):
        off = t * width
        og_views = [_tokens(r, 2, off, width) for r in refs[:n_og]]
        out = _out_body(og_views, _tokens(x_ref, 2, off, width), w_ref, lg_ref, lb_ref, False)
        o_ref[0, pl.ds(off, width), :] = out.T


def _tokens(ref, axis, off, width):
    idx = [slice(None)] * len(ref.shape)
    if isinstance(axis, tuple):
        idx[axis[0]] = pl.ds(off // V_CHUNK, 1)
        idx[axis[1]] = pl.ds(off % V_CHUNK, width)
    else:
        idx[axis] = pl.ds(off, width)
    return ref.at[tuple(idx)]


def _out_in_kernel(*refs, n_og, n_in, x_token_major, in_body, tab_pos, out_axes):
    x_ref, w_ref, lg_ref, lb_ref = refs[n_og:n_og + 4]
    in_params = refs[n_og + 4:n_og + 4 + n_in]
    xo_ref = refs[n_og + 4 + n_in]
    in_outs = refs[n_og + 5 + n_in:]
    width = TM // SUB_TILES
    outs = []
    for t in range(SUB_TILES):
        off = t * width
        og_views = [_tokens(r, 2, off, width) for r in refs[:n_og]]
        x_view = _tokens(x_ref, 1 if x_token_major else 2, off, width)
        out = _out_body(og_views, x_view, w_ref, lg_ref, lb_ref, x_token_major)
        xo_ref[0, :, pl.ds(off, width)] = out
        outs.append(out.astype(BF16))
    for t in range(SUB_TILES):
        off = t * width
        params = [_tokens(r, 1, off, width) if p == tab_pos else r
                  for p, r in enumerate(in_params)]
        views = [_tokens(r, ax, off, width) for r, ax in zip(in_outs, out_axes)]
        in_body(outs[t], *params, *views)


def _out_specs(ogs, x_token_major):
    d = D_MODEL
    x_spec = (pl.BlockSpec((1, TM, d), lambda i, j: (i, j, 0)) if x_token_major
              else pl.BlockSpec((1, d, TM), _tok))
    return [pl.BlockSpec((1, og.shape[1], TM), _tok) for og in ogs] + [
        x_spec, _resident((d, d)), _resident((d, LANES)), _resident((d, LANES))]


def _out_proj_last(ogs, xt, w_t, lg, lb):
    b = xt.shape[0]
    return pl.pallas_call(
        functools.partial(_out_kernel, n_og=len(ogs)),
        grid=(b, SEQ // TM),
        in_specs=_out_specs(ogs, False),
        out_specs=pl.BlockSpec((1, TM, D_MODEL), lambda i, j: (i, j, 0)),
        out_shape=jax.ShapeDtypeStruct((b, SEQ, D_MODEL), F32),
        compiler_params=_proj_params(),
        name="out_proj_ln",
    )(*ogs, xt, w_t, lg, lb)


def _out_in(ogs, xt, w_t, lg, lb, next_even, in_args, x_token_major):
    b = xt.shape[0]
    if next_even:
        in_body, (in_specs, out_specs, out_shape) = _even_in_body, _even_in_specs(b)
        tab_pos, out_axes, name = EVEN_TAB_POS, EVEN_OUT_TOKEN_AXES, "out_even_in_proj"
    else:
        in_body, (in_specs, out_specs, out_shape) = _odd_in_body, _odd_in_specs(b)
        tab_pos, out_axes, name = ODD_TAB_POS, ODD_OUT_TOKEN_AXES, "out_odd_in_proj"
    return pl.pallas_call(
        functools.partial(_out_in_kernel, n_og=len(ogs), n_in=len(in_args),
                          x_token_major=x_token_major, in_body=in_body, tab_pos=tab_pos,
                          out_axes=out_axes),
        grid=(b, SEQ // TM),
        in_specs=_out_specs(ogs, x_token_major) + in_specs,
        out_specs=[pl.BlockSpec((1, D_MODEL, TM), _tok)] + out_specs,
        out_shape=[jax.ShapeDtypeStruct((b, D_MODEL, SEQ), F32)] + out_shape,
        compiler_params=_proj_params(),
        name=name,
    )(*ogs, xt, w_t, lg, lb, *in_args)


def _col(v):
    v = v.astype(F32)
    return jnp.broadcast_to(v[:, None], (v.shape[0], LANES))


def _score_bounds(nq, kmax2):
    return jnp.sqrt(jnp.max(nq, axis=-1) * kmax2) * BOUND_SLACK


def _angles_t(pos, dims, theta):
    inv = theta ** (-jnp.arange(0, dims, 2, dtype=F32) / dims)
    ang = pos.astype(F32)[:, None] * inv[None, :]
    return jnp.cos(ang).T, jnp.sin(ang).T


@jax.jit
def _forward(x, ev_w_in, ev_w_out, ev_lam, ev_subln, ev_qnorm, ev_knorm, ev_ln_g, ev_ln_b,
             od_w_in, od_qnorm, od_kvnorm, od_w_qb, od_w_kvb, od_w_out, od_ln_g, od_ln_b):
    s = x.shape[1]
    pos = jnp.arange(s, dtype=jnp.int32)
    row = jnp.repeat(jnp.arange(s // GRID_W, dtype=jnp.int32), GRID_W)
    col = jnp.tile(jnp.arange(GRID_W, dtype=jnp.int32), s // GRID_W)
    tab_ev = jnp.concatenate(
        _angles_t(pos, A_ROT, ROPE_THETA) + _angles_t(row, B_DIM // 2, AXIAL_THETA)
        + _angles_t(col, B_DIM // 2, AXIAL_THETA), axis=0)
    tab_od = jnp.concatenate(_angles_t(pos, C_ROPE, ROPE_THETA), axis=0)

    def even_args(i):
        return (ev_w_in[i].T.astype(BF16), tab_ev, _col(ev_qnorm[i]), _col(ev_knorm[i]))

    def odd_args(i):
        return (od_w_in[i].T.astype(BF16), od_w_qb[i].T.astype(BF16),
                od_w_kvb[i].T.astype(BF16), tab_od, _col(od_qnorm[i]), _col(od_kvnorm[i]))

    xt = x
    proj = _even_in(x, *even_args(0))
    for layer in range(DEPTH):
        i = layer // 2
        if layer % 2 == 0:
            qa, ka, va, qb, kb, vb, gate, nq, nk = proj
            kmax = jnp.max(nk, axis=-1)
            kmax = jnp.concatenate(
                [kmax[:, :2 * A_HEADS],
                 jnp.repeat(kmax[:, 2 * A_HEADS:2 * A_HEADS + B_KV_HEADS], B_GROUP, axis=1)],
                axis=1)
            bnd = _score_bounds(nq, kmax)
            lam_init = 0.8 - 0.6 * math.exp(-0.3 * layer)
            ogs = [_attn_a(bnd, ev_lam[i].astype(F32), qa, ka, va, gate, _col(ev_subln[i]),
                           lam_init),
                   _attn_b(bnd, qb, kb, vb, gate)]
            out_args = (ev_w_out[i].T.astype(BF16), _col(ev_ln_g[i]), _col(ev_ln_b[i]))
        else:
            q, k, v, gate, nq, nk = proj
            ogs = [_attn_c(_score_bounds(nq, jnp.max(nk, axis=-1)), q, k, v, gate)]
            out_args = (od_w_out[i].T.astype(BF16), _col(od_ln_g[i]), _col(od_ln_b[i]))
        if layer == DEPTH - 1:
            return _out_proj_last(ogs, xt, *out_args)
        nxt = (layer + 1) // 2
        if layer % 2 == 0:
            xt, *proj = _out_in(ogs, xt, *out_args, False, odd_args(nxt), layer == 0)
        else:
            xt, *proj = _out_in(ogs, xt, *out_args, True, even_args(nxt), False)


def kernel(x, ev_w_in, ev_w_out, ev_lam, ev_subln, ev_qnorm, ev_knorm, ev_ln_g, ev_ln_b,
           od_w_in, od_qnorm, od_kvnorm, od_w_qb, od_w_kvb, od_w_out, od_ln_g, od_ln_b):
    return _forward(x, ev_w_in, ev_w_out, ev_lam, ev_subln, ev_qnorm, ev_knorm, ev_ln_g,
                    ev_ln_b, od_w_in, od_qnorm, od_kvnorm, od_w_qb, od_w_kvb, od_w_out,
                    od_ln_g, od_ln_b)
```

```python
import functools
import math

import jax
import jax.numpy as jnp
from jax import lax
from jax.experimental import pallas as pl
from jax.experimental.pallas import tpu as pltpu

F32 = jnp.float32
BF16 = jnp.bfloat16

D_MODEL = 1024
SEQ = 4096
DEPTH = 4
GRID_W = 64
ROPE_THETA = 500000.0
AXIAL_THETA = 10000.0
LN_EPS = 1e-5
RMS_EPS = 1e-6

A_HEADS = 4
A_QK_DIM = 64
A_V_DIM = 128
A_WIDTH = 512
A_ROT = 16
B_Q_HEADS = 8
B_KV_HEADS = 2
B_GROUP = 4
B_DIM = 64
B_WIDTH = 512
EV_IN = 3328

C_HEADS = 16
C_NOPE = 64
C_ROPE = 32
C_V = 64
C_Q_LORA = 256
C_KV_LORA = 128
C_QK = C_NOPE + C_ROPE
OD_IN = 1440

ALPHA = (2 * DEPTH) ** 0.25
LOG2E = 1.4426950408889634

QSCALE_AB = A_QK_DIM ** -0.5 * LOG2E
QSCALE_C = C_QK ** -0.5 * LOG2E

LANES = 128
MXU_TILE = 256
LOOKAHEAD_ITEMS = {1: 2, 2: 3, 4: 8}
MAX_LOCKSTEP = 4
ONES_ROWS = 16
K_PAD = 128
NORM_ROWS = 16
SAFE_BOUND = 60.0
BOUND_SLACK = 1.02
TM = 1024
V_CHUNK = 512
SUB_TILES = 4
EVEN_TAB_POS, EVEN_OUT_TOKEN_AXES = 1, (2, 2, (1, 3), 2, 1, (1, 3), 2, 2, 2)
ODD_TAB_POS, ODD_OUT_TOKEN_AXES = 3, (2, 2, (1, 3), 2, 2, 2)
KV_CHUNK_A = 512
KV_CHUNK_BC = 256
TQ_A = 1024
TQ_B = 4096
TQ_C = 4096
VMEM_LIMIT = 56 * 1024 * 1024


def _rep(ref, n):
    a = ref[...]
    return jnp.concatenate([a] * (n // LANES), axis=1)


def _rot(x1, x2, cos, sin):
    return x1 * cos - x2 * sin, x2 * cos + x1 * sin


def _silu(x):
    return x * jax.nn.sigmoid(x)


def _sumsq(x):
    return jnp.sum(x * x, axis=0, keepdims=True)


def _tok(i, j):
    return (i, 0, j)


def _const2(i, j):
    return (0, 0)


def _resident(shape):
    return pl.BlockSpec(shape, _const2, pipeline_mode=pl.Buffered(1))


def _proj_params():
    return pltpu.CompilerParams(
        dimension_semantics=("parallel", "parallel"), vmem_limit_bytes=VMEM_LIMIT)


def _even_in_body(xb, w_ref, tab_ref, qn_ref, kn_ref,
                  qa_ref, ka_ref, va_ref, qb_ref, kb_ref, vb_ref, g_ref, nq_ref, nk_ref,
                  *, x_token_major=False):
    contract = (((1,), (1 if x_token_major else 0,)), ((), ()))

    def proj(r0, r1):
        return lax.dot_general(w_ref[r0:r1, :], xb, contract, preferred_element_type=F32)

    cos_a, sin_a = tab_ref[0:8, :], tab_ref[8:16, :]
    cos_r, sin_r = tab_ref[16:32, :], tab_ref[32:48, :]
    cos_c, sin_c = tab_ref[48:64, :], tab_ref[64:80, :]

    def rope_a(h):
        outs = []
        for hc in range(2 * A_HEADS):
            b = hc * A_QK_DIM
            r1, r2 = _rot(h[b:b + 8], h[b + 8:b + 16], cos_a, sin_a)
            outs.append(jnp.concatenate([r1, r2, h[b + 16:b + 64]], axis=0))
        return outs

    g_ref[0] = _silu(proj(2304, 3328)).astype(BF16)
    qs = [x * QSCALE_AB for x in rope_a(proj(0, 512))]
    qa_ref[0] = jnp.concatenate(qs, axis=0).astype(BF16)
    ks = rope_a(proj(512, 1024))
    nq = [_sumsq(x) for x in qs]
    nk = [_sumsq(x) for x in ks]
    for h in range(A_HEADS):
        kt = jnp.concatenate([ks[2 * h], ks[2 * h + 1]], axis=0)
        ka_ref[0, h] = kt.T.astype(BF16)

    def norm_axial(h, g):
        ms = jnp.mean(h * h, axis=0, keepdims=True)
        y = h * lax.rsqrt(ms + RMS_EPS) * g
        a1, a2 = _rot(y[0:16], y[16:32], cos_r, sin_r)
        b1, b2 = _rot(y[32:48], y[48:64], cos_c, sin_c)
        return jnp.concatenate([a1, a2, b1, b2], axis=0)

    width = xb.shape[0 if x_token_major else 1]
    qn = _rep(qn_ref, width)
    kn = _rep(kn_ref, width)
    hq = proj(1536, 2048)
    qs = [norm_axial(hq[h * 64:(h + 1) * 64], qn) * QSCALE_AB for h in range(B_Q_HEADS)]
    qb_ref[0] = jnp.concatenate(qs, axis=0).astype(BF16)
    hkv = proj(2048, 2304)
    ks = [norm_axial(hkv[h * 64:(h + 1) * 64], kn) for h in range(B_KV_HEADS)]
    kb_ref[0] = jnp.concatenate(ks, axis=0).T.astype(BF16)
    nq_ref[0] = jnp.concatenate(nq + [_sumsq(x) for x in qs], axis=0)
    nk = nk + [_sumsq(x) for x in ks]
    nk_ref[0] = jnp.concatenate(nk + [jnp.zeros_like(nk[0])] * (NORM_ROWS - len(nk)), axis=0)
    vb_ref[0, 0] = hkv[128:256].astype(BF16)
    va_ref[0, 0] = proj(1024, 1536).astype(BF16)


def _even_in_kernel(x_ref, *refs):
    n_in = len(refs) - len(EVEN_OUT_TOKEN_AXES)
    width = TM // SUB_TILES
    for t in range(SUB_TILES):
        off = t * width
        params = [_tokens(r, 1, off, width) if p == EVEN_TAB_POS else r
                  for p, r in enumerate(refs[:n_in])]
        views = [_tokens(r, ax, off, width) for r, ax in zip(refs[n_in:], EVEN_OUT_TOKEN_AXES)]
        _even_in_body(x_ref[0, pl.ds(off, width), :].astype(BF16), *params, *views,
                      x_token_major=True)


def _even_in_specs(b):
    d, s = D_MODEL, SEQ
    nt = s // TM
    tok = _tok
    chunk = lambda i, j: (i, j, 0, 0)
    return (
        [
            _resident((EV_IN, d)),
            pl.BlockSpec((80, TM), lambda i, j: (0, j)),
            _resident((B_DIM, LANES)),
            _resident((B_DIM, LANES)),
        ],
        [
            pl.BlockSpec((1, 512, TM), tok),
            pl.BlockSpec((1, A_HEADS, TM, K_PAD), lambda i, j: (i, 0, j, 0)),
            pl.BlockSpec((1, TM // V_CHUNK, 512, V_CHUNK), chunk),
            pl.BlockSpec((1, 512, TM), tok),
            pl.BlockSpec((1, TM, K_PAD), lambda i, j: (i, j, 0)),
            pl.BlockSpec((1, TM // V_CHUNK, 128, V_CHUNK), chunk),
            pl.BlockSpec((1, 1024, TM), tok),
            pl.BlockSpec((1, NORM_ROWS, TM), tok),
            pl.BlockSpec((1, NORM_ROWS, TM), tok),
        ],
        [
            jax.ShapeDtypeStruct((b, 512, s), BF16),
            jax.ShapeDtypeStruct((b, A_HEADS, s, K_PAD), BF16),
            jax.ShapeDtypeStruct((b, s // V_CHUNK, 512, V_CHUNK), BF16),
            jax.ShapeDtypeStruct((b, 512, s), BF16),
            jax.ShapeDtypeStruct((b, s, K_PAD), BF16),
            jax.ShapeDtypeStruct((b, s // V_CHUNK, 128, V_CHUNK), BF16),
            jax.ShapeDtypeStruct((b, 1024, s), BF16),
            jax.ShapeDtypeStruct((b, NORM_ROWS, s), F32),
            jax.ShapeDtypeStruct((b, NORM_ROWS, s), F32),
        ],
    )


def _even_in(x, w_t, tab, qn, kn):
    b = x.shape[0]
    in_specs, out_specs, out_shape = _even_in_specs(b)
    return pl.pallas_call(
        _even_in_kernel,
        grid=(b, SEQ // TM),
        in_specs=[pl.BlockSpec((1, TM, D_MODEL), lambda i, j: (i, j, 0))] + in_specs,
        out_specs=out_specs,
        out_shape=out_shape,
        compiler_params=_proj_params(),
        name="even_in_proj",
    )(x, w_t, tab, qn, kn)


def _odd_in_body(xb, w_ref, wq_ref, wkv_ref, tab_ref, qn_ref, kvn_ref,
                 q_ref, k_ref, v_ref, g_ref, nq_ref, nk_ref):
    cos, sin = tab_ref[0:16, :], tab_ref[16:32, :]

    def rms(h, g):
        ms = jnp.mean(h * h, axis=0, keepdims=True)
        return h * lax.rsqrt(ms + RMS_EPS) * g

    g_ref[0] = _silu(jnp.dot(w_ref[416:1440, :], xb, preferred_element_type=F32)).astype(BF16)
    lat = jnp.dot(w_ref[0:416, :], xb, preferred_element_type=F32)
    width = xb.shape[1]
    cqn = rms(lat[0:256], _rep(qn_ref, width)).astype(BF16)
    q = jnp.dot(wq_ref[...], cqn, preferred_element_type=F32)
    qs = []
    for h in range(C_HEADS):
        b = h * C_QK
        r1, r2 = _rot(q[b + 64:b + 80], q[b + 80:b + 96], cos, sin)
        qs.append(jnp.concatenate([q[b:b + 64], r1, r2], axis=0) * QSCALE_C)
    q_ref[0] = jnp.concatenate(qs, axis=0).astype(BF16)
    nq_ref[0] = jnp.concatenate([_sumsq(x) for x in qs], axis=0)

    ckvn = rms(lat[256:384], _rep(kvn_ref, width)).astype(BF16)
    kv = jnp.dot(wkv_ref[...], ckvn, preferred_element_type=F32)
    r1, r2 = _rot(lat[384:400], lat[400:416], cos, sin)
    zpad = jnp.zeros((K_PAD - C_QK, width), F32)
    nkr = _sumsq(r1) + _sumsq(r2)
    nk = []
    for h in range(C_HEADS):
        kn = kv[h * 128:h * 128 + 64]
        nk.append(_sumsq(kn) + nkr)
        kt = jnp.concatenate([kn, r1, r2, zpad], axis=0)
        k_ref[0, h] = kt.T.astype(BF16)
    nk_ref[0] = jnp.concatenate(nk, axis=0)
    v_ref[0, 0] = jnp.concatenate(
        [kv[h * 128 + 64:h * 128 + 128] for h in range(C_HEADS)], axis=0).astype(BF16)


def _odd_in_specs(b):
    d, s = D_MODEL, SEQ
    nt = s // TM
    tok = _tok
    return (
        [
            _resident((OD_IN, d)),
            _resident((C_HEADS * C_QK, C_Q_LORA)),
            _resident((C_HEADS * 128, C_KV_LORA)),
            pl.BlockSpec((32, TM), lambda i, j: (0, j)),
            _resident((C_Q_LORA, LANES)),
            _resident((C_KV_LORA, LANES)),
        ],
        [
            pl.BlockSpec((1, C_HEADS * C_QK, TM), tok),
            pl.BlockSpec((1, C_HEADS, TM, K_PAD), lambda i, j: (i, 0, j, 0)),
            pl.BlockSpec((1, TM // V_CHUNK, 1024, V_CHUNK), lambda i, j: (i, j, 0, 0)),
            pl.BlockSpec((1, 1024, TM), tok),
            pl.BlockSpec((1, NORM_ROWS, TM), tok),
            pl.BlockSpec((1, NORM_ROWS, TM), tok),
        ],
        [
            jax.ShapeDtypeStruct((b, C_HEADS * C_QK, s), BF16),
            jax.ShapeDtypeStruct((b, C_HEADS, s, K_PAD), BF16),
            jax.ShapeDtypeStruct((b, s // V_CHUNK, 1024, V_CHUNK), BF16),
            jax.ShapeDtypeStruct((b, 1024, s), BF16),
            jax.ShapeDtypeStruct((b, NORM_ROWS, s), F32),
            jax.ShapeDtypeStruct((b, NORM_ROWS, s), F32),
        ],
    )


def _softmax_pv(qp_at, ntiles, k_at, v_ref, dv, kv_chunk, bounds=None):
    nsets = len(qp_at)
    streams = [(s, t) for s in range(nsets) for t in range(ntiles)]
    n = len(streams)
    nchunks = SEQ // kv_chunk
    halves = kv_chunk // MXU_TILE
    ones = jnp.ones((ONES_ROWS, kv_chunk), BF16)

    def scores(ci, j):
        s, t = streams[j]
        qp = qp_at[s](t)
        return [jnp.dot(k_at(ci * kv_chunk + h * MXU_TILE), qp, preferred_element_type=F32)
                for h in range(halves)]

    m = [None] * n
    acc = [None] * n
    lsum = [None] * n
    items = [(ci, j) for g in range(0, n, MAX_LOCKSTEP) for ci in range(nchunks)
             for j in range(g, min(g + MAX_LOCKSTEP, n))]
    lookahead = LOOKAHEAD_ITEMS[min(n, MAX_LOCKSTEP)]
    pending = [scores(*it) for it in items[:lookahead]]
    for idx, (ci, j) in enumerate(items):
        if idx + lookahead < len(items):
            pending.append(scores(*items[idx + lookahead]))
        sc = pending.pop(0)
        r0 = ci * kv_chunk
        v = v_ref[0, r0 // V_CHUNK, :, pl.ds(r0 % V_CHUNK, kv_chunk)]
        if bounds is None:
            v = jnp.concatenate([v, ones], axis=0)
            cm = functools.reduce(jnp.maximum, [jnp.max(s, axis=0, keepdims=True) for s in sc])
            ref = cm if ci == 0 else jnp.maximum(m[j], cm)
        else:
            ref = bounds[j // ntiles]
        ps = [jnp.exp2(s - ref) for s in sc]
        pv = functools.reduce(jnp.add, [
            jnp.dot(v[:, h * MXU_TILE:(h + 1) * MXU_TILE], ps[h].astype(BF16),
                    preferred_element_type=F32) for h in range(halves)])
        if bounds is not None:
            part = functools.reduce(jnp.add, [
                jnp.sum(p.reshape(MXU_TILE // 8, 8, MXU_TILE), axis=0) for p in ps])
            lsum[j] = part if ci == 0 else lsum[j] + part
        if ci == 0:
            acc[j] = pv
        elif bounds is None:
            acc[j] = acc[j] * jnp.exp2(m[j] - ref) + pv
        else:
            acc[j] = acc[j] + pv
        m[j] = ref
    out = []
    for t in range(ntiles):
        per_set = []
        for s in range(nsets):
            a = acc[s * ntiles + t]
            if bounds is None:
                per_set.append((a[0:dv], a[dv:dv + 1]))
            else:
                per_set.append((a, jnp.sum(lsum[s * ntiles + t], axis=0, keepdims=True)))
        out.append(per_set)
    return out


def _attend(qp_at, ntiles, bounds, k_at, v_ref, dv, kv_chunk, finish):
    safe = functools.reduce(jnp.logical_and, [b <= SAFE_BOUND for b in bounds])

    @pl.when(safe)
    def _():
        for t, res in enumerate(_softmax_pv(qp_at, ntiles, k_at, v_ref, dv, kv_chunk, bounds)):
            finish(t, res)

    @pl.when(jnp.logical_not(safe))
    def _():
        for t, res in enumerate(_softmax_pv(qp_at, ntiles, k_at, v_ref, dv, kv_chunk)):
            finish(t, res)


def _attn_params():
    return pltpu.CompilerParams(
        dimension_semantics=("parallel", "parallel", "parallel"), vmem_limit_bytes=VMEM_LIMIT)


def _attn_a_kernel(bnd_ref, lam_ref, q_ref, k_ref, v_ref, gate_ref, sg_ref, o_ref, *, lam_init):
    i, h = pl.program_id(0), pl.program_id(1)
    tq = q_ref.shape[2]
    z = jnp.zeros((A_QK_DIM, MXU_TILE), BF16)

    def tile(c, t):
        return q_ref[0, c * A_QK_DIM:(c + 1) * A_QK_DIM, t * MXU_TILE:(t + 1) * MXU_TILE]

    qp_at = [lambda t: jnp.concatenate([tile(0, t), z], axis=0),
             lambda t: jnp.concatenate([z, tile(1, t)], axis=0)]

    def finish(t, res):
        (a0, l0), (a1, l1) = res
        lp = lam_ref[...]
        lam = (jnp.exp(jnp.sum(lp[0:1] * lp[1:2], axis=1, keepdims=True))
               - jnp.exp(jnp.sum(lp[2:3] * lp[3:4], axis=1, keepdims=True)) + lam_init)
        o = a0 / l0 - lam * (a1 / l1)
        ms = jnp.mean(o * o, axis=0, keepdims=True)
        o = o * lax.rsqrt(ms + RMS_EPS) * _rep(sg_ref, MXU_TILE) * (1.0 - lam_init)
        cols = pl.ds(t * MXU_TILE, MXU_TILE)
        o_ref[0, :, cols] = (o * gate_ref[0, :, cols]).astype(BF16)

    _attend(qp_at, tq // MXU_TILE, [bnd_ref[i, 2 * h], bnd_ref[i, 2 * h + 1]],
            lambda r0: k_ref[0, 0, pl.ds(r0, MXU_TILE), :], v_ref, A_V_DIM, KV_CHUNK_A, finish)


def _attn_a(bnd, lam_p, qa, ka, va, gate, sg, lam_init):
    b = qa.shape[0]
    nt = SEQ // V_CHUNK
    return pl.pallas_call(
        functools.partial(_attn_a_kernel, lam_init=lam_init),
        grid=(b, A_HEADS, SEQ // TQ_A),
        in_specs=[
            pl.BlockSpec(memory_space=pltpu.SMEM),
            pl.BlockSpec((4, A_QK_DIM), lambda i, h, t: (0, 0)),
            pl.BlockSpec((1, 128, TQ_A), lambda i, h, t: (i, h, t)),
            pl.BlockSpec((1, 1, SEQ, K_PAD), lambda i, h, t: (i, h, 0, 0)),
            pl.BlockSpec((1, nt, A_V_DIM, V_CHUNK), lambda i, h, t: (i, 0, h, 0)),
            pl.BlockSpec((1, A_V_DIM, TQ_A), lambda i, h, t: (i, h, t)),
            pl.BlockSpec((A_V_DIM, LANES), lambda i, h, t: (0, 0)),
        ],
        out_specs=pl.BlockSpec((1, A_V_DIM, TQ_A), lambda i, h, t: (i, h, t)),
        out_shape=jax.ShapeDtypeStruct((b, A_WIDTH, SEQ), BF16),
        compiler_params=_attn_params(),
        name="attn_diff",
    )(bnd, lam_p, qa, ka, va, gate, sg)


def _attn_b_kernel(bnd_ref, q_ref, k_ref, v_ref, gate_ref, o_ref):
    i, h = pl.program_id(0), pl.program_id(1)
    first = h < B_GROUP
    z = jnp.zeros((B_DIM, MXU_TILE), BF16)

    def qp(t):
        q = q_ref[0, :, t * MXU_TILE:(t + 1) * MXU_TILE]
        return jnp.concatenate([jnp.where(first, q, z), jnp.where(first, z, q)], axis=0)

    def finish(t, res):
        ((a, l),) = res
        cols = pl.ds(t * MXU_TILE, MXU_TILE)
        o_ref[0, :, cols] = (a / l * gate_ref[0, :, cols]).astype(BF16)

    _attend([qp], q_ref.shape[2] // MXU_TILE, [bnd_ref[i, 2 * A_HEADS + h]],
            lambda r0: k_ref[0, pl.ds(r0, MXU_TILE), :], v_ref, B_DIM, KV_CHUNK_BC, finish)


def _attn_b(bnd, qb, kb, vb, gate):
    b = qb.shape[0]
    nt = SEQ // V_CHUNK
    return pl.pallas_call(
        _attn_b_kernel,
        grid=(b, B_Q_HEADS, SEQ // TQ_B),
        in_specs=[
            pl.BlockSpec(memory_space=pltpu.SMEM),
            pl.BlockSpec((1, B_DIM, TQ_B), lambda i, h, t: (i, h, t)),
            pl.BlockSpec((1, SEQ, K_PAD), lambda i, h, t: (i, 0, 0)),
            pl.BlockSpec((1, nt, B_DIM, V_CHUNK), lambda i, h, t: (i, 0, h // B_GROUP, 0)),
            pl.BlockSpec((1, B_DIM, TQ_B), lambda i, h, t: (i, A_WIDTH // B_DIM + h, t)),
        ],
        out_specs=pl.BlockSpec((1, B_DIM, TQ_B), lambda i, h, t: (i, h, t)),
        out_shape=jax.ShapeDtypeStruct((b, B_WIDTH, SEQ), BF16),
        compiler_params=_attn_params(),
        name="attn_gqa",
    )(bnd, qb, kb, vb, gate)


def _attn_c_kernel(bnd_ref, q_ref, k_ref, v_ref, gate_ref, o_ref):
    i, h = pl.program_id(0), pl.program_id(1)
    z = jnp.zeros((K_PAD - C_QK, MXU_TILE), BF16)

    def qp(t):
        return jnp.concatenate([q_ref[0, :, t * MXU_TILE:(t + 1) * MXU_TILE], z], axis=0)

    def finish(t, res):
        ((a, l),) = res
        cols = pl.ds(t * MXU_TILE, MXU_TILE)
        o_ref[0, :, cols] = (a / l * gate_ref[0, :, cols]).astype(BF16)

    _attend([qp], q_ref.shape[2] // MXU_TILE, [bnd_ref[i, h]],
            lambda r0: k_ref[0, 0, pl.ds(r0, MXU_TILE), :], v_ref, C_V, KV_CHUNK_BC, finish)


def _attn_c(bnd, q, k, v, gate):
    b = q.shape[0]
    nt = SEQ // V_CHUNK
    return pl.pallas_call(
        _attn_c_kernel,
        grid=(b, C_HEADS, SEQ // TQ_C),
        in_specs=[
            pl.BlockSpec(memory_space=pltpu.SMEM),
            pl.BlockSpec((1, C_QK, TQ_C), lambda i, h, t: (i, h, t)),
            pl.BlockSpec((1, 1, SEQ, K_PAD), lambda i, h, t: (i, h, 0, 0)),
            pl.BlockSpec((1, nt, C_V, V_CHUNK), lambda i, h, t: (i, 0, h, 0)),
            pl.BlockSpec((1, C_V, TQ_C), lambda i, h, t: (i, h, t)),
        ],
        out_specs=pl.BlockSpec((1, C_V, TQ_C), lambda i, h, t: (i, h, t)),
        out_shape=jax.ShapeDtypeStruct((b, C_HEADS * C_V, SEQ), BF16),
        compiler_params=_attn_params(),
        name="attn_mla",
    )(bnd, q, k, v, gate)


def _out_body(og_refs, x_ref, w_ref, lg_ref, lb_ref, x_token_major):
    og = jnp.concatenate([r[0] for r in og_refs], axis=0) if len(og_refs) > 1 else og_refs[0][0]
    y = jnp.dot(w_ref[...], og, preferred_element_type=F32)
    x = x_ref[0].T if x_token_major else x_ref[0]
    z = ALPHA * x + y
    mu = jnp.mean(z, axis=0, keepdims=True)
    d = z - mu
    var = jnp.mean(d * d, axis=0, keepdims=True)
    width = y.shape[1]
    return d * lax.rsqrt(var + LN_EPS) * _rep(lg_ref, width) + _rep(lb_ref, width)


def _out_kernel(*refs, n_og):
    x_ref, w_ref, lg_ref, lb_ref, o_ref = refs[n_og:]
    width = TM // SUB_TILES
    for t in range(SUB_TILES):
        off = t * width
        og_views = [_tokens(r, 2, off, width) for r in refs[:n_og]]
        out = _out_body(og_views, _tokens(x_ref, 2, off, width), w_ref, lg_ref, lb_ref, False)
        o_ref[0, pl.ds(off, width), :] = out.T


def _tokens(ref, axis, off, width):
    idx = [slice(None)] * len(ref.shape)
    if isinstance(axis, tuple):
        idx[axis[0]] = pl.ds(off // V_CHUNK, 1)
        idx[axis[1]] = pl.ds(off % V_CHUNK, width)
    else:
        idx[axis] = pl.ds(off, width)
    return ref.at[tuple(idx)]


def _out_in_kernel(*refs, n_og, n_in, x_token_major, in_body, tab_pos, out_axes):
    x_ref, w_ref, lg_ref, lb_ref = refs[n_og:n_og + 4]
    in_params = refs[n_og + 4:n_og + 4 + n_in]
    xo_ref = refs[n_og + 4 + n_in]
    in_outs = refs[n_og + 5 + n_in:]
    width = TM // SUB_TILES
    outs = []
    for t in range(SUB_TILES):
        off = t * width
        og_views = [_tokens(r, 2, off, width) for r in refs[:n_og]]
        x_view = _tokens(x_ref, 1 if x_token_major else 2, off, width)
        out = _out_body(og_views, x_view, w_ref, lg_ref, lb_ref, x_token_major)
        xo_ref[0, :, pl.ds(off, width)] = out
        outs.append(out.astype(BF16))
    for t in range(SUB_TILES):
        off = t * width
        params = [_tokens(r, 1, off, width) if p == tab_pos else r
                  for p, r in enumerate(in_params)]
        views = [_tokens(r, ax, off, width) for r, ax in zip(in_outs, out_axes)]
        in_body(outs[t], *params, *views)


def _out_specs(ogs, x_token_major):
    d = D_MODEL
    x_spec = (pl.BlockSpec((1, TM, d), lambda i, j: (i, j, 0)) if x_token_major
              else pl.BlockSpec((1, d, TM), _tok))
    return [pl.BlockSpec((1, og.shape[1], TM), _tok) for og in ogs] + [
        x_spec, _resident((d, d)), _resident((d, LANES)), _resident((d, LANES))]


def _out_proj_last(ogs, xt, w_t, lg, lb):
    b = xt.shape[0]
    return pl.pallas_call(
        functools.partial(_out_kernel, n_og=len(ogs)),
        grid=(b, SEQ // TM),
        in_specs=_out_specs(ogs, False),
        out_specs=pl.BlockSpec((1, TM, D_MODEL), lambda i, j: (i, j, 0)),
        out_shape=jax.ShapeDtypeStruct((b, SEQ, D_MODEL), F32),
        compiler_params=_proj_params(),
        name="out_proj_ln",
    )(*ogs, xt, w_t, lg, lb)


def _out_in(ogs, xt, w_t, lg, lb, next_even, in_args, x_token_major):
    b = xt.shape[0]
    if next_even:
        in_body, (in_specs, out_specs, out_shape) = _even_in_body, _even_in_specs(b)
        tab_pos, out_axes, name = EVEN_TAB_POS, EVEN_OUT_TOKEN_AXES, "out_even_in_proj"
    else:
        in_body, (in_specs, out_specs, out_shape) = _odd_in_body, _odd_in_specs(b)
        tab_pos, out_axes, name = ODD_TAB_POS, ODD_OUT_TOKEN_AXES, "out_odd_in_proj"
    return pl.pallas_call(
        functools.partial(_out_in_kernel, n_og=len(ogs), n_in=len(in_args),
                          x_token_major=x_token_major, in_body=in_body, tab_pos=tab_pos,
                          out_axes=out_axes),
        grid=(b, SEQ // TM),
        in_specs=_out_specs(ogs, x_token_major) + in_specs,
        out_specs=[pl.BlockSpec((1, D_MODEL, TM), _tok)] + out_specs,
        out_shape=[jax.ShapeDtypeStruct((b, D_MODEL, SEQ), F32)] + out_shape,
        compiler_params=_proj_params(),
        name=name,
    )(*ogs, xt, w_t, lg, lb, *in_args)


def _col(v):
    v = v.astype(F32)
    return jnp.broadcast_to(v[:, None], (v.shape[0], LANES))


def _score_bounds(nq, kmax2):
    return jnp.sqrt(jnp.max(nq, axis=-1) * kmax2) * BOUND_SLACK


def _angles_t(pos, dims, theta):
    inv = theta ** (-jnp.arange(0, dims, 2, dtype=F32) / dims)
    ang = pos.astype(F32)[:, None] * inv[None, :]
    return jnp.cos(ang).T, jnp.sin(ang).T


@jax.jit
def _forward(x, ev_w_in, ev_w_out, ev_lam, ev_subln, ev_qnorm, ev_knorm, ev_ln_g, ev_ln_b,
             od_w_in, od_qnorm, od_kvnorm, od_w_qb, od_w_kvb, od_w_out, od_ln_g, od_ln_b):
    s = x.shape[1]
    pos = jnp.arange(s, dtype=jnp.int32)
    row = jnp.repeat(jnp.arange(s // GRID_W, dtype=jnp.int32), GRID_W)
    col = jnp.tile(jnp.arange(GRID_W, dtype=jnp.int32), s // GRID_W)
    tab_ev = jnp.concatenate(
        _angles_t(pos, A_ROT, ROPE_THETA) + _angles_t(row, B_DIM // 2, AXIAL_THETA)
        + _angles_t(col, B_DIM // 2, AXIAL_THETA), axis=0)
    tab_od = jnp.concatenate(_angles_t(pos, C_ROPE, ROPE_THETA), axis=0)

    def even_args(i):
        return (ev_w_in[i].T.astype(BF16), tab_ev, _col(ev_qnorm[i]), _col(ev_knorm[i]))

    def odd_args(i):
        return (od_w_in[i].T.astype(BF16), od_w_qb[i].T.astype(BF16),
                od_w_kvb[i].T.astype(BF16), tab_od, _col(od_qnorm[i]), _col(od_kvnorm[i]))

    xt = x
    proj = _even_in(x, *even_args(0))
    for layer in range(DEPTH):
        i = layer // 2
        if layer % 2 == 0:
            qa, ka, va, qb, kb, vb, gate, nq, nk = proj
            kmax = jnp.max(nk, axis=-1)
            kmax = jnp.concatenate(
                [kmax[:, :2 * A_HEADS],
                 jnp.repeat(kmax[:, 2 * A_HEADS:2 * A_HEADS + B_KV_HEADS], B_GROUP, axis=1)],
                axis=1)
            bnd = _score_bounds(nq, kmax)
            lam_init = 0.8 - 0.6 * math.exp(-0.3 * layer)
            ogs = [_attn_a(bnd, ev_lam[i].astype(F32), qa, ka, va, gate, _col(ev_subln[i]),
                           lam_init),
                   _attn_b(bnd, qb, kb, vb, gate)]
            out_args = (ev_w_out[i].T.astype(BF16), _col(ev_ln_g[i]), _col(ev_ln_b[i]))
        else:
            q, k, v, gate, nq, nk = proj
            ogs = [_attn_c(_score_bounds(nq, jnp.max(nk, axis=-1)), q, k, v, gate)]
            out_args = (od_w_out[i].T.astype(BF16), _col(od_ln_g[i]), _col(od_ln_b[i]))
        if layer == DEPTH - 1:
            return _out_proj_last(ogs, xt, *out_args)
        nxt = (layer + 1) // 2
        if layer % 2 == 0:
            xt, *proj = _out_in(ogs, xt, *out_args, False, odd_args(nxt), layer == 0)
        else:
            xt, *proj = _out_in(ogs, xt, *out_args, True, even_args(nxt), False)


def kernel(x, ev_w_in, ev_w_out, ev_lam, ev_subln, ev_qnorm, ev_knorm, ev_ln_g, ev_ln_b,
           od_w_in, od_qnorm, od_kvnorm, od_w_qb, od_w_kvb, od_w_out, od_ln_g, od_ln_b):
    return _forward(x, ev_w_in, ev_w_out, ev_lam, ev_subln, ev_qnorm, ev_knorm, ev_ln_g,
                    ev_ln_b, od_w_in, od_qnorm, od_kvnorm, od_w_qb, od_w_kvb, od_w_out,
                    od_ln_g, od_ln_b)
```
